```python
import jax, jax.numpy as jnp
from jax import lax
import numpy as np

D_MODEL = 1024
BATCH = 8
SEQ = 4096
DEPTH = 1
DEC_BATCH = 16
DEC_SEQ = 64
PAST_LEN = 4096

CHUNK = 64
N_PREV_CHUNKS = 8
BAND_PAST = N_PREV_CHUNKS * CHUNK
BAND = BAND_PAST + CHUNK
HEAD_DIM = 64
H_A = 8
H_B = 8
WIDTH_A = H_A * HEAD_DIM
WIDTH_B = H_B * HEAD_DIM
MIX_WIDTH = WIDTH_A + WIDTH_B
IN_COLS = 3 * WIDTH_A + 3 * WIDTH_B + H_B
REL_CLIP = 128
Q_BLOCK = 128
N_MEM = 256
H_M = 4
HEAD_DIM_M = D_MODEL // H_M
D_FF = 2816
CONV_W = 3
EPS = 1e-6

kernel_name = "hybrid_chunk_band_fox_stream_step"


def rms_norm(x, g):
    x32 = x.astype(jnp.float32)
    y = x32 * lax.rsqrt(jnp.mean(x32 * x32, axis=-1, keepdims=True) + EPS)
    return (y * g.astype(jnp.float32)).astype(x.dtype)


def attend(q, k, v, bias, mask):
    scale = q.shape[-1] ** -0.5
    logits = jnp.einsum('bqhd,bkhd->bhqk', q, k).astype(jnp.float32) * scale
    if bias is not None:
        logits = logits + bias
    if mask is not None:
        logits = jnp.where(mask, logits, -jnp.inf)
    p = jax.nn.softmax(logits, axis=-1)
    return jnp.einsum('bhqk,bkhd->bqhd', p.astype(v.dtype), v)


def in_proj(h, w_in, b_f, g_qa, g_ka, g_qb, g_kb):
    b, t, _ = h.shape
    p = h @ w_in
    def heads(lo, width, nh):
        return p[..., lo:lo + width].reshape(b, t, nh, HEAD_DIM)
    qa = rms_norm(heads(0, WIDTH_A, H_A), g_qa)
    ka = rms_norm(heads(WIDTH_A, WIDTH_A, H_A), g_ka)
    va = heads(2 * WIDTH_A, WIDTH_A, H_A)
    o = 3 * WIDTH_A
    qb = rms_norm(heads(o, WIDTH_B, H_B), g_qb)
    kb = rms_norm(heads(o + WIDTH_B, WIDTH_B, H_B), g_kb)
    vb = heads(o + 2 * WIDTH_B, WIDTH_B, H_B)
    f_logit = p[..., o + 3 * WIDTH_B:]
    logf = jax.nn.log_sigmoid(f_logit.astype(jnp.float32) + b_f.astype(jnp.float32))
    return qa, ka, va, qb, kb, vb, logf


def rel_bias_lookup(table, tq, ts):
    rel = jnp.clip(tq[:, None] - ts[None, :], -REL_CLIP, REL_CLIP) + REL_CLIP
    return table.astype(jnp.float32)[:, rel][None]


def chunk_band_attn_prompt(qa, ka, va, table):
    b, t, h, d = qa.shape
    pad = jnp.zeros((b, BAND_PAST, h, d), ka.dtype)
    kp = jnp.concatenate([pad, ka], axis=1)
    vp = jnp.concatenate([pad.astype(va.dtype), va], axis=1)
    def one_chunk(c):
        start = c * CHUNK
        q_c = lax.dynamic_slice_in_dim(qa, start, CHUNK, 1)
        k_c = lax.dynamic_slice_in_dim(kp, start, BAND, 1)
        v_c = lax.dynamic_slice_in_dim(vp, start, BAND, 1)
        tq = start + jnp.arange(CHUNK)
        ts = start - BAND_PAST + jnp.arange(BAND)
        bias = rel_bias_lookup(table, tq, ts)
        mask = (ts >= 0)[None, None, None, :]
        return attend(q_c, k_c, v_c, bias, mask)
    out = lax.map(one_chunk, jnp.arange(t // CHUNK))
    return jnp.moveaxis(out, 0, 1).reshape(b, t, h * d)


def chunk_band_attn_sample(qa, ka, va, cache_k, cache_v, table):
    b, s, h, d = qa.shape
    la = cache_k.shape[1]
    k = jnp.concatenate([cache_k, ka], axis=1)
    v = jnp.concatenate([cache_v, va], axis=1)
    tq = la + jnp.arange(s)
    ts = jnp.arange(la + s)
    bias = rel_bias_lookup(table, tq, ts)
    return attend(qa, k, v, bias, None).reshape(b, s, h * d)


def forget_attn_prompt(qb, kb, vb, logf):
    b, t, h, d = qb.shape
    cum_h = jnp.cumsum(logf, axis=1).transpose(0, 2, 1)
    ts = jnp.arange(t)
    def one_block(i):
        start = i * Q_BLOCK
        q_i = lax.dynamic_slice_in_dim(qb, start, Q_BLOCK, 1)
        c_i = lax.dynamic_slice_in_dim(cum_h, start, Q_BLOCK, 2)
        bias = c_i[..., :, None] - cum_h[..., None, :]
        tq = start + jnp.arange(Q_BLOCK)
        mask = (tq[:, None] >= ts[None, :])[None, None]
        return attend(q_i, kb, vb, bias, mask)
    out = lax.map(one_block, jnp.arange(t // Q_BLOCK))
    return jnp.moveaxis(out, 0, 1).reshape(b, t, h * d)


def forget_attn_sample(qb, kb, vb, logf, cache_k, cache_v, cache_logf):
    b, s, h, d = qb.shape
    p = cache_k.shape[1]
    k = jnp.concatenate([cache_k, kb], axis=1)
    v = jnp.concatenate([cache_v, vb], axis=1)
    all_logf = jnp.concatenate([cache_logf.astype(jnp.float32), logf], axis=1)
    cum_h = jnp.cumsum(all_logf, axis=1).transpose(0, 2, 1)
    bias = cum_h[..., p:, None] - cum_h[..., None, :]
    tq = p + jnp.arange(s)
    ts = jnp.arange(p + s)
    mask = (tq[:, None] >= ts[None, :])[None, None]
    return attend(qb, k, v, bias, mask).reshape(b, s, h * d)


def mem_kv(mem, g_mem, w_mkv, g_mk):
    b, n, _ = mem.shape
    kv = rms_norm(mem, g_mem) @ w_mkv
    k = rms_norm(kv[..., :D_MODEL].reshape(b, n, H_M, HEAD_DIM_M), g_mk)
    v = kv[..., D_MODEL:].reshape(b, n, H_M, HEAD_DIM_M)
    return k, v


def mem_attend(h, mk, mv, w_mq, g_mq, w_mo):
    b, t, _ = h.shape
    q = rms_norm((h @ w_mq).reshape(b, t, H_M, HEAD_DIM_M), g_mq)
    return attend(q, mk, mv, None, None).reshape(b, t, D_MODEL) @ w_mo


def conv_ffn(h, prev, w_up, w_conv, b_conv, w_down):
    gu = h @ w_up
    gate, val = gu[..., :D_FF], gu[..., D_FF:]
    t = gate.shape[1]
    gp = jnp.concatenate([prev.astype(gate.dtype), gate], axis=1)
    conv = sum(gp[:, j:j + t] * w_conv[j] for j in range(CONV_W)) + b_conv
    y = (jax.nn.silu(conv) * val) @ w_down
    return y, gp[:, -(CONV_W - 1):]


def setup_inputs(seed: int = 0) -> dict:
    key = jax.random.key(seed)
    ks = jax.random.split(key, 32)
    f32 = jnp.float32
    L = DEPTH
    la = min(BAND_PAST, PAST_LEN)
    def nrm(k, shape, scale=1.0):
        return jax.random.normal(k, shape, f32) * scale
    def gain(k, shape):
        return 1.0 + 0.1 * nrm(k, shape)
    return {
        'x_prompt': nrm(ks[0], (BATCH, SEQ, D_MODEL)),
        'x_sample': nrm(ks[1], (DEC_BATCH, DEC_SEQ, D_MODEL)),
        'cache_a_k': nrm(ks[2], (L, DEC_BATCH, la, H_A, HEAD_DIM)),
        'cache_a_v': nrm(ks[3], (L, DEC_BATCH, la, H_A, HEAD_DIM)),
        'cache_b_k': nrm(ks[4], (L, DEC_BATCH, PAST_LEN, H_B, HEAD_DIM)),
        'cache_b_v': nrm(ks[5], (L, DEC_BATCH, PAST_LEN, H_B, HEAD_DIM)),
        'cache_b_logf': jax.nn.log_sigmoid(3.0 + nrm(ks[6], (L, DEC_BATCH, PAST_LEN, H_B))),
        'cache_mem_k': nrm(ks[7], (L, DEC_BATCH, N_MEM, H_M, HEAD_DIM_M)),
        'cache_mem_v': nrm(ks[8], (L, DEC_BATCH, N_MEM, H_M, HEAD_DIM_M)),
        'state_conv': nrm(ks[9], (L, DEC_BATCH, CONV_W - 1, D_FF)),
        'mem_prompt': nrm(ks[10], (BATCH, N_MEM, D_MODEL)),
        'w_in': nrm(ks[11], (L, D_MODEL, IN_COLS), D_MODEL ** -0.5),
        'b_f': 3.0 + 0.1 * nrm(ks[12], (L, H_B)),
        'g_qa': gain(ks[13], (L, HEAD_DIM)),
        'g_ka': gain(ks[14], (L, HEAD_DIM)),
        'rel_bias': nrm(ks[15], (L, H_A, 2 * REL_CLIP + 1), 0.1),
        'g_qb': gain(ks[16], (L, HEAD_DIM)),
        'g_kb': gain(ks[17], (L, HEAD_DIM)),
        'w_o': nrm(ks[18], (L, MIX_WIDTH, D_MODEL), MIX_WIDTH ** -0.5),
        'g_norm1': gain(ks[19], (L, D_MODEL)),
        'g_norm2': gain(ks[20], (L, D_MODEL)),
        'g_mem': gain(ks[21], (L, D_MODEL)),
        'w_mq': nrm(ks[22], (L, D_MODEL, D_MODEL), D_MODEL ** -0.5),
        'w_mkv': nrm(ks[23], (L, D_MODEL, 2 * D_MODEL), D_MODEL ** -0.5),
        'g_mq': gain(ks[24], (L, HEAD_DIM_M)),
        'g_mk': gain(ks[25], (L, HEAD_DIM_M)),
        'w_mo': nrm(ks[26], (L, D_MODEL, D_MODEL), D_MODEL ** -0.5),
        'g_norm3': gain(ks[27], (L, D_MODEL)),
        'w_up': nrm(ks[28], (L, D_MODEL, 2 * D_FF), D_MODEL ** -0.5),
        'w_conv': nrm(ks[29], (L, CONV_W, D_FF), CONV_W ** -0.5),
        'b_conv': nrm(ks[30], (L, D_FF), 0.02),
        'w_down': nrm(ks[31], (L, D_FF, D_MODEL), D_FF ** -0.5),
    }


def reference(x_prompt, x_sample, cache_a_k, cache_a_v, cache_b_k, cache_b_v, cache_b_logf,
              cache_mem_k, cache_mem_v, state_conv, mem_prompt,
              w_in, b_f, g_qa, g_ka, rel_bias, g_qb, g_kb, w_o, g_norm1, g_norm2, g_mem,
              w_mq, w_mkv, g_mq, g_mk, w_mo, g_norm3, w_up, w_conv, b_conv, w_down):
    xp, xs = x_prompt, x_sample
    bp, tp = xp.shape[0], xp.shape[1]
    la_p = min(BAND_PAST, tp)
    ak_p, av_p, bk_p, bv_p, bl_p, mk_p, mv_p, cv_p = [], [], [], [], [], [], [], []
    ak_s, av_s, bk_s, bv_s, bl_s, cv_s = [], [], [], [], [], []
    for l in range(DEPTH):
        h = rms_norm(xp, g_norm1[l])
        qa, ka, va, qb, kb, vb, logf = in_proj(h, w_in[l], b_f[l], g_qa[l], g_ka[l], g_qb[l], g_kb[l])
        oa = chunk_band_attn_prompt(qa, ka, va, rel_bias[l])
        ob = forget_attn_prompt(qb, kb, vb, logf)
        xp = xp + jnp.concatenate([oa, ob], axis=-1) @ w_o[l]
        mk, mv = mem_kv(mem_prompt, g_mem[l], w_mkv[l], g_mk[l])
        xp = xp + mem_attend(rms_norm(xp, g_norm2[l]), mk, mv, w_mq[l], g_mq[l], w_mo[l])
        zeros_prev = jnp.zeros((bp, CONV_W - 1, D_FF), xp.dtype)
        f, conv_p = conv_ffn(rms_norm(xp, g_norm3[l]), zeros_prev, w_up[l], w_conv[l], b_conv[l], w_down[l])
        xp = xp + f
        ak_p.append(ka[:, tp - la_p:]); av_p.append(va[:, tp - la_p:])
        bk_p.append(kb); bv_p.append(vb); bl_p.append(logf)
        mk_p.append(mk); mv_p.append(mv); cv_p.append(conv_p)

        h = rms_norm(xs, g_norm1[l])
        qa, ka, va, qb, kb, vb, logf = in_proj(h, w_in[l], b_f[l], g_qa[l], g_ka[l], g_qb[l], g_kb[l])
        oa = chunk_band_attn_sample(qa, ka, va, cache_a_k[l], cache_a_v[l], rel_bias[l])
        ob = forget_attn_sample(qb, kb, vb, logf, cache_b_k[l], cache_b_v[l], cache_b_logf[l])
        xs = xs + jnp.concatenate([oa, ob], axis=-1) @ w_o[l]
        xs = xs + mem_attend(rms_norm(xs, g_norm2[l]), cache_mem_k[l], cache_mem_v[l], w_mq[l], g_mq[l], w_mo[l])
        f, conv_s = conv_ffn(rms_norm(xs, g_norm3[l]), state_conv[l], w_up[l], w_conv[l], b_conv[l], w_down[l])
        xs = xs + f
        ak_s.append(ka); av_s.append(va)
        bk_s.append(kb); bv_s.append(vb); bl_s.append(logf)
        cv_s.append(conv_s)

    st = lambda lst: jnp.stack(lst, axis=0)
    return (xp, xs,
            st(ak_p), st(av_p), st(bk_p), st(bv_p), st(bl_p), st(mk_p), st(mv_p), st(cv_p),
            st(ak_s), st(av_s), st(bk_s), st(bv_s), st(bl_s), st(cv_s))
```

```python
import functools

import jax
import jax.numpy as jnp
from jax import lax
from jax.experimental import pallas as pl
from jax.experimental.pallas import tpu as pltpu

F32 = jnp.float32
BF16 = jnp.bfloat16
EPS = 1e-6
NEG_INF = float("-inf")

HEAD_DIM = 64
N_HEADS = 8
WIDTH = N_HEADS * HEAD_DIM
CHUNK = 64
N_PREV_CHUNKS = 8
BAND_PAST = N_PREV_CHUNKS * CHUNK
REL_CLIP = 128
N_MEM_HEADS = 4
CONV_W = 3

ROW_TILE = 512
BAND_Q = 256
BAND_WIN = BAND_PAST + BAND_Q
FOX_BLOCK = 512
FOX_CACHE_BLOCK = 1024
MXU_COLS = 256
LANES = 128
SUBLANES = 8
VMEM_LIMIT = 56 * 1024 * 1024


def _dot(a, b):
    return jnp.dot(a, b, preferred_element_type=F32)


def _dot_nt(a, b):
    return lax.dot_general(a, b, (((1,), (1,)), ((), ())), preferred_element_type=F32)


def _rms(x, g):
    return x * lax.rsqrt(jnp.mean(x * x, axis=-1, keepdims=True) + EPS) * g


def _split3(x):
    hi = x.astype(BF16)
    r1 = x - hi.astype(F32)
    mid = r1.astype(BF16)
    lo = (r1 - mid.astype(F32)).astype(BF16)
    return hi, mid, lo


def _params(*sem):
    return pltpu.CompilerParams(dimension_semantics=sem, vmem_limit_bytes=VMEM_LIMIT)


def _inproj_kernel(x_ref, g1_ref, w_ref, wf_ref, bf_ref, gains_ref, seg_ref,
                   qa_ref, ka_ref, va_ref, qb_ref, kb_ref, vb_ref,
                   kaf_ref, vaf_ref, kbf_ref, vbf_ref, logf_ref, *, tail_period):
    h = _rms(x_ref[...], g1_ref[...]).astype(BF16)
    seg = seg_ref[...]

    def proj(g):
        return _dot(h, w_ref[:, g * WIDTH:(g + 1) * WIDTH])

    def head_norm(p, row):
        p2 = (p * p).astype(BF16)
        ss = jnp.concatenate([_dot(p2[:, :MXU_COLS], seg), _dot(p2[:, MXU_COLS:], seg)], axis=1)
        return p * lax.rsqrt(ss * (1.0 / HEAD_DIM) + EPS) * gains_ref[row:row + 1, :]

    is_tail = (pl.program_id(0) % tail_period) == tail_period - 1
    scale = HEAD_DIM ** -0.5

    qa_ref[...] = (head_norm(proj(0), 0) * scale).astype(BF16)
    ka = head_norm(proj(1), 1)
    ka_ref[...] = ka.astype(BF16)
    va = proj(2)
    va_ref[...] = va.astype(BF16)

    @pl.when(is_tail)
    def _():
        kaf_ref[0] = ka
        vaf_ref[0] = va

    qb_ref[...] = (head_norm(proj(3), 2) * scale).astype(BF16)
    kb = head_norm(proj(4), 3)
    kb_ref[...] = kb.astype(BF16)
    kbf_ref[...] = kb
    vb = proj(5)
    vb_ref[...] = vb.astype(BF16)
    vbf_ref[...] = vb

    z = _dot(h, wf_ref[...]) + bf_ref[...]
    logf = jnp.minimum(z, 0.0) - jnp.log1p(jnp.exp(-jnp.abs(z)))
    logf_ref[...] = logf[:, :N_HEADS]


def _in_proj(x2d, g1, w_main, w_f, b_f, gains, seg, *, tail_period):
    n, d = x2d.shape
    tm = ROW_TILE
    assert n % tm == 0 and (n // tm) % tail_period == 0
    n_tail = n // tm // tail_period
    row_bf = jax.ShapeDtypeStruct((n, WIDTH), BF16)
    row_f = jax.ShapeDtypeStruct((n, WIDTH), F32)
    tail_f = jax.ShapeDtypeStruct((n_tail, tm, WIDTH), F32)
    rows = pl.BlockSpec((tm, WIDTH), lambda i: (i, 0))
    tail = pl.BlockSpec((1, tm, WIDTH), lambda i: (i // tail_period, 0, 0))
    full = lambda a: pl.BlockSpec(a.shape, lambda i: (0,) * a.ndim)
    return pl.pallas_call(
        functools.partial(_inproj_kernel, tail_period=tail_period),
        grid=(n // tm,),
        in_specs=[pl.BlockSpec((tm, d), lambda i: (i, 0)), full(g1), full(w_main), full(w_f),
                  full(b_f), full(gains), full(seg)],
        out_specs=[rows] * 6 + [tail, tail, rows, rows, pl.BlockSpec((tm, N_HEADS), lambda i: (i, 0))],
        out_shape=[row_bf] * 6 + [tail_f, tail_f, row_f, row_f, jax.ShapeDtypeStruct((n, N_HEADS), F32)],
        compiler_params=_params("arbitrary"),
        name="in_proj",
    )(x2d, g1, w_main, w_f, b_f, gains, seg)


def _memkv_kernel(mem_ref, g_ref, w_ref, gk_ref, kf_ref, vf_ref, kb_ref, vb_ref):
    d = mem_ref.shape[-1]
    hd = d // N_MEM_HEADS
    h = _rms(mem_ref[0], g_ref[...]).astype(BF16)
    kv = _dot(h, w_ref[...])
    for hh in range(N_MEM_HEADS):
        k = _rms(kv[:, hh * hd:(hh + 1) * hd], gk_ref[...])
        kf_ref[0, :, hh * hd:(hh + 1) * hd] = k
        kb_ref[0, :, hh * hd:(hh + 1) * hd] = k.astype(BF16)
    v = kv[:, d:]
    vf_ref[0] = v
    vb_ref[0] = v.astype(BF16)


def _mem_kv(mem, g_mem, w_mkv, g_mk):
    b, n, d = mem.shape
    blk = pl.BlockSpec((1, n, d), lambda i: (i, 0, 0))
    full = lambda a: pl.BlockSpec(a.shape, lambda i: (0,) * a.ndim)
    f = jax.ShapeDtypeStruct((b, n, d), F32)
    h = jax.ShapeDtypeStruct((b, n, d), BF16)
    return pl.pallas_call(
        _memkv_kernel, grid=(b,),
        in_specs=[blk, full(g_mem), full(w_mkv), full(g_mk)],
        out_specs=[blk] * 4, out_shape=[f, f, h, h],
        compiler_params=_params("arbitrary"), name="mem_kv",
    )(mem, g_mem, w_mkv, g_mk)


def _cumsum_kernel(x_ref, o_ref):
    r, t = x_ref.shape
    k = lax.broadcasted_iota(jnp.int32, (LANES, LANES), 0)
    j = lax.broadcasted_iota(jnp.int32, (LANES, LANES), 1)
    tri = jnp.where(k <= j, 1.0, 0.0).astype(BF16)
    carry = jnp.zeros((r, 1), F32)
    for blk in range(t // LANES):
        hi, mid, lo = _split3(x_ref[:, blk * LANES:(blk + 1) * LANES])
        cs = _dot(hi, tri) + _dot(mid, tri) + _dot(lo, tri)
        o_ref[:, blk * LANES:(blk + 1) * LANES] = cs + carry
        carry = carry + cs[:, LANES - 1:LANES]


def _cumsum_lanes(x):
    assert x.shape[1] % LANES == 0
    return pl.pallas_call(_cumsum_kernel, out_shape=jax.ShapeDtypeStruct(x.shape, F32), name="cumsum")(x)


def _band_bias_kernel(tbl_ref, o_ref):
    nrel = tbl_ref.shape[1]
    qblk, win = o_ref.shape[1:]
    wrap = qblk + win
    jp = lax.broadcasted_iota(jnp.int32, (nrel, wrap), 1)
    r = lax.broadcasted_iota(jnp.int32, (nrel, wrap), 0)
    d = jnp.where(jp < win, jp, jp - wrap)
    rel = jnp.clip(BAND_PAST - d, -REL_CLIP, REL_CLIP) + REL_CLIP
    onehot = jnp.where(rel == r, 1.0, 0.0).astype(BF16)
    hi, mid, lo = _split3(tbl_ref[...])
    base = _dot(hi, onehot) + _dot(mid, onehot) + _dot(lo, onehot)
    qc = lax.broadcasted_iota(jnp.int32, (qblk, win), 0) // CHUNK
    kc = lax.broadcasted_iota(jnp.int32, (qblk, win), 1) // CHUNK
    inband = (kc >= qc) & (kc <= qc + N_PREV_CHUNKS)
    for h in range(N_HEADS):
        row = jnp.broadcast_to(base[h:h + 1, :], (qblk, wrap))
        toeplitz = pltpu.roll(row, 0, 1, stride=1, stride_axis=0)
        o_ref[h] = jnp.where(inband, toeplitz[:, :win], NEG_INF)


def _band_bias(rel_table):
    nrel = 2 * REL_CLIP + 1
    pad = -nrel % LANES
    tbl = jnp.pad(rel_table.astype(F32), ((0, 0), (0, pad)))
    return pl.pallas_call(
        _band_bias_kernel, out_shape=jax.ShapeDtypeStruct((N_HEADS, BAND_Q, BAND_WIN), F32),
        name="band_bias")(tbl)


def _softmax_pv(s, vs):
    m = jnp.max(s, axis=-1, keepdims=True)
    e = jnp.exp(s - m)
    l = jnp.sum(e, axis=-1, keepdims=True)
    eb = e.astype(BF16)
    o, lo = None, 0
    for v in vs:
        part = _dot(eb[:, lo:lo + v.shape[0]], v)
        o = part if o is None else o + part
        lo += v.shape[0]
    return o * (1.0 / l)


def _band_prompt_kernel(q_ref, k0_ref, k1_ref, k2_ref, v0_ref, v1_ref, v2_ref, bias_ref, o_ref):
    i = pl.program_id(1)
    for h in range(N_HEADS):
        hs = slice(h * HEAD_DIM, (h + 1) * HEAD_DIM)
        q = q_ref[0, :, hs]
        parts = []
        for p, k_ref in enumerate((k0_ref, k1_ref, k2_ref)):
            s = _dot_nt(q, k_ref[0, :, hs])
            if p < 2:
                s = jnp.where(i + p >= 2, s, NEG_INF)
            parts.append(s)
        s = jnp.concatenate(parts, axis=1) + bias_ref[h]
        o = _softmax_pv(s, [v0_ref[0, :, hs], v1_ref[0, :, hs], v2_ref[0, :, hs]])
        o_ref[0, :, hs] = o.astype(BF16)


def _band_prompt(qa, ka, va, bias):
    b, t, _ = qa.shape
    tq = BAND_Q
    assert t % tq == 0
    blk = lambda back: pl.BlockSpec((1, tq, WIDTH), lambda bi, i: (bi, jnp.maximum(i - back, 0), 0))
    return pl.pallas_call(
        _band_prompt_kernel, grid=(b, t // tq),
        in_specs=[blk(0), blk(2), blk(1), blk(0), blk(2), blk(1), blk(0),
                  pl.BlockSpec(bias.shape, lambda bi, i: (0, 0, 0))],
        out_specs=blk(0), out_shape=jax.ShapeDtypeStruct((b, t, WIDTH), BF16),
        compiler_params=_params("arbitrary", "arbitrary"), name="band_prompt",
    )(qa, ka, ka, ka, va, va, va, bias)


def _band_sample_kernel(q_ref, kc_ref, vc_ref, kn_ref, vn_ref, bias_ref, o_ref):
    for h in range(N_HEADS):
        hs = slice(h * HEAD_DIM, (h + 1) * HEAD_DIM)
        q = q_ref[0, :, hs]
        s = jnp.concatenate([_dot_nt(q, kc_ref[0, :, hs].astype(BF16)), _dot_nt(q, kn_ref[0, :, hs])], axis=1)
        s = s + bias_ref[h, :CHUNK, :BAND_PAST + CHUNK]
        o = _softmax_pv(s, [vc_ref[0, :, hs].astype(BF16), vn_ref[0, :, hs]])
        o_ref[0, :, hs] = o.astype(BF16)


def _band_sample(qa, ka, va, cache_k, cache_v, bias):
    b, s, _ = qa.shape
    la = cache_k.shape[1]
    assert s == CHUNK and la == BAND_PAST
    new = pl.BlockSpec((1, s, WIDTH), lambda bi: (bi, 0, 0))
    old = pl.BlockSpec((1, la, WIDTH), lambda bi: (bi, 0, 0))
    return pl.pallas_call(
        _band_sample_kernel, grid=(b,),
        in_specs=[new, old, old, new, new, pl.BlockSpec(bias.shape, lambda bi: (0, 0, 0))],
        out_specs=new, out_shape=jax.ShapeDtypeStruct((b, s, WIDTH), BF16),
        compiler_params=_params("arbitrary"), name="band_sample",
    )(qa, cache_k, cache_v, ka, va, bias)


def _fox_init(m_sc, l_sc, acc_sc):
    m_sc[...] = jnp.full(m_sc.shape, NEG_INF, F32)
    l_sc[...] = jnp.zeros(l_sc.shape, F32)
    acc_sc[...] = jnp.zeros(acc_sc.shape, F32)


def _fox_update(h, q, k, v, c_q, c_k, causal, m_sc, l_sc, acc_sc):
    s = _dot_nt(q, k) + (c_q - c_k)
    if causal:
        row = lax.broadcasted_iota(jnp.int32, s.shape, 0)
        col = lax.broadcasted_iota(jnp.int32, s.shape, 1)
        s = jnp.where(row >= col, s, NEG_INF)
    m_prev = m_sc[h]
    m_new = jnp.maximum(m_prev, jnp.max(s, axis=-1, keepdims=True))
    alpha = jnp.exp(m_prev - m_new)
    p = jnp.exp(s - m_new)
    l_sc[h] = alpha * l_sc[h] + jnp.sum(p, axis=-1, keepdims=True)
    acc_sc[h] = alpha * acc_sc[h] + _dot(p.astype(BF16), v)
    m_sc[h] = m_new


def _fox_finish(o_ref, l_sc, acc_sc):
    for h in range(N_HEADS):
        o_ref[0, :, h * HEAD_DIM:(h + 1) * HEAD_DIM] = (acc_sc[h] * (1.0 / l_sc[h])).astype(BF16)


def _fox_prompt_kernel(qi_ref, kj_ref, q_ref, k_ref, v_ref, crow_ref, ccol_ref, o_ref, m_sc, l_sc, acc_sc):
    p = pl.program_id(1)
    qi, kj = qi_ref[p], kj_ref[p]

    @pl.when(kj == 0)
    def _():
        _fox_init(m_sc, l_sc, acc_sc)

    def step(causal):
        for h in range(N_HEADS):
            hs = slice(h * HEAD_DIM, (h + 1) * HEAD_DIM)
            _fox_update(h, q_ref[0, :, hs], k_ref[0, :, hs], v_ref[0, :, hs],
                        ccol_ref[0, :, h:h + 1], crow_ref[0, h:h + 1, :], causal, m_sc, l_sc, acc_sc)

    @pl.when(kj < qi)
    def _():
        step(False)

    @pl.when(kj == qi)
    def _():
        step(True)
        _fox_finish(o_ref, l_sc, acc_sc)


def _fox_prompt(qb, kb, vb, crow, ccol):
    b, t, _ = qb.shape
    tq = FOX_BLOCK
    assert t % tq == 0
    pairs = [(i, j) for i in range(t // tq) for j in range(i + 1)]
    qi = jnp.array([p[0] for p in pairs], jnp.int32)
    kj = jnp.array([p[1] for p in pairs], jnp.int32)
    qblk = pl.BlockSpec((1, tq, WIDTH), lambda bi, p, qi, kj: (bi, qi[p], 0))
    kblk = pl.BlockSpec((1, tq, WIDTH), lambda bi, p, qi, kj: (bi, kj[p], 0))
    grid_spec = pltpu.PrefetchScalarGridSpec(
        num_scalar_prefetch=2, grid=(b, len(pairs)),
        in_specs=[qblk, kblk, kblk,
                  pl.BlockSpec((1, N_HEADS, tq), lambda bi, p, qi, kj: (bi, 0, kj[p])),
                  pl.BlockSpec((1, tq, N_HEADS), lambda bi, p, qi, kj: (bi, qi[p], 0))],
        out_specs=qblk,
        scratch_shapes=[pltpu.VMEM((N_HEADS, tq, 1), F32), pltpu.VMEM((N_HEADS, tq, 1), F32),
                        pltpu.VMEM((N_HEADS, tq, HEAD_DIM), F32)])
    return pl.pallas_call(
        _fox_prompt_kernel, grid_spec=grid_spec, out_shape=jax.ShapeDtypeStruct((b, t, WIDTH), BF16),
        compiler_params=_params("arbitrary", "arbitrary"), name="fox_prompt",
    )(qi, kj, qb, kb, vb, crow, ccol)


def _fox_sample_kernel(q_ref, kc_ref, vc_ref, kn_ref, vn_ref, crow_ref, crow_new_ref, ccol_ref, o_ref,
                       m_sc, l_sc, acc_sc):
    j = pl.program_id(1)
    last = pl.num_programs(1) - 1

    @pl.when(j == 0)
    def _():
        _fox_init(m_sc, l_sc, acc_sc)

    @pl.when(j < last)
    def _():
        for h in range(N_HEADS):
            hs = slice(h * HEAD_DIM, (h + 1) * HEAD_DIM)
            _fox_update(h, q_ref[0, :, hs], kc_ref[0, :, hs].astype(BF16), vc_ref[0, :, hs].astype(BF16),
                        ccol_ref[0, :, h:h + 1], crow_ref[0, h:h + 1, :], False, m_sc, l_sc, acc_sc)

    @pl.when(j == last)
    def _():
        for h in range(N_HEADS):
            hs = slice(h * HEAD_DIM, (h + 1) * HEAD_DIM)
            _fox_update(h, q_ref[0, :, hs], kn_ref[0, :, hs], vn_ref[0, :, hs],
                        ccol_ref[0, :, h:h + 1], crow_new_ref[0, h:h + 1, :], True, m_sc, l_sc, acc_sc)
        _fox_finish(o_ref, l_sc, acc_sc)


def _fox_sample(qb, kb, vb, cache_k, cache_v, crow, crow_new, ccol):
    b, s, _ = qb.shape
    past = cache_k.shape[1]
    tk = FOX_CACHE_BLOCK
    assert past % tk == 0
    nk = past // tk
    new = pl.BlockSpec((1, s, WIDTH), lambda bi, j: (bi, 0, 0))
    old = pl.BlockSpec((1, tk, WIDTH), lambda bi, j: (bi, jnp.minimum(j, nk - 1), 0))
    return pl.pallas_call(
        _fox_sample_kernel, grid=(b, nk + 1),
        in_specs=[new, old, old, new, new,
                  pl.BlockSpec((1, N_HEADS, tk), lambda bi, j: (bi, 0, jnp.minimum(j, nk - 1))),
                  pl.BlockSpec((1, N_HEADS, s), lambda bi, j: (bi, 0, 0)),
                  pl.BlockSpec((1, s, N_HEADS), lambda bi, j: (bi, 0, 0))],
        out_specs=new, out_shape=jax.ShapeDtypeStruct((b, s, WIDTH), BF16),
        scratch_shapes=[pltpu.VMEM((N_HEADS, s, 1), F32), pltpu.VMEM((N_HEADS, s, 1), F32),
                        pltpu.VMEM((N_HEADS, s, HEAD_DIM), F32)],
        compiler_params=_params("arbitrary", "arbitrary"), name="fox_sample",
    )(qb, cache_k, cache_v, kb, vb, crow, crow_new, ccol)


def _mid_kernel(x_ref, oa_ref, ob_ref, wo_ref, g2_ref, wq_ref, gq_ref, mk_ref, mv_ref, wmo_ref, o_ref,
                om_sc, *, seg):
    tm, d = x_ref.shape
    hd = d // N_MEM_HEADS
    x1 = x_ref[...] + _dot(oa_ref[...], wo_ref[:WIDTH, :]) + _dot(ob_ref[...], wo_ref[WIDTH:, :])
    h2 = _rms(x1, g2_ref[...]).astype(BF16)
    qm = _dot(h2, wq_ref[...])
    scale = hd ** -0.5
    for hh in range(N_MEM_HEADS):
        cs = slice(hh * hd, (hh + 1) * hd)
        q = (_rms(qm[:, cs], gq_ref[...]) * scale).astype(BF16)
        for sg in range(tm // seg):
            rs = slice(sg * seg, (sg + 1) * seg)
            s = _dot_nt(q[rs], mk_ref[sg, :, cs])
            om_sc[rs, cs] = _softmax_pv(s, [mv_ref[sg, :, cs]]).astype(BF16)
    o_ref[...] = x1 + _dot(om_sc[...], wmo_ref[...])


def _mid(x2d, oa, ob, w_o, g2, w_mq, g_mq, mk, mv, w_mo, *, seq):
    n, d = x2d.shape
    tm = ROW_TILE
    seg = min(seq, tm)
    tiles_per_batch = seq // seg
    nb = tm // seg
    nmem = mk.shape[1]
    assert n % tm == 0 and seq % seg == 0 and tm % seg == 0
    rows = lambda w: pl.BlockSpec((tm, w), lambda i: (i, 0))
    full = lambda a: pl.BlockSpec(a.shape, lambda i: (0,) * a.ndim)
    mem = pl.BlockSpec((nb, nmem, d), lambda i: (i // tiles_per_batch, 0, 0))
    return pl.pallas_call(
        functools.partial(_mid_kernel, seg=seg), grid=(n // tm,),
        in_specs=[rows(d), rows(WIDTH), rows(WIDTH), full(w_o), full(g2), full(w_mq), full(g_mq), mem, mem,
                  full(w_mo)],
        out_specs=rows(d), out_shape=jax.ShapeDtypeStruct((n, d), F32),
        scratch_shapes=[pltpu.VMEM((tm, d), BF16)],
        compiler_params=_params("arbitrary"), name="mid",
    )(x2d, oa, ob, w_o, g2, w_mq, g_mq, mk, mv, w_mo)


def _ffn_kernel(x_ref, g3_ref, wup_ref, wc_ref, bc_ref, wd_ref, st_ref, o_ref, last_ref, carry_sc,
                *, seg, tiles_per_batch, ff_chunk):
    tm, d = x_ref.shape
    dff = wd_ref.shape[0]
    i = pl.program_id(0)
    x = x_ref[...]
    h3 = _rms(x, g3_ref[...]).astype(BF16)
    o_ref[...] = x
    if tiles_per_batch > 1:
        @pl.when(i == 0)
        def _():
            carry_sc[...] = jnp.zeros(carry_sc.shape, F32)
    rowid = lax.broadcasted_iota(jnp.int32, (SUBLANES, 1), 0)
    first_tile = (i % tiles_per_batch) == 0
    for c in range(dff // ff_chunk):
        cs = slice(c * ff_chunk, (c + 1) * ff_chunk)
        gate = _dot(h3, wup_ref[:, cs])
        val = _dot(h3, wup_ref[:, dff + c * ff_chunk:dff + (c + 1) * ff_chunk])
        w0, w1, w2 = wc_ref[0:1, cs], wc_ref[1:2, cs], wc_ref[2:3, cs]
        ys = []
        for sg in range(tm // seg):
            g = gate[sg * seg:(sg + 1) * seg]
            if tiles_per_batch == 1:
                prev = st_ref[sg, :, cs]
            else:
                prev = jnp.where(first_tile, st_ref[0, :, cs], carry_sc[SUBLANES - 2:, cs])
            p2, p1 = prev[0:1], prev[1:2]
            s1 = pltpu.roll(g, 1, 0)
            s2 = pltpu.roll(g, 2, 0)
            top1 = jnp.where(rowid == 0, p1, s1[:SUBLANES])
            top2 = jnp.where(rowid == 0, p2, jnp.where(rowid == 1, p1, s2[:SUBLANES]))
            s1 = jnp.concatenate([top1, s1[SUBLANES:]], axis=0)
            s2 = jnp.concatenate([top2, s2[SUBLANES:]], axis=0)
            conv = s2 * w0 + s1 * w1 + g * w2 + bc_ref[:, cs]
            ys.append(conv * jax.nn.sigmoid(conv) * val[sg * seg:(sg + 1) * seg])
            last_ref[sg, :, cs] = g[seg - SUBLANES:]
        if tiles_per_batch > 1:
            carry_sc[:, cs] = gate[tm - SUBLANES:]
        y = ys[0] if len(ys) == 1 else jnp.concatenate(ys, axis=0)
        o_ref[...] += _dot(y.astype(BF16), wd_ref[cs, :])


def _conv_ffn(x2d, g3, w_up, w_conv, b_conv, w_down, state, *, seq, ff_chunk):
    n, d = x2d.shape
    dff = w_down.shape[0]
    tm = ROW_TILE
    seg = min(seq, tm)
    tiles_per_batch = seq // seg
    nb = tm // seg
    assert n % tm == 0 and seq % seg == 0 and tm % seg == 0 and dff % ff_chunk == 0 and seg >= SUBLANES
    rows = pl.BlockSpec((tm, d), lambda i: (i, 0))
    full = lambda a: pl.BlockSpec(a.shape, lambda i: (0,) * a.ndim)
    per_batch = lambda r: pl.BlockSpec((nb, r, dff), lambda i: (i // tiles_per_batch, 0, 0))
    return pl.pallas_call(
        functools.partial(_ffn_kernel, seg=seg, tiles_per_batch=tiles_per_batch, ff_chunk=ff_chunk),
        grid=(n // tm,),
        in_specs=[rows, full(g3), full(w_up), full(w_conv), full(b_conv), full(w_down), per_batch(CONV_W - 1)],
        out_specs=[rows, per_batch(SUBLANES)],
        out_shape=[jax.ShapeDtypeStruct((n, d), F32), jax.ShapeDtypeStruct((n // seq, SUBLANES, dff), F32)],
        scratch_shapes=[pltpu.VMEM((SUBLANES, dff), F32)],
        compiler_params=_params("arbitrary"), name="conv_ffn",
    )(x2d, g3, w_up, w_conv, b_conv, w_down, state)


def _row(v):
    return v.astype(F32).reshape(1, -1)


def _layer_weights(l, w_in, b_f, g_qa, g_ka, rel_bias, g_qb, g_kb, w_o, g_norm1, g_norm2, g_mem, w_mq, w_mkv,
                   g_mq, g_mk, w_mo, g_norm3, w_up, w_conv, b_conv, w_down):
    n_main = 6 * WIDTH
    tile = lambda g: jnp.tile(g[l].astype(F32), N_HEADS)
    blk = jnp.arange(MXU_COLS) // HEAD_DIM
    return dict(
        w_main=w_in[l][:, :n_main].astype(BF16),
        w_f=jnp.pad(w_in[l][:, n_main:], ((0, 0), (0, LANES - N_HEADS))).astype(BF16),
        b_f=jnp.pad(_row(b_f[l]), ((0, 0), (0, LANES - N_HEADS))),
        gains=jnp.stack([tile(g_qa), tile(g_ka), tile(g_qb), tile(g_kb)]),
        seg=(blk[:, None] == blk[None, :]).astype(BF16),
        rel=rel_bias[l],
        w_o=w_o[l].astype(BF16), g1=_row(g_norm1[l]), g2=_row(g_norm2[l]), g3=_row(g_norm3[l]),
        g_mem=_row(g_mem[l]), w_mq=w_mq[l].astype(BF16), w_mkv=w_mkv[l].astype(BF16),
        g_mq=_row(g_mq[l]), g_mk=_row(g_mk[l]), w_mo=w_mo[l].astype(BF16),
        w_up=w_up[l].astype(BF16), w_conv=w_conv[l].astype(F32), b_conv=_row(b_conv[l]),
        w_down=w_down[l].astype(BF16))


def _post_attention(x2d, oa, ob, mk, mv, state, w, *, seq, ff_chunk):
    n, d = x2d.shape
    x2 = _mid(x2d, oa.reshape(n, WIDTH), ob.reshape(n, WIDTH), w["w_o"], w["g2"], w["w_mq"], w["g_mq"],
              mk, mv, w["w_mo"], seq=seq)
    y, last = _conv_ffn(x2, w["g3"], w["w_up"], w["w_conv"], w["b_conv"], w["w_down"], state,
                        seq=seq, ff_chunk=ff_chunk)
    return y, last[:, SUBLANES - (CONV_W - 1):]


def _prompt_layer(x, mem, w, bias):
    b, t, d = x.shape
    la = min(BAND_PAST, t)
    assert la == ROW_TILE and t % ROW_TILE == 0
    qa, ka, va, qb, kb, vb, ka_tail, va_tail, kb_f, vb_f, logf = _in_proj(
        x.reshape(b * t, d), w["g1"], w["w_main"], w["w_f"], w["b_f"], w["gains"], w["seg"],
        tail_period=t // ROW_TILE)
    r3 = lambda a: a.reshape(b, t, WIDTH)
    oa = _band_prompt(r3(qa), r3(ka), r3(va), bias)
    logf = logf.reshape(b, t, N_HEADS)
    crow = _cumsum_lanes(logf.transpose(0, 2, 1).reshape(b * N_HEADS, t)).reshape(b, N_HEADS, t)
    ob = _fox_prompt(r3(qb), r3(kb), r3(vb), crow, crow.transpose(0, 2, 1))
    mk_f, mv_f, mk, mv = _mem_kv(mem, w["g_mem"], w["w_mkv"], w["g_mk"])
    dff = w["w_down"].shape[0]
    y, conv = _post_attention(x.reshape(b * t, d), oa, ob, mk, mv, jnp.zeros((b, CONV_W - 1, dff), F32), w,
                              seq=t, ff_chunk=dff // 11)
    heads = lambda a, n: a.reshape(b, n, N_HEADS, HEAD_DIM)
    mem_heads = lambda a: a.reshape(b, mem.shape[1], N_MEM_HEADS, d // N_MEM_HEADS)
    return (y.reshape(b, t, d), heads(ka_tail, la), heads(va_tail, la), heads(kb_f, t), heads(vb_f, t), logf,
            mem_heads(mk_f), mem_heads(mv_f), conv)


def _sample_layer(x, cache_a_k, cache_a_v, cache_b_k, cache_b_v, cache_b_logf, cache_mem_k, cache_mem_v,
                  state_conv, w, bias):
    b, s, d = x.shape
    past = cache_b_k.shape[1]
    assert ROW_TILE % s == 0 and (b * s) % ROW_TILE == 0
    qa, ka, va, qb, kb, vb, ka_f, va_f, kb_f, vb_f, logf = _in_proj(
        x.reshape(b * s, d), w["g1"], w["w_main"], w["w_f"], w["b_f"], w["gains"], w["seg"], tail_period=1)
    r3 = lambda a: a.reshape(b, s, WIDTH)
    flat = lambda a: a.reshape(a.shape[0], a.shape[1], -1)
    oa = _band_sample(r3(qa), r3(ka), r3(va), flat(cache_a_k), flat(cache_a_v), bias)
    logf = logf.reshape(b, s, N_HEADS)
    total = past + s
    all_logf = jnp.concatenate([cache_b_logf.astype(F32), logf], axis=1).transpose(0, 2, 1)
    all_logf = jnp.pad(all_logf, ((0, 0), (0, 0), (0, -total % LANES)))
    cum = _cumsum_lanes(all_logf.reshape(b * N_HEADS, -1)).reshape(b, N_HEADS, -1)
    crow_new = cum[:, :, past:total]
    ob = _fox_sample(r3(qb), r3(kb), r3(vb), flat(cache_b_k), flat(cache_b_v), cum[:, :, :past], crow_new,
                     crow_new.transpose(0, 2, 1))
    dff = w["w_down"].shape[0]
    y, conv = _post_attention(x.reshape(b * s, d), oa, ob, flat(cache_mem_k).astype(BF16),
                              flat(cache_mem_v).astype(BF16), state_conv.astype(F32), w,
                              seq=s, ff_chunk=dff // 11)
    heads = lambda a: a.reshape(b, s, N_HEADS, HEAD_DIM)
    return y.reshape(b, s, d), heads(ka_f), heads(va_f), heads(kb_f), heads(vb_f), logf, conv


def kernel(x_prompt, x_sample, cache_a_k, cache_a_v, cache_b_k, cache_b_v, cache_b_logf, cache_mem_k, cache_mem_v, state_conv, mem_prompt, w_in, b_f, g_qa, g_ka, rel_bias, g_qb, g_kb, w_o, g_norm1, g_norm2, g_mem, w_mq, w_mkv, g_mq, g_mk, w_mo, g_norm3, w_up, w_conv, b_conv, w_down):
    depth = w_in.shape[0]
    xp, xs = x_prompt, x_sample
    outs_p, outs_s = [], []
    for l in range(depth):
        w = _layer_weights(l, w_in, b_f, g_qa, g_ka, rel_bias, g_qb, g_kb, w_o, g_norm1, g_norm2, g_mem, w_mq,
                           w_mkv, g_mq, g_mk, w_mo, g_norm3, w_up, w_conv, b_conv, w_down)
        bias = _band_bias(w["rel"])
        xp, *rest_p = _prompt_layer(xp, mem_prompt, w, bias)
        xs, *rest_s = _sample_layer(xs, cache_a_k[l], cache_a_v[l], cache_b_k[l], cache_b_v[l], cache_b_logf[l],
                                    cache_mem_k[l], cache_mem_v[l], state_conv[l], w, bias)
        outs_p.append(rest_p)
        outs_s.append(rest_s)
    stack = lambda outs: [jnp.stack(col, axis=0) for col in zip(*outs)]
    return (xp, xs, *stack(outs_p), *stack(outs_s))
```

```python
import functools

import jax
import jax.numpy as jnp
from jax import lax
from jax.experimental import pallas as pl
from jax.experimental.pallas import tpu as pltpu

F32 = jnp.float32
BF16 = jnp.bfloat16
EPS = 1e-6
NEG_INF = float("-inf")

HEAD_DIM = 64
N_HEADS = 8
WIDTH = N_HEADS * HEAD_DIM
CHUNK = 64
N_PREV_CHUNKS = 8
BAND_PAST = N_PREV_CHUNKS * CHUNK
REL_CLIP = 128
N_MEM_HEADS = 4
CONV_W = 3

ROW_TILE = 512
BAND_Q = 256
BAND_WIN = BAND_PAST + BAND_Q
FOX_BLOCK = 512
FOX_CACHE_BLOCK = 1024
MXU_COLS = 256
LANES = 128
SUBLANES = 8
VMEM_LIMIT = 56 * 1024 * 1024
LOG2E = 1.4426950408889634


def _dot(a, b):
    return jnp.dot(a, b, preferred_element_type=F32)


def _dot_nt(a, b):
    return lax.dot_general(a, b, (((1,), (1,)), ((), ())), preferred_element_type=F32)


def _rms(x, g):
    return x * lax.rsqrt(jnp.mean(x * x, axis=-1, keepdims=True) + EPS) * g


def _split3(x):
    hi = x.astype(BF16)
    r1 = x - hi.astype(F32)
    mid = r1.astype(BF16)
    lo = (r1 - mid.astype(F32)).astype(BF16)
    return hi, mid, lo


def _store_heads(ref, x):
    rows = x.shape[0]
    for h in range(N_HEADS):
        ref[pl.ds(h, rows, stride=N_HEADS), :] = x[:, h * HEAD_DIM:(h + 1) * HEAD_DIM]


def _load_head(ref, h):
    return ref[pl.ds(h, ref.shape[0] // N_HEADS, stride=N_HEADS), :].astype(BF16)


def _params(*sem):
    return pltpu.CompilerParams(dimension_semantics=sem, vmem_limit_bytes=VMEM_LIMIT)


def _inproj_kernel(x_ref, g1_ref, w_ref, wf_ref, bf_ref, gains_ref, seg_ref,
                   qa_ref, ka_ref, va_ref, qb_ref, kb_ref, vb_ref,
                   kaf_ref, vaf_ref, kbf_ref, vbf_ref, logf_ref, *, tail_period):
    h = _rms(x_ref[...], g1_ref[...]).astype(BF16)
    seg = seg_ref[...]

    def proj(g):
        return _dot(h, w_ref[:, g * WIDTH:(g + 1) * WIDTH])

    def head_norm(p, row):
        p2 = (p * p).astype(BF16)
        ss = jnp.concatenate([_dot(p2[:, :MXU_COLS], seg), _dot(p2[:, MXU_COLS:], seg)], axis=1)
        return p * lax.rsqrt(ss * (1.0 / HEAD_DIM) + EPS) * gains_ref[row:row + 1, :]

    is_tail = (pl.program_id(0) % tail_period) == tail_period - 1
    scale = HEAD_DIM ** -0.5 * LOG2E

    qa_ref[...] = (head_norm(proj(0), 0) * scale).astype(BF16)
    ka = head_norm(proj(1), 1)
    ka_ref[...] = ka.astype(BF16)
    va = proj(2)
    va_ref[...] = va.astype(BF16)

    @pl.when(is_tail)
    def _():
        _store_heads(kaf_ref.at[0], ka)
        _store_heads(vaf_ref.at[0], va)

    qb_ref[...] = (head_norm(proj(3), 2) * scale).astype(BF16)
    kb = head_norm(proj(4), 3)
    kb_ref[...] = kb.astype(BF16)
    _store_heads(kbf_ref, kb)
    vb = proj(5)
    vb_ref[...] = vb.astype(BF16)
    _store_heads(vbf_ref, vb)

    z = _dot(h, wf_ref[...]) + bf_ref[...]
    logf = jnp.minimum(z, 0.0) - jnp.log1p(jnp.exp(-jnp.abs(z)))
    logf_ref[...] = logf[:, :N_HEADS]


def _in_proj(x2d, g1, w_main, w_f, b_f, gains, seg, *, tail_period):
    n, d = x2d.shape
    tm = ROW_TILE
    assert n % tm == 0 and (n // tm) % tail_period == 0
    n_tail = n // tm // tail_period
    row_bf = jax.ShapeDtypeStruct((n, WIDTH), BF16)
    head_f = jax.ShapeDtypeStruct((n * N_HEADS, HEAD_DIM), F32)
    tail_f = jax.ShapeDtypeStruct((n_tail, tm * N_HEADS, HEAD_DIM), F32)
    rows = pl.BlockSpec((tm, WIDTH), lambda i: (i, 0))
    head_rows = pl.BlockSpec((tm * N_HEADS, HEAD_DIM), lambda i: (i, 0))
    tail = pl.BlockSpec((1, tm * N_HEADS, HEAD_DIM), lambda i: (i // tail_period, 0, 0))
    full = lambda a: pl.BlockSpec(a.shape, lambda i: (0,) * a.ndim)
    return pl.pallas_call(
        functools.partial(_inproj_kernel, tail_period=tail_period),
        grid=(n // tm,),
        in_specs=[pl.BlockSpec((tm, d), lambda i: (i, 0)), full(g1), full(w_main), full(w_f),
                  full(b_f), full(gains), full(seg)],
        out_specs=[rows] * 6 + [tail, tail, head_rows, head_rows, pl.BlockSpec((tm, N_HEADS), lambda i: (i, 0))],
        out_shape=[row_bf] * 6 + [tail_f, tail_f, head_f, head_f, jax.ShapeDtypeStruct((n, N_HEADS), F32)],
        compiler_params=_params("arbitrary"),
        name="in_proj",
    )(x2d, g1, w_main, w_f, b_f, gains, seg)


def _memkv_kernel(mem_ref, g_ref, w_ref, gk_ref, kf_ref, vf_ref, kb_ref, vb_ref):
    d = mem_ref.shape[-1]
    hd = d // N_MEM_HEADS
    h = _rms(mem_ref[0], g_ref[...]).astype(BF16)
    kv = _dot(h, w_ref[...])
    for hh in range(N_MEM_HEADS):
        k = _rms(kv[:, hh * hd:(hh + 1) * hd], gk_ref[...])
        kf_ref[0, :, hh * hd:(hh + 1) * hd] = k
        kb_ref[0, :, hh * hd:(hh + 1) * hd] = k.astype(BF16)
    v = kv[:, d:]
    vf_ref[0] = v
    vb_ref[0] = v.astype(BF16)


def _mem_kv(mem, g_mem, w_mkv, g_mk):
    b, n, d = mem.shape
    blk = pl.BlockSpec((1, n, d), lambda i: (i, 0, 0))
    full = lambda a: pl.BlockSpec(a.shape, lambda i: (0,) * a.ndim)
    f = jax.ShapeDtypeStruct((b, n, d), F32)
    h = jax.ShapeDtypeStruct((b, n, d), BF16)
    return pl.pallas_call(
        _memkv_kernel, grid=(b,),
        in_specs=[blk, full(g_mem), full(w_mkv), full(g_mk)],
        out_specs=[blk] * 4, out_shape=[f, f, h, h],
        compiler_params=_params("arbitrary"), name="mem_kv",
    )(mem, g_mem, w_mkv, g_mk)


def _cumsum_kernel(x_ref, o_ref):
    r, t = x_ref.shape
    k = lax.broadcasted_iota(jnp.int32, (LANES, LANES), 0)
    j = lax.broadcasted_iota(jnp.int32, (LANES, LANES), 1)
    tri = jnp.where(k <= j, 1.0, 0.0).astype(BF16)
    carry = jnp.zeros((r, 1), F32)
    for blk in range(t // LANES):
        hi, mid, lo = _split3(x_ref[:, blk * LANES:(blk + 1) * LANES])
        cs = _dot(hi, tri) + _dot(mid, tri) + _dot(lo, tri)
        o_ref[:, blk * LANES:(blk + 1) * LANES] = (cs + carry) * LOG2E
        carry = carry + cs[:, LANES - 1:LANES]


def _cumsum_lanes(x):
    assert x.shape[1] % LANES == 0
    return pl.pallas_call(_cumsum_kernel, out_shape=jax.ShapeDtypeStruct(x.shape, F32), name="cumsum")(x)


def _band_bias_kernel(tbl_ref, o_ref):
    nrel = tbl_ref.shape[1]
    qblk, win = o_ref.shape[1:]
    wrap = qblk + win
    jp = lax.broadcasted_iota(jnp.int32, (nrel, wrap), 1)
    r = lax.broadcasted_iota(jnp.int32, (nrel, wrap), 0)
    d = jnp.where(jp < win, jp, jp - wrap)
    rel = jnp.clip(BAND_PAST - d, -REL_CLIP, REL_CLIP) + REL_CLIP
    onehot = jnp.where(rel == r, 1.0, 0.0).astype(BF16)
    hi, mid, lo = _split3(tbl_ref[...])
    base = (_dot(hi, onehot) + _dot(mid, onehot) + _dot(lo, onehot)) * LOG2E
    qc = lax.broadcasted_iota(jnp.int32, (qblk, win), 0) // CHUNK
    kc = lax.broadcasted_iota(jnp.int32, (qblk, win), 1) // CHUNK
    inband = (kc >= qc) & (kc <= qc + N_PREV_CHUNKS)
    for h in range(N_HEADS):
        row = jnp.broadcast_to(base[h:h + 1, :], (qblk, wrap))
        toeplitz = pltpu.roll(row, 0, 1, stride=1, stride_axis=0)
        o_ref[h] = jnp.where(inband, toeplitz[:, :win], NEG_INF)


def _band_bias(rel_table):
    nrel = 2 * REL_CLIP + 1
    pad = -nrel % LANES
    tbl = jnp.pad(rel_table.astype(F32), ((0, 0), (0, pad)))
    return pl.pallas_call(
        _band_bias_kernel, out_shape=jax.ShapeDtypeStruct((N_HEADS, BAND_Q, BAND_WIN), F32),
        name="band_bias")(tbl)


def _softmax_pv(s, vs):
    m = jnp.max(s, axis=-1, keepdims=True)
    e = jnp.exp2(s - m)
    l = jnp.sum(e, axis=-1, keepdims=True)
    eb = e.astype(BF16)
    o, lo = None, 0
    for v in vs:
        part = _dot(eb[:, lo:lo + v.shape[0]], v)
        o = part if o is None else o + part
        lo += v.shape[0]
    return o * (1.0 / l)


def _band_prompt_kernel(q_ref, k0_ref, k1_ref, k2_ref, v0_ref, v1_ref, v2_ref, bias_ref, o_ref):
    i = pl.program_id(1)
    for h in range(N_HEADS):
        hs = slice(h * HEAD_DIM, (h + 1) * HEAD_DIM)
        q = q_ref[0, :, hs]
        parts = []
        for p, k_ref in enumerate((k0_ref, k1_ref, k2_ref)):
            s = _dot_nt(q, k_ref[0, :, hs])
            if p < 2:
                s = jnp.where(i + p >= 2, s, NEG_INF)
            parts.append(s)
        s = jnp.concatenate(parts, axis=1) + bias_ref[h]
        o = _softmax_pv(s, [v0_ref[0, :, hs], v1_ref[0, :, hs], v2_ref[0, :, hs]])
        o_ref[0, :, hs] = o.astype(BF16)


def _band_prompt(qa, ka, va, bias):
    b, t, _ = qa.shape
    tq = BAND_Q
    assert t % tq == 0
    blk = lambda back: pl.BlockSpec((1, tq, WIDTH), lambda bi, i: (bi, jnp.maximum(i - back, 0), 0))
    return pl.pallas_call(
        _band_prompt_kernel, grid=(b, t // tq),
        in_specs=[blk(0), blk(2), blk(1), blk(0), blk(2), blk(1), blk(0),
                  pl.BlockSpec(bias.shape, lambda bi, i: (0, 0, 0))],
        out_specs=blk(0), out_shape=jax.ShapeDtypeStruct((b, t, WIDTH), BF16),
        compiler_params=_params("arbitrary", "arbitrary"), name="band_prompt",
    )(qa, ka, ka, ka, va, va, va, bias)


def _band_sample_kernel(q_ref, kc_ref, vc_ref, kn_ref, vn_ref, bias_ref, o_ref):
    for h in range(N_HEADS):
        hs = slice(h * HEAD_DIM, (h + 1) * HEAD_DIM)
        q = q_ref[0, :, hs]
        s = jnp.concatenate([_dot_nt(q, _load_head(kc_ref.at[0], h)), _dot_nt(q, kn_ref[0, :, hs])], axis=1)
        s = s + bias_ref[h, :CHUNK, :BAND_PAST + CHUNK]
        o = _softmax_pv(s, [_load_head(vc_ref.at[0], h), vn_ref[0, :, hs]])
        o_ref[0, :, hs] = o.astype(BF16)


def _band_sample(qa, ka, va, cache_k, cache_v, bias):
    b, s, _ = qa.shape
    la = cache_k.shape[1] // N_HEADS
    assert s == CHUNK and la == BAND_PAST
    new = pl.BlockSpec((1, s, WIDTH), lambda bi: (bi, 0, 0))
    old = pl.BlockSpec((1, la * N_HEADS, HEAD_DIM), lambda bi: (bi, 0, 0))
    return pl.pallas_call(
        _band_sample_kernel, grid=(b,),
        in_specs=[new, old, old, new, new, pl.BlockSpec(bias.shape, lambda bi: (0, 0, 0))],
        out_specs=new, out_shape=jax.ShapeDtypeStruct((b, s, WIDTH), BF16),
        compiler_params=_params("arbitrary"), name="band_sample",
    )(qa, cache_k, cache_v, ka, va, bias)


def _fox_scratch(tq, tk):
    rows = lambda w: pltpu.VMEM((N_HEADS, tq, w), F32)
    return [rows(tk), rows(LANES), rows(LANES), rows(LANES), rows(LANES), rows(HEAD_DIM)]


def _fox_init(ccol_ref, sc):
    _, cq_sc, m_sc, _, l_sc, acc_sc = sc
    for h in range(N_HEADS):
        cq_sc[h] = jnp.broadcast_to(ccol_ref[0, :, h:h + 1], cq_sc.shape[1:])
    m_sc[...] = jnp.full(m_sc.shape, NEG_INF, F32)
    l_sc[...] = jnp.zeros(l_sc.shape, F32)
    acc_sc[...] = jnp.zeros(acc_sc.shape, F32)


def _fox_block(q_ref, k_head, v_head, crow_ref, tk, causal, sc):
    s_sc, cq_sc, m_sc, a_sc, l_sc, acc_sc = sc
    tq = q_ref.shape[1]
    groups = [slice(g * LANES, (g + 1) * LANES) for g in range(tk // LANES)]
    if causal:
        row = lax.broadcasted_iota(jnp.int32, (tq, LANES), 0)
        lane = lax.broadcasted_iota(jnp.int32, (tq, LANES), 1)
    for h in range(N_HEADS):
        s = _dot_nt(q_ref[0, :, h * HEAD_DIM:(h + 1) * HEAD_DIM], k_head(h))
        cq = cq_sc[h]
        mx = None
        for gs in groups:
            sg = s[:, gs] + (cq - crow_ref[0, h:h + 1, gs])
            if causal:
                sg = jnp.where(row >= lane + gs.start, sg, NEG_INF)
            s_sc[h, :, gs] = sg
            mx = sg if mx is None else jnp.maximum(mx, sg)
        m_prev = m_sc[h]
        m_new = jnp.maximum(m_prev, jnp.max(mx, axis=-1, keepdims=True))
        a_sc[h] = jnp.exp2(m_prev - m_new)
        m_sc[h] = m_new
    for h in range(N_HEADS):
        m_new = m_sc[h]
        rs, ps = None, []
        for gs in groups:
            p = jnp.exp2(s_sc[h, :, gs] - m_new)
            ps.append(p.astype(BF16))
            rs = p if rs is None else rs + p
        alpha = a_sc[h]
        l_sc[h] = alpha * l_sc[h] + rs
        acc_sc[h] = alpha[:, :HEAD_DIM] * acc_sc[h] + _dot(jnp.concatenate(ps, axis=1), v_head(h))


def _fox_finish(o_ref, sc):
    l_sc, acc_sc = sc[4], sc[5]
    for h in range(N_HEADS):
        l = jnp.sum(l_sc[h], axis=-1, keepdims=True)
        o_ref[0, :, h * HEAD_DIM:(h + 1) * HEAD_DIM] = (acc_sc[h] * (1.0 / l)).astype(BF16)


def _head(ref):
    return lambda h: ref[0, :, h * HEAD_DIM:(h + 1) * HEAD_DIM]


def _fox_prompt_kernel(qi_ref, kj_ref, q_ref, k_ref, v_ref, crow_ref, ccol_ref, o_ref, *sc):
    p = pl.program_id(1)
    qi, kj = qi_ref[p], kj_ref[p]
    tk = k_ref.shape[1]

    @pl.when(kj == 0)
    def _():
        _fox_init(ccol_ref, sc)

    @pl.when(kj < qi)
    def _():
        _fox_block(q_ref, _head(k_ref), _head(v_ref), crow_ref, tk, False, sc)

    @pl.when(kj == qi)
    def _():
        _fox_block(q_ref, _head(k_ref), _head(v_ref), crow_ref, tk, True, sc)
        _fox_finish(o_ref, sc)


def _fox_prompt(qb, kb, vb, crow, ccol):
    b, t, _ = qb.shape
    tq = FOX_BLOCK
    assert t % tq == 0
    pairs = [(i, j) for i in range(t // tq) for j in range(i + 1)]
    qi = jnp.array([p[0] for p in pairs], jnp.int32)
    kj = jnp.array([p[1] for p in pairs], jnp.int32)
    qblk = pl.BlockSpec((1, tq, WIDTH), lambda bi, p, qi, kj: (bi, qi[p], 0))
    kblk = pl.BlockSpec((1, tq, WIDTH), lambda bi, p, qi, kj: (bi, kj[p], 0))
    grid_spec = pltpu.PrefetchScalarGridSpec(
        num_scalar_prefetch=2, grid=(b, len(pairs)),
        in_specs=[qblk, kblk, kblk,
                  pl.BlockSpec((1, N_HEADS, tq), lambda bi, p, qi, kj: (bi, 0, kj[p])),
                  pl.BlockSpec((1, tq, N_HEADS), lambda bi, p, qi, kj: (bi, qi[p], 0))],
        out_specs=qblk, scratch_shapes=_fox_scratch(tq, tq))
    return pl.pallas_call(
        _fox_prompt_kernel, grid_spec=grid_spec, out_shape=jax.ShapeDtypeStruct((b, t, WIDTH), BF16),
        compiler_params=_params("arbitrary", "arbitrary"), name="fox_prompt",
    )(qi, kj, qb, kb, vb, crow, ccol)


def _fox_sample_kernel(q_ref, kc_ref, vc_ref, kn_ref, vn_ref, crow_ref, crow_new_ref, ccol_ref, o_ref, *sc):
    j = pl.program_id(1)
    last = pl.num_programs(1) - 1

    @pl.when(j == 0)
    def _():
        _fox_init(ccol_ref, sc)

    @pl.when(j < last)
    def _():
        _fox_block(q_ref, functools.partial(_load_head, kc_ref.at[0]), functools.partial(_load_head, vc_ref.at[0]),
                   crow_ref, kc_ref.shape[1] // N_HEADS, False, sc)

    @pl.when(j == last)
    def _():
        _fox_block(q_ref, _head(kn_ref), _head(vn_ref), crow_new_ref, kn_ref.shape[1], True, sc)
        _fox_finish(o_ref, sc)


def _fox_sample(qb, kb, vb, cache_k, cache_v, crow, crow_new, ccol):
    b, s, _ = qb.shape
    past = cache_k.shape[1] // N_HEADS
    tk, tn = FOX_CACHE_BLOCK, kb.shape[1]
    assert past % tk == 0 and tn % LANES == 0 and tn >= s
    nk = past // tk
    new = lambda r: pl.BlockSpec((1, r, WIDTH), lambda bi, j: (bi, 0, 0))
    old = pl.BlockSpec((1, tk * N_HEADS, HEAD_DIM), lambda bi, j: (bi, jnp.minimum(j, nk - 1), 0))
    return pl.pallas_call(
        _fox_sample_kernel, grid=(b, nk + 1),
        in_specs=[new(s), old, old, new(tn), new(tn),
                  pl.BlockSpec((1, N_HEADS, tk), lambda bi, j: (bi, 0, jnp.minimum(j, nk - 1))),
                  pl.BlockSpec((1, N_HEADS, tn), lambda bi, j: (bi, 0, 0)),
                  pl.BlockSpec((1, s, N_HEADS), lambda bi, j: (bi, 0, 0))],
        out_specs=new(s), out_shape=jax.ShapeDtypeStruct((b, s, WIDTH), BF16),
        scratch_shapes=_fox_scratch(s, tk),
        compiler_params=_params("arbitrary", "arbitrary"), name="fox_sample",
    )(qb, cache_k, cache_v, kb, vb, crow, crow_new, ccol)


def _mid_kernel(x_ref, oa_ref, ob_ref, wo_ref, g2_ref, wq_ref, gq_ref, mk_ref, mv_ref, wmo_ref, o_ref,
                om_sc, *, seg):
    tm, d = x_ref.shape
    hd = d // N_MEM_HEADS
    x1 = x_ref[...] + _dot(oa_ref[...], wo_ref[:WIDTH, :]) + _dot(ob_ref[...], wo_ref[WIDTH:, :])
    h2 = _rms(x1, g2_ref[...]).astype(BF16)
    qm = _dot(h2, wq_ref[...])
    scale = hd ** -0.5 * LOG2E
    for hh in range(N_MEM_HEADS):
        cs = slice(hh * hd, (hh + 1) * hd)
        q = (_rms(qm[:, cs], gq_ref[...]) * scale).astype(BF16)
        for sg in range(tm // seg):
            rs = slice(sg * seg, (sg + 1) * seg)
            s = _dot_nt(q[rs], mk_ref[sg, :, cs])
            om_sc[rs, cs] = _softmax_pv(s, [mv_ref[sg, :, cs]]).astype(BF16)
    o_ref[...] = x1 + _dot(om_sc[...], wmo_ref[...])


def _mid(x2d, oa, ob, w_o, g2, w_mq, g_mq, mk, mv, w_mo, *, seq):
    n, d = x2d.shape
    tm = ROW_TILE
    seg = min(seq, tm)
    tiles_per_batch = seq // seg
    nb = tm // seg
    nmem = mk.shape[1]
    assert n % tm == 0 and seq % seg == 0 and tm % seg == 0
    rows = lambda w: pl.BlockSpec((tm, w), lambda i: (i, 0))
    full = lambda a: pl.BlockSpec(a.shape, lambda i: (0,) * a.ndim)
    mem = pl.BlockSpec((nb, nmem, d), lambda i: (i // tiles_per_batch, 0, 0))
    return pl.pallas_call(
        functools.partial(_mid_kernel, seg=seg), grid=(n // tm,),
        in_specs=[rows(d), rows(WIDTH), rows(WIDTH), full(w_o), full(g2), full(w_mq), full(g_mq), mem, mem,
                  full(w_mo)],
        out_specs=rows(d), out_shape=jax.ShapeDtypeStruct((n, d), F32),
        scratch_shapes=[pltpu.VMEM((tm, d), BF16)],
        compiler_params=_params("arbitrary"), name="mid",
    )(x2d, oa, ob, w_o, g2, w_mq, g_mq, mk, mv, w_mo)


def _ffn_kernel(x_ref, g3_ref, wup_ref, wc_ref, bc_ref, wd_ref, st_ref, o_ref, last_ref, carry_sc,
                *, seg, tiles_per_batch, ff_chunk):
    tm, d = x_ref.shape
    dff = wd_ref.shape[0]
    i = pl.program_id(0)
    x = x_ref[...]
    h3 = _rms(x, g3_ref[...]).astype(BF16)
    o_ref[...] = x
    if tiles_per_batch > 1:
        @pl.when(i == 0)
        def _():
            carry_sc[...] = jnp.zeros(carry_sc.shape, F32)
    rowid = lax.broadcasted_iota(jnp.int32, (SUBLANES, 1), 0)
    first_tile = (i % tiles_per_batch) == 0
    for c in range(dff // ff_chunk):
        cs = slice(c * ff_chunk, (c + 1) * ff_chunk)
        gate = _dot(h3, wup_ref[:, cs])
        val = _dot(h3, wup_ref[:, dff + c * ff_chunk:dff + (c + 1) * ff_chunk])
        w0, w1, w2 = wc_ref[0:1, cs], wc_ref[1:2, cs], wc_ref[2:3, cs]
        ys = []
        for sg in range(tm // seg):
            g = gate[sg * seg:(sg + 1) * seg]
            if tiles_per_batch == 1:
                prev = st_ref[sg, :, cs]
            else:
                prev = jnp.where(first_tile, st_ref[0, :, cs], carry_sc[SUBLANES - 2:, cs])
            p2, p1 = prev[0:1], prev[1:2]
            s1 = pltpu.roll(g, 1, 0)
            s2 = pltpu.roll(g, 2, 0)
            top1 = jnp.where(rowid == 0, p1, s1[:SUBLANES])
            top2 = jnp.where(rowid == 0, p2, jnp.where(rowid == 1, p1, s2[:SUBLANES]))
            s1 = jnp.concatenate([top1, s1[SUBLANES:]], axis=0)
            s2 = jnp.concatenate([top2, s2[SUBLANES:]], axis=0)
            conv = s2 * w0 + s1 * w1 + g * w2 + bc_ref[:, cs]
            ys.append(conv * jax.nn.sigmoid(conv) * val[sg * seg:(sg + 1) * seg])
            last_ref[sg, :, cs] = g[seg - SUBLANES:]
        if tiles_per_batch > 1:
            carry_sc[:, cs] = gate[tm - SUBLANES:]
        y = ys[0] if len(ys) == 1 else jnp.concatenate(ys, axis=0)
        o_ref[...] += _dot(y.astype(BF16), wd_ref[cs, :])


def _conv_ffn(x2d, g3, w_up, w_conv, b_conv, w_down, state, *, seq, ff_chunk):
    n, d = x2d.shape
    dff = w_down.shape[0]
    tm = ROW_TILE
    seg = min(seq, tm)
    tiles_per_batch = seq // seg
    nb = tm // seg
    assert n % tm == 0 and seq % seg == 0 and tm % seg == 0 and dff % ff_chunk == 0 and seg >= SUBLANES
    rows = pl.BlockSpec((tm, d), lambda i: (i, 0))
    full = lambda a: pl.BlockSpec(a.shape, lambda i: (0,) * a.ndim)
    per_batch = lambda r: pl.BlockSpec((nb, r, dff), lambda i: (i // tiles_per_batch, 0, 0))
    return pl.pallas_call(
        functools.partial(_ffn_kernel, seg=seg, tiles_per_batch=tiles_per_batch, ff_chunk=ff_chunk),
        grid=(n // tm,),
        in_specs=[rows, full(g3), full(w_up), full(w_conv), full(b_conv), full(w_down), per_batch(CONV_W - 1)],
        out_specs=[rows, per_batch(SUBLANES)],
        out_shape=[jax.ShapeDtypeStruct((n, d), F32), jax.ShapeDtypeStruct((n // seq, SUBLANES, dff), F32)],
        scratch_shapes=[pltpu.VMEM((SUBLANES, dff), F32)],
        compiler_params=_params("arbitrary"), name="conv_ffn",
    )(x2d, g3, w_up, w_conv, b_conv, w_down, state)


def _row(v):
    return v.astype(F32).reshape(1, -1)


def _layer_weights(l, w_in, b_f, g_qa, g_ka, rel_bias, g_qb, g_kb, w_o, g_norm1, g_norm2, g_mem, w_mq, w_mkv,
                   g_mq, g_mk, w_mo, g_norm3, w_up, w_conv, b_conv, w_down):
    n_main = 6 * WIDTH
    tile = lambda g: jnp.tile(g[l].astype(F32), N_HEADS)
    blk = jnp.arange(MXU_COLS) // HEAD_DIM
    return dict(
        w_main=w_in[l][:, :n_main].astype(BF16),
        w_f=jnp.pad(w_in[l][:, n_main:], ((0, 0), (0, LANES - N_HEADS))).astype(BF16),
        b_f=jnp.pad(_row(b_f[l]), ((0, 0), (0, LANES - N_HEADS))),
        gains=jnp.stack([tile(g_qa), tile(g_ka), tile(g_qb), tile(g_kb)]),
        seg=(blk[:, None] == blk[None, :]).astype(BF16),
        rel=rel_bias[l],
        w_o=w_o[l].astype(BF16), g1=_row(g_norm1[l]), g2=_row(g_norm2[l]), g3=_row(g_norm3[l]),
        g_mem=_row(g_mem[l]), w_mq=w_mq[l].astype(BF16), w_mkv=w_mkv[l].astype(BF16),
        g_mq=_row(g_mq[l]), g_mk=_row(g_mk[l]), w_mo=w_mo[l].astype(BF16),
        w_up=w_up[l].astype(BF16), w_conv=w_conv[l].astype(F32), b_conv=_row(b_conv[l]),
        w_down=w_down[l].astype(BF16))


def _post_attention(x2d, oa, ob, mk, mv, state, w, *, seq, ff_chunk):
    n, d = x2d.shape
    x2 = _mid(x2d, oa.reshape(n, WIDTH), ob.reshape(n, WIDTH), w["w_o"], w["g2"], w["w_mq"], w["g_mq"],
              mk, mv, w["w_mo"], seq=seq)
    y, last = _conv_ffn(x2, w["g3"], w["w_up"], w["w_conv"], w["b_conv"], w["w_down"], state,
                        seq=seq, ff_chunk=ff_chunk)
    return y, last[:, SUBLANES - (CONV_W - 1):]


def _prompt_layer(x, mem, w, bias):
    b, t, d = x.shape
    la = min(BAND_PAST, t)
    assert la == ROW_TILE and t % ROW_TILE == 0
    qa, ka, va, qb, kb, vb, ka_tail, va_tail, kb_f, vb_f, logf = _in_proj(
        x.reshape(b * t, d), w["g1"], w["w_main"], w["w_f"], w["b_f"], w["gains"], w["seg"],
        tail_period=t // ROW_TILE)
    r3 = lambda a: a.reshape(b, t, WIDTH)
    oa = _band_prompt(r3(qa), r3(ka), r3(va), bias)
    logf = logf.reshape(b, t, N_HEADS)
    crow = _cumsum_lanes(logf.transpose(0, 2, 1).reshape(b * N_HEADS, t)).reshape(b, N_HEADS, t)
    ob = _fox_prompt(r3(qb), r3(kb), r3(vb), crow, crow.transpose(0, 2, 1))
    mk_f, mv_f, mk, mv = _mem_kv(mem, w["g_mem"], w["w_mkv"], w["g_mk"])
    dff = w["w_down"].shape[0]
    y, conv = _post_attention(x.reshape(b * t, d), oa, ob, mk, mv, jnp.zeros((b, CONV_W - 1, dff), F32), w,
                              seq=t, ff_chunk=dff // 11)
    heads = lambda a, n: a.reshape(b, n, N_HEADS, HEAD_DIM)
    mem_heads = lambda a: a.reshape(b, mem.shape[1], N_MEM_HEADS, d // N_MEM_HEADS)
    return (y.reshape(b, t, d), heads(ka_tail, la), heads(va_tail, la), heads(kb_f, t), heads(vb_f, t), logf,
            mem_heads(mk_f), mem_heads(mv_f), conv)


def _sample_layer(x, cache_a_k, cache_a_v, cache_b_k, cache_b_v, cache_b_logf, cache_mem_k, cache_mem_v,
                  state_conv, w, bias):
    b, s, d = x.shape
    past = cache_b_k.shape[1]
    assert ROW_TILE % s == 0 and (b * s) % ROW_TILE == 0
    qa, ka, va, qb, kb, vb, ka_f, va_f, kb_f, vb_f, logf = _in_proj(
        x.reshape(b * s, d), w["g1"], w["w_main"], w["w_f"], w["b_f"], w["gains"], w["seg"], tail_period=1)
    r3 = lambda a: a.reshape(b, s, WIDTH)
    flat = lambda a: a.reshape(a.shape[0], a.shape[1], -1)
    head_rows = lambda a: a.reshape(a.shape[0], a.shape[1] * N_HEADS, HEAD_DIM)
    oa = _band_sample(r3(qa), r3(ka), r3(va), head_rows(cache_a_k), head_rows(cache_a_v), bias)
    logf = logf.reshape(b, s, N_HEADS)
    total = past + s
    all_logf = jnp.concatenate([cache_b_logf.astype(F32), logf], axis=1).transpose(0, 2, 1)
    all_logf = jnp.pad(all_logf, ((0, 0), (0, 0), (0, -total % LANES)))
    cum = _cumsum_lanes(all_logf.reshape(b * N_HEADS, -1)).reshape(b, N_HEADS, -1)
    crow_new = cum[:, :, past:total]
    pad_rows = lambda a: jnp.pad(a, ((0, 0), (0, -s % LANES), (0, 0)))
    ob = _fox_sample(r3(qb), pad_rows(r3(kb)), pad_rows(r3(vb)), head_rows(cache_b_k), head_rows(cache_b_v),
                     cum[:, :, :past], jnp.pad(crow_new, ((0, 0), (0, 0), (0, -s % LANES))),
                     crow_new.transpose(0, 2, 1))
    dff = w["w_down"].shape[0]
    y, conv = _post_attention(x.reshape(b * s, d), oa, ob, flat(cache_mem_k).astype(BF16),
                              flat(cache_mem_v).astype(BF16), state_conv.astype(F32), w,
                              seq=s, ff_chunk=dff // 11)
    heads = lambda a: a.reshape(b, s, N_HEADS, HEAD_DIM)
    return y.reshape(b, s, d), heads(ka_f), heads(va_f), heads(kb_f), heads(vb_f), logf, conv


def kernel(x_prompt, x_sample, cache_a_k, cache_a_v, cache_b_k, cache_b_v, cache_b_logf, cache_mem_k, cache_mem_v, state_conv, mem_prompt, w_in, b_f, g_qa, g_ka, rel_bias, g_qb, g_kb, w_o, g_norm1, g_norm2, g_mem, w_mq, w_mkv, g_mq, g_mk, w_mo, g_norm3, w_up, w_conv, b_conv, w_down):
    depth = w_in.shape[0]
    xp, xs = x_prompt, x_sample
    outs_p, outs_s = [], []
    for l in range(depth):
        w = _layer_weights(l, w_in, b_f, g_qa, g_ka, rel_bias, g_qb, g_kb, w_o, g_norm1, g_norm2, g_mem, w_mq,
                           w_mkv, g_mq, g_mk, w_mo, g_norm3, w_up, w_conv, b_conv, w_down)
        bias = _band_bias(w["rel"])
        xp, *rest_p = _prompt_layer(xp, mem_prompt, w, bias)
        xs, *rest_s = _sample_layer(xs, cache_a_k[l], cache_a_v[l], cache_b_k[l], cache_b_v[l], cache_b_logf[l],
                                    cache_mem_k[l], cache_mem_v[l], state_conv[l], w, bias)
        outs_p.append(rest_p)
        outs_s.append(rest_s)
    stack = lambda outs: [jnp.stack(col, axis=0) for col in zip(*outs)]
    return (xp, xs, *stack(outs_p), *stack(outs_s))
```

```python
import functools

import jax
import jax.numpy as jnp
from jax import lax
from jax.experimental import pallas as pl
from jax.experimental.pallas import tpu as pltpu

F32 = jnp.float32
BF16 = jnp.bfloat16
EPS = 1e-6
NEG_INF = float("-inf")

HEAD_DIM = 64
N_HEADS = 8
WIDTH = N_HEADS * HEAD_DIM
CHUNK = 64
N_PREV_CHUNKS = 8
BAND_PAST = N_PREV_CHUNKS * CHUNK
REL_CLIP = 128
N_MEM_HEADS = 4
CONV_W = 3

ROW_TILE = 512
BAND_Q = 256
BAND_WIN = BAND_PAST + BAND_Q
FOX_BLOCK = 512
FOX_CACHE_BLOCK = 1024
MXU_COLS = 256
LANES = 128
SUBLANES = 8
BF16_ROWS = 16
VMEM_LIMIT = 56 * 1024 * 1024
LOG2E = 1.4426950408889634


def _dot(a, b):
    return jnp.dot(a, b, preferred_element_type=F32)


def _dot_nt(a, b):
    return lax.dot_general(a, b, (((1,), (1,)), ((), ())), preferred_element_type=F32)


def _rms(x, g):
    return x * lax.rsqrt(jnp.mean(x * x, axis=-1, keepdims=True) + EPS) * g


def _split3(x):
    hi = x.astype(BF16)
    r1 = x - hi.astype(F32)
    mid = r1.astype(BF16)
    lo = (r1 - mid.astype(F32)).astype(BF16)
    return hi, mid, lo


def _store_heads(ref, x):
    rows = x.shape[0]
    for h in range(N_HEADS):
        ref[pl.ds(h, rows, stride=N_HEADS), :] = x[:, h * HEAD_DIM:(h + 1) * HEAD_DIM]


def _params(*sem):
    return pltpu.CompilerParams(dimension_semantics=sem, vmem_limit_bytes=VMEM_LIMIT)


def _inproj_kernel(x_ref, g1_ref, w_ref, wf_ref, bf_ref, gains_ref, seg_ref,
                   qa_ref, ka_ref, va_ref, qb_ref, kb_ref, vb_ref,
                   kaf_ref, vaf_ref, kbf_ref, vbf_ref, logf_ref, *, tail_period):
    h = _rms(x_ref[...], g1_ref[...]).astype(BF16)
    seg = seg_ref[...]

    def proj(g):
        return _dot(h, w_ref[:, g * WIDTH:(g + 1) * WIDTH])

    def head_norm(p, row):
        p2 = (p * p).astype(BF16)
        ss = jnp.concatenate([_dot(p2[:, :MXU_COLS], seg), _dot(p2[:, MXU_COLS:], seg)], axis=1)
        return p * lax.rsqrt(ss * (1.0 / HEAD_DIM) + EPS) * gains_ref[row:row + 1, :]

    is_tail = (pl.program_id(0) % tail_period) == tail_period - 1
    scale = HEAD_DIM ** -0.5 * LOG2E

    qa_ref[...] = (head_norm(proj(0), 0) * scale).astype(BF16)
    ka = head_norm(proj(1), 1)
    ka_ref[...] = ka.astype(BF16)
    va = proj(2)
    va_ref[...] = va.astype(BF16)

    @pl.when(is_tail)
    def _():
        _store_heads(kaf_ref.at[0], ka)
        _store_heads(vaf_ref.at[0], va)

    qb_ref[...] = (head_norm(proj(3), 2) * scale).astype(BF16)
    kb = head_norm(proj(4), 3)
    kb_ref[...] = kb.astype(BF16)
    _store_heads(kbf_ref, kb)
    vb = proj(5)
    vb_ref[...] = vb.astype(BF16)
    _store_heads(vbf_ref, vb)

    z = _dot_nt(wf_ref[...], h)[:N_HEADS] + bf_ref[...]
    logf_ref[...] = jnp.minimum(z, 0.0) - jnp.log1p(jnp.exp(-jnp.abs(z)))


def _in_proj(x2d, g1, w_main, w_f, b_f, gains, seg, *, tail_period):
    n, d = x2d.shape
    tm = ROW_TILE
    assert n % tm == 0 and (n // tm) % tail_period == 0
    n_tail = n // tm // tail_period
    row_bf = jax.ShapeDtypeStruct((n, WIDTH), BF16)
    head_f = jax.ShapeDtypeStruct((n * N_HEADS, HEAD_DIM), F32)
    tail_f = jax.ShapeDtypeStruct((n_tail, tm * N_HEADS, HEAD_DIM), F32)
    rows = pl.BlockSpec((tm, WIDTH), lambda i: (i, 0))
    head_rows = pl.BlockSpec((tm * N_HEADS, HEAD_DIM), lambda i: (i, 0))
    tail = pl.BlockSpec((1, tm * N_HEADS, HEAD_DIM), lambda i: (i // tail_period, 0, 0))
    full = lambda a: pl.BlockSpec(a.shape, lambda i: (0,) * a.ndim)
    return pl.pallas_call(
        functools.partial(_inproj_kernel, tail_period=tail_period),
        grid=(n // tm,),
        in_specs=[pl.BlockSpec((tm, d), lambda i: (i, 0)), full(g1), full(w_main), full(w_f),
                  full(b_f), full(gains), full(seg)],
        out_specs=[rows] * 6 + [tail, tail, head_rows, head_rows, pl.BlockSpec((N_HEADS, tm), lambda i: (0, i))],
        out_shape=[row_bf] * 6 + [tail_f, tail_f, head_f, head_f, jax.ShapeDtypeStruct((N_HEADS, n), F32)],
        compiler_params=_params("arbitrary"),
        name="in_proj",
    )(x2d, g1, w_main, w_f, b_f, gains, seg)


def _memkv_kernel(mem_ref, g_ref, w_ref, gk_ref, kf_ref, vf_ref, kb_ref, vb_ref):
    d = mem_ref.shape[-1]
    hd = d // N_MEM_HEADS
    h = _rms(mem_ref[0], g_ref[...]).astype(BF16)
    kv = _dot(h, w_ref[...])
    for hh in range(N_MEM_HEADS):
        k = _rms(kv[:, hh * hd:(hh + 1) * hd], gk_ref[...])
        kf_ref[0, :, hh * hd:(hh + 1) * hd] = k
        kb_ref[0, :, hh * hd:(hh + 1) * hd] = k.astype(BF16)
    v = kv[:, d:]
    vf_ref[0] = v
    vb_ref[0] = v.astype(BF16)


def _mem_kv(mem, g_mem, w_mkv, g_mk):
    b, n, d = mem.shape
    blk = pl.BlockSpec((1, n, d), lambda i: (i, 0, 0))
    full = lambda a: pl.BlockSpec(a.shape, lambda i: (0,) * a.ndim)
    f = jax.ShapeDtypeStruct((b, n, d), F32)
    h = jax.ShapeDtypeStruct((b, n, d), BF16)
    return pl.pallas_call(
        _memkv_kernel, grid=(b,),
        in_specs=[blk, full(g_mem), full(w_mkv), full(g_mk)],
        out_specs=[blk] * 4, out_shape=[f, f, h, h],
        compiler_params=_params("arbitrary"), name="mem_kv",
    )(mem, g_mem, w_mkv, g_mk)


def _cumsum_kernel(x_ref, o_ref):
    r, t = x_ref.shape
    k = lax.broadcasted_iota(jnp.int32, (LANES, LANES), 0)
    j = lax.broadcasted_iota(jnp.int32, (LANES, LANES), 1)
    tri = jnp.where(k <= j, 1.0, 0.0).astype(BF16)
    carry = jnp.zeros((r, 1), F32)
    for blk in range(t // LANES):
        hi, mid, lo = _split3(x_ref[:, blk * LANES:(blk + 1) * LANES])
        cs = _dot(hi, tri) + _dot(mid, tri) + _dot(lo, tri)
        o_ref[:, blk * LANES:(blk + 1) * LANES] = (cs + carry) * LOG2E
        carry = carry + cs[:, LANES - 1:LANES]


def _cumsum_lanes(x):
    assert x.shape[1] % LANES == 0
    return pl.pallas_call(_cumsum_kernel, out_shape=jax.ShapeDtypeStruct(x.shape, F32), name="cumsum")(x)


def _band_bias_kernel(tbl_ref, o_ref):
    nrel = tbl_ref.shape[1]
    qblk, win = o_ref.shape[1:]
    wrap = qblk + win
    jp = lax.broadcasted_iota(jnp.int32, (nrel, wrap), 1)
    r = lax.broadcasted_iota(jnp.int32, (nrel, wrap), 0)
    d = jnp.where(jp < win, jp, jp - wrap)
    rel = jnp.clip(BAND_PAST - d, -REL_CLIP, REL_CLIP) + REL_CLIP
    onehot = jnp.where(rel == r, 1.0, 0.0).astype(BF16)
    hi, mid, lo = _split3(tbl_ref[...])
    base = (_dot(hi, onehot) + _dot(mid, onehot) + _dot(lo, onehot)) * LOG2E
    qc = lax.broadcasted_iota(jnp.int32, (qblk, win), 0) // CHUNK
    kc = lax.broadcasted_iota(jnp.int32, (qblk, win), 1) // CHUNK
    inband = (kc >= qc) & (kc <= qc + N_PREV_CHUNKS)
    for h in range(N_HEADS):
        row = jnp.broadcast_to(base[h:h + 1, :], (qblk, wrap))
        toeplitz = pltpu.roll(row, 0, 1, stride=1, stride_axis=0)
        o_ref[h] = jnp.where(inband, toeplitz[:, :win], NEG_INF)


def _band_bias(rel_table):
    nrel = 2 * REL_CLIP + 1
    pad = -nrel % LANES
    tbl = jnp.pad(rel_table.astype(F32), ((0, 0), (0, pad)))
    return pl.pallas_call(
        _band_bias_kernel, out_shape=jax.ShapeDtypeStruct((N_HEADS, BAND_Q, BAND_WIN), F32),
        name="band_bias")(tbl)


def _softmax_pv(s, vs, transposed=None):
    transposed = transposed or (False,) * len(vs)
    m = jnp.max(s, axis=-1, keepdims=True)
    e = jnp.exp2(s - m)
    l = jnp.sum(e, axis=-1, keepdims=True)
    eb = e.astype(BF16)
    o, lo = None, 0
    for v, vt in zip(vs, transposed):
        keys = v.shape[1] if vt else v.shape[0]
        part = _dot_nt(eb[:, lo:lo + keys], v) if vt else _dot(eb[:, lo:lo + keys], v)
        o = part if o is None else o + part
        lo += keys
    return o * (1.0 / l)


def _band_prompt_kernel(q_ref, k0_ref, k1_ref, k2_ref, v0_ref, v1_ref, v2_ref, bias_ref, o_ref, s_sc, m_sc):
    i = pl.program_id(1)
    tq = q_ref.shape[1]
    k_refs, v_refs = (k0_ref, k1_ref, k2_ref), (v0_ref, v1_ref, v2_ref)

    def logits(early):
        for h in range(N_HEADS):
            q = _head(q_ref, h)
            mx = None
            for p, k_ref in enumerate(k_refs):
                s = _dot_nt(q, _head(k_ref, h))
                for g in range(tq // LANES):
                    cols = slice(p * tq + g * LANES, p * tq + (g + 1) * LANES)
                    sg = s[:, g * LANES:(g + 1) * LANES] + bias_ref[h, :, cols]
                    if early and p < 2:
                        sg = jnp.where(i + p >= 2, sg, NEG_INF)
                    s_sc[h, :, cols] = sg
                    mx = sg if mx is None else jnp.maximum(mx, sg)
            m_sc[h] = jnp.broadcast_to(jnp.max(mx, axis=-1, keepdims=True), m_sc.shape[1:])

    @pl.when(i >= 2)
    def _():
        logits(False)

    @pl.when(i < 2)
    def _():
        logits(True)

    for h in range(N_HEADS):
        m = m_sc[h]
        rs, o = None, None
        for p, v_ref in enumerate(v_refs):
            ps = []
            for g in range(tq // LANES):
                e = jnp.exp2(s_sc[h, :, p * tq + g * LANES:p * tq + (g + 1) * LANES] - m)
                ps.append(e.astype(BF16))
                rs = e if rs is None else rs + e
            part = _dot(jnp.concatenate(ps, axis=1), _head(v_ref, h))
            o = part if o is None else o + part
        l = jnp.sum(rs, axis=-1, keepdims=True)
        o_ref[0, :, h * HEAD_DIM:(h + 1) * HEAD_DIM] = (o * (1.0 / l)).astype(BF16)


def _band_prompt(qa, ka, va, bias):
    b, t, _ = qa.shape
    tq = BAND_Q
    assert t % tq == 0
    blk = lambda back: pl.BlockSpec((1, tq, WIDTH), lambda bi, i: (bi, jnp.maximum(i - back, 0), 0))
    return pl.pallas_call(
        _band_prompt_kernel, grid=(b, t // tq),
        in_specs=[blk(0), blk(2), blk(1), blk(0), blk(2), blk(1), blk(0),
                  pl.BlockSpec(bias.shape, lambda bi, i: (0, 0, 0))],
        out_specs=blk(0), out_shape=jax.ShapeDtypeStruct((b, t, WIDTH), BF16),
        scratch_shapes=[pltpu.VMEM((N_HEADS, tq, BAND_WIN), F32), pltpu.VMEM((N_HEADS, tq, LANES), F32)],
        compiler_params=_params("arbitrary", "arbitrary"), name="band_prompt",
    )(qa, ka, ka, ka, va, va, va, bias)


def _band_sample_kernel(q_ref, kc_ref, vc_ref, kn_ref, vn_ref, bias_ref, o_ref):
    for h in range(N_HEADS):
        hs = slice(h * HEAD_DIM, (h + 1) * HEAD_DIM)
        q = q_ref[0, :, hs]
        s = jnp.concatenate([_dot(q, kc_ref[0, h].astype(BF16)), _dot_nt(q, kn_ref[0, :, hs])], axis=1)
        s = s + bias_ref[h, :CHUNK, :BAND_PAST + CHUNK]
        o = _softmax_pv(s, [vc_ref[0, h].astype(BF16), vn_ref[0, :, hs]], transposed=(True, False))
        o_ref[0, :, hs] = o.astype(BF16)


def _band_sample(qa, ka, va, cache_k, cache_v, bias):
    b, s, _ = qa.shape
    la = cache_k.shape[3]
    assert s == CHUNK and la == BAND_PAST
    new = pl.BlockSpec((1, s, WIDTH), lambda bi: (bi, 0, 0))
    old = pl.BlockSpec((1, N_HEADS, HEAD_DIM, la), lambda bi: (bi, 0, 0, 0))
    return pl.pallas_call(
        _band_sample_kernel, grid=(b,),
        in_specs=[new, old, old, new, new, pl.BlockSpec(bias.shape, lambda bi: (0, 0, 0))],
        out_specs=new, out_shape=jax.ShapeDtypeStruct((b, s, WIDTH), BF16),
        compiler_params=_params("arbitrary"), name="band_sample",
    )(qa, cache_k, cache_v, ka, va, bias)


def _fox_scratch(tq, tk):
    rows = lambda w: pltpu.VMEM((N_HEADS, tq, w), F32)
    return [rows(tk), rows(LANES), rows(LANES), rows(LANES), rows(LANES), rows(HEAD_DIM)]


def _fox_init(ccol_ref, sc):
    _, cq_sc, m_sc, _, l_sc, acc_sc = sc
    for h in range(N_HEADS):
        cq_sc[h] = jnp.broadcast_to(ccol_ref[0, :, h:h + 1], cq_sc.shape[1:])
    m_sc[...] = jnp.full(m_sc.shape, NEG_INF, F32)
    l_sc[...] = jnp.zeros(l_sc.shape, F32)
    acc_sc[...] = jnp.zeros(acc_sc.shape, F32)


def _fox_block(qk, pv, crow_ref, tk, causal, sc):
    s_sc, cq_sc, m_sc, a_sc, l_sc, acc_sc = sc
    tq = s_sc.shape[1]
    groups = [slice(g * LANES, (g + 1) * LANES) for g in range(tk // LANES)]
    if causal:
        row = lax.broadcasted_iota(jnp.int32, (tq, LANES), 0)
        lane = lax.broadcasted_iota(jnp.int32, (tq, LANES), 1)
    for h in range(N_HEADS):
        s = qk(h)
        cq = cq_sc[h]
        mx = None
        for gs in groups:
            sg = s[:, gs] + (cq - crow_ref[0, h:h + 1, gs])
            if causal:
                sg = jnp.where(row >= lane + gs.start, sg, NEG_INF)
            s_sc[h, :, gs] = sg
            mx = sg if mx is None else jnp.maximum(mx, sg)
        m_prev = m_sc[h]
        m_new = jnp.maximum(m_prev, jnp.max(mx, axis=-1, keepdims=True))
        a_sc[h] = jnp.exp2(m_prev - m_new)
        m_sc[h] = m_new
    for h in range(N_HEADS):
        m_new = m_sc[h]
        rs, ps = None, []
        for gs in groups:
            p = jnp.exp2(s_sc[h, :, gs] - m_new)
            ps.append(p.astype(BF16))
            rs = p if rs is None else rs + p
        alpha = a_sc[h]
        l_sc[h] = alpha * l_sc[h] + rs
        acc_sc[h] = alpha[:, :HEAD_DIM] * acc_sc[h] + pv(h, jnp.concatenate(ps, axis=1))


def _fox_finish(o_ref, sc):
    l_sc, acc_sc = sc[4], sc[5]
    for h in range(N_HEADS):
        l = jnp.sum(l_sc[h], axis=-1, keepdims=True)
        o_ref[0, :, h * HEAD_DIM:(h + 1) * HEAD_DIM] = (acc_sc[h] * (1.0 / l)).astype(BF16)


def _head(ref, h):
    return ref[0, :, h * HEAD_DIM:(h + 1) * HEAD_DIM]


def _row_major_products(q_ref, k_ref, v_ref):
    return (lambda h: _dot_nt(_head(q_ref, h), _head(k_ref, h))), (lambda h, p: _dot(p, _head(v_ref, h)))


def _fox_prompt_kernel(qi_ref, kj_ref, q_ref, k_ref, v_ref, crow_ref, ccol_ref, o_ref, *sc):
    p = pl.program_id(1)
    qi, kj = qi_ref[p], kj_ref[p]
    tk = k_ref.shape[1]
    qk, pv = _row_major_products(q_ref, k_ref, v_ref)

    @pl.when(kj == 0)
    def _():
        _fox_init(ccol_ref, sc)

    @pl.when(kj < qi)
    def _():
        _fox_block(qk, pv, crow_ref, tk, False, sc)

    @pl.when(kj == qi)
    def _():
        _fox_block(qk, pv, crow_ref, tk, True, sc)
        _fox_finish(o_ref, sc)


def _fox_prompt(qb, kb, vb, crow, ccol):
    b, t, _ = qb.shape
    tq = FOX_BLOCK
    assert t % tq == 0
    pairs = [(i, j) for i in range(t // tq) for j in range(i + 1)]
    qi = jnp.array([p[0] for p in pairs], jnp.int32)
    kj = jnp.array([p[1] for p in pairs], jnp.int32)
    qblk = pl.BlockSpec((1, tq, WIDTH), lambda bi, p, qi, kj: (bi, qi[p], 0))
    kblk = pl.BlockSpec((1, tq, WIDTH), lambda bi, p, qi, kj: (bi, kj[p], 0))
    grid_spec = pltpu.PrefetchScalarGridSpec(
        num_scalar_prefetch=2, grid=(b, len(pairs)),
        in_specs=[qblk, kblk, kblk,
                  pl.BlockSpec((1, N_HEADS, tq), lambda bi, p, qi, kj: (bi, 0, kj[p])),
                  pl.BlockSpec((1, tq, N_HEADS), lambda bi, p, qi, kj: (bi, qi[p], 0))],
        out_specs=qblk, scratch_shapes=_fox_scratch(tq, tq))
    return pl.pallas_call(
        _fox_prompt_kernel, grid_spec=grid_spec, out_shape=jax.ShapeDtypeStruct((b, t, WIDTH), BF16),
        compiler_params=_params("arbitrary", "arbitrary"), name="fox_prompt",
    )(qi, kj, qb, kb, vb, crow, ccol)


def _fox_sample_kernel(q_ref, kc_ref, vc_ref, kn_ref, vn_ref, crow_ref, crow_new_ref, ccol_ref, o_ref, *sc):
    j = pl.program_id(1)
    last = pl.num_programs(1) - 1

    @pl.when(j == 0)
    def _():
        _fox_init(ccol_ref, sc)

    @pl.when(j < last)
    def _():
        _fox_block(lambda h: _dot(_head(q_ref, h), kc_ref[0, h].astype(BF16)),
                   lambda h, p: _dot_nt(p, vc_ref[0, h].astype(BF16)), crow_ref, kc_ref.shape[3], False, sc)

    @pl.when(j == last)
    def _():
        qk, pv = _row_major_products(q_ref, kn_ref, vn_ref)
        _fox_block(qk, pv, crow_new_ref, kn_ref.shape[1], True, sc)
        _fox_finish(o_ref, sc)


def _fox_sample(qb, kb, vb, cache_k, cache_v, crow, crow_new, ccol):
    b, s, _ = qb.shape
    past = cache_k.shape[3]
    tk, tn = FOX_CACHE_BLOCK, kb.shape[1]
    assert past % tk == 0 and tn % LANES == 0 and tn >= s
    nk = past // tk
    new = lambda r: pl.BlockSpec((1, r, WIDTH), lambda bi, j: (bi, 0, 0))
    old = pl.BlockSpec((1, N_HEADS, HEAD_DIM, tk), lambda bi, j: (bi, 0, 0, jnp.minimum(j, nk - 1)))
    return pl.pallas_call(
        _fox_sample_kernel, grid=(b, nk + 1),
        in_specs=[new(s), old, old, new(tn), new(tn),
                  pl.BlockSpec((1, N_HEADS, tk), lambda bi, j: (bi, 0, jnp.minimum(j, nk - 1))),
                  pl.BlockSpec((1, N_HEADS, tn), lambda bi, j: (bi, 0, 0)),
                  pl.BlockSpec((1, s, N_HEADS), lambda bi, j: (bi, 0, 0))],
        out_specs=new(s), out_shape=jax.ShapeDtypeStruct((b, s, WIDTH), BF16),
        scratch_shapes=_fox_scratch(s, tk),
        compiler_params=_params("arbitrary", "arbitrary"), name="fox_sample",
    )(qb, cache_k, cache_v, kb, vb, crow, crow_new, ccol)


def _mid_kernel(x_ref, oa_ref, ob_ref, wo_ref, g2_ref, wq_ref, gq_ref, mk_ref, mv_ref, wmo_ref, o_ref,
                om_sc, *, seg):
    tm, d = x_ref.shape
    hd = d // N_MEM_HEADS
    x1 = x_ref[...] + _dot(oa_ref[...], wo_ref[:WIDTH, :]) + _dot(ob_ref[...], wo_ref[WIDTH:, :])
    h2 = _rms(x1, g2_ref[...]).astype(BF16)
    qm = _dot(h2, wq_ref[...])
    scale = hd ** -0.5 * LOG2E
    for hh in range(N_MEM_HEADS):
        cs = slice(hh * hd, (hh + 1) * hd)
        q = (_rms(qm[:, cs], gq_ref[...]) * scale).astype(BF16)
        for sg in range(tm // seg):
            rs = slice(sg * seg, (sg + 1) * seg)
            s = _dot_nt(q[rs], mk_ref[sg, :, cs])
            om_sc[rs, cs] = _softmax_pv(s, [mv_ref[sg, :, cs]]).astype(BF16)
    o_ref[...] = x1 + _dot(om_sc[...], wmo_ref[...])


def _mid(x2d, oa, ob, w_o, g2, w_mq, g_mq, mk, mv, w_mo, *, seq):
    n, d = x2d.shape
    tm = ROW_TILE
    seg = min(seq, tm)
    tiles_per_batch = seq // seg
    nb = tm // seg
    nmem = mk.shape[1]
    assert n % tm == 0 and seq % seg == 0 and tm % seg == 0
    rows = lambda w: pl.BlockSpec((tm, w), lambda i: (i, 0))
    full = lambda a: pl.BlockSpec(a.shape, lambda i: (0,) * a.ndim)
    mem = pl.BlockSpec((nb, nmem, d), lambda i: (i // tiles_per_batch, 0, 0))
    return pl.pallas_call(
        functools.partial(_mid_kernel, seg=seg), grid=(n // tm,),
        in_specs=[rows(d), rows(WIDTH), rows(WIDTH), full(w_o), full(g2), full(w_mq), full(g_mq), mem, mem,
                  full(w_mo)],
        out_specs=rows(d), out_shape=jax.ShapeDtypeStruct((n, d), F32),
        scratch_shapes=[pltpu.VMEM((tm, d), BF16)],
        compiler_params=_params("arbitrary"), name="mid",
    )(x2d, oa, ob, w_o, g2, w_mq, g_mq, mk, mv, w_mo)


def _ffn_kernel(x_ref, g3_ref, wup_ref, wc_ref, bc_ref, wd_ref, st_ref, o_ref, last_ref, carry_sc, y_sc,
                *, seg, tiles_per_batch, ff_chunk):
    tm, d = x_ref.shape
    dff = wd_ref.shape[0]
    i = pl.program_id(0)
    h3 = _rms(x_ref[...], g3_ref[...]).astype(BF16)
    if tiles_per_batch > 1:
        @pl.when(i == 0)
        def _():
            carry_sc[...] = jnp.zeros(carry_sc.shape, F32)
    rowid = lax.broadcasted_iota(jnp.int32, (SUBLANES, 1), 0)
    first_tile = (i % tiles_per_batch) == 0
    for c in range(dff // ff_chunk):
        cs = slice(c * ff_chunk, (c + 1) * ff_chunk)
        gate = _dot(h3, wup_ref[:, cs])
        val = _dot(h3, wup_ref[:, dff + c * ff_chunk:dff + (c + 1) * ff_chunk])
        w0, w1, w2 = wc_ref[0:1, cs], wc_ref[1:2, cs], wc_ref[2:3, cs]
        for sg in range(tm // seg):
            rows = slice(sg * seg, (sg + 1) * seg)
            g = gate[rows]
            if tiles_per_batch == 1:
                prev = st_ref[sg, :, cs]
            else:
                prev = jnp.where(first_tile, st_ref[0, :, cs], carry_sc[SUBLANES - 2:, cs])
            p2, p1 = prev[0:1], prev[1:2]
            s1 = pltpu.roll(g, 1, 0)
            s2 = pltpu.roll(g, 2, 0)
            top1 = jnp.where(rowid == 0, p1, s1[:SUBLANES])
            top2 = jnp.where(rowid == 0, p2, jnp.where(rowid == 1, p1, s2[:SUBLANES]))
            s1 = jnp.concatenate([top1, s1[SUBLANES:]], axis=0)
            s2 = jnp.concatenate([top2, s2[SUBLANES:]], axis=0)
            conv = s2 * w0 + s1 * w1 + g * w2 + bc_ref[:, cs]
            y_sc[rows, cs] = (conv * jax.nn.sigmoid(conv) * val[rows]).astype(BF16)
            last_ref[sg, :, cs] = g[seg - SUBLANES:]
        if tiles_per_batch > 1:
            carry_sc[:, cs] = gate[tm - SUBLANES:]
    o_ref[...] = x_ref[...] + _dot(y_sc[...], wd_ref[...])


def _conv_ffn(x2d, g3, w_up, w_conv, b_conv, w_down, state, *, seq, ff_chunk):
    n, d = x2d.shape
    dff = w_down.shape[0]
    tm = ROW_TILE
    seg = min(seq, tm)
    tiles_per_batch = seq // seg
    nb = tm // seg
    assert n % tm == 0 and seq % seg == 0 and tm % seg == 0 and dff % ff_chunk == 0 and seg >= SUBLANES
    rows = pl.BlockSpec((tm, d), lambda i: (i, 0))
    full = lambda a: pl.BlockSpec(a.shape, lambda i: (0,) * a.ndim)
    per_batch = lambda r: pl.BlockSpec((nb, r, dff), lambda i: (i // tiles_per_batch, 0, 0))
    return pl.pallas_call(
        functools.partial(_ffn_kernel, seg=seg, tiles_per_batch=tiles_per_batch, ff_chunk=ff_chunk),
        grid=(n // tm,),
        in_specs=[rows, full(g3), full(w_up), full(w_conv), full(b_conv), full(w_down), per_batch(CONV_W - 1)],
        out_specs=[rows, per_batch(SUBLANES)],
        out_shape=[jax.ShapeDtypeStruct((n, d), F32), jax.ShapeDtypeStruct((n // seq, SUBLANES, dff), F32)],
        scratch_shapes=[pltpu.VMEM((SUBLANES, dff), F32), pltpu.VMEM((tm, dff), BF16)],
        compiler_params=_params("arbitrary"), name="conv_ffn",
    )(x2d, g3, w_up, w_conv, b_conv, w_down, state)


def _row(v):
    return v.astype(F32).reshape(1, -1)


def _layer_weights(l, w_in, b_f, g_qa, g_ka, rel_bias, g_qb, g_kb, w_o, g_norm1, g_norm2, g_mem, w_mq, w_mkv,
                   g_mq, g_mk, w_mo, g_norm3, w_up, w_conv, b_conv, w_down):
    n_main = 6 * WIDTH
    tile = lambda g: jnp.tile(g[l].astype(F32), N_HEADS)
    blk = jnp.arange(MXU_COLS) // HEAD_DIM
    return dict(
        w_main=w_in[l][:, :n_main].astype(BF16),
        w_f=jnp.pad(w_in[l][:, n_main:].T, ((0, BF16_ROWS - N_HEADS), (0, 0))).astype(BF16),
        b_f=b_f[l].astype(F32).reshape(N_HEADS, 1),
        gains=jnp.stack([tile(g_qa), tile(g_ka), tile(g_qb), tile(g_kb)]),
        seg=(blk[:, None] == blk[None, :]).astype(BF16),
        rel=rel_bias[l],
        w_o=w_o[l].astype(BF16), g1=_row(g_norm1[l]), g2=_row(g_norm2[l]), g3=_row(g_norm3[l]),
        g_mem=_row(g_mem[l]), w_mq=w_mq[l].astype(BF16), w_mkv=w_mkv[l].astype(BF16),
        g_mq=_row(g_mq[l]), g_mk=_row(g_mk[l]), w_mo=w_mo[l].astype(BF16),
        w_up=w_up[l].astype(BF16), w_conv=w_conv[l].astype(F32), b_conv=_row(b_conv[l]),
        w_down=w_down[l].astype(BF16))


def _post_attention(x2d, oa, ob, mk, mv, state, w, *, seq, ff_chunk):
    n, d = x2d.shape
    x2 = _mid(x2d, oa.reshape(n, WIDTH), ob.reshape(n, WIDTH), w["w_o"], w["g2"], w["w_mq"], w["g_mq"],
              mk, mv, w["w_mo"], seq=seq)
    y, last = _conv_ffn(x2, w["g3"], w["w_up"], w["w_conv"], w["b_conv"], w["w_down"], state,
                        seq=seq, ff_chunk=ff_chunk)
    return y, last[:, SUBLANES - (CONV_W - 1):]


def _prompt_layer(x, mem, w, bias):
    b, t, d = x.shape
    la = min(BAND_PAST, t)
    assert la == ROW_TILE and t % ROW_TILE == 0
    qa, ka, va, qb, kb, vb, ka_tail, va_tail, kb_f, vb_f, logf = _in_proj(
        x.reshape(b * t, d), w["g1"], w["w_main"], w["w_f"], w["b_f"], w["gains"], w["seg"],
        tail_period=t // ROW_TILE)
    r3 = lambda a: a.reshape(b, t, WIDTH)
    oa = _band_prompt(r3(qa), r3(ka), r3(va), bias)
    logf = logf.reshape(N_HEADS, b, t).transpose(1, 0, 2)
    crow = _cumsum_lanes(logf.reshape(b * N_HEADS, t)).reshape(b, N_HEADS, t)
    logf = logf.transpose(0, 2, 1)
    ob = _fox_prompt(r3(qb), r3(kb), r3(vb), crow, crow.transpose(0, 2, 1))
    mk_f, mv_f, mk, mv = _mem_kv(mem, w["g_mem"], w["w_mkv"], w["g_mk"])
    dff = w["w_down"].shape[0]
    y, conv = _post_attention(x.reshape(b * t, d), oa, ob, mk, mv, jnp.zeros((b, CONV_W - 1, dff), F32), w,
                              seq=t, ff_chunk=dff // 11)
    heads = lambda a, n: a.reshape(b, n, N_HEADS, HEAD_DIM)
    mem_heads = lambda a: a.reshape(b, mem.shape[1], N_MEM_HEADS, d // N_MEM_HEADS)
    return (y.reshape(b, t, d), heads(ka_tail, la), heads(va_tail, la), heads(kb_f, t), heads(vb_f, t), logf,
            mem_heads(mk_f), mem_heads(mv_f), conv)


def _sample_layer(x, cache_a_k, cache_a_v, cache_b_k, cache_b_v, cache_b_logf, cache_mem_k, cache_mem_v,
                  state_conv, w, bias):
    b, s, d = x.shape
    past = cache_b_k.shape[1]
    assert ROW_TILE % s == 0 and (b * s) % ROW_TILE == 0
    qa, ka, va, qb, kb, vb, ka_f, va_f, kb_f, vb_f, logf = _in_proj(
        x.reshape(b * s, d), w["g1"], w["w_main"], w["w_f"], w["b_f"], w["gains"], w["seg"], tail_period=1)
    r3 = lambda a: a.reshape(b, s, WIDTH)
    flat = lambda a: a.reshape(a.shape[0], a.shape[1], -1)
    time_minor = lambda a: a.transpose(0, 2, 3, 1)
    oa = _band_sample(r3(qa), r3(ka), r3(va), time_minor(cache_a_k), time_minor(cache_a_v), bias)
    logf = logf.reshape(N_HEADS, b, s).transpose(1, 0, 2)
    total = past + s
    all_logf = jnp.concatenate([cache_b_logf.astype(F32).transpose(0, 2, 1), logf], axis=2)
    all_logf = jnp.pad(all_logf, ((0, 0), (0, 0), (0, -total % LANES)))
    logf = logf.transpose(0, 2, 1)
    cum = _cumsum_lanes(all_logf.reshape(b * N_HEADS, -1)).reshape(b, N_HEADS, -1)
    crow_new = cum[:, :, past:total]
    pad_rows = lambda a: jnp.pad(a, ((0, 0), (0, -s % LANES), (0, 0)))
    ob = _fox_sample(r3(qb), pad_rows(r3(kb)), pad_rows(r3(vb)), time_minor(cache_b_k), time_minor(cache_b_v),
                     cum[:, :, :past], jnp.pad(crow_new, ((0, 0), (0, 0), (0, -s % LANES))),
                     crow_new.transpose(0, 2, 1))
    dff = w["w_down"].shape[0]
    y, conv = _post_attention(x.reshape(b * s, d), oa, ob, flat(cache_mem_k).astype(BF16),
                              flat(cache_mem_v).astype(BF16), state_conv.astype(F32), w,
                              seq=s, ff_chunk=dff // 11)
    heads = lambda a: a.reshape(b, s, N_HEADS, HEAD_DIM)
    return y.reshape(b, s, d), heads(ka_f), heads(va_f), heads(kb_f), heads(vb_f), logf, conv


def kernel(x_prompt, x_sample, cache_a_k, cache_a_v, cache_b_k, cache_b_v, cache_b_logf, cache_mem_k, cache_mem_v, state_conv, mem_prompt, w_in, b_f, g_qa, g_ka, rel_bias, g_qb, g_kb, w_o, g_norm1, g_norm2, g_mem, w_mq, w_mkv, g_mq, g_mk, w_mo, g_norm3, w_up, w_conv, b_conv, w_down):
    depth = w_in.shape[0]
    xp, xs = x_prompt, x_sample
    outs_p, outs_s = [], []
    for l in range(depth):
        w = _layer_weights(l, w_in, b_f, g_qa, g_ka, rel_bias, g_qb, g_kb, w_o, g_norm1, g_norm2, g_mem, w_mq,
                           w_mkv, g_mq, g_mk, w_mo, g_norm3, w_up, w_conv, b_conv, w_down)
        bias = _band_bias(w["rel"])
        xp, *rest_p = _prompt_layer(xp, mem_prompt, w, bias)
        xs, *rest_s = _sample_layer(xs, cache_a_k[l], cache_a_v[l], cache_b_k[l], cache_b_v[l], cache_b_logf[l],
                                    cache_mem_k[l], cache_mem_v[l], state_conv[l], w, bias)
        outs_p.append(rest_p)
        outs_s.append(rest_s)
    stack = lambda outs: [jnp.stack(col, axis=0) for col in zip(*outs)]
    return (xp, xs, *stack(outs_p), *stack(outs_s))
```

```python
import functools

import jax
import jax.numpy as jnp
from jax import lax
from jax.experimental import pallas as pl
from jax.experimental.pallas import tpu as pltpu

F32 = jnp.float32
BF16 = jnp.bfloat16
EPS = 1e-6
NEG_INF = float("-inf")

HEAD_DIM = 64
N_HEADS = 8
WIDTH = N_HEADS * HEAD_DIM
CHUNK = 64
N_PREV_CHUNKS = 8
BAND_PAST = N_PREV_CHUNKS * CHUNK
REL_CLIP = 128
N_MEM_HEADS = 4
CONV_W = 3

ROW_TILE = 512
BAND_Q = 256
BAND_WIN = BAND_PAST + BAND_Q
FOX_BLOCK = 512
FOX_CACHE_BLOCK = 1024
FOX_BAND = 64
MXU_COLS = 256
LANES = 128
SUBLANES = 8
BF16_ROWS = 16
VMEM_LIMIT = 56 * 1024 * 1024
LOG2E = 1.4426950408889634


def _dot(a, b):
    return jnp.dot(a, b, preferred_element_type=F32)


def _dot_nt(a, b):
    return lax.dot_general(a, b, (((1,), (1,)), ((), ())), preferred_element_type=F32)


def _rms(x, g):
    return x * lax.rsqrt(jnp.mean(x * x, axis=-1, keepdims=True) + EPS) * g


def _split3(x):
    hi = x.astype(BF16)
    r1 = x - hi.astype(F32)
    mid = r1.astype(BF16)
    lo = (r1 - mid.astype(F32)).astype(BF16)
    return hi, mid, lo


def _store_heads(ref, x):
    rows = x.shape[0]
    for h in range(N_HEADS):
        ref[pl.ds(h, rows, stride=N_HEADS), :] = x[:, h * HEAD_DIM:(h + 1) * HEAD_DIM]


def _params(*sem):
    return pltpu.CompilerParams(dimension_semantics=sem, vmem_limit_bytes=VMEM_LIMIT)


def _inproj_kernel(x_ref, g1_ref, w_ref, wf_ref, bf_ref, gains_ref, seg_ref,
                   qa_ref, ka_ref, va_ref, qb_ref, kb_ref, vb_ref,
                   kaf_ref, vaf_ref, kbf_ref, vbf_ref, logf_ref, *, tail_period):
    h = _rms(x_ref[...], g1_ref[...]).astype(BF16)
    seg = seg_ref[...]

    def proj(g):
        return _dot(h, w_ref[:, g * WIDTH:(g + 1) * WIDTH])

    def head_norm(p, row):
        p2 = (p * p).astype(BF16)
        ss = jnp.concatenate([_dot(p2[:, :MXU_COLS], seg), _dot(p2[:, MXU_COLS:], seg)], axis=1)
        return p * lax.rsqrt(ss * (1.0 / HEAD_DIM) + EPS) * gains_ref[row:row + 1, :]

    is_tail = (pl.program_id(0) % tail_period) == tail_period - 1
    scale = HEAD_DIM ** -0.5 * LOG2E

    qa_ref[...] = (head_norm(proj(0), 0) * scale).astype(BF16)
    ka = head_norm(proj(1), 1)
    ka_ref[...] = ka.astype(BF16)
    va = proj(2)
    va_ref[...] = va.astype(BF16)

    @pl.when(is_tail)
    def _():
        _store_heads(kaf_ref.at[0], ka)
        _store_heads(vaf_ref.at[0], va)

    qb_ref[...] = (head_norm(proj(3), 2) * scale).astype(BF16)
    kb = head_norm(proj(4), 3)
    kb_ref[...] = kb.astype(BF16)
    _store_heads(kbf_ref, kb)
    vb = proj(5)
    vb_ref[...] = vb.astype(BF16)
    _store_heads(vbf_ref, vb)

    z = _dot_nt(wf_ref[...], h)[:N_HEADS] + bf_ref[...]
    logf_ref[...] = jnp.minimum(z, 0.0) - jnp.log1p(jnp.exp(-jnp.abs(z)))


def _in_proj(x2d, g1, w_main, w_f, b_f, gains, seg, *, tail_period):
    n, d = x2d.shape
    tm = ROW_TILE
    assert n % tm == 0 and (n // tm) % tail_period == 0
    n_tail = n // tm // tail_period
    row_bf = jax.ShapeDtypeStruct((n, WIDTH), BF16)
    head_f = jax.ShapeDtypeStruct((n * N_HEADS, HEAD_DIM), F32)
    tail_f = jax.ShapeDtypeStruct((n_tail, tm * N_HEADS, HEAD_DIM), F32)
    rows = pl.BlockSpec((tm, WIDTH), lambda i: (i, 0))
    head_rows = pl.BlockSpec((tm * N_HEADS, HEAD_DIM), lambda i: (i, 0))
    tail = pl.BlockSpec((1, tm * N_HEADS, HEAD_DIM), lambda i: (i // tail_period, 0, 0))
    full = lambda a: pl.BlockSpec(a.shape, lambda i: (0,) * a.ndim)
    return pl.pallas_call(
        functools.partial(_inproj_kernel, tail_period=tail_period),
        grid=(n // tm,),
        in_specs=[pl.BlockSpec((tm, d), lambda i: (i, 0)), full(g1), full(w_main), full(w_f),
                  full(b_f), full(gains), full(seg)],
        out_specs=[rows] * 6 + [tail, tail, head_rows, head_rows, pl.BlockSpec((N_HEADS, tm), lambda i: (0, i))],
        out_shape=[row_bf] * 6 + [tail_f, tail_f, head_f, head_f, jax.ShapeDtypeStruct((N_HEADS, n), F32)],
        compiler_params=_params("arbitrary"),
        name="in_proj",
    )(x2d, g1, w_main, w_f, b_f, gains, seg)


def _memkv_kernel(mem_ref, g_ref, w_ref, gk_ref, kf_ref, vf_ref, kb_ref, vb_ref):
    d = mem_ref.shape[-1]
    hd = d // N_MEM_HEADS
    h = _rms(mem_ref[0], g_ref[...]).astype(BF16)
    kv = _dot(h, w_ref[...])
    for hh in range(N_MEM_HEADS):
        k = _rms(kv[:, hh * hd:(hh + 1) * hd], gk_ref[...])
        kf_ref[0, :, hh * hd:(hh + 1) * hd] = k
        kb_ref[0, :, hh * hd:(hh + 1) * hd] = k.astype(BF16)
    v = kv[:, d:]
    vf_ref[0] = v
    vb_ref[0] = v.astype(BF16)


def _mem_kv(mem, g_mem, w_mkv, g_mk):
    b, n, d = mem.shape
    blk = pl.BlockSpec((1, n, d), lambda i: (i, 0, 0))
    full = lambda a: pl.BlockSpec(a.shape, lambda i: (0,) * a.ndim)
    f = jax.ShapeDtypeStruct((b, n, d), F32)
    h = jax.ShapeDtypeStruct((b, n, d), BF16)
    return pl.pallas_call(
        _memkv_kernel, grid=(b,),
        in_specs=[blk, full(g_mem), full(w_mkv), full(g_mk)],
        out_specs=[blk] * 4, out_shape=[f, f, h, h],
        compiler_params=_params("arbitrary"), name="mem_kv",
    )(mem, g_mem, w_mkv, g_mk)


def _cumsum_kernel(x_ref, o_ref):
    r, t = x_ref.shape
    k = lax.broadcasted_iota(jnp.int32, (LANES, LANES), 0)
    j = lax.broadcasted_iota(jnp.int32, (LANES, LANES), 1)
    tri = jnp.where(k <= j, 1.0, 0.0).astype(BF16)
    carry = jnp.zeros((r, 1), F32)
    for blk in range(t // LANES):
        hi, mid, lo = _split3(x_ref[:, blk * LANES:(blk + 1) * LANES])
        cs = _dot(hi, tri) + _dot(mid, tri) + _dot(lo, tri)
        o_ref[:, blk * LANES:(blk + 1) * LANES] = (cs + carry) * LOG2E
        carry = carry + cs[:, LANES - 1:LANES]


def _cumsum_lanes(x):
    assert x.shape[1] % LANES == 0
    return pl.pallas_call(_cumsum_kernel, out_shape=jax.ShapeDtypeStruct(x.shape, F32), name="cumsum")(x)


def _band_bias_kernel(tbl_ref, o_ref):
    nrel = tbl_ref.shape[1]
    qblk, win = o_ref.shape[1:]
    wrap = qblk + win
    jp = lax.broadcasted_iota(jnp.int32, (nrel, wrap), 1)
    r = lax.broadcasted_iota(jnp.int32, (nrel, wrap), 0)
    d = jnp.where(jp < win, jp, jp - wrap)
    rel = jnp.clip(BAND_PAST - d, -REL_CLIP, REL_CLIP) + REL_CLIP
    onehot = jnp.where(rel == r, 1.0, 0.0).astype(BF16)
    hi, mid, lo = _split3(tbl_ref[...])
    base = (_dot(hi, onehot) + _dot(mid, onehot) + _dot(lo, onehot)) * LOG2E
    qc = lax.broadcasted_iota(jnp.int32, (qblk, win), 0) // CHUNK
    kc = lax.broadcasted_iota(jnp.int32, (qblk, win), 1) // CHUNK
    inband = (kc >= qc) & (kc <= qc + N_PREV_CHUNKS)
    for h in range(N_HEADS):
        row = jnp.broadcast_to(base[h:h + 1, :], (qblk, wrap))
        toeplitz = pltpu.roll(row, 0, 1, stride=1, stride_axis=0)
        o_ref[h] = jnp.where(inband, toeplitz[:, :win], NEG_INF)


def _band_bias(rel_table):
    nrel = 2 * REL_CLIP + 1
    pad = -nrel % LANES
    tbl = jnp.pad(rel_table.astype(F32), ((0, 0), (0, pad)))
    return pl.pallas_call(
        _band_bias_kernel, out_shape=jax.ShapeDtypeStruct((N_HEADS, BAND_Q, BAND_WIN), F32),
        name="band_bias")(tbl)


def _softmax_pv(s, vs, transposed=None):
    transposed = transposed or (False,) * len(vs)
    m = jnp.max(s, axis=-1, keepdims=True)
    e = jnp.exp2(s - m)
    l = jnp.sum(e, axis=-1, keepdims=True)
    eb = e.astype(BF16)
    o, lo = None, 0
    for v, vt in zip(vs, transposed):
        keys = v.shape[1] if vt else v.shape[0]
        part = _dot_nt(eb[:, lo:lo + keys], v) if vt else _dot(eb[:, lo:lo + keys], v)
        o = part if o is None else o + part
        lo += keys
    return o * (1.0 / l)


def _band_prompt_kernel(q_ref, k0_ref, k1_ref, k2_ref, v0_ref, v1_ref, v2_ref, bias_ref, o_ref, s_sc, m_sc, qm_sc):
    i = pl.program_id(1)
    tq = q_ref.shape[1]
    k_refs, v_refs = (k0_ref, k1_ref, k2_ref), (v0_ref, v1_ref, v2_ref)
    bands = [slice(r, r + CHUNK) for r in range(0, tq, CHUNK)]
    n_groups = BAND_WIN // LANES
    per_block = tq // LANES
    group = lambda g: slice(g * LANES, (g + 1) * LANES)
    visible = lambda c: [g for g in range(n_groups) if 2 * g + 1 >= c and 2 * g <= c + N_PREV_CHUNKS]
    low = _low_half(tq)
    for h in range(N_HEADS):
        q_pair = _pair(q_ref, h)
        qm_sc[h] = jnp.where(low if h % 2 == 0 else ~low, q_pair, jnp.zeros_like(q_pair))

    def logits(early):
        for h in range(N_HEADS):
            qm = qm_sc[h]
            parts = [_dot_nt(qm, _pair(k_ref, h)) for k_ref in k_refs]
            for c, rows in enumerate(bands):
                mx = None
                for g in visible(c):
                    p, gp = divmod(g, per_block)
                    sg = parts[p][rows, group(gp)] + bias_ref[h, rows, group(g)]
                    if early and p < 2:
                        sg = jnp.where(i + p >= 2, sg, NEG_INF)
                    s_sc[h, rows, group(g)] = sg
                    mx = sg if mx is None else jnp.maximum(mx, sg)
                m_sc[h, rows] = jnp.broadcast_to(jnp.max(mx, axis=-1, keepdims=True), (CHUNK, LANES))

    values = functools.partial(_band_values, v_refs, o_ref, s_sc, m_sc, bands, visible)

    @pl.when(i >= 2)
    def _():
        logits(False)

    @pl.when(i < 2)
    def _():
        logits(True)

    values()


def _band_values(v_refs, o_ref, s_sc, m_sc, bands, visible):
    tq = o_ref.shape[1]
    n_groups = s_sc.shape[2] // LANES
    group = lambda g: slice(g * LANES, (g + 1) * LANES)
    for h in range(N_HEADS):
        p_bands, invs = [], []
        for c, rows in enumerate(bands):
            m = m_sc[h, rows]
            rs, ps = None, []
            for g in range(n_groups):
                if g in visible(c):
                    e = jnp.exp2(s_sc[h, rows, group(g)] - m)
                    ps.append(e.astype(BF16))
                    rs = e if rs is None else rs + e
                else:
                    ps.append(jnp.zeros((CHUNK, LANES), BF16))
            p_bands.append(jnp.concatenate(ps, axis=1))
            invs.append(1.0 / jnp.sum(rs, axis=-1, keepdims=True))
        p_all = jnp.concatenate(p_bands, axis=0)
        o = None
        for p, v_ref in enumerate(v_refs):
            part = _dot(p_all[:, p * tq:(p + 1) * tq], _pair(v_ref, h))
            o = part if o is None else o + part
        half = slice((h % 2) * HEAD_DIM, (h % 2 + 1) * HEAD_DIM)
        o_ref[0, :, h * HEAD_DIM:(h + 1) * HEAD_DIM] = (o * jnp.concatenate(invs, axis=0))[:, half].astype(BF16)


def _band_prompt(qa, ka, va, bias):
    b, t, _ = qa.shape
    tq = BAND_Q
    assert t % tq == 0
    blk = lambda back: pl.BlockSpec((1, tq, WIDTH), lambda bi, i: (bi, jnp.maximum(i - back, 0), 0))
    return pl.pallas_call(
        _band_prompt_kernel, grid=(b, t // tq),
        in_specs=[blk(0), blk(2), blk(1), blk(0), blk(2), blk(1), blk(0),
                  pl.BlockSpec(bias.shape, lambda bi, i: (0, 0, 0))],
        out_specs=blk(0), out_shape=jax.ShapeDtypeStruct((b, t, WIDTH), BF16),
        scratch_shapes=[pltpu.VMEM((N_HEADS, tq, BAND_WIN), F32), pltpu.VMEM((N_HEADS, tq, LANES), F32),
                        pltpu.VMEM((N_HEADS, tq, LANES), BF16)],
        compiler_params=_params("arbitrary", "arbitrary"), name="band_prompt",
    )(qa, ka, ka, ka, va, va, va, bias)


def _band_sample_kernel(q_ref, kc_ref, vc_ref, kn_ref, vn_ref, bias_ref, o_ref):
    for h in range(N_HEADS):
        hs = slice(h * HEAD_DIM, (h + 1) * HEAD_DIM)
        q = q_ref[0, :, hs]
        s = jnp.concatenate([_dot(q, kc_ref[0, h].astype(BF16)), _dot_nt(q, kn_ref[0, :, hs])], axis=1)
        s = s + bias_ref[h, :CHUNK, :BAND_PAST + CHUNK]
        o = _softmax_pv(s, [vc_ref[0, h].astype(BF16), vn_ref[0, :, hs]], transposed=(True, False))
        o_ref[0, :, hs] = o.astype(BF16)


def _band_sample(qa, ka, va, cache_k, cache_v, bias):
    b, s, _ = qa.shape
    la = cache_k.shape[3]
    assert s == CHUNK and la == BAND_PAST
    new = pl.BlockSpec((1, s, WIDTH), lambda bi: (bi, 0, 0))
    old = pl.BlockSpec((1, N_HEADS, HEAD_DIM, la), lambda bi: (bi, 0, 0, 0))
    return pl.pallas_call(
        _band_sample_kernel, grid=(b,),
        in_specs=[new, old, old, new, new, pl.BlockSpec(bias.shape, lambda bi: (0, 0, 0))],
        out_specs=new, out_shape=jax.ShapeDtypeStruct((b, s, WIDTH), BF16),
        compiler_params=_params("arbitrary"), name="band_sample",
    )(qa, cache_k, cache_v, ka, va, bias)


def _fox_scratch(tq, tk):
    rows = lambda w: pltpu.VMEM((N_HEADS, tq, w), F32)
    return [rows(tk), rows(LANES), rows(LANES), rows(LANES), rows(LANES),
            pltpu.VMEM((N_HEADS, tq, LANES), BF16), pltpu.VMEM((N_HEADS // 2, tq, LANES), F32)]


def _low_half(rows):
    return lax.broadcasted_iota(jnp.int32, (rows, LANES), 1) < HEAD_DIM


def _fox_init(q_ref, ccol_ref, sc):
    _, cq_sc, m_sc, _, l_sc, qm_sc, acc_sc = sc
    tq = cq_sc.shape[1]
    low = _low_half(tq)
    for h in range(N_HEADS):
        cq_sc[h] = jnp.broadcast_to(ccol_ref[0, :, h:h + 1], (tq, LANES))
        q_pair = q_ref[0, :, (h // 2) * LANES:(h // 2 + 1) * LANES]
        qm_sc[h] = jnp.where(low if h % 2 == 0 else ~low, q_pair, jnp.zeros_like(q_pair))
    m_sc[...] = jnp.full(m_sc.shape, NEG_INF, F32)
    l_sc[...] = jnp.zeros(l_sc.shape, F32)
    acc_sc[...] = jnp.zeros(acc_sc.shape, F32)


def _fox_block(qk, pv, crow_ref, tk, causal, sc):
    s_sc, cq_sc, m_sc, a_sc, l_sc, qm_sc, acc_sc = sc
    tq = s_sc.shape[1]
    band = min(tq, FOX_BAND)
    bands = [slice(r, r + band) for r in range(0, tq, band)]
    n_groups = tk // LANES
    group = lambda g: slice(g * LANES, (g + 1) * LANES)

    def visible(rows):
        if not causal:
            return [(g, False) for g in range(n_groups)]
        return [(g, (g + 1) * LANES - 1 > rows.start) for g in range(n_groups) if g * LANES < rows.stop]

    if causal:
        ahead = (lax.broadcasted_iota(jnp.int32, (band, LANES), 0)
                 - lax.broadcasted_iota(jnp.int32, (band, LANES), 1))
    for h in range(N_HEADS):
        s = qk(h, qm_sc[h])
        for rows in bands:
            cq = cq_sc[h, rows]
            mx = None
            for g, masked in visible(rows):
                sg = s[rows, group(g)] + (cq - crow_ref[0, h:h + 1, group(g)])
                if masked:
                    sg = jnp.where(ahead >= g * LANES - rows.start, sg, NEG_INF)
                s_sc[h, rows, group(g)] = sg
                mx = sg if mx is None else jnp.maximum(mx, sg)
            m_prev = m_sc[h, rows]
            m_new = jnp.maximum(m_prev, jnp.max(mx, axis=-1, keepdims=True))
            a_sc[h, rows] = jnp.exp2(m_prev - m_new)
            m_sc[h, rows] = m_new
    for h in range(N_HEADS):
        p_bands = []
        for rows in bands:
            seen = visible(rows)
            m_new = m_sc[h, rows]
            rs, ps = None, []
            for g, _ in seen:
                p = jnp.exp2(s_sc[h, rows, group(g)] - m_new)
                ps.append(p.astype(BF16))
                rs = p if rs is None else rs + p
            l_sc[h, rows] = a_sc[h, rows] * l_sc[h, rows] + rs
            ps += [jnp.zeros((band, LANES), BF16)] * (n_groups - len(seen))
            p_bands.append(jnp.concatenate(ps, axis=1))
        half = slice((h % 2) * HEAD_DIM, (h % 2 + 1) * HEAD_DIM)
        new = a_sc[h] * acc_sc[h // 2] + pv(h, jnp.concatenate(p_bands, axis=0))
        acc_sc[h // 2, :, half] = new[:, half]


def _fox_finish(o_ref, sc):
    l_sc, acc_sc = sc[4], sc[6]
    low = _low_half(acc_sc.shape[1])
    total = lambda h: jnp.sum(l_sc[h], axis=-1, keepdims=True)
    for g in range(N_HEADS // 2):
        inv = jnp.where(low, 1.0 / total(2 * g), 1.0 / total(2 * g + 1))
        o_ref[0, :, g * LANES:(g + 1) * LANES] = (acc_sc[g] * inv).astype(BF16)


def _head(ref, h):
    return ref[0, :, h * HEAD_DIM:(h + 1) * HEAD_DIM]


def _pair(ref, h):
    return ref[0, :, (h // 2) * LANES:(h // 2 + 1) * LANES]


def _row_major_products(k_ref, v_ref):
    return (lambda h, qm: _dot_nt(qm, _pair(k_ref, h))), (lambda h, p: _dot(p, _pair(v_ref, h)))


def _fox_prompt_kernel(qi_ref, kj_ref, q_ref, k_ref, v_ref, crow_ref, ccol_ref, o_ref, *sc):
    p = pl.program_id(1)
    qi, kj = qi_ref[p], kj_ref[p]
    tk = k_ref.shape[1]
    qk, pv = _row_major_products(k_ref, v_ref)

    @pl.when(kj == 0)
    def _():
        _fox_init(q_ref, ccol_ref, sc)

    @pl.when(kj < qi)
    def _():
        _fox_block(qk, pv, crow_ref, tk, False, sc)

    @pl.when(kj == qi)
    def _():
        _fox_block(qk, pv, crow_ref, tk, True, sc)
        _fox_finish(o_ref, sc)


def _fox_prompt(qb, kb, vb, crow, ccol):
    b, t, _ = qb.shape
    tq = FOX_BLOCK
    assert t % tq == 0
    pairs = [(i, j) for i in range(t // tq) for j in range(i + 1)]
    qi = jnp.array([p[0] for p in pairs], jnp.int32)
    kj = jnp.array([p[1] for p in pairs], jnp.int32)
    qblk = pl.BlockSpec((1, tq, WIDTH), lambda bi, p, qi, kj: (bi, qi[p], 0))
    kblk = pl.BlockSpec((1, tq, WIDTH), lambda bi, p, qi, kj: (bi, kj[p], 0))
    grid_spec = pltpu.PrefetchScalarGridSpec(
        num_scalar_prefetch=2, grid=(b, len(pairs)),
        in_specs=[qblk, kblk, kblk,
                  pl.BlockSpec((1, N_HEADS, tq), lambda bi, p, qi, kj: (bi, 0, kj[p])),
                  pl.BlockSpec((1, tq, N_HEADS), lambda bi, p, qi, kj: (bi, qi[p], 0))],
        out_specs=qblk, scratch_shapes=_fox_scratch(tq, tq))
    return pl.pallas_call(
        _fox_prompt_kernel, grid_spec=grid_spec, out_shape=jax.ShapeDtypeStruct((b, t, WIDTH), BF16),
        compiler_params=_params("arbitrary", "arbitrary"), name="fox_prompt",
    )(qi, kj, qb, kb, vb, crow, ccol)


def _fox_sample_kernel(q_ref, kc_ref, vc_ref, kn_ref, vn_ref, crow_ref, crow_new_ref, ccol_ref, o_ref, *sc):
    j = pl.program_id(1)
    last = pl.num_programs(1) - 1

    @pl.when(j == 0)
    def _():
        _fox_init(q_ref, ccol_ref, sc)

    def cached_pair(ref, h):
        g = h // 2
        return jnp.concatenate([ref[0, 2 * g], ref[0, 2 * g + 1]], axis=0).astype(BF16)

    @pl.when(j < last)
    def _():
        _fox_block(lambda h, qm: _dot(qm, cached_pair(kc_ref, h)),
                   lambda h, p: _dot_nt(p, cached_pair(vc_ref, h)), crow_ref, kc_ref.shape[3], False, sc)

    @pl.when(j == last)
    def _():
        qk, pv = _row_major_products(kn_ref, vn_ref)
        _fox_block(qk, pv, crow_new_ref, kn_ref.shape[1], True, sc)
        _fox_finish(o_ref, sc)


def _fox_sample(qb, kb, vb, cache_k, cache_v, crow, crow_new, ccol):
    b, s, _ = qb.shape
    past = cache_k.shape[3]
    tk, tn = FOX_CACHE_BLOCK, kb.shape[1]
    assert past % tk == 0 and tn % LANES == 0 and tn >= s
    nk = past // tk
    new = lambda r: pl.BlockSpec((1, r, WIDTH), lambda bi, j: (bi, 0, 0))
    old = pl.BlockSpec((1, N_HEADS, HEAD_DIM, tk), lambda bi, j: (bi, 0, 0, jnp.minimum(j, nk - 1)))
    return pl.pallas_call(
        _fox_sample_kernel, grid=(b, nk + 1),
        in_specs=[new(s), old, old, new(tn), new(tn),
                  pl.BlockSpec((1, N_HEADS, tk), lambda bi, j: (bi, 0, jnp.minimum(j, nk - 1))),
                  pl.BlockSpec((1, N_HEADS, tn), lambda bi, j: (bi, 0, 0)),
                  pl.BlockSpec((1, s, N_HEADS), lambda bi, j: (bi, 0, 0))],
        out_specs=new(s), out_shape=jax.ShapeDtypeStruct((b, s, WIDTH), BF16),
        scratch_shapes=_fox_scratch(s, tk),
        compiler_params=_params("arbitrary", "arbitrary"), name="fox_sample",
    )(qb, cache_k, cache_v, kb, vb, crow, crow_new, ccol)


def _mid_kernel(x_ref, oa_ref, ob_ref, wo_ref, g2_ref, wq_ref, gq_ref, mk_ref, mv_ref, wmo_ref, o_ref,
                om_sc, *, seg):
    tm, d = x_ref.shape
    hd = d // N_MEM_HEADS
    x1 = x_ref[...] + _dot(oa_ref[...], wo_ref[:WIDTH, :]) + _dot(ob_ref[...], wo_ref[WIDTH:, :])
    h2 = _rms(x1, g2_ref[...]).astype(BF16)
    qm = _dot(h2, wq_ref[...])
    scale = hd ** -0.5 * LOG2E
    for hh in range(N_MEM_HEADS):
        cs = slice(hh * hd, (hh + 1) * hd)
        q = (_rms(qm[:, cs], gq_ref[...]) * scale).astype(BF16)
        for sg in range(tm // seg):
            rs = slice(sg * seg, (sg + 1) * seg)
            s = _dot_nt(q[rs], mk_ref[sg, :, cs])
            om_sc[rs, cs] = _softmax_pv(s, [mv_ref[sg, :, cs]]).astype(BF16)
    o_ref[...] = x1 + _dot(om_sc[...], wmo_ref[...])


def _mid(x2d, oa, ob, w_o, g2, w_mq, g_mq, mk, mv, w_mo, *, seq):
    n, d = x2d.shape
    tm = ROW_TILE
    seg = min(seq, tm)
    tiles_per_batch = seq // seg
    nb = tm // seg
    nmem = mk.shape[1]
    assert n % tm == 0 and seq % seg == 0 and tm % seg == 0
    rows = lambda w: pl.BlockSpec((tm, w), lambda i: (i, 0))
    full = lambda a: pl.BlockSpec(a.shape, lambda i: (0,) * a.ndim)
    mem = pl.BlockSpec((nb, nmem, d), lambda i: (i // tiles_per_batch, 0, 0))
    return pl.pallas_call(
        functools.partial(_mid_kernel, seg=seg), grid=(n // tm,),
        in_specs=[rows(d), rows(WIDTH), rows(WIDTH), full(w_o), full(g2), full(w_mq), full(g_mq), mem, mem,
                  full(w_mo)],
        out_specs=rows(d), out_shape=jax.ShapeDtypeStruct((n, d), F32),
        scratch_shapes=[pltpu.VMEM((tm, d), BF16)],
        compiler_params=_params("arbitrary"), name="mid",
    )(x2d, oa, ob, w_o, g2, w_mq, g_mq, mk, mv, w_mo)


def _ffn_kernel(x_ref, g3_ref, wup_ref, wc_ref, bc_ref, wd_ref, st_ref, o_ref, last_ref, carry_sc, y_sc,
                *, seg, tiles_per_batch, ff_chunk):
    tm, d = x_ref.shape
    dff = wd_ref.shape[0]
    i = pl.program_id(0)
    h3 = _rms(x_ref[...], g3_ref[...]).astype(BF16)
    if tiles_per_batch > 1:
        @pl.when(i == 0)
        def _():
            carry_sc[...] = jnp.zeros(carry_sc.shape, F32)
    rowid = lax.broadcasted_iota(jnp.int32, (SUBLANES, 1), 0)
    first_tile = (i % tiles_per_batch) == 0
    for c in range(dff // ff_chunk):
        cs = slice(c * ff_chunk, (c + 1) * ff_chunk)
        gate = _dot(h3, wup_ref[:, cs])
        val = _dot(h3, wup_ref[:, dff + c * ff_chunk:dff + (c + 1) * ff_chunk])
        w0, w1, w2 = wc_ref[0:1, cs], wc_ref[1:2, cs], wc_ref[2:3, cs]
        for sg in range(tm // seg):
            rows = slice(sg * seg, (sg + 1) * seg)
            g = gate[rows]
            if tiles_per_batch == 1:
                prev = st_ref[sg, :, cs]
            else:
                prev = jnp.where(first_tile, st_ref[0, :, cs], carry_sc[SUBLANES - 2:, cs])
            p2, p1 = prev[0:1], prev[1:2]
            s1 = pltpu.roll(g, 1, 0)
            s2 = pltpu.roll(g, 2, 0)
            top1 = jnp.where(rowid == 0, p1, s1[:SUBLANES])
            top2 = jnp.where(rowid == 0, p2, jnp.where(rowid == 1, p1, s2[:SUBLANES]))
            s1 = jnp.concatenate([top1, s1[SUBLANES:]], axis=0)
            s2 = jnp.concatenate([top2, s2[SUBLANES:]], axis=0)
            conv = s2 * w0 + s1 * w1 + g * w2 + bc_ref[:, cs]
            y_sc[rows, cs] = (conv * jax.nn.sigmoid(conv) * val[rows]).astype(BF16)
            last_ref[sg, :, cs] = g[seg - SUBLANES:]
        if tiles_per_batch > 1:
            carry_sc[:, cs] = gate[tm - SUBLANES:]
    o_ref[...] = x_ref[...] + _dot(y_sc[...], wd_ref[...])


def _conv_ffn(x2d, g3, w_up, w_conv, b_conv, w_down, state, *, seq, ff_chunk):
    n, d = x2d.shape
    dff = w_down.shape[0]
    tm = ROW_TILE
    seg = min(seq, tm)
    tiles_per_batch = seq // seg
    nb = tm // seg
    assert n % tm == 0 and seq % seg == 0 and tm % seg == 0 and dff % ff_chunk == 0 and seg >= SUBLANES
    rows = pl.BlockSpec((tm, d), lambda i: (i, 0))
    full = lambda a: pl.BlockSpec(a.shape, lambda i: (0,) * a.ndim)
    per_batch = lambda r: pl.BlockSpec((nb, r, dff), lambda i: (i // tiles_per_batch, 0, 0))
    return pl.pallas_call(
        functools.partial(_ffn_kernel, seg=seg, tiles_per_batch=tiles_per_batch, ff_chunk=ff_chunk),
        grid=(n // tm,),
        in_specs=[rows, full(g3), full(w_up), full(w_conv), full(b_conv), full(w_down), per_batch(CONV_W - 1)],
        out_specs=[rows, per_batch(SUBLANES)],
        out_shape=[jax.ShapeDtypeStruct((n, d), F32), jax.ShapeDtypeStruct((n // seq, SUBLANES, dff), F32)],
        scratch_shapes=[pltpu.VMEM((SUBLANES, dff), F32), pltpu.VMEM((tm, dff), BF16)],
        compiler_params=_params("arbitrary"), name="conv_ffn",
    )(x2d, g3, w_up, w_conv, b_conv, w_down, state)


def _row(v):
    return v.astype(F32).reshape(1, -1)


def _layer_weights(l, w_in, b_f, g_qa, g_ka, rel_bias, g_qb, g_kb, w_o, g_norm1, g_norm2, g_mem, w_mq, w_mkv,
                   g_mq, g_mk, w_mo, g_norm3, w_up, w_conv, b_conv, w_down):
    n_main = 6 * WIDTH
    tile = lambda g: jnp.tile(g[l].astype(F32), N_HEADS)
    blk = jnp.arange(MXU_COLS) // HEAD_DIM
    return dict(
        w_main=w_in[l][:, :n_main].astype(BF16),
        w_f=jnp.pad(w_in[l][:, n_main:].T, ((0, BF16_ROWS - N_HEADS), (0, 0))).astype(BF16),
        b_f=b_f[l].astype(F32).reshape(N_HEADS, 1),
        gains=jnp.stack([tile(g_qa), tile(g_ka), tile(g_qb), tile(g_kb)]),
        seg=(blk[:, None] == blk[None, :]).astype(BF16),
        rel=rel_bias[l],
        w_o=w_o[l].astype(BF16), g1=_row(g_norm1[l]), g2=_row(g_norm2[l]), g3=_row(g_norm3[l]),
        g_mem=_row(g_mem[l]), w_mq=w_mq[l].astype(BF16), w_mkv=w_mkv[l].astype(BF16),
        g_mq=_row(g_mq[l]), g_mk=_row(g_mk[l]), w_mo=w_mo[l].astype(BF16),
        w_up=w_up[l].astype(BF16), w_conv=w_conv[l].astype(F32), b_conv=_row(b_conv[l]),
        w_down=w_down[l].astype(BF16))


def _post_attention(x2d, oa, ob, mk, mv, state, w, *, seq, ff_chunk):
    n, d = x2d.shape
    x2 = _mid(x2d, oa.reshape(n, WIDTH), ob.reshape(n, WIDTH), w["w_o"], w["g2"], w["w_mq"], w["g_mq"],
              mk, mv, w["w_mo"], seq=seq)
    y, last = _conv_ffn(x2, w["g3"], w["w_up"], w["w_conv"], w["b_conv"], w["w_down"], state,
                        seq=seq, ff_chunk=ff_chunk)
    return y, last[:, SUBLANES - (CONV_W - 1):]


def _prompt_layer(x, mem, w, bias):
    b, t, d = x.shape
    la = min(BAND_PAST, t)
    assert la == ROW_TILE and t % ROW_TILE == 0
    qa, ka, va, qb, kb, vb, ka_tail, va_tail, kb_f, vb_f, logf = _in_proj(
        x.reshape(b * t, d), w["g1"], w["w_main"], w["w_f"], w["b_f"], w["gains"], w["seg"],
        tail_period=t // ROW_TILE)
    r3 = lambda a: a.reshape(b, t, WIDTH)
    oa = _band_prompt(r3(qa), r3(ka), r3(va), bias)
    logf = logf.reshape(N_HEADS, b, t).transpose(1, 0, 2)
    crow = _cumsum_lanes(logf.reshape(b * N_HEADS, t)).reshape(b, N_HEADS, t)
    logf = logf.transpose(0, 2, 1)
    ob = _fox_prompt(r3(qb), r3(kb), r3(vb), crow, crow.transpose(0, 2, 1))
    mk_f, mv_f, mk, mv = _mem_kv(mem, w["g_mem"], w["w_mkv"], w["g_mk"])
    dff = w["w_down"].shape[0]
    y, conv = _post_attention(x.reshape(b * t, d), oa, ob, mk, mv, jnp.zeros((b, CONV_W - 1, dff), F32), w,
                              seq=t, ff_chunk=dff // 11)
    heads = lambda a, n: a.reshape(b, n, N_HEADS, HEAD_DIM)
    mem_heads = lambda a: a.reshape(b, mem.shape[1], N_MEM_HEADS, d // N_MEM_HEADS)
    return (y.reshape(b, t, d), heads(ka_tail, la), heads(va_tail, la), heads(kb_f, t), heads(vb_f, t), logf,
            mem_heads(mk_f), mem_heads(mv_f), conv)


def _sample_layer(x, cache_a_k, cache_a_v, cache_b_k, cache_b_v, cache_b_logf, cache_mem_k, cache_mem_v,
                  state_conv, w, bias):
    b, s, d = x.shape
    past = cache_b_k.shape[1]
    assert ROW_TILE % s == 0 and (b * s) % ROW_TILE == 0
    qa, ka, va, qb, kb, vb, ka_f, va_f, kb_f, vb_f, logf = _in_proj(
        x.reshape(b * s, d), w["g1"], w["w_main"], w["w_f"], w["b_f"], w["gains"], w["seg"], tail_period=1)
    r3 = lambda a: a.reshape(b, s, WIDTH)
    flat = lambda a: a.reshape(a.shape[0], a.shape[1], -1)
    time_minor = lambda a: a.transpose(0, 2, 3, 1)
    oa = _band_sample(r3(qa), r3(ka), r3(va), time_minor(cache_a_k), time_minor(cache_a_v), bias)
    logf = logf.reshape(N_HEADS, b, s).transpose(1, 0, 2)
    total = past + s
    all_logf = jnp.concatenate([cache_b_logf.astype(F32).transpose(0, 2, 1), logf], axis=2)
    all_logf = jnp.pad(all_logf, ((0, 0), (0, 0), (0, -total % LANES)))
    logf = logf.transpose(0, 2, 1)
    cum = _cumsum_lanes(all_logf.reshape(b * N_HEADS, -1)).reshape(b, N_HEADS, -1)
    crow_new = cum[:, :, past:total]
    pad_rows = lambda a: jnp.pad(a, ((0, 0), (0, -s % LANES), (0, 0)))
    ob = _fox_sample(r3(qb), pad_rows(r3(kb)), pad_rows(r3(vb)), time_minor(cache_b_k), time_minor(cache_b_v),
                     cum[:, :, :past], jnp.pad(crow_new, ((0, 0), (0, 0), (0, -s % LANES))),
                     crow_new.transpose(0, 2, 1))
    dff = w["w_down"].shape[0]
    y, conv = _post_attention(x.reshape(b * s, d), oa, ob, flat(cache_mem_k).astype(BF16),
                              flat(cache_mem_v).astype(BF16), state_conv.astype(F32), w,
                              seq=s, ff_chunk=dff // 11)
    heads = lambda a: a.reshape(b, s, N_HEADS, HEAD_DIM)
    return y.reshape(b, s, d), heads(ka_f), heads(va_f), heads(kb_f), heads(vb_f), logf, conv


def kernel(x_prompt, x_sample, cache_a_k, cache_a_v, cache_b_k, cache_b_v, cache_b_logf, cache_mem_k, cache_mem_v, state_conv, mem_prompt, w_in, b_f, g_qa, g_ka, rel_bias, g_qb, g_kb, w_o, g_norm1, g_norm2, g_mem, w_mq, w_mkv, g_mq, g_mk, w_mo, g_norm3, w_up, w_conv, b_conv, w_down):
    depth = w_in.shape[0]
    xp, xs = x_prompt, x_sample
    outs_p, outs_s = [], []
    for l in range(depth):
        w = _layer_weights(l, w_in, b_f, g_qa, g_ka, rel_bias, g_qb, g_kb, w_o, g_norm1, g_norm2, g_mem, w_mq,
                           w_mkv, g_mq, g_mk, w_mo, g_norm3, w_up, w_conv, b_conv, w_down)
        bias = _band_bias(w["rel"])
        xp, *rest_p = _prompt_layer(xp, mem_prompt, w, bias)
        xs, *rest_s = _sample_layer(xs, cache_a_k[l], cache_a_v[l], cache_b_k[l], cache_b_v[l], cache_b_logf[l],
                                    cache_mem_k[l], cache_mem_v[l], state_conv[l], w, bias)
        outs_p.append(rest_p)
        outs_s.append(rest_s)
    stack = lambda outs: [jnp.stack(col, axis=0) for col in zip(*outs)]
    return (xp, xs, *stack(outs_p), *stack(outs_s))
```

```python
import functools

import jax
import jax.numpy as jnp
from jax import lax
from jax.experimental import pallas as pl
from jax.experimental.pallas import tpu as pltpu

F32 = jnp.float32
BF16 = jnp.bfloat16
EPS = 1e-6
NEG_INF = float("-inf")

HEAD_DIM = 64
N_HEADS = 8
WIDTH = N_HEADS * HEAD_DIM
CHUNK = 64
N_PREV_CHUNKS = 8
BAND_PAST = N_PREV_CHUNKS * CHUNK
REL_CLIP = 128
N_MEM_HEADS = 4
CONV_W = 3

ROW_TILE = 512
FFN_TILE = 1024
BAND_Q = 256
BAND_WIN = BAND_PAST + BAND_Q
FOX_BLOCK = 512
FOX_CACHE_BLOCK = 1024
FOX_BAND = 64
MXU_COLS = 256
LANES = 128
SUBLANES = 8
BF16_ROWS = 16
VMEM_LIMIT = 56 * 1024 * 1024
LOG2E = 1.4426950408889634


def _dot(a, b):
    return jnp.dot(a, b, preferred_element_type=F32)


def _dot_nt(a, b):
    return lax.dot_general(a, b, (((1,), (1,)), ((), ())), preferred_element_type=F32)


def _rms(x, g):
    return x * lax.rsqrt(jnp.mean(x * x, axis=-1, keepdims=True) + EPS) * g


def _split3(x):
    hi = x.astype(BF16)
    r1 = x - hi.astype(F32)
    mid = r1.astype(BF16)
    lo = (r1 - mid.astype(F32)).astype(BF16)
    return hi, mid, lo


def _store_heads(ref, x):
    rows = x.shape[0]
    for h in range(N_HEADS):
        ref[pl.ds(h, rows, stride=N_HEADS), :] = x[:, h * HEAD_DIM:(h + 1) * HEAD_DIM]


def _resident(a):
    return pl.BlockSpec(a.shape, lambda i: (0,) * a.ndim, pipeline_mode=pl.Buffered(1))


def _params(*sem):
    return pltpu.CompilerParams(dimension_semantics=sem, vmem_limit_bytes=VMEM_LIMIT)


def _inproj_kernel(x_ref, g1_ref, w_ref, wf_ref, bf_ref, gains_ref, seg_ref,
                   qa_ref, ka_ref, va_ref, qb_ref, kb_ref, vb_ref,
                   kaf_ref, vaf_ref, kbf_ref, vbf_ref, logf_ref, *, tail_period, time_minor):
    h = _rms(x_ref[...], g1_ref[...]).astype(BF16)
    seg = seg_ref[...]

    def store_f32(ref, x):
        if time_minor:
            ref[...] = x.T
        else:
            _store_heads(ref, x)

    def proj(g):
        return _dot(h, w_ref[:, g * WIDTH:(g + 1) * WIDTH])

    def head_norm(p, row):
        p2 = (p * p).astype(BF16)
        ss = jnp.concatenate([_dot(p2[:, :MXU_COLS], seg), _dot(p2[:, MXU_COLS:], seg)], axis=1)
        return p * lax.rsqrt(ss * (1.0 / HEAD_DIM) + EPS) * gains_ref[row:row + 1, :]

    is_tail = (pl.program_id(0) % tail_period) == tail_period - 1
    scale = HEAD_DIM ** -0.5 * LOG2E

    qa_ref[...] = (head_norm(proj(0), 0) * scale).astype(BF16)
    ka = head_norm(proj(1), 1)
    ka_ref[...] = ka.astype(BF16)
    va = proj(2)
    va_ref[...] = va.astype(BF16)

    @pl.when(is_tail)
    def _():
        store_f32(kaf_ref.at[0], ka)
        store_f32(vaf_ref.at[0], va)

    qb_ref[...] = (head_norm(proj(3), 2) * scale).astype(BF16)
    kb = head_norm(proj(4), 3)
    kb_ref[...] = kb.astype(BF16)
    store_f32(kbf_ref.at[0], kb)
    vb = proj(5)
    vb_ref[...] = vb.astype(BF16)
    store_f32(vbf_ref.at[0], vb)

    z = _dot_nt(wf_ref[...], h)[:N_HEADS] + bf_ref[...]
    logf_ref[...] = jnp.minimum(z, 0.0) - jnp.log1p(jnp.exp(-jnp.abs(z)))


def _in_proj(x2d, g1, w_main, w_f, b_f, gains, seg, *, tail_period, time_minor):
    n, d = x2d.shape
    tm = ROW_TILE
    assert n % tm == 0 and (n // tm) % tail_period == 0
    n_tail = n // tm // tail_period
    row_bf = jax.ShapeDtypeStruct((n, WIDTH), BF16)
    rows = pl.BlockSpec((tm, WIDTH), lambda i: (i, 0))
    if time_minor:
        head_f = jax.ShapeDtypeStruct((n_tail, WIDTH, tm * tail_period), F32)
        tail_f = jax.ShapeDtypeStruct((n_tail, WIDTH, tm), F32)
        head_rows = pl.BlockSpec((1, WIDTH, tm), lambda i: (i // tail_period, 0, i % tail_period))
        tail = pl.BlockSpec((1, WIDTH, tm), lambda i: (i // tail_period, 0, 0))
    else:
        head_f = jax.ShapeDtypeStruct((n // tm, tm * N_HEADS, HEAD_DIM), F32)
        tail_f = jax.ShapeDtypeStruct((n_tail, tm * N_HEADS, HEAD_DIM), F32)
        head_rows = pl.BlockSpec((1, tm * N_HEADS, HEAD_DIM), lambda i: (i, 0, 0))
        tail = pl.BlockSpec((1, tm * N_HEADS, HEAD_DIM), lambda i: (i // tail_period, 0, 0))
    full = _resident
    return pl.pallas_call(
        functools.partial(_inproj_kernel, tail_period=tail_period, time_minor=time_minor),
        grid=(n // tm,),
        in_specs=[pl.BlockSpec((tm, d), lambda i: (i, 0)), full(g1), full(w_main), full(w_f),
                  full(b_f), full(gains), full(seg)],
        out_specs=[rows] * 6 + [tail, tail, head_rows, head_rows, pl.BlockSpec((N_HEADS, tm), lambda i: (0, i))],
        out_shape=[row_bf] * 6 + [tail_f, tail_f, head_f, head_f, jax.ShapeDtypeStruct((N_HEADS, n), F32)],
        compiler_params=_params("arbitrary"),
        name="in_proj",
    )(x2d, g1, w_main, w_f, b_f, gains, seg)


def _memkv_kernel(mem_ref, g_ref, w_ref, gk_ref, kf_ref, vf_ref, kb_ref, vb_ref):
    d = mem_ref.shape[-1]
    hd = d // N_MEM_HEADS
    h = _rms(mem_ref[0], g_ref[...]).astype(BF16)
    kv = _dot(h, w_ref[...])
    for hh in range(N_MEM_HEADS):
        k = _rms(kv[:, hh * hd:(hh + 1) * hd], gk_ref[...])
        kf_ref[0, :, hh * hd:(hh + 1) * hd] = k
        kb_ref[0, :, hh * hd:(hh + 1) * hd] = k.astype(BF16)
    v = kv[:, d:]
    vf_ref[0] = v
    vb_ref[0] = v.astype(BF16)


def _mem_kv(mem, g_mem, w_mkv, g_mk):
    b, n, d = mem.shape
    blk = pl.BlockSpec((1, n, d), lambda i: (i, 0, 0))
    full = _resident
    f = jax.ShapeDtypeStruct((b, n, d), F32)
    h = jax.ShapeDtypeStruct((b, n, d), BF16)
    return pl.pallas_call(
        _memkv_kernel, grid=(b,),
        in_specs=[blk, full(g_mem), full(w_mkv), full(g_mk)],
        out_specs=[blk] * 4, out_shape=[f, f, h, h],
        compiler_params=_params("arbitrary"), name="mem_kv",
    )(mem, g_mem, w_mkv, g_mk)


def _cumsum_kernel(x_ref, o_ref):
    r, t = x_ref.shape
    k = lax.broadcasted_iota(jnp.int32, (LANES, LANES), 0)
    j = lax.broadcasted_iota(jnp.int32, (LANES, LANES), 1)
    tri = jnp.where(k <= j, 1.0, 0.0).astype(BF16)
    carry = jnp.zeros((r, 1), F32)
    for blk in range(t // LANES):
        hi, mid, lo = _split3(x_ref[:, blk * LANES:(blk + 1) * LANES])
        cs = _dot(hi, tri) + _dot(mid, tri) + _dot(lo, tri)
        o_ref[:, blk * LANES:(blk + 1) * LANES] = (cs + carry) * LOG2E
        carry = carry + cs[:, LANES - 1:LANES]


def _cumsum_lanes(x):
    assert x.shape[1] % LANES == 0
    return pl.pallas_call(_cumsum_kernel, out_shape=jax.ShapeDtypeStruct(x.shape, F32), name="cumsum")(x)


def _band_bias_kernel(tbl_ref, o_ref):
    nrel = tbl_ref.shape[1]
    qblk, win = o_ref.shape[1:]
    wrap = qblk + win
    jp = lax.broadcasted_iota(jnp.int32, (nrel, wrap), 1)
    r = lax.broadcasted_iota(jnp.int32, (nrel, wrap), 0)
    d = jnp.where(jp < win, jp, jp - wrap)
    rel = jnp.clip(BAND_PAST - d, -REL_CLIP, REL_CLIP) + REL_CLIP
    onehot = jnp.where(rel == r, 1.0, 0.0).astype(BF16)
    hi, mid, lo = _split3(tbl_ref[...])
    base = (_dot(hi, onehot) + _dot(mid, onehot) + _dot(lo, onehot)) * LOG2E
    qc = lax.broadcasted_iota(jnp.int32, (qblk, win), 0) // CHUNK
    kc = lax.broadcasted_iota(jnp.int32, (qblk, win), 1) // CHUNK
    inband = (kc >= qc) & (kc <= qc + N_PREV_CHUNKS)
    for h in range(N_HEADS):
        row = jnp.broadcast_to(base[h:h + 1, :], (qblk, wrap))
        toeplitz = pltpu.roll(row, 0, 1, stride=1, stride_axis=0)
        o_ref[h] = jnp.where(inband, toeplitz[:, :win], NEG_INF)


def _band_bias(rel_table):
    nrel = 2 * REL_CLIP + 1
    pad = -nrel % LANES
    tbl = jnp.pad(rel_table.astype(F32), ((0, 0), (0, pad)))
    return pl.pallas_call(
        _band_bias_kernel, out_shape=jax.ShapeDtypeStruct((N_HEADS, BAND_Q, BAND_WIN), F32),
        name="band_bias")(tbl)


def _softmax_pv(s, vs, transposed=None):
    transposed = transposed or (False,) * len(vs)
    m = jnp.max(s, axis=-1, keepdims=True)
    e = jnp.exp2(s - m)
    l = jnp.sum(e, axis=-1, keepdims=True)
    eb = e.astype(BF16)
    o, lo = None, 0
    for v, vt in zip(vs, transposed):
        keys = v.shape[1] if vt else v.shape[0]
        part = _dot_nt(eb[:, lo:lo + keys], v) if vt else _dot(eb[:, lo:lo + keys], v)
        o = part if o is None else o + part
        lo += keys
    return o * (1.0 / l)


def _band_prompt_kernel(q_ref, k0_ref, k1_ref, k2_ref, v0_ref, v1_ref, v2_ref, bias_ref, o_ref, s_sc, m_sc, qm_sc):
    i = pl.program_id(1)
    tq = q_ref.shape[1]
    k_refs, v_refs = (k0_ref, k1_ref, k2_ref), (v0_ref, v1_ref, v2_ref)
    bands = [slice(r, r + CHUNK) for r in range(0, tq, CHUNK)]
    n_groups = BAND_WIN // LANES
    per_block = tq // LANES
    group = lambda g: slice(g * LANES, (g + 1) * LANES)
    visible = lambda c: [g for g in range(n_groups) if 2 * g + 1 >= c and 2 * g <= c + N_PREV_CHUNKS]
    low = _low_half(tq)
    for h in range(N_HEADS):
        q_pair = _pair(q_ref, h)
        qm_sc[h] = jnp.where(low if h % 2 == 0 else ~low, q_pair, jnp.zeros_like(q_pair))

    def logits(early):
        for h in range(N_HEADS):
            qm = qm_sc[h]
            parts = [_dot_nt(qm, _pair(k_ref, h)) for k_ref in k_refs]
            for c, rows in enumerate(bands):
                mx = None
                for g in visible(c):
                    p, gp = divmod(g, per_block)
                    sg = parts[p][rows, group(gp)] + bias_ref[h, rows, group(g)]
                    if early and p < 2:
                        sg = jnp.where(i + p >= 2, sg, NEG_INF)
                    s_sc[h, rows, group(g)] = sg
                    mx = sg if mx is None else jnp.maximum(mx, sg)
                m_sc[h, rows] = jnp.broadcast_to(jnp.max(mx, axis=-1, keepdims=True), (CHUNK, LANES))

    values = functools.partial(_band_values, v_refs, o_ref, s_sc, m_sc, bands, visible)

    @pl.when(i >= 2)
    def _():
        logits(False)

    @pl.when(i < 2)
    def _():
        logits(True)

    values()


def _band_values(v_refs, o_ref, s_sc, m_sc, bands, visible):
    tq = o_ref.shape[1]
    n_groups = s_sc.shape[2] // LANES
    group = lambda g: slice(g * LANES, (g + 1) * LANES)
    for h in range(N_HEADS):
        p_bands, invs = [], []
        for c, rows in enumerate(bands):
            m = m_sc[h, rows]
            rs, ps = None, []
            for g in range(n_groups):
                if g in visible(c):
                    e = jnp.exp2(s_sc[h, rows, group(g)] - m)
                    ps.append(e.astype(BF16))
                    rs = e if rs is None else rs + e
                else:
                    ps.append(jnp.zeros((CHUNK, LANES), BF16))
            p_bands.append(jnp.concatenate(ps, axis=1))
            invs.append(1.0 / jnp.sum(rs, axis=-1, keepdims=True))
        p_all = jnp.concatenate(p_bands, axis=0)
        o = None
        for p, v_ref in enumerate(v_refs):
            part = _dot(p_all[:, p * tq:(p + 1) * tq], _pair(v_ref, h))
            o = part if o is None else o + part
        half = slice((h % 2) * HEAD_DIM, (h % 2 + 1) * HEAD_DIM)
        o_ref[0, :, h * HEAD_DIM:(h + 1) * HEAD_DIM] = (o * jnp.concatenate(invs, axis=0))[:, half].astype(BF16)


def _band_prompt(qa, ka, va, bias):
    b, t, _ = qa.shape
    tq = BAND_Q
    assert t % tq == 0
    blk = lambda back: pl.BlockSpec((1, tq, WIDTH), lambda bi, i: (bi, jnp.maximum(i - back, 0), 0))
    return pl.pallas_call(
        _band_prompt_kernel, grid=(b, t // tq),
        in_specs=[blk(0), blk(2), blk(1), blk(0), blk(2), blk(1), blk(0),
                  pl.BlockSpec(bias.shape, lambda bi, i: (0, 0, 0))],
        out_specs=blk(0), out_shape=jax.ShapeDtypeStruct((b, t, WIDTH), BF16),
        scratch_shapes=[pltpu.VMEM((N_HEADS, tq, BAND_WIN), F32), pltpu.VMEM((N_HEADS, tq, LANES), F32),
                        pltpu.VMEM((N_HEADS, tq, LANES), BF16)],
        compiler_params=_params("arbitrary", "arbitrary"), name="band_prompt",
    )(qa, ka, ka, ka, va, va, va, bias)


def _band_sample_kernel(q_ref, kc_ref, vc_ref, kn_ref, vn_ref, bias_ref, o_ref):
    for h in range(N_HEADS):
        hs = slice(h * HEAD_DIM, (h + 1) * HEAD_DIM)
        q = q_ref[0, :, hs]
        s = jnp.concatenate([_dot(q, kc_ref[0, h].astype(BF16)), _dot_nt(q, kn_ref[0, :, hs])], axis=1)
        s = s + bias_ref[h, :CHUNK, :BAND_PAST + CHUNK]
        o = _softmax_pv(s, [vc_ref[0, h].astype(BF16), vn_ref[0, :, hs]], transposed=(True, False))
        o_ref[0, :, hs] = o.astype(BF16)


def _band_sample(qa, ka, va, cache_k, cache_v, bias):
    b, s, _ = qa.shape
    la = cache_k.shape[3]
    assert s == CHUNK and la == BAND_PAST
    new = pl.BlockSpec((1, s, WIDTH), lambda bi: (bi, 0, 0))
    old = pl.BlockSpec((1, N_HEADS, HEAD_DIM, la), lambda bi: (bi, 0, 0, 0))
    return pl.pallas_call(
        _band_sample_kernel, grid=(b,),
        in_specs=[new, old, old, new, new, pl.BlockSpec(bias.shape, lambda bi: (0, 0, 0))],
        out_specs=new, out_shape=jax.ShapeDtypeStruct((b, s, WIDTH), BF16),
        compiler_params=_params("arbitrary"), name="band_sample",
    )(qa, cache_k, cache_v, ka, va, bias)


def _fox_scratch(tq, tk):
    rows = lambda w: pltpu.VMEM((N_HEADS, tq, w), F32)
    return [rows(tk), rows(LANES), rows(LANES), rows(LANES), rows(LANES),
            pltpu.VMEM((N_HEADS, tq, LANES), BF16), pltpu.VMEM((N_HEADS // 2, tq, LANES), F32)]


def _low_half(rows):
    return lax.broadcasted_iota(jnp.int32, (rows, LANES), 1) < HEAD_DIM


def _fox_init(q_ref, ccol_ref, sc):
    _, cq_sc, m_sc, _, l_sc, qm_sc, acc_sc = sc
    tq = cq_sc.shape[1]
    low = _low_half(tq)
    for h in range(N_HEADS):
        cq_sc[h] = jnp.broadcast_to(ccol_ref[0, :, h:h + 1], (tq, LANES))
        q_pair = q_ref[0, :, (h // 2) * LANES:(h // 2 + 1) * LANES]
        qm_sc[h] = jnp.where(low if h % 2 == 0 else ~low, q_pair, jnp.zeros_like(q_pair))
    m_sc[...] = jnp.full(m_sc.shape, NEG_INF, F32)
    l_sc[...] = jnp.zeros(l_sc.shape, F32)
    acc_sc[...] = jnp.zeros(acc_sc.shape, F32)


def _fox_block(qk, pv, crow_ref, tk, causal, sc):
    s_sc, cq_sc, m_sc, a_sc, l_sc, qm_sc, acc_sc = sc
    tq = s_sc.shape[1]
    band = min(tq, FOX_BAND)
    bands = [slice(r, r + band) for r in range(0, tq, band)]
    n_groups = tk // LANES
    group = lambda g: slice(g * LANES, (g + 1) * LANES)

    def visible(rows):
        if not causal:
            return [(g, False) for g in range(n_groups)]
        return [(g, (g + 1) * LANES - 1 > rows.start) for g in range(n_groups) if g * LANES < rows.stop]

    if causal:
        ahead = (lax.broadcasted_iota(jnp.int32, (band, LANES), 0)
                 - lax.broadcasted_iota(jnp.int32, (band, LANES), 1))
    for h in range(N_HEADS):
        s = qk(h, qm_sc[h])
        for rows in bands:
            cq = cq_sc[h, rows]
            mx = None
            for g, masked in visible(rows):
                sg = s[rows, group(g)] + (cq - crow_ref[0, h:h + 1, group(g)])
                if masked:
                    sg = jnp.where(ahead >= g * LANES - rows.start, sg, NEG_INF)
                s_sc[h, rows, group(g)] = sg
                mx = sg if mx is None else jnp.maximum(mx, sg)
            m_prev = m_sc[h, rows]
            m_new = jnp.maximum(m_prev, jnp.max(mx, axis=-1, keepdims=True))
            a_sc[h, rows] = jnp.exp2(m_prev - m_new)
            m_sc[h, rows] = m_new
    for h in range(N_HEADS):
        p_bands = []
        for rows in bands:
            seen = visible(rows)
            m_new = m_sc[h, rows]
            rs, ps = None, []
            for g, _ in seen:
                p = jnp.exp2(s_sc[h, rows, group(g)] - m_new)
                ps.append(p.astype(BF16))
                rs = p if rs is None else rs + p
            l_sc[h, rows] = a_sc[h, rows] * l_sc[h, rows] + rs
            ps += [jnp.zeros((band, LANES), BF16)] * (n_groups - len(seen))
            p_bands.append(jnp.concatenate(ps, axis=1))
        half = slice((h % 2) * HEAD_DIM, (h % 2 + 1) * HEAD_DIM)
        new = a_sc[h] * acc_sc[h // 2] + pv(h, jnp.concatenate(p_bands, axis=0))
        acc_sc[h // 2, :, half] = new[:, half]


def _fox_finish(o_ref, sc):
    l_sc, acc_sc = sc[4], sc[6]
    low = _low_half(acc_sc.shape[1])
    total = lambda h: jnp.sum(l_sc[h], axis=-1, keepdims=True)
    for g in range(N_HEADS // 2):
        inv = jnp.where(low, 1.0 / total(2 * g), 1.0 / total(2 * g + 1))
        o_ref[0, :, g * LANES:(g + 1) * LANES] = (acc_sc[g] * inv).astype(BF16)


def _head(ref, h):
    return ref[0, :, h * HEAD_DIM:(h + 1) * HEAD_DIM]


def _pair(ref, h):
    return ref[0, :, (h // 2) * LANES:(h // 2 + 1) * LANES]


def _row_major_products(k_ref, v_ref):
    return (lambda h, qm: _dot_nt(qm, _pair(k_ref, h))), (lambda h, p: _dot(p, _pair(v_ref, h)))


def _fox_prompt_kernel(qi_ref, kj_ref, q_ref, k_ref, v_ref, crow_ref, ccol_ref, o_ref, *sc):
    p = pl.program_id(1)
    qi, kj = qi_ref[p], kj_ref[p]
    tk = k_ref.shape[1]
    qk, pv = _row_major_products(k_ref, v_ref)

    @pl.when(kj == 0)
    def _():
        _fox_init(q_ref, ccol_ref, sc)

    @pl.when(kj < qi)
    def _():
        _fox_block(qk, pv, crow_ref, tk, False, sc)

    @pl.when(kj == qi)
    def _():
        _fox_block(qk, pv, crow_ref, tk, True, sc)
        _fox_finish(o_ref, sc)


def _fox_prompt(qb, kb, vb, crow, ccol):
    b, t, _ = qb.shape
    tq = FOX_BLOCK
    assert t % tq == 0
    pairs = [(i, j) for i in range(t // tq) for j in range(i + 1)]
    qi = jnp.array([p[0] for p in pairs], jnp.int32)
    kj = jnp.array([p[1] for p in pairs], jnp.int32)
    qblk = pl.BlockSpec((1, tq, WIDTH), lambda bi, p, qi, kj: (bi, qi[p], 0))
    kblk = pl.BlockSpec((1, tq, WIDTH), lambda bi, p, qi, kj: (bi, kj[p], 0))
    grid_spec = pltpu.PrefetchScalarGridSpec(
        num_scalar_prefetch=2, grid=(b, len(pairs)),
        in_specs=[qblk, kblk, kblk,
                  pl.BlockSpec((1, N_HEADS, tq), lambda bi, p, qi, kj: (bi, 0, kj[p])),
                  pl.BlockSpec((1, tq, N_HEADS), lambda bi, p, qi, kj: (bi, qi[p], 0))],
        out_specs=qblk, scratch_shapes=_fox_scratch(tq, tq))
    return pl.pallas_call(
        _fox_prompt_kernel, grid_spec=grid_spec, out_shape=jax.ShapeDtypeStruct((b, t, WIDTH), BF16),
        compiler_params=_params("arbitrary", "arbitrary"), name="fox_prompt",
    )(qi, kj, qb, kb, vb, crow, ccol)


def _fox_sample_kernel(q_ref, kc_ref, vc_ref, kn_ref, vn_ref, crow_ref, crow_new_ref, ccol_ref, o_ref, *sc):
    j = pl.program_id(1)
    last = pl.num_programs(1) - 1

    @pl.when(j == 0)
    def _():
        _fox_init(q_ref, ccol_ref, sc)

    def cached_pair(ref, h):
        g = h // 2
        return ref[0, 2 * g:2 * g + 2].astype(BF16).reshape(2 * HEAD_DIM, ref.shape[3])

    @pl.when(j < last)
    def _():
        _fox_block(lambda h, qm: _dot(qm, cached_pair(kc_ref, h)),
                   lambda h, p: _dot_nt(p, cached_pair(vc_ref, h)), crow_ref, kc_ref.shape[3], False, sc)

    @pl.when(j == last)
    def _():
        qk, pv = _row_major_products(kn_ref, vn_ref)
        _fox_block(qk, pv, crow_new_ref, kn_ref.shape[1], True, sc)
        _fox_finish(o_ref, sc)


def _fox_sample(qb, kb, vb, cache_k, cache_v, crow, crow_new, ccol):
    b, s, _ = qb.shape
    past = cache_k.shape[3]
    tk, tn = FOX_CACHE_BLOCK, kb.shape[1]
    assert past % tk == 0 and tn % LANES == 0 and tn >= s
    nk = past // tk
    new = lambda r: pl.BlockSpec((1, r, WIDTH), lambda bi, j: (bi, 0, 0))
    old = pl.BlockSpec((1, N_HEADS, HEAD_DIM, tk), lambda bi, j: (bi, 0, 0, jnp.minimum(j, nk - 1)))
    return pl.pallas_call(
        _fox_sample_kernel, grid=(b, nk + 1),
        in_specs=[new(s), old, old, new(tn), new(tn),
                  pl.BlockSpec((1, N_HEADS, tk), lambda bi, j: (bi, 0, jnp.minimum(j, nk - 1))),
                  pl.BlockSpec((1, N_HEADS, tn), lambda bi, j: (bi, 0, 0)),
                  pl.BlockSpec((1, s, N_HEADS), lambda bi, j: (bi, 0, 0))],
        out_specs=new(s), out_shape=jax.ShapeDtypeStruct((b, s, WIDTH), BF16),
        scratch_shapes=_fox_scratch(s, tk),
        compiler_params=_params("arbitrary", "arbitrary"), name="fox_sample",
    )(qb, cache_k, cache_v, kb, vb, crow, crow_new, ccol)


def _mid_kernel(x_ref, oa_ref, ob_ref, wo_ref, g2_ref, wq_ref, gq_ref, mk_ref, mv_ref, wmo_ref, o_ref,
                om_sc, *, seg):
    tm, d = x_ref.shape
    hd = d // N_MEM_HEADS
    x1 = x_ref[...] + _dot(oa_ref[...], wo_ref[:WIDTH, :]) + _dot(ob_ref[...], wo_ref[WIDTH:, :])
    h2 = _rms(x1, g2_ref[...]).astype(BF16)
    qm = _dot(h2, wq_ref[...])
    scale = hd ** -0.5 * LOG2E
    for hh in range(N_MEM_HEADS):
        cs = slice(hh * hd, (hh + 1) * hd)
        q = (_rms(qm[:, cs], gq_ref[...]) * scale).astype(BF16)
        for sg in range(tm // seg):
            rs = slice(sg * seg, (sg + 1) * seg)
            s = _dot_nt(q[rs], mk_ref[sg, :, cs])
            om_sc[rs, cs] = _softmax_pv(s, [mv_ref[sg, :, cs]]).astype(BF16)
    o_ref[...] = x1 + _dot(om_sc[...], wmo_ref[...])


def _mid(x2d, oa, ob, w_o, g2, w_mq, g_mq, mk, mv, w_mo, *, seq):
    n, d = x2d.shape
    tm = ROW_TILE
    seg = min(seq, tm)
    tiles_per_batch = seq // seg
    nb = tm // seg
    nmem = mk.shape[1]
    assert n % tm == 0 and seq % seg == 0 and tm % seg == 0
    rows = lambda w: pl.BlockSpec((tm, w), lambda i: (i, 0))
    full = _resident
    mem = pl.BlockSpec((nb, nmem, d), lambda i: (i // tiles_per_batch, 0, 0))
    return pl.pallas_call(
        functools.partial(_mid_kernel, seg=seg), grid=(n // tm,),
        in_specs=[rows(d), rows(WIDTH), rows(WIDTH), full(w_o), full(g2), full(w_mq), full(g_mq), mem, mem,
                  full(w_mo)],
        out_specs=rows(d), out_shape=jax.ShapeDtypeStruct((n, d), F32),
        scratch_shapes=[pltpu.VMEM((tm, d), BF16)],
        compiler_params=_params("arbitrary"), name="mid",
    )(x2d, oa, ob, w_o, g2, w_mq, g_mq, mk, mv, w_mo)


def _ffn_kernel(x_ref, g3_ref, wup_ref, wc_ref, bc_ref, wd_ref, st_ref, o_ref, last_ref, carry_sc, y_sc,
                *, seg, tiles_per_batch, ff_chunk):
    tm, d = x_ref.shape
    dff = wd_ref.shape[0]
    i = pl.program_id(0)
    h3 = _rms(x_ref[...], g3_ref[...]).astype(BF16)
    if tiles_per_batch > 1:
        @pl.when(i == 0)
        def _():
            carry_sc[...] = jnp.zeros(carry_sc.shape, F32)
    rowid = lax.broadcasted_iota(jnp.int32, (SUBLANES, 1), 0)
    first_tile = (i % tiles_per_batch) == 0
    for c in range(dff // ff_chunk):
        cs = slice(c * ff_chunk, (c + 1) * ff_chunk)
        gate = _dot(h3, wup_ref[:, cs])
        val = _dot(h3, wup_ref[:, dff + c * ff_chunk:dff + (c + 1) * ff_chunk])
        w0, w1, w2 = wc_ref[0:1, cs], wc_ref[1:2, cs], wc_ref[2:3, cs]
        for sg in range(tm // seg):
            rows = slice(sg * seg, (sg + 1) * seg)
            g = gate[rows]
            if tiles_per_batch == 1:
                prev = st_ref[sg, :, cs]
            else:
                prev = jnp.where(first_tile, st_ref[0, :, cs], carry_sc[SUBLANES - 2:, cs])
            p2, p1 = prev[0:1], prev[1:2]
            s1 = pltpu.roll(g, 1, 0)
            s2 = pltpu.roll(g, 2, 0)
            top1 = jnp.where(rowid == 0, p1, s1[:SUBLANES])
            top2 = jnp.where(rowid == 0, p2, jnp.where(rowid == 1, p1, s2[:SUBLANES]))
            s1 = jnp.concatenate([top1, s1[SUBLANES:]], axis=0)
            s2 = jnp.concatenate([top2, s2[SUBLANES:]], axis=0)
            conv = s2 * w0 + s1 * w1 + g * w2 + bc_ref[:, cs]
            y_sc[rows, cs] = (conv * jax.nn.sigmoid(conv) * val[rows]).astype(BF16)
            last_ref[sg, :, cs] = g[seg - SUBLANES:]
        if tiles_per_batch > 1:
            carry_sc[:, cs] = gate[tm - SUBLANES:]
    o_ref[...] = x_ref[...] + _dot(y_sc[...], wd_ref[...])


def _conv_ffn(x2d, g3, w_up, w_conv, b_conv, w_down, state, *, seq, ff_chunk):
    n, d = x2d.shape
    dff = w_down.shape[0]
    tm = min(FFN_TILE, n)
    seg = min(seq, tm)
    tiles_per_batch = seq // seg
    nb = tm // seg
    assert n % tm == 0 and seq % seg == 0 and tm % seg == 0 and dff % ff_chunk == 0 and seg >= SUBLANES
    rows = pl.BlockSpec((tm, d), lambda i: (i, 0))
    full = _resident
    per_batch = lambda r: pl.BlockSpec((nb, r, dff), lambda i: (i // tiles_per_batch, 0, 0))
    return pl.pallas_call(
        functools.partial(_ffn_kernel, seg=seg, tiles_per_batch=tiles_per_batch, ff_chunk=ff_chunk),
        grid=(n // tm,),
        in_specs=[rows, full(g3), full(w_up), full(w_conv), full(b_conv), full(w_down), per_batch(CONV_W - 1)],
        out_specs=[rows, per_batch(SUBLANES)],
        out_shape=[jax.ShapeDtypeStruct((n, d), F32), jax.ShapeDtypeStruct((n // seq, SUBLANES, dff), F32)],
        scratch_shapes=[pltpu.VMEM((SUBLANES, dff), F32), pltpu.VMEM((tm, dff), BF16)],
        compiler_params=_params("arbitrary"), name="conv_ffn",
    )(x2d, g3, w_up, w_conv, b_conv, w_down, state)


def _row(v):
    return v.astype(F32).reshape(1, -1)


def _layer_weights(l, w_in, b_f, g_qa, g_ka, rel_bias, g_qb, g_kb, w_o, g_norm1, g_norm2, g_mem, w_mq, w_mkv,
                   g_mq, g_mk, w_mo, g_norm3, w_up, w_conv, b_conv, w_down):
    n_main = 6 * WIDTH
    tile = lambda g: jnp.tile(g[l].astype(F32), N_HEADS)
    blk = jnp.arange(MXU_COLS) // HEAD_DIM
    return dict(
        w_main=w_in[l][:, :n_main].astype(BF16),
        w_f=jnp.pad(w_in[l][:, n_main:].T, ((0, BF16_ROWS - N_HEADS), (0, 0))).astype(BF16),
        b_f=b_f[l].astype(F32).reshape(N_HEADS, 1),
        gains=jnp.stack([tile(g_qa), tile(g_ka), tile(g_qb), tile(g_kb)]),
        seg=(blk[:, None] == blk[None, :]).astype(BF16),
        rel=rel_bias[l],
        w_o=w_o[l].astype(BF16), g1=_row(g_norm1[l]), g2=_row(g_norm2[l]), g3=_row(g_norm3[l]),
        g_mem=_row(g_mem[l]), w_mq=w_mq[l].astype(BF16), w_mkv=w_mkv[l].astype(BF16),
        g_mq=_row(g_mq[l]), g_mk=_row(g_mk[l]), w_mo=w_mo[l].astype(BF16),
        w_up=w_up[l].astype(BF16), w_conv=w_conv[l].astype(F32), b_conv=_row(b_conv[l]),
        w_down=w_down[l].astype(BF16))


def _post_attention(x2d, oa, ob, mk, mv, state, w, *, seq, ff_chunk):
    n, d = x2d.shape
    x2 = _mid(x2d, oa.reshape(n, WIDTH), ob.reshape(n, WIDTH), w["w_o"], w["g2"], w["w_mq"], w["g_mq"],
              mk, mv, w["w_mo"], seq=seq)
    y, last = _conv_ffn(x2, w["g3"], w["w_up"], w["w_conv"], w["b_conv"], w["w_down"], state,
                        seq=seq, ff_chunk=ff_chunk)
    return y, last[:, SUBLANES - (CONV_W - 1):]


def _prompt_layer(x, mem, w, bias):
    b, t, d = x.shape
    la = min(BAND_PAST, t)
    assert la == ROW_TILE and t % ROW_TILE == 0
    qa, ka, va, qb, kb, vb, ka_tail, va_tail, kb_f, vb_f, logf = _in_proj(
        x.reshape(b * t, d), w["g1"], w["w_main"], w["w_f"], w["b_f"], w["gains"], w["seg"],
        tail_period=t // ROW_TILE, time_minor=True)
    r3 = lambda a: a.reshape(b, t, WIDTH)
    oa = _band_prompt(r3(qa), r3(ka), r3(va), bias)
    logf = logf.reshape(N_HEADS, b, t).transpose(1, 0, 2)
    crow = _cumsum_lanes(logf.reshape(b * N_HEADS, t)).reshape(b, N_HEADS, t)
    logf = logf.transpose(0, 2, 1)
    ob = _fox_prompt(r3(qb), r3(kb), r3(vb), crow, crow.transpose(0, 2, 1))
    mk_f, mv_f, mk, mv = _mem_kv(mem, w["g_mem"], w["w_mkv"], w["g_mk"])
    dff = w["w_down"].shape[0]
    y, conv = _post_attention(x.reshape(b * t, d), oa, ob, mk, mv, jnp.zeros((b, CONV_W - 1, dff), F32), w,
                              seq=t, ff_chunk=dff // 11)
    heads = lambda a, n: a.reshape(b, N_HEADS, HEAD_DIM, n).transpose(0, 3, 1, 2)
    mem_heads = lambda a: a.reshape(b, mem.shape[1], N_MEM_HEADS, d // N_MEM_HEADS)
    return (y.reshape(b, t, d), heads(ka_tail, la), heads(va_tail, la), heads(kb_f, t), heads(vb_f, t), logf,
            mem_heads(mk_f), mem_heads(mv_f), conv)


def _sample_layer(x, cache_a_k, cache_a_v, cache_b_k, cache_b_v, cache_b_logf, cache_mem_k, cache_mem_v,
                  state_conv, w, bias):
    b, s, d = x.shape
    past = cache_b_k.shape[1]
    assert ROW_TILE % s == 0 and (b * s) % ROW_TILE == 0
    qa, ka, va, qb, kb, vb, ka_f, va_f, kb_f, vb_f, logf = _in_proj(
        x.reshape(b * s, d), w["g1"], w["w_main"], w["w_f"], w["b_f"], w["gains"], w["seg"], tail_period=1,
        time_minor=False)
    r3 = lambda a: a.reshape(b, s, WIDTH)
    flat = lambda a: a.reshape(a.shape[0], a.shape[1], -1)
    time_minor = lambda a: a.transpose(0, 2, 3, 1)
    oa = _band_sample(r3(qa), r3(ka), r3(va), time_minor(cache_a_k), time_minor(cache_a_v), bias)
    logf = logf.reshape(N_HEADS, b, s).transpose(1, 0, 2)
    total = past + s
    all_logf = jnp.concatenate([cache_b_logf.astype(F32).transpose(0, 2, 1), logf], axis=2)
    all_logf = jnp.pad(all_logf, ((0, 0), (0, 0), (0, -total % LANES)))
    logf = logf.transpose(0, 2, 1)
    cum = _cumsum_lanes(all_logf.reshape(b * N_HEADS, -1)).reshape(b, N_HEADS, -1)
    crow_new = cum[:, :, past:total]
    pad_rows = lambda a: jnp.pad(a, ((0, 0), (0, -s % LANES), (0, 0)))
    ob = _fox_sample(r3(qb), pad_rows(r3(kb)), pad_rows(r3(vb)), time_minor(cache_b_k), time_minor(cache_b_v),
                     cum[:, :, :past], jnp.pad(crow_new, ((0, 0), (0, 0), (0, -s % LANES))),
                     crow_new.transpose(0, 2, 1))
    dff = w["w_down"].shape[0]
    y, conv = _post_attention(x.reshape(b * s, d), oa, ob, flat(cache_mem_k).astype(BF16),
                              flat(cache_mem_v).astype(BF16), state_conv.astype(F32), w,
                              seq=s, ff_chunk=dff // 11)
    heads = lambda a: a.reshape(b, s, N_HEADS, HEAD_DIM)
    return y.reshape(b, s, d), heads(ka_f), heads(va_f), heads(kb_f), heads(vb_f), logf, conv


def kernel(x_prompt, x_sample, cache_a_k, cache_a_v, cache_b_k, cache_b_v, cache_b_logf, cache_mem_k, cache_mem_v, state_conv, mem_prompt, w_in, b_f, g_qa, g_ka, rel_bias, g_qb, g_kb, w_o, g_norm1, g_norm2, g_mem, w_mq, w_mkv, g_mq, g_mk, w_mo, g_norm3, w_up, w_conv, b_conv, w_down):
    depth = w_in.shape[0]
    xp, xs = x_prompt, x_sample
    outs_p, outs_s = [], []
    for l in range(depth):
        w = _layer_weights(l, w_in, b_f, g_qa, g_ka, rel_bias, g_qb, g_kb, w_o, g_norm1, g_norm2, g_mem, w_mq,
                           w_mkv, g_mq, g_mk, w_mo, g_norm3, w_up, w_conv, b_conv, w_down)
        bias = _band_bias(w["rel"])
        xp, *rest_p = _prompt_layer(xp, mem_prompt, w, bias)
        xs, *rest_s = _sample_layer(xs, cache_a_k[l], cache_a_v[l], cache_b_k[l], cache_b_v[l], cache_b_logf[l],
                                    cache_mem_k[l], cache_mem_v[l], state_conv[l], w, bias)
        outs_p.append(rest_p)
        outs_s.append(rest_s)
    stack = lambda outs: [jnp.stack(col, axis=0) for col in zip(*outs)]
    return (xp, xs, *stack(outs_p), *stack(outs_s))
```

```python
import functools

import jax
import jax.numpy as jnp
from jax import lax
from jax.experimental import pallas as pl
from jax.experimental.pallas import tpu as pltpu

F32 = jnp.float32
BF16 = jnp.bfloat16
EPS = 1e-6
NEG_INF = float("-inf")

HEAD_DIM = 64
N_HEADS = 8
WIDTH = N_HEADS * HEAD_DIM
CHUNK = 64
N_PREV_CHUNKS = 8
BAND_PAST = N_PREV_CHUNKS * CHUNK
REL_CLIP = 128
N_MEM_HEADS = 4
CONV_W = 3

ROW_TILE = 512
FFN_TILE = 1024
BAND_Q = 256
BAND_WIN = BAND_PAST + BAND_Q
FOX_BLOCK = 512
FOX_CACHE_BLOCK = 4096
FOX_BAND = 64
MXU_COLS = 256
LANES = 128
SUBLANES = 8
BF16_ROWS = 16
VMEM_LIMIT = 56 * 1024 * 1024
LOG2E = 1.4426950408889634


def _dot(a, b):
    return jnp.dot(a, b, preferred_element_type=F32)


def _dot_nt(a, b):
    return lax.dot_general(a, b, (((1,), (1,)), ((), ())), preferred_element_type=F32)


def _rms(x, g):
    return x * lax.rsqrt(jnp.mean(x * x, axis=-1, keepdims=True) + EPS) * g


def _split3(x):
    hi = x.astype(BF16)
    r1 = x - hi.astype(F32)
    mid = r1.astype(BF16)
    lo = (r1 - mid.astype(F32)).astype(BF16)
    return hi, mid, lo


def _store_heads(ref, x):
    rows = x.shape[0]
    for h in range(N_HEADS):
        ref[pl.ds(h, rows, stride=N_HEADS), :] = x[:, h * HEAD_DIM:(h + 1) * HEAD_DIM]


def _resident(a):
    return pl.BlockSpec(a.shape, lambda i: (0,) * a.ndim, pipeline_mode=pl.Buffered(1))


def _params(*sem):
    return pltpu.CompilerParams(dimension_semantics=sem, vmem_limit_bytes=VMEM_LIMIT)


def _inproj_kernel(x_ref, g1_ref, w_ref, wf_ref, bf_ref, gains_ref, seg_ref,
                   qa_ref, ka_ref, va_ref, qb_ref, kb_ref, vb_ref,
                   kaf_ref, vaf_ref, kbf_ref, vbf_ref, logf_ref, *, tail_period, time_minor):
    h = _rms(x_ref[...], g1_ref[...]).astype(BF16)
    seg = seg_ref[...]

    def store_f32(ref, x):
        if time_minor:
            ref[...] = x.T
        else:
            _store_heads(ref, x)

    def proj(g):
        return _dot(h, w_ref[:, g * WIDTH:(g + 1) * WIDTH])

    def head_norm(p, row):
        p2 = (p * p).astype(BF16)
        ss = jnp.concatenate([_dot(p2[:, :MXU_COLS], seg), _dot(p2[:, MXU_COLS:], seg)], axis=1)
        return p * lax.rsqrt(ss * (1.0 / HEAD_DIM) + EPS) * gains_ref[row:row + 1, :]

    is_tail = (pl.program_id(0) % tail_period) == tail_period - 1
    scale = HEAD_DIM ** -0.5 * LOG2E

    qa_ref[...] = (head_norm(proj(0), 0) * scale).astype(BF16)
    ka = head_norm(proj(1), 1)
    ka_ref[...] = ka.astype(BF16)
    va = proj(2)
    va_ref[...] = va.astype(BF16)

    @pl.when(is_tail)
    def _():
        store_f32(kaf_ref.at[0], ka)
        store_f32(vaf_ref.at[0], va)

    qb_ref[...] = (head_norm(proj(3), 2) * scale).astype(BF16)
    kb = head_norm(proj(4), 3)
    kb_ref[...] = kb.astype(BF16)
    store_f32(kbf_ref.at[0], kb)
    vb = proj(5)
    vb_ref[...] = vb.astype(BF16)
    store_f32(vbf_ref.at[0], vb)

    z = _dot_nt(wf_ref[...], h)[:N_HEADS] + bf_ref[...]
    logf_ref[...] = jnp.minimum(z, 0.0) - jnp.log1p(jnp.exp(-jnp.abs(z)))


def _in_proj(x2d, g1, w_main, w_f, b_f, gains, seg, *, tail_period, time_minor):
    n, d = x2d.shape
    tm = ROW_TILE
    assert n % tm == 0 and (n // tm) % tail_period == 0
    n_tail = n // tm // tail_period
    row_bf = jax.ShapeDtypeStruct((n, WIDTH), BF16)
    rows = pl.BlockSpec((tm, WIDTH), lambda i: (i, 0))
    if time_minor:
        head_f = jax.ShapeDtypeStruct((n_tail, WIDTH, tm * tail_period), F32)
        tail_f = jax.ShapeDtypeStruct((n_tail, WIDTH, tm), F32)
        head_rows = pl.BlockSpec((1, WIDTH, tm), lambda i: (i // tail_period, 0, i % tail_period))
        tail = pl.BlockSpec((1, WIDTH, tm), lambda i: (i // tail_period, 0, 0))
    else:
        head_f = jax.ShapeDtypeStruct((n // tm, tm * N_HEADS, HEAD_DIM), F32)
        tail_f = jax.ShapeDtypeStruct((n_tail, tm * N_HEADS, HEAD_DIM), F32)
        head_rows = pl.BlockSpec((1, tm * N_HEADS, HEAD_DIM), lambda i: (i, 0, 0))
        tail = pl.BlockSpec((1, tm * N_HEADS, HEAD_DIM), lambda i: (i // tail_period, 0, 0))
    full = _resident
    return pl.pallas_call(
        functools.partial(_inproj_kernel, tail_period=tail_period, time_minor=time_minor),
        grid=(n // tm,),
        in_specs=[pl.BlockSpec((tm, d), lambda i: (i, 0)), full(g1), full(w_main), full(w_f),
                  full(b_f), full(gains), full(seg)],
        out_specs=[rows] * 6 + [tail, tail, head_rows, head_rows, pl.BlockSpec((N_HEADS, tm), lambda i: (0, i))],
        out_shape=[row_bf] * 6 + [tail_f, tail_f, head_f, head_f, jax.ShapeDtypeStruct((N_HEADS, n), F32)],
        compiler_params=_params("arbitrary"),
        name="in_proj",
    )(x2d, g1, w_main, w_f, b_f, gains, seg)


def _store_mem_rows(ref, x, hh):
    n = x.shape[0]
    for lg in range(2):
        ref[pl.ds(lg * N_MEM_HEADS + hh, n, stride=2 * N_MEM_HEADS), :] = x[:, lg * LANES:(lg + 1) * LANES]


def _memkv_kernel(mem_ref, g_ref, w_ref, gk_ref, kf_ref, vf_ref, kb_ref, vb_ref):
    d = mem_ref.shape[-1]
    hd = d // N_MEM_HEADS
    h = _rms(mem_ref[0], g_ref[...]).astype(BF16)
    kv = _dot(h, w_ref[...])
    for hh in range(N_MEM_HEADS):
        cs = slice(hh * hd, (hh + 1) * hd)
        k = _rms(kv[:, cs], gk_ref[...])
        v = kv[:, d + hh * hd:d + (hh + 1) * hd]
        _store_mem_rows(kf_ref.at[0], k, hh)
        _store_mem_rows(vf_ref.at[0], v, hh)
        kb_ref[0, :, cs] = k.astype(BF16)
        vb_ref[0, :, cs] = v.astype(BF16)


def _mem_kv(mem, g_mem, w_mkv, g_mk):
    b, n, d = mem.shape
    assert d == N_MEM_HEADS * 2 * LANES
    blk = pl.BlockSpec((1, n, d), lambda i: (i, 0, 0))
    blk_f = pl.BlockSpec((1, n * 2 * N_MEM_HEADS, LANES), lambda i: (i, 0, 0))
    full = _resident
    f = jax.ShapeDtypeStruct((b, n * 2 * N_MEM_HEADS, LANES), F32)
    h = jax.ShapeDtypeStruct((b, n, d), BF16)
    return pl.pallas_call(
        _memkv_kernel, grid=(b,),
        in_specs=[blk, full(g_mem), full(w_mkv), full(g_mk)],
        out_specs=[blk_f, blk_f, blk, blk], out_shape=[f, f, h, h],
        compiler_params=_params("arbitrary"), name="mem_kv",
    )(mem, g_mem, w_mkv, g_mk)


def _cumsum_kernel(x_ref, o_ref):
    r, t = x_ref.shape
    k = lax.broadcasted_iota(jnp.int32, (LANES, LANES), 0)
    j = lax.broadcasted_iota(jnp.int32, (LANES, LANES), 1)
    tri = jnp.where(k <= j, 1.0, 0.0).astype(BF16)
    carry = jnp.zeros((r, 1), F32)
    for blk in range(t // LANES):
        hi, mid, lo = _split3(x_ref[:, blk * LANES:(blk + 1) * LANES])
        cs = _dot(hi, tri) + _dot(mid, tri) + _dot(lo, tri)
        o_ref[:, blk * LANES:(blk + 1) * LANES] = (cs + carry) * LOG2E
        carry = carry + cs[:, LANES - 1:LANES]


def _cumsum_lanes(x):
    assert x.shape[1] % LANES == 0
    return pl.pallas_call(_cumsum_kernel, out_shape=jax.ShapeDtypeStruct(x.shape, F32), name="cumsum")(x)


def _band_bias_kernel(tbl_ref, o_ref):
    nrel = tbl_ref.shape[1]
    qblk, win = o_ref.shape[1:]
    wrap = qblk + win
    jp = lax.broadcasted_iota(jnp.int32, (nrel, wrap), 1)
    r = lax.broadcasted_iota(jnp.int32, (nrel, wrap), 0)
    d = jnp.where(jp < win, jp, jp - wrap)
    rel = jnp.clip(BAND_PAST - d, -REL_CLIP, REL_CLIP) + REL_CLIP
    onehot = jnp.where(rel == r, 1.0, 0.0).astype(BF16)
    hi, mid, lo = _split3(tbl_ref[...])
    base = (_dot(hi, onehot) + _dot(mid, onehot) + _dot(lo, onehot)) * LOG2E
    qc = lax.broadcasted_iota(jnp.int32, (qblk, win), 0) // CHUNK
    kc = lax.broadcasted_iota(jnp.int32, (qblk, win), 1) // CHUNK
    inband = (kc >= qc) & (kc <= qc + N_PREV_CHUNKS)
    for h in range(N_HEADS):
        row = jnp.broadcast_to(base[h:h + 1, :], (qblk, wrap))
        toeplitz = pltpu.roll(row, 0, 1, stride=1, stride_axis=0)
        o_ref[h] = jnp.where(inband, toeplitz[:, :win], NEG_INF)


def _band_bias(rel_table):
    nrel = 2 * REL_CLIP + 1
    pad = -nrel % LANES
    tbl = jnp.pad(rel_table.astype(F32), ((0, 0), (0, pad)))
    return pl.pallas_call(
        _band_bias_kernel, out_shape=jax.ShapeDtypeStruct((N_HEADS, BAND_Q, BAND_WIN), F32),
        name="band_bias")(tbl)


def _softmax_pv(s, vs, transposed=None):
    transposed = transposed or (False,) * len(vs)
    m = jnp.max(s, axis=-1, keepdims=True)
    e = jnp.exp2(s - m)
    l = jnp.sum(e, axis=-1, keepdims=True)
    eb = e.astype(BF16)
    o, lo = None, 0
    for v, vt in zip(vs, transposed):
        keys = v.shape[1] if vt else v.shape[0]
        part = _dot_nt(eb[:, lo:lo + keys], v) if vt else _dot(eb[:, lo:lo + keys], v)
        o = part if o is None else o + part
        lo += keys
    return o * (1.0 / l)


def _band_prompt_kernel(q_ref, k0_ref, k1_ref, k2_ref, v0_ref, v1_ref, v2_ref, bias_ref, o_ref, s_sc, m_sc, qm_sc):
    i = pl.program_id(1)
    tq = q_ref.shape[1]
    k_refs, v_refs = (k0_ref, k1_ref, k2_ref), (v0_ref, v1_ref, v2_ref)
    bands = [slice(r, r + CHUNK) for r in range(0, tq, CHUNK)]
    n_groups = BAND_WIN // LANES
    per_block = tq // LANES
    group = lambda g: slice(g * LANES, (g + 1) * LANES)
    visible = lambda c: [g for g in range(n_groups) if 2 * g + 1 >= c and 2 * g <= c + N_PREV_CHUNKS]
    low = _low_half(tq)
    for h in range(N_HEADS):
        q_pair = _pair(q_ref, h)
        qm_sc[h] = jnp.where(low if h % 2 == 0 else ~low, q_pair, jnp.zeros_like(q_pair))

    def logits(early):
        for h in range(N_HEADS):
            qm = qm_sc[h]
            parts = [_dot_nt(qm, _pair(k_ref, h)) for k_ref in k_refs]
            for c, rows in enumerate(bands):
                mx = None
                for g in visible(c):
                    p, gp = divmod(g, per_block)
                    sg = parts[p][rows, group(gp)] + bias_ref[h, rows, group(g)]
                    if early and p < 2:
                        sg = jnp.where(i + p >= 2, sg, NEG_INF)
                    s_sc[h, rows, group(g)] = sg
                    mx = sg if mx is None else jnp.maximum(mx, sg)
                m_sc[h, rows] = jnp.broadcast_to(jnp.max(mx, axis=-1, keepdims=True), (CHUNK, LANES))

    values = functools.partial(_band_values, v_refs, o_ref, s_sc, m_sc, bands, visible)

    @pl.when(i >= 2)
    def _():
        logits(False)

    @pl.when(i < 2)
    def _():
        logits(True)

    values()


def _band_values(v_refs, o_ref, s_sc, m_sc, bands, visible):
    tq = o_ref.shape[1]
    n_groups = s_sc.shape[2] // LANES
    group = lambda g: slice(g * LANES, (g + 1) * LANES)
    for h in range(N_HEADS):
        p_bands, invs = [], []
        for c, rows in enumerate(bands):
            m = m_sc[h, rows]
            rs, ps = None, []
            for g in range(n_groups):
                if g in visible(c):
                    e = jnp.exp2(s_sc[h, rows, group(g)] - m)
                    ps.append(e.astype(BF16))
                    rs = e if rs is None else rs + e
                else:
                    ps.append(jnp.zeros((CHUNK, LANES), BF16))
            p_bands.append(jnp.concatenate(ps, axis=1))
            invs.append(1.0 / jnp.sum(rs, axis=-1, keepdims=True))
        p_all = jnp.concatenate(p_bands, axis=0)
        o = None
        for p, v_ref in enumerate(v_refs):
            part = _dot(p_all[:, p * tq:(p + 1) * tq], _pair(v_ref, h))
            o = part if o is None else o + part
        half = slice((h % 2) * HEAD_DIM, (h % 2 + 1) * HEAD_DIM)
        o_ref[0, :, h * HEAD_DIM:(h + 1) * HEAD_DIM] = (o * jnp.concatenate(invs, axis=0))[:, half].astype(BF16)


def _band_prompt(qa, ka, va, bias):
    b, t, _ = qa.shape
    tq = BAND_Q
    assert t % tq == 0
    blk = lambda back: pl.BlockSpec((1, tq, WIDTH), lambda bi, i: (bi, jnp.maximum(i - back, 0), 0))
    return pl.pallas_call(
        _band_prompt_kernel, grid=(b, t // tq),
        in_specs=[blk(0), blk(2), blk(1), blk(0), blk(2), blk(1), blk(0),
                  pl.BlockSpec(bias.shape, lambda bi, i: (0, 0, 0))],
        out_specs=blk(0), out_shape=jax.ShapeDtypeStruct((b, t, WIDTH), BF16),
        scratch_shapes=[pltpu.VMEM((N_HEADS, tq, BAND_WIN), F32), pltpu.VMEM((N_HEADS, tq, LANES), F32),
                        pltpu.VMEM((N_HEADS, tq, LANES), BF16)],
        compiler_params=_params("arbitrary", "arbitrary"), name="band_prompt",
    )(qa, ka, ka, ka, va, va, va, bias)


def _band_sample_kernel(q_ref, kc_ref, vc_ref, kn_ref, vn_ref, bias_ref, o_ref):
    for h in range(N_HEADS):
        hs = slice(h * HEAD_DIM, (h + 1) * HEAD_DIM)
        q = q_ref[0, :, hs]
        s = jnp.concatenate([_dot(q, kc_ref[0, h].astype(BF16)), _dot_nt(q, kn_ref[0, :, hs])], axis=1)
        s = s + bias_ref[h, :CHUNK, :BAND_PAST + CHUNK]
        o = _softmax_pv(s, [vc_ref[0, h].astype(BF16), vn_ref[0, :, hs]], transposed=(True, False))
        o_ref[0, :, hs] = o.astype(BF16)


def _band_sample(qa, ka, va, cache_k, cache_v, bias):
    b, s, _ = qa.shape
    la = cache_k.shape[3]
    assert s == CHUNK and la == BAND_PAST
    new = pl.BlockSpec((1, s, WIDTH), lambda bi: (bi, 0, 0))
    old = pl.BlockSpec((1, N_HEADS, HEAD_DIM, la), lambda bi: (bi, 0, 0, 0))
    return pl.pallas_call(
        _band_sample_kernel, grid=(b,),
        in_specs=[new, old, old, new, new, pl.BlockSpec(bias.shape, lambda bi: (0, 0, 0))],
        out_specs=new, out_shape=jax.ShapeDtypeStruct((b, s, WIDTH), BF16),
        compiler_params=_params("arbitrary"), name="band_sample",
    )(qa, cache_k, cache_v, ka, va, bias)


def _fox_scratch(tq, tk):
    rows = lambda w: pltpu.VMEM((N_HEADS, tq, w), F32)
    return [rows(tk), rows(LANES), rows(LANES), rows(LANES), rows(LANES),
            pltpu.VMEM((N_HEADS, tq, LANES), BF16), pltpu.VMEM((N_HEADS // 2, tq, LANES), F32)]


def _low_half(rows):
    return lax.broadcasted_iota(jnp.int32, (rows, LANES), 1) < HEAD_DIM


def _fox_init(q_ref, ccol_ref, sc):
    _, cq_sc, m_sc, _, l_sc, qm_sc, acc_sc = sc
    tq = cq_sc.shape[1]
    low = _low_half(tq)
    for h in range(N_HEADS):
        cq_sc[h] = jnp.broadcast_to(ccol_ref[0, :, h:h + 1], (tq, LANES))
        q_pair = q_ref[0, :, (h // 2) * LANES:(h // 2 + 1) * LANES]
        qm_sc[h] = jnp.where(low if h % 2 == 0 else ~low, q_pair, jnp.zeros_like(q_pair))
    m_sc[...] = jnp.full(m_sc.shape, NEG_INF, F32)
    l_sc[...] = jnp.zeros(l_sc.shape, F32)
    acc_sc[...] = jnp.zeros(acc_sc.shape, F32)


def _fox_block(qk, pv, crow_ref, tk, causal, sc):
    s_sc, cq_sc, m_sc, a_sc, l_sc, qm_sc, acc_sc = sc
    tq = s_sc.shape[1]
    band = min(tq, FOX_BAND)
    bands = [slice(r, r + band) for r in range(0, tq, band)]
    n_groups = tk // LANES
    group = lambda g: slice(g * LANES, (g + 1) * LANES)

    def visible(rows):
        if not causal:
            return [(g, False) for g in range(n_groups)]
        return [(g, (g + 1) * LANES - 1 > rows.start) for g in range(n_groups) if g * LANES < rows.stop]

    if causal:
        ahead = (lax.broadcasted_iota(jnp.int32, (band, LANES), 0)
                 - lax.broadcasted_iota(jnp.int32, (band, LANES), 1))
    for h in range(N_HEADS):
        s = qk(h, qm_sc[h])
        for rows in bands:
            cq = cq_sc[h, rows]
            mx = None
            for g, masked in visible(rows):
                sg = s[rows, group(g)] + (cq - crow_ref[0, h:h + 1, group(g)])
                if masked:
                    sg = jnp.where(ahead >= g * LANES - rows.start, sg, NEG_INF)
                s_sc[h, rows, group(g)] = sg
                mx = sg if mx is None else jnp.maximum(mx, sg)
            m_prev = m_sc[h, rows]
            m_new = jnp.maximum(m_prev, jnp.max(mx, axis=-1, keepdims=True))
            a_sc[h, rows] = jnp.exp2(m_prev - m_new)
            m_sc[h, rows] = m_new
    for h in range(N_HEADS):
        p_bands = []
        for rows in bands:
            seen = visible(rows)
            m_new = m_sc[h, rows]
            rs, ps = None, []
            for g, _ in seen:
                p = jnp.exp2(s_sc[h, rows, group(g)] - m_new)
                ps.append(p.astype(BF16))
                rs = p if rs is None else rs + p
            l_sc[h, rows] = a_sc[h, rows] * l_sc[h, rows] + rs
            ps += [jnp.zeros((band, LANES), BF16)] * (n_groups - len(seen))
            p_bands.append(jnp.concatenate(ps, axis=1))
        half = slice((h % 2) * HEAD_DIM, (h % 2 + 1) * HEAD_DIM)
        new = a_sc[h] * acc_sc[h // 2] + pv(h, jnp.concatenate(p_bands, axis=0))
        acc_sc[h // 2, :, half] = new[:, half]


def _fox_finish(o_ref, sc):
    l_sc, acc_sc = sc[4], sc[6]
    low = _low_half(acc_sc.shape[1])
    total = lambda h: jnp.sum(l_sc[h], axis=-1, keepdims=True)
    for g in range(N_HEADS // 2):
        inv = jnp.where(low, 1.0 / total(2 * g), 1.0 / total(2 * g + 1))
        o_ref[0, :, g * LANES:(g + 1) * LANES] = (acc_sc[g] * inv).astype(BF16)


def _head(ref, h):
    return ref[0, :, h * HEAD_DIM:(h + 1) * HEAD_DIM]


def _pair(ref, h):
    return ref[0, :, (h // 2) * LANES:(h // 2 + 1) * LANES]


def _row_major_products(k_ref, v_ref):
    return (lambda h, qm: _dot_nt(qm, _pair(k_ref, h))), (lambda h, p: _dot(p, _pair(v_ref, h)))


def _fox_prompt_kernel(qi_ref, kj_ref, q_ref, k_ref, v_ref, crow_ref, ccol_ref, o_ref, *sc):
    p = pl.program_id(1)
    qi, kj = qi_ref[p], kj_ref[p]
    tk = k_ref.shape[1]
    qk, pv = _row_major_products(k_ref, v_ref)

    for first in (True, False):
        started = (kj == 0) if first else (kj > 0)

        @pl.when(started & (kj < qi))
        def _():
            if first:
                _fox_init(q_ref, ccol_ref, sc)
            _fox_block(qk, pv, crow_ref, tk, False, sc)

        @pl.when(started & (kj == qi))
        def _():
            if first:
                _fox_init(q_ref, ccol_ref, sc)
            _fox_block(qk, pv, crow_ref, tk, True, sc)
            _fox_finish(o_ref, sc)


def _fox_prompt(qb, kb, vb, crow, ccol):
    b, t, _ = qb.shape
    tq = FOX_BLOCK
    assert t % tq == 0
    pairs = [(i, j) for i in range(t // tq) for j in range(i + 1)]
    qi = jnp.array([p[0] for p in pairs], jnp.int32)
    kj = jnp.array([p[1] for p in pairs], jnp.int32)
    qblk = pl.BlockSpec((1, tq, WIDTH), lambda bi, p, qi, kj: (bi, qi[p], 0))
    kblk = pl.BlockSpec((1, tq, WIDTH), lambda bi, p, qi, kj: (bi, kj[p], 0))
    grid_spec = pltpu.PrefetchScalarGridSpec(
        num_scalar_prefetch=2, grid=(b, len(pairs)),
        in_specs=[qblk, kblk, kblk,
                  pl.BlockSpec((1, N_HEADS, tq), lambda bi, p, qi, kj: (bi, 0, kj[p])),
                  pl.BlockSpec((1, tq, N_HEADS), lambda bi, p, qi, kj: (bi, qi[p], 0))],
        out_specs=qblk, scratch_shapes=_fox_scratch(tq, tq))
    return pl.pallas_call(
        _fox_prompt_kernel, grid_spec=grid_spec, out_shape=jax.ShapeDtypeStruct((b, t, WIDTH), BF16),
        compiler_params=_params("arbitrary", "arbitrary"), name="fox_prompt",
    )(qi, kj, qb, kb, vb, crow, ccol)


def _fox_sample_kernel(q_ref, kc_ref, vc_ref, kn_ref, vn_ref, crow_ref, crow_new_ref, ccol_ref, o_ref, *sc):
    j = pl.program_id(1)
    last = pl.num_programs(1) - 1

    @pl.when(j == 0)
    def _():
        _fox_init(q_ref, ccol_ref, sc)

    def cached_pair(ref, h):
        g = h // 2
        return ref[0, 2 * g:2 * g + 2].astype(BF16).reshape(2 * HEAD_DIM, ref.shape[3])

    @pl.when(j < last)
    def _():
        _fox_block(lambda h, qm: _dot(qm, cached_pair(kc_ref, h)),
                   lambda h, p: _dot_nt(p, cached_pair(vc_ref, h)), crow_ref, kc_ref.shape[3], False, sc)

    @pl.when(j == last)
    def _():
        qk, pv = _row_major_products(kn_ref, vn_ref)
        _fox_block(qk, pv, crow_new_ref, kn_ref.shape[1], True, sc)
        _fox_finish(o_ref, sc)


def _fox_sample(qb, kb, vb, cache_k, cache_v, crow, crow_new, ccol):
    b, s, _ = qb.shape
    past = cache_k.shape[3]
    tk, tn = min(FOX_CACHE_BLOCK, past), kb.shape[1]
    assert past % tk == 0 and tn % LANES == 0 and tn >= s
    nk = past // tk
    new = lambda r: pl.BlockSpec((1, r, WIDTH), lambda bi, j: (bi, 0, 0))
    old = pl.BlockSpec((1, N_HEADS, HEAD_DIM, tk), lambda bi, j: (bi, 0, 0, jnp.minimum(j, nk - 1)))
    return pl.pallas_call(
        _fox_sample_kernel, grid=(b, nk + 1),
        in_specs=[new(s), old, old, new(tn), new(tn),
                  pl.BlockSpec((1, N_HEADS, tk), lambda bi, j: (bi, 0, jnp.minimum(j, nk - 1))),
                  pl.BlockSpec((1, N_HEADS, tn), lambda bi, j: (bi, 0, 0)),
                  pl.BlockSpec((1, s, N_HEADS), lambda bi, j: (bi, 0, 0))],
        out_specs=new(s), out_shape=jax.ShapeDtypeStruct((b, s, WIDTH), BF16),
        scratch_shapes=_fox_scratch(s, tk),
        compiler_params=_params("arbitrary", "arbitrary"), name="fox_sample",
    )(qb, cache_k, cache_v, kb, vb, crow, crow_new, ccol)


def _mid_kernel(x_ref, oa_ref, ob_ref, wo_ref, g2_ref, wq_ref, gq_ref, mk_ref, mv_ref, wmo_ref, o_ref,
                om_sc, *, seg):
    tm, d = x_ref.shape
    hd = d // N_MEM_HEADS
    x1 = x_ref[...] + _dot(oa_ref[...], wo_ref[:WIDTH, :]) + _dot(ob_ref[...], wo_ref[WIDTH:, :])
    h2 = _rms(x1, g2_ref[...]).astype(BF16)
    qm = _dot(h2, wq_ref[...])
    scale = hd ** -0.5 * LOG2E
    for hh in range(N_MEM_HEADS):
        cs = slice(hh * hd, (hh + 1) * hd)
        q = (_rms(qm[:, cs], gq_ref[...]) * scale).astype(BF16)
        def mem_head(ref, sg):
            if ref.shape[2] == d:
                return ref[sg, :, cs]
            rows = lambda lg: ref[sg, pl.ds(lg * N_MEM_HEADS + hh, ref.shape[1] // (2 * N_MEM_HEADS),
                                            stride=2 * N_MEM_HEADS), :]
            return jnp.concatenate([rows(0), rows(1)], axis=1).astype(BF16)

        for sg in range(tm // seg):
            rs = slice(sg * seg, (sg + 1) * seg)
            s = _dot_nt(q[rs], mem_head(mk_ref, sg))
            om_sc[rs, cs] = _softmax_pv(s, [mem_head(mv_ref, sg)]).astype(BF16)
    o_ref[...] = x1 + _dot(om_sc[...], wmo_ref[...])


def _cache_mem_rows(x):
    b, n, nh, hd = x.shape
    assert nh == N_MEM_HEADS and hd == 2 * LANES
    return x.reshape(b, n, nh, 2, LANES).transpose(0, 1, 3, 2, 4).reshape(b, n * 2 * nh, LANES)


def _mid(x2d, oa, ob, w_o, g2, w_mq, g_mq, mk, mv, w_mo, *, seq):
    n, d = x2d.shape
    tm = ROW_TILE
    seg = min(seq, tm)
    tiles_per_batch = seq // seg
    nb = tm // seg
    assert n % tm == 0 and seq % seg == 0 and tm % seg == 0
    rows = lambda w: pl.BlockSpec((tm, w), lambda i: (i, 0))
    full = _resident
    mem = pl.BlockSpec((nb,) + mk.shape[1:], lambda i: (i // tiles_per_batch, 0, 0))
    return pl.pallas_call(
        functools.partial(_mid_kernel, seg=seg), grid=(n // tm,),
        in_specs=[rows(d), rows(WIDTH), rows(WIDTH), full(w_o), full(g2), full(w_mq), full(g_mq), mem, mem,
                  full(w_mo)],
        out_specs=rows(d), out_shape=jax.ShapeDtypeStruct((n, d), F32),
        scratch_shapes=[pltpu.VMEM((tm, d), BF16)],
        compiler_params=_params("arbitrary"), name="mid",
    )(x2d, oa, ob, w_o, g2, w_mq, g_mq, mk, mv, w_mo)


def _ffn_kernel(x_ref, g3_ref, wup_ref, wc_ref, bc_ref, wd_ref, st_ref, o_ref, last_ref, carry_sc, y_sc,
                *, seg, tiles_per_batch, ff_chunk):
    tm, d = x_ref.shape
    dff = wd_ref.shape[0]
    i = pl.program_id(0)
    h3 = _rms(x_ref[...], g3_ref[...]).astype(BF16)
    if tiles_per_batch > 1:
        @pl.when(i == 0)
        def _():
            carry_sc[...] = jnp.zeros(carry_sc.shape, F32)
    rowid = lax.broadcasted_iota(jnp.int32, (SUBLANES, 1), 0)
    first_tile = (i % tiles_per_batch) == 0
    for c in range(dff // ff_chunk):
        cs = slice(c * ff_chunk, (c + 1) * ff_chunk)
        gate = _dot(h3, wup_ref[:, cs])
        val = _dot(h3, wup_ref[:, dff + c * ff_chunk:dff + (c + 1) * ff_chunk])
        w0, w1, w2 = wc_ref[0:1, cs], wc_ref[1:2, cs], wc_ref[2:3, cs]
        for sg in range(tm // seg):
            rows = slice(sg * seg, (sg + 1) * seg)
            g = gate[rows]
            if tiles_per_batch == 1:
                prev = st_ref[sg, :, cs]
            else:
                prev = jnp.where(first_tile, st_ref[0, :, cs], carry_sc[SUBLANES - 2:, cs])
            p2, p1 = prev[0:1], prev[1:2]
            s1 = pltpu.roll(g, 1, 0)
            s2 = pltpu.roll(g, 2, 0)
            top1 = jnp.where(rowid == 0, p1, s1[:SUBLANES])
            top2 = jnp.where(rowid == 0, p2, jnp.where(rowid == 1, p1, s2[:SUBLANES]))
            s1 = jnp.concatenate([top1, s1[SUBLANES:]], axis=0)
            s2 = jnp.concatenate([top2, s2[SUBLANES:]], axis=0)
            conv = s2 * w0 + s1 * w1 + g * w2 + bc_ref[:, cs]
            y_sc[rows, cs] = (conv * jax.nn.sigmoid(conv) * val[rows]).astype(BF16)
            last_ref[sg, :, cs] = g[seg - SUBLANES:]
        if tiles_per_batch > 1:
            carry_sc[:, cs] = gate[tm - SUBLANES:]
    o_ref[...] = x_ref[...] + _dot(y_sc[...], wd_ref[...])


def _conv_ffn(x2d, g3, w_up, w_conv, b_conv, w_down, state, *, seq, ff_chunk):
    n, d = x2d.shape
    dff = w_down.shape[0]
    tm = min(FFN_TILE, n)
    seg = min(seq, tm)
    tiles_per_batch = seq // seg
    nb = tm // seg
    assert n % tm == 0 and seq % seg == 0 and tm % seg == 0 and dff % ff_chunk == 0 and seg >= SUBLANES
    rows = pl.BlockSpec((tm, d), lambda i: (i, 0))
    full = _resident
    per_batch = lambda r: pl.BlockSpec((nb, r, dff), lambda i: (i // tiles_per_batch, 0, 0))
    return pl.pallas_call(
        functools.partial(_ffn_kernel, seg=seg, tiles_per_batch=tiles_per_batch, ff_chunk=ff_chunk),
        grid=(n // tm,),
        in_specs=[rows, full(g3), full(w_up), full(w_conv), full(b_conv), full(w_down), per_batch(CONV_W - 1)],
        out_specs=[rows, per_batch(SUBLANES)],
        out_shape=[jax.ShapeDtypeStruct((n, d), F32), jax.ShapeDtypeStruct((n // seq, SUBLANES, dff), F32)],
        scratch_shapes=[pltpu.VMEM((SUBLANES, dff), F32), pltpu.VMEM((tm, dff), BF16)],
        compiler_params=_params("arbitrary"), name="conv_ffn",
    )(x2d, g3, w_up, w_conv, b_conv, w_down, state)


def _row(v):
    return v.astype(F32).reshape(1, -1)


def _layer_weights(l, w_in, b_f, g_qa, g_ka, rel_bias, g_qb, g_kb, w_o, g_norm1, g_norm2, g_mem, w_mq, w_mkv,
                   g_mq, g_mk, w_mo, g_norm3, w_up, w_conv, b_conv, w_down):
    n_main = 6 * WIDTH
    tile = lambda g: jnp.tile(g[l].astype(F32), N_HEADS)
    blk = jnp.arange(MXU_COLS) // HEAD_DIM
    return dict(
        w_main=w_in[l][:, :n_main].astype(BF16),
        w_f=jnp.pad(w_in[l][:, n_main:].T, ((0, BF16_ROWS - N_HEADS), (0, 0))).astype(BF16),
        b_f=b_f[l].astype(F32).reshape(N_HEADS, 1),
        gains=jnp.stack([tile(g_qa), tile(g_ka), tile(g_qb), tile(g_kb)]),
        seg=(blk[:, None] == blk[None, :]).astype(BF16),
        rel=rel_bias[l],
        w_o=w_o[l].astype(BF16), g1=_row(g_norm1[l]), g2=_row(g_norm2[l]), g3=_row(g_norm3[l]),
        g_mem=_row(g_mem[l]), w_mq=w_mq[l].astype(BF16), w_mkv=w_mkv[l].astype(BF16),
        g_mq=_row(g_mq[l]), g_mk=_row(g_mk[l]), w_mo=w_mo[l].astype(BF16),
        w_up=w_up[l].astype(BF16), w_conv=w_conv[l].astype(F32), b_conv=_row(b_conv[l]),
        w_down=w_down[l].astype(BF16))


def _post_attention(x2d, oa, ob, mk, mv, state, w, *, seq, ff_chunk):
    n, d = x2d.shape
    x2 = _mid(x2d, oa.reshape(n, WIDTH), ob.reshape(n, WIDTH), w["w_o"], w["g2"], w["w_mq"], w["g_mq"],
              mk, mv, w["w_mo"], seq=seq)
    y, last = _conv_ffn(x2, w["g3"], w["w_up"], w["w_conv"], w["b_conv"], w["w_down"], state,
                        seq=seq, ff_chunk=ff_chunk)
    return y, last[:, SUBLANES - (CONV_W - 1):]


def _prompt_layer(x, mem, w, bias):
    b, t, d = x.shape
    la = min(BAND_PAST, t)
    assert la == ROW_TILE and t % ROW_TILE == 0
    qa, ka, va, qb, kb, vb, ka_tail, va_tail, kb_f, vb_f, logf = _in_proj(
        x.reshape(b * t, d), w["g1"], w["w_main"], w["w_f"], w["b_f"], w["gains"], w["seg"],
        tail_period=t // ROW_TILE, time_minor=True)
    r3 = lambda a: a.reshape(b, t, WIDTH)
    oa = _band_prompt(r3(qa), r3(ka), r3(va), bias)
    logf = logf.reshape(N_HEADS, b, t).transpose(1, 0, 2)
    crow = _cumsum_lanes(logf.reshape(b * N_HEADS, t)).reshape(b, N_HEADS, t)
    logf = logf.transpose(0, 2, 1)
    ob = _fox_prompt(r3(qb), r3(kb), r3(vb), crow, crow.transpose(0, 2, 1))
    mk_f, mv_f, mk, mv = _mem_kv(mem, w["g_mem"], w["w_mkv"], w["g_mk"])
    dff = w["w_down"].shape[0]
    y, conv = _post_attention(x.reshape(b * t, d), oa, ob, mk, mv, jnp.zeros((b, CONV_W - 1, dff), F32), w,
                              seq=t, ff_chunk=dff // 11)
    heads = lambda a, n: a.reshape(b, N_HEADS, HEAD_DIM, n).transpose(0, 3, 1, 2)
    mem_heads = lambda a: (a.reshape(b, mem.shape[1], 2, N_MEM_HEADS, LANES).transpose(0, 1, 3, 2, 4)
                           .reshape(b, mem.shape[1], N_MEM_HEADS, d // N_MEM_HEADS))
    return (y.reshape(b, t, d), heads(ka_tail, la), heads(va_tail, la), heads(kb_f, t), heads(vb_f, t), logf,
            mem_heads(mk_f), mem_heads(mv_f), conv)


def _sample_layer(x, cache_a_k, cache_a_v, cache_b_k, cache_b_v, cache_b_logf, cache_mem_k, cache_mem_v,
                  state_conv, w, bias):
    b, s, d = x.shape
    past = cache_b_k.shape[1]
    assert ROW_TILE % s == 0 and (b * s) % ROW_TILE == 0
    qa, ka, va, qb, kb, vb, ka_f, va_f, kb_f, vb_f, logf = _in_proj(
        x.reshape(b * s, d), w["g1"], w["w_main"], w["w_f"], w["b_f"], w["gains"], w["seg"], tail_period=1,
        time_minor=False)
    r3 = lambda a: a.reshape(b, s, WIDTH)
    flat = lambda a: a.reshape(a.shape[0], a.shape[1], -1)
    time_minor = lambda a: a.transpose(0, 2, 3, 1)
    oa = _band_sample(r3(qa), r3(ka), r3(va), time_minor(cache_a_k), time_minor(cache_a_v), bias)
    logf = logf.reshape(N_HEADS, b, s).transpose(1, 0, 2)
    total = past + s
    all_logf = jnp.concatenate([cache_b_logf.astype(F32).transpose(0, 2, 1), logf], axis=2)
    all_logf = jnp.pad(all_logf, ((0, 0), (0, 0), (0, -total % LANES)))
    logf = logf.transpose(0, 2, 1)
    cum = _cumsum_lanes(all_logf.reshape(b * N_HEADS, -1)).reshape(b, N_HEADS, -1)
    crow_new = cum[:, :, past:total]
    pad_rows = lambda a: jnp.pad(a, ((0, 0), (0, -s % LANES), (0, 0)))
    ob = _fox_sample(r3(qb), pad_rows(r3(kb)), pad_rows(r3(vb)), time_minor(cache_b_k), time_minor(cache_b_v),
                     cum[:, :, :past], jnp.pad(crow_new, ((0, 0), (0, 0), (0, -s % LANES))),
                     crow_new.transpose(0, 2, 1))
    dff = w["w_down"].shape[0]
    y, conv = _post_attention(x.reshape(b * s, d), oa, ob, _cache_mem_rows(cache_mem_k.astype(F32)),
                              _cache_mem_rows(cache_mem_v.astype(F32)), state_conv.astype(F32), w,
                              seq=s, ff_chunk=dff // 11)
    heads = lambda a: a.reshape(b, s, N_HEADS, HEAD_DIM)
    return y.reshape(b, s, d), heads(ka_f), heads(va_f), heads(kb_f), heads(vb_f), logf, conv


def kernel(x_prompt, x_sample, cache_a_k, cache_a_v, cache_b_k, cache_b_v, cache_b_logf, cache_mem_k, cache_mem_v, state_conv, mem_prompt, w_in, b_f, g_qa, g_ka, rel_bias, g_qb, g_kb, w_o, g_norm1, g_norm2, g_mem, w_mq, w_mkv, g_mq, g_mk, w_mo, g_norm3, w_up, w_conv, b_conv, w_down):
    depth = w_in.shape[0]
    xp, xs = x_prompt, x_sample
    outs_p, outs_s = [], []
    for l in range(depth):
        w = _layer_weights(l, w_in, b_f, g_qa, g_ka, rel_bias, g_qb, g_kb, w_o, g_norm1, g_norm2, g_mem, w_mq,
                           w_mkv, g_mq, g_mk, w_mo, g_norm3, w_up, w_conv, b_conv, w_down)
        bias = _band_bias(w["rel"])
        xp, *rest_p = _prompt_layer(xp, mem_prompt, w, bias)
        xs, *rest_s = _sample_layer(xs, cache_a_k[l], cache_a_v[l], cache_b_k[l], cache_b_v[l], cache_b_logf[l],
                                    cache_mem_k[l], cache_mem_v[l], state_conv[l], w, bias)
        outs_p.append(rest_p)
        outs_s.append(rest_s)
    stack = lambda outs: [jnp.stack(col, axis=0) for col in zip(*outs)]
    return (xp, xs, *stack(outs_p), *stack(outs_s))
```

```python
import functools

import jax
import jax.numpy as jnp
from jax import lax
from jax.experimental import pallas as pl
from jax.experimental.pallas import tpu as pltpu

F32 = jnp.float32
BF16 = jnp.bfloat16
EPS = 1e-6
NEG_INF = float("-inf")

HEAD_DIM = 64
N_HEADS = 8
WIDTH = N_HEADS * HEAD_DIM
CHUNK = 64
N_PREV_CHUNKS = 8
BAND_PAST = N_PREV_CHUNKS * CHUNK
REL_CLIP = 128
N_MEM_HEADS = 4
CONV_W = 3

ROW_TILE = 512
FFN_TILE = 1024
BAND_Q = 256
BAND_WIN = BAND_PAST + BAND_Q
FOX_BLOCK = 512
FOX_CACHE_BLOCK = 2048
FOX_BAND = 64
MXU_COLS = 256
LANES = 128
SUBLANES = 8
BF16_ROWS = 16
VMEM_LIMIT = 56 * 1024 * 1024
LOG2E = 1.4426950408889634


def _dot(a, b):
    return jnp.dot(a, b, preferred_element_type=F32)


def _dot_nt(a, b):
    return lax.dot_general(a, b, (((1,), (1,)), ((), ())), preferred_element_type=F32)


def _rms(x, g):
    return x * lax.rsqrt(jnp.mean(x * x, axis=-1, keepdims=True) + EPS) * g


def _split3(x):
    hi = x.astype(BF16)
    r1 = x - hi.astype(F32)
    mid = r1.astype(BF16)
    lo = (r1 - mid.astype(F32)).astype(BF16)
    return hi, mid, lo


def _store_heads(ref, x):
    rows = x.shape[0]
    for h in range(N_HEADS):
        ref[pl.ds(h, rows, stride=N_HEADS), :] = x[:, h * HEAD_DIM:(h + 1) * HEAD_DIM]


def _resident(a):
    return pl.BlockSpec(a.shape, lambda i: (0,) * a.ndim, pipeline_mode=pl.Buffered(1))


def _params(*sem):
    return pltpu.CompilerParams(dimension_semantics=sem, vmem_limit_bytes=VMEM_LIMIT)


def _inproj_kernel(x_ref, g1_ref, w_ref, wf_ref, bf_ref, gains_ref, seg_ref,
                   qa_ref, ka_ref, va_ref, qb_ref, kb_ref, vb_ref,
                   kaf_ref, vaf_ref, kbf_ref, vbf_ref, logf_ref, *, tail_period, time_minor):
    h = _rms(x_ref[...], g1_ref[...]).astype(BF16)
    seg = seg_ref[...]

    def store_f32(ref, x):
        if time_minor:
            ref[...] = x.T
        else:
            _store_heads(ref, x)

    def proj(g):
        return _dot(h, w_ref[:, g * WIDTH:(g + 1) * WIDTH])

    def head_norm(p, row):
        p2 = (p * p).astype(BF16)
        ss = jnp.concatenate([_dot(p2[:, :MXU_COLS], seg), _dot(p2[:, MXU_COLS:], seg)], axis=1)
        return p * lax.rsqrt(ss * (1.0 / HEAD_DIM) + EPS) * gains_ref[row:row + 1, :]

    is_tail = (pl.program_id(0) % tail_period) == tail_period - 1
    scale = HEAD_DIM ** -0.5 * LOG2E

    kb = head_norm(proj(4), 3)
    kb_ref[...] = kb.astype(BF16)
    store_f32(kbf_ref.at[0], kb)
    vb = proj(5)
    vb_ref[...] = vb.astype(BF16)
    store_f32(vbf_ref.at[0], vb)
    z = _dot_nt(wf_ref[...], h)[:N_HEADS] + bf_ref[...]
    logf_ref[...] = jnp.minimum(z, 0.0) - jnp.log1p(jnp.exp(-jnp.abs(z)))
    qb_ref[...] = (head_norm(proj(3), 2) * scale).astype(BF16)
    qa_ref[...] = (head_norm(proj(0), 0) * scale).astype(BF16)
    ka = head_norm(proj(1), 1)
    ka_ref[...] = ka.astype(BF16)
    va = proj(2)
    va_ref[...] = va.astype(BF16)

    @pl.when(is_tail)
    def _():
        store_f32(kaf_ref.at[0], ka)
        store_f32(vaf_ref.at[0], va)


def _in_proj(x2d, g1, w_main, w_f, b_f, gains, seg, *, tail_period, time_minor):
    n, d = x2d.shape
    tm = ROW_TILE
    assert n % tm == 0 and (n // tm) % tail_period == 0
    n_tail = n // tm // tail_period
    row_bf = jax.ShapeDtypeStruct((n, WIDTH), BF16)
    rows = pl.BlockSpec((tm, WIDTH), lambda i: (i, 0))
    if time_minor:
        head_f = jax.ShapeDtypeStruct((n_tail, WIDTH, tm * tail_period), F32)
        tail_f = jax.ShapeDtypeStruct((n_tail, WIDTH, tm), F32)
        head_rows = pl.BlockSpec((1, WIDTH, tm), lambda i: (i // tail_period, 0, i % tail_period))
        tail = pl.BlockSpec((1, WIDTH, tm), lambda i: (i // tail_period, 0, 0))
    else:
        head_f = jax.ShapeDtypeStruct((n // tm, tm * N_HEADS, HEAD_DIM), F32)
        tail_f = jax.ShapeDtypeStruct((n_tail, tm * N_HEADS, HEAD_DIM), F32)
        head_rows = pl.BlockSpec((1, tm * N_HEADS, HEAD_DIM), lambda i: (i, 0, 0))
        tail = pl.BlockSpec((1, tm * N_HEADS, HEAD_DIM), lambda i: (i // tail_period, 0, 0))
    full = _resident
    return pl.pallas_call(
        functools.partial(_inproj_kernel, tail_period=tail_period, time_minor=time_minor),
        grid=(n // tm,),
        in_specs=[pl.BlockSpec((tm, d), lambda i: (i, 0)), full(g1), full(w_main), full(w_f),
                  full(b_f), full(gains), full(seg)],
        out_specs=[rows] * 6 + [tail, tail, head_rows, head_rows, pl.BlockSpec((N_HEADS, tm), lambda i: (0, i))],
        out_shape=[row_bf] * 6 + [tail_f, tail_f, head_f, head_f, jax.ShapeDtypeStruct((N_HEADS, n), F32)],
        compiler_params=_params("arbitrary"),
        name="in_proj",
    )(x2d, g1, w_main, w_f, b_f, gains, seg)


def _store_mem_rows(ref, x, hh):
    n = x.shape[0]
    for lg in range(2):
        ref[pl.ds(lg * N_MEM_HEADS + hh, n, stride=2 * N_MEM_HEADS), :] = x[:, lg * LANES:(lg + 1) * LANES]


def _memkv_kernel(mem_ref, g_ref, w_ref, gk_ref, kf_ref, vf_ref, kb_ref, vb_ref):
    d = mem_ref.shape[-1]
    hd = d // N_MEM_HEADS
    h = _rms(mem_ref[0], g_ref[...]).astype(BF16)
    kv = _dot(h, w_ref[...])
    for hh in range(N_MEM_HEADS):
        cs = slice(hh * hd, (hh + 1) * hd)
        k = _rms(kv[:, cs], gk_ref[...])
        v = kv[:, d + hh * hd:d + (hh + 1) * hd]
        _store_mem_rows(kf_ref.at[0], k, hh)
        _store_mem_rows(vf_ref.at[0], v, hh)
        kb_ref[0, :, cs] = k.astype(BF16)
        vb_ref[0, :, cs] = v.astype(BF16)


def _mem_kv(mem, g_mem, w_mkv, g_mk):
    b, n, d = mem.shape
    assert d == N_MEM_HEADS * 2 * LANES
    blk = pl.BlockSpec((1, n, d), lambda i: (i, 0, 0))
    blk_f = pl.BlockSpec((1, n * 2 * N_MEM_HEADS, LANES), lambda i: (i, 0, 0))
    full = _resident
    f = jax.ShapeDtypeStruct((b, n * 2 * N_MEM_HEADS, LANES), F32)
    h = jax.ShapeDtypeStruct((b, n, d), BF16)
    return pl.pallas_call(
        _memkv_kernel, grid=(b,),
        in_specs=[blk, full(g_mem), full(w_mkv), full(g_mk)],
        out_specs=[blk_f, blk_f, blk, blk], out_shape=[f, f, h, h],
        compiler_params=_params("arbitrary"), name="mem_kv",
    )(mem, g_mem, w_mkv, g_mk)


def _cumsum_kernel(x_ref, o_ref):
    r, t = x_ref.shape
    k = lax.broadcasted_iota(jnp.int32, (LANES, LANES), 0)
    j = lax.broadcasted_iota(jnp.int32, (LANES, LANES), 1)
    tri = jnp.where(k <= j, 1.0, 0.0).astype(BF16)
    carry = jnp.zeros((r, 1), F32)
    for blk in range(t // LANES):
        hi, mid, lo = _split3(x_ref[:, blk * LANES:(blk + 1) * LANES])
        cs = _dot(hi, tri) + _dot(mid, tri) + _dot(lo, tri)
        o_ref[:, blk * LANES:(blk + 1) * LANES] = (cs + carry) * LOG2E
        carry = carry + cs[:, LANES - 1:LANES]


def _cumsum_lanes(x):
    assert x.shape[1] % LANES == 0
    return pl.pallas_call(_cumsum_kernel, out_shape=jax.ShapeDtypeStruct(x.shape, F32), name="cumsum")(x)


def _band_bias_kernel(tbl_ref, o_ref):
    nrel = tbl_ref.shape[1]
    qblk, win = o_ref.shape[1:]
    wrap = qblk + win
    jp = lax.broadcasted_iota(jnp.int32, (nrel, wrap), 1)
    r = lax.broadcasted_iota(jnp.int32, (nrel, wrap), 0)
    d = jnp.where(jp < win, jp, jp - wrap)
    rel = jnp.clip(BAND_PAST - d, -REL_CLIP, REL_CLIP) + REL_CLIP
    onehot = jnp.where(rel == r, 1.0, 0.0).astype(BF16)
    hi, mid, lo = _split3(tbl_ref[...])
    base = (_dot(hi, onehot) + _dot(mid, onehot) + _dot(lo, onehot)) * LOG2E
    qc = lax.broadcasted_iota(jnp.int32, (qblk, win), 0) // CHUNK
    kc = lax.broadcasted_iota(jnp.int32, (qblk, win), 1) // CHUNK
    inband = (kc >= qc) & (kc <= qc + N_PREV_CHUNKS)
    for h in range(N_HEADS):
        row = jnp.broadcast_to(base[h:h + 1, :], (qblk, wrap))
        toeplitz = pltpu.roll(row, 0, 1, stride=1, stride_axis=0)
        o_ref[h] = jnp.where(inband, toeplitz[:, :win], NEG_INF)


def _band_bias(rel_table):
    nrel = 2 * REL_CLIP + 1
    pad = -nrel % LANES
    tbl = jnp.pad(rel_table.astype(F32), ((0, 0), (0, pad)))
    return pl.pallas_call(
        _band_bias_kernel, out_shape=jax.ShapeDtypeStruct((N_HEADS, BAND_Q, BAND_WIN), F32),
        name="band_bias")(tbl)


def _softmax_pv(s, vs, transposed=None):
    transposed = transposed or (False,) * len(vs)
    m = jnp.max(s, axis=-1, keepdims=True)
    e = jnp.exp2(s - m)
    l = jnp.sum(e, axis=-1, keepdims=True)
    eb = e.astype(BF16)
    o, lo = None, 0
    for v, vt in zip(vs, transposed):
        keys = v.shape[1] if vt else v.shape[0]
        part = _dot_nt(eb[:, lo:lo + keys], v) if vt else _dot(eb[:, lo:lo + keys], v)
        o = part if o is None else o + part
        lo += keys
    return o * (1.0 / l)


def _band_prompt_kernel(q_ref, k0_ref, k1_ref, k2_ref, v0_ref, v1_ref, v2_ref, bias_ref, o_ref, s_sc, m_sc, qm_sc):
    i = pl.program_id(1)
    tq = q_ref.shape[1]
    k_refs, v_refs = (k0_ref, k1_ref, k2_ref), (v0_ref, v1_ref, v2_ref)
    bands = [slice(r, r + CHUNK) for r in range(0, tq, CHUNK)]
    n_groups = BAND_WIN // LANES
    per_block = tq // LANES
    group = lambda g: slice(g * LANES, (g + 1) * LANES)
    visible = lambda c: [g for g in range(n_groups) if 2 * g + 1 >= c and 2 * g <= c + N_PREV_CHUNKS]
    low = _low_half(tq)
    for h in range(N_HEADS):
        q_pair = _pair(q_ref, h)
        qm_sc[h] = jnp.where(low if h % 2 == 0 else ~low, q_pair, jnp.zeros_like(q_pair))

    def logits(early):
        for h in range(N_HEADS):
            qm = qm_sc[h]
            parts = [_dot_nt(qm, _pair(k_ref, h)) for k_ref in k_refs]
            for c, rows in enumerate(bands):
                mx = None
                for g in visible(c):
                    p, gp = divmod(g, per_block)
                    sg = parts[p][rows, group(gp)] + bias_ref[h, rows, group(g)]
                    if early and p < 2:
                        sg = jnp.where(i + p >= 2, sg, NEG_INF)
                    s_sc[h, rows, group(g)] = sg
                    mx = sg if mx is None else jnp.maximum(mx, sg)
                m_sc[h, rows] = jnp.broadcast_to(jnp.max(mx, axis=-1, keepdims=True), (CHUNK, LANES))

    values = functools.partial(_band_values, v_refs, o_ref, s_sc, m_sc, bands, visible)

    @pl.when(i >= 2)
    def _():
        logits(False)

    @pl.when(i < 2)
    def _():
        logits(True)

    values()


def _band_values(v_refs, o_ref, s_sc, m_sc, bands, visible):
    tq = o_ref.shape[1]
    n_groups = s_sc.shape[2] // LANES
    group = lambda g: slice(g * LANES, (g + 1) * LANES)
    for h in range(N_HEADS):
        p_bands, invs = [], []
        for c, rows in enumerate(bands):
            m = m_sc[h, rows]
            rs, ps = None, []
            for g in range(n_groups):
                if g in visible(c):
                    e = jnp.exp2(s_sc[h, rows, group(g)] - m)
                    ps.append(e.astype(BF16))
                    rs = e if rs is None else rs + e
                else:
                    ps.append(jnp.zeros((CHUNK, LANES), BF16))
            p_bands.append(jnp.concatenate(ps, axis=1))
            invs.append(1.0 / jnp.sum(rs, axis=-1, keepdims=True))
        p_all = jnp.concatenate(p_bands, axis=0)
        o = None
        for p, v_ref in enumerate(v_refs):
            part = _dot(p_all[:, p * tq:(p + 1) * tq], _pair(v_ref, h))
            o = part if o is None else o + part
        half = slice((h % 2) * HEAD_DIM, (h % 2 + 1) * HEAD_DIM)
        o_ref[0, :, h * HEAD_DIM:(h + 1) * HEAD_DIM] = (o * jnp.concatenate(invs, axis=0))[:, half].astype(BF16)


def _band_prompt(qa, ka, va, bias):
    b, t, _ = qa.shape
    tq = BAND_Q
    assert t % tq == 0
    blk = lambda back: pl.BlockSpec((1, tq, WIDTH), lambda bi, i: (bi, jnp.maximum(i - back, 0), 0))
    return pl.pallas_call(
        _band_prompt_kernel, grid=(b, t // tq),
        in_specs=[blk(0), blk(2), blk(1), blk(0), blk(2), blk(1), blk(0),
                  pl.BlockSpec(bias.shape, lambda bi, i: (0, 0, 0))],
        out_specs=blk(0), out_shape=jax.ShapeDtypeStruct((b, t, WIDTH), BF16),
        scratch_shapes=[pltpu.VMEM((N_HEADS, tq, BAND_WIN), F32), pltpu.VMEM((N_HEADS, tq, LANES), F32),
                        pltpu.VMEM((N_HEADS, tq, LANES), BF16)],
        compiler_params=_params("arbitrary", "arbitrary"), name="band_prompt",
    )(qa, ka, ka, ka, va, va, va, bias)


def _band_sample_kernel(q_ref, kc_ref, vc_ref, kn_ref, vn_ref, bias_ref, o_ref):
    for h in range(N_HEADS):
        hs = slice(h * HEAD_DIM, (h + 1) * HEAD_DIM)
        q = q_ref[0, :, hs]
        s = jnp.concatenate([_dot(q, kc_ref[0, h].astype(BF16)), _dot_nt(q, kn_ref[0, :, hs])], axis=1)
        s = s + bias_ref[h, :CHUNK, :BAND_PAST + CHUNK]
        o = _softmax_pv(s, [vc_ref[0, h].astype(BF16), vn_ref[0, :, hs]], transposed=(True, False))
        o_ref[0, :, hs] = o.astype(BF16)


def _band_sample(qa, ka, va, cache_k, cache_v, bias):
    b, s, _ = qa.shape
    la = cache_k.shape[3]
    assert s == CHUNK and la == BAND_PAST
    new = pl.BlockSpec((1, s, WIDTH), lambda bi: (bi, 0, 0))
    old = pl.BlockSpec((1, N_HEADS, HEAD_DIM, la), lambda bi: (bi, 0, 0, 0))
    return pl.pallas_call(
        _band_sample_kernel, grid=(b,),
        in_specs=[new, old, old, new, new, pl.BlockSpec(bias.shape, lambda bi: (0, 0, 0))],
        out_specs=new, out_shape=jax.ShapeDtypeStruct((b, s, WIDTH), BF16),
        compiler_params=_params("arbitrary"), name="band_sample",
    )(qa, cache_k, cache_v, ka, va, bias)


def _fox_scratch(tq, tk):
    rows = lambda w: pltpu.VMEM((N_HEADS, tq, w), F32)
    return [rows(tk), rows(LANES), rows(LANES), rows(LANES), rows(LANES),
            pltpu.VMEM((N_HEADS, tq, LANES), BF16), pltpu.VMEM((N_HEADS // 2, tq, LANES), F32)]


def _low_half(rows):
    return lax.broadcasted_iota(jnp.int32, (rows, LANES), 1) < HEAD_DIM


def _fox_init(q_ref, cq_ref, sc):
    _, cq_sc, m_sc, _, l_sc, qm_sc, acc_sc = sc
    tq = cq_sc.shape[1]
    low = _low_half(tq)
    for h in range(N_HEADS):
        cq_sc[h] = jnp.broadcast_to(cq_ref[0, h:h + 1, :], (LANES, cq_ref.shape[2])).T[:tq]
        q_pair = q_ref[0, :, (h // 2) * LANES:(h // 2 + 1) * LANES]
        qm_sc[h] = jnp.where(low if h % 2 == 0 else ~low, q_pair, jnp.zeros_like(q_pair))
    m_sc[...] = jnp.full(m_sc.shape, NEG_INF, F32)
    l_sc[...] = jnp.zeros(l_sc.shape, F32)
    acc_sc[...] = jnp.zeros(acc_sc.shape, F32)


def _fox_block(qk, pv, crow_ref, tk, causal, sc):
    s_sc, cq_sc, m_sc, a_sc, l_sc, qm_sc, acc_sc = sc
    tq = s_sc.shape[1]
    band = min(tq, FOX_BAND)
    bands = [slice(r, r + band) for r in range(0, tq, band)]
    n_groups = tk // LANES
    group = lambda g: slice(g * LANES, (g + 1) * LANES)

    def visible(rows):
        if not causal:
            return [(g, False) for g in range(n_groups)]
        return [(g, (g + 1) * LANES - 1 > rows.start) for g in range(n_groups) if g * LANES < rows.stop]

    if causal:
        ahead = (lax.broadcasted_iota(jnp.int32, (band, LANES), 0)
                 - lax.broadcasted_iota(jnp.int32, (band, LANES), 1))
    for h in range(N_HEADS):
        s = qk(h, qm_sc[h])
        for rows in bands:
            cq = cq_sc[h, rows]
            mx = None
            for g, masked in visible(rows):
                sg = s[rows, group(g)] + (cq - crow_ref[0, h:h + 1, group(g)])
                if masked:
                    sg = jnp.where(ahead >= g * LANES - rows.start, sg, NEG_INF)
                s_sc[h, rows, group(g)] = sg
                mx = sg if mx is None else jnp.maximum(mx, sg)
            m_prev = m_sc[h, rows]
            m_new = jnp.maximum(m_prev, jnp.max(mx, axis=-1, keepdims=True))
            a_sc[h, rows] = jnp.exp2(m_prev - m_new)
            m_sc[h, rows] = m_new
    for h in range(N_HEADS):
        p_bands = []
        for rows in bands:
            seen = visible(rows)
            m_new = m_sc[h, rows]
            rs, ps = None, []
            for g, _ in seen:
                p = jnp.exp2(s_sc[h, rows, group(g)] - m_new)
                ps.append(p.astype(BF16))
                rs = p if rs is None else rs + p
            l_sc[h, rows] = a_sc[h, rows] * l_sc[h, rows] + rs
            ps += [jnp.zeros((band, LANES), BF16)] * (n_groups - len(seen))
            p_bands.append(jnp.concatenate(ps, axis=1))
        half = slice((h % 2) * HEAD_DIM, (h % 2 + 1) * HEAD_DIM)
        new = a_sc[h] * acc_sc[h // 2] + pv(h, jnp.concatenate(p_bands, axis=0))
        acc_sc[h // 2, :, half] = new[:, half]


def _fox_finish(o_ref, sc):
    l_sc, acc_sc = sc[4], sc[6]
    low = _low_half(acc_sc.shape[1])
    total = lambda h: jnp.sum(l_sc[h], axis=-1, keepdims=True)
    for g in range(N_HEADS // 2):
        inv = jnp.where(low, 1.0 / total(2 * g), 1.0 / total(2 * g + 1))
        o_ref[0, :, g * LANES:(g + 1) * LANES] = (acc_sc[g] * inv).astype(BF16)


def _head(ref, h):
    return ref[0, :, h * HEAD_DIM:(h + 1) * HEAD_DIM]


def _pair(ref, h):
    return ref[0, :, (h // 2) * LANES:(h // 2 + 1) * LANES]


def _row_major_products(k_ref, v_ref):
    return (lambda h, qm: _dot_nt(qm, _pair(k_ref, h))), (lambda h, p: _dot(p, _pair(v_ref, h)))


def _fox_prompt_kernel(qi_ref, kj_ref, q_ref, k_ref, v_ref, crow_ref, cq_ref, o_ref, *sc):
    p = pl.program_id(1)
    qi, kj = qi_ref[p], kj_ref[p]
    tk = k_ref.shape[1]
    qk, pv = _row_major_products(k_ref, v_ref)

    for first in (True, False):
        started = (kj == 0) if first else (kj > 0)

        @pl.when(started & (kj < qi))
        def _():
            if first:
                _fox_init(q_ref, cq_ref, sc)
            _fox_block(qk, pv, crow_ref, tk, False, sc)

        @pl.when(started & (kj == qi))
        def _():
            if first:
                _fox_init(q_ref, cq_ref, sc)
            _fox_block(qk, pv, crow_ref, tk, True, sc)
            _fox_finish(o_ref, sc)


def _fox_prompt(qb, kb, vb, crow):
    b, t, _ = qb.shape
    tq = FOX_BLOCK
    assert t % tq == 0
    pairs = [(i, j) for i in range(t // tq) for j in range(i + 1)]
    qi = jnp.array([p[0] for p in pairs], jnp.int32)
    kj = jnp.array([p[1] for p in pairs], jnp.int32)
    qblk = pl.BlockSpec((1, tq, WIDTH), lambda bi, p, qi, kj: (bi, qi[p], 0))
    kblk = pl.BlockSpec((1, tq, WIDTH), lambda bi, p, qi, kj: (bi, kj[p], 0))
    grid_spec = pltpu.PrefetchScalarGridSpec(
        num_scalar_prefetch=2, grid=(b, len(pairs)),
        in_specs=[qblk, kblk, kblk,
                  pl.BlockSpec((1, N_HEADS, tq), lambda bi, p, qi, kj: (bi, 0, kj[p])),
                  pl.BlockSpec((1, N_HEADS, tq), lambda bi, p, qi, kj: (bi, 0, qi[p]))],
        out_specs=qblk, scratch_shapes=_fox_scratch(tq, tq))
    return pl.pallas_call(
        _fox_prompt_kernel, grid_spec=grid_spec, out_shape=jax.ShapeDtypeStruct((b, t, WIDTH), BF16),
        compiler_params=_params("arbitrary", "arbitrary"), name="fox_prompt",
    )(qi, kj, qb, kb, vb, crow, crow)


def _fox_sample_kernel(q_ref, kc_ref, vc_ref, kn_ref, vn_ref, crow_ref, crow_new_ref, o_ref, *sc):
    j = pl.program_id(1)
    last = pl.num_programs(1) - 1

    @pl.when(j == 0)
    def _():
        _fox_init(q_ref, crow_new_ref, sc)

    def cached_pair(ref, h):
        g = h // 2
        return ref[0, 2 * g:2 * g + 2].astype(BF16).reshape(2 * HEAD_DIM, ref.shape[3])

    _fox_block(lambda h, qm: _dot(qm, cached_pair(kc_ref, h)),
               lambda h, p: _dot_nt(p, cached_pair(vc_ref, h)), crow_ref, kc_ref.shape[3], False, sc)

    @pl.when(j == last)
    def _():
        qk, pv = _row_major_products(kn_ref, vn_ref)
        _fox_block(qk, pv, crow_new_ref, kn_ref.shape[1], True, sc)
        _fox_finish(o_ref, sc)


def _fox_sample(qb, kb, vb, cache_k, cache_v, crow, crow_new):
    b, s, _ = qb.shape
    past = cache_k.shape[3]
    tk, tn = min(FOX_CACHE_BLOCK, past), kb.shape[1]
    assert past % tk == 0 and tn % LANES == 0 and tn >= s
    nk = past // tk
    new = lambda r: pl.BlockSpec((1, r, WIDTH), lambda bi, j: (bi, 0, 0))
    old = pl.BlockSpec((1, N_HEADS, HEAD_DIM, tk), lambda bi, j: (bi, 0, 0, j))
    return pl.pallas_call(
        _fox_sample_kernel, grid=(b, nk),
        in_specs=[new(s), old, old, new(tn), new(tn),
                  pl.BlockSpec((1, N_HEADS, tk), lambda bi, j: (bi, 0, j)),
                  pl.BlockSpec((1, N_HEADS, tn), lambda bi, j: (bi, 0, 0))],
        out_specs=new(s), out_shape=jax.ShapeDtypeStruct((b, s, WIDTH), BF16),
        scratch_shapes=_fox_scratch(s, tk),
        compiler_params=_params("arbitrary", "arbitrary"), name="fox_sample",
    )(qb, cache_k, cache_v, kb, vb, crow, crow_new)


def _mid_kernel(x_ref, oa_ref, ob_ref, wo_ref, g2_ref, wq_ref, gq_ref, mk_ref, mv_ref, wmo_ref, o_ref,
                om_sc, *, seg):
    tm, d = x_ref.shape
    hd = d // N_MEM_HEADS
    x1 = x_ref[...] + _dot(oa_ref[...], wo_ref[:WIDTH, :]) + _dot(ob_ref[...], wo_ref[WIDTH:, :])
    h2 = _rms(x1, g2_ref[...]).astype(BF16)
    qm = _dot(h2, wq_ref[...])
    scale = hd ** -0.5 * LOG2E
    for hh in range(N_MEM_HEADS):
        cs = slice(hh * hd, (hh + 1) * hd)
        q = (_rms(qm[:, cs], gq_ref[...]) * scale).astype(BF16)
        def mem_head(ref, sg):
            if ref.shape[2] == d:
                return ref[sg, :, cs]
            rows = lambda lg: ref[sg, pl.ds(lg * N_MEM_HEADS + hh, ref.shape[1] // (2 * N_MEM_HEADS),
                                            stride=2 * N_MEM_HEADS), :]
            return jnp.concatenate([rows(0), rows(1)], axis=1).astype(BF16)

        for sg in range(tm // seg):
            rs = slice(sg * seg, (sg + 1) * seg)
            s = _dot_nt(q[rs], mem_head(mk_ref, sg))
            om_sc[rs, cs] = _softmax_pv(s, [mem_head(mv_ref, sg)]).astype(BF16)
    o_ref[...] = x1 + _dot(om_sc[...], wmo_ref[...])


def _cache_mem_rows(x):
    b, n, nh, hd = x.shape
    assert nh == N_MEM_HEADS and hd == 2 * LANES
    return x.reshape(b, n, nh, 2, LANES).transpose(0, 1, 3, 2, 4).reshape(b, n * 2 * nh, LANES)


def _mid(x2d, oa, ob, w_o, g2, w_mq, g_mq, mk, mv, w_mo, *, seq):
    n, d = x2d.shape
    tm = ROW_TILE
    seg = min(seq, tm)
    tiles_per_batch = seq // seg
    nb = tm // seg
    assert n % tm == 0 and seq % seg == 0 and tm % seg == 0
    rows = lambda w: pl.BlockSpec((tm, w), lambda i: (i, 0))
    full = _resident
    mem = pl.BlockSpec((nb,) + mk.shape[1:], lambda i: (i // tiles_per_batch, 0, 0))
    return pl.pallas_call(
        functools.partial(_mid_kernel, seg=seg), grid=(n // tm,),
        in_specs=[rows(d), rows(WIDTH), rows(WIDTH), full(w_o), full(g2), full(w_mq), full(g_mq), mem, mem,
                  full(w_mo)],
        out_specs=rows(d), out_shape=jax.ShapeDtypeStruct((n, d), F32),
        scratch_shapes=[pltpu.VMEM((tm, d), BF16)],
        compiler_params=_params("arbitrary"), name="mid",
    )(x2d, oa, ob, w_o, g2, w_mq, g_mq, mk, mv, w_mo)


def _ffn_kernel(x_ref, g3_ref, wup_ref, wc_ref, bc_ref, wd_ref, st_ref, o_ref, last_ref, carry_sc, y_sc,
                *, seg, tiles_per_batch, ff_chunk):
    tm, d = x_ref.shape
    dff = wd_ref.shape[0]
    i = pl.program_id(0)
    h3 = _rms(x_ref[...], g3_ref[...]).astype(BF16)
    if tiles_per_batch > 1:
        @pl.when(i == 0)
        def _():
            carry_sc[...] = jnp.zeros(carry_sc.shape, F32)
    rowid = lax.broadcasted_iota(jnp.int32, (SUBLANES, 1), 0)
    first_tile = (i % tiles_per_batch) == 0
    for c in range(dff // ff_chunk):
        cs = slice(c * ff_chunk, (c + 1) * ff_chunk)
        gate = _dot(h3, wup_ref[:, cs])
        val = _dot(h3, wup_ref[:, dff + c * ff_chunk:dff + (c + 1) * ff_chunk])
        w0, w1, w2 = wc_ref[0:1, cs], wc_ref[1:2, cs], wc_ref[2:3, cs]
        for sg in range(tm // seg):
            rows = slice(sg * seg, (sg + 1) * seg)
            g = gate[rows]
            if tiles_per_batch == 1:
                prev = st_ref[sg, :, cs]
            else:
                prev = jnp.where(first_tile, st_ref[0, :, cs], carry_sc[SUBLANES - 2:, cs])
            p2, p1 = prev[0:1], prev[1:2]
            s1 = pltpu.roll(g, 1, 0)
            s2 = pltpu.roll(g, 2, 0)
            top1 = jnp.where(rowid == 0, p1, s1[:SUBLANES])
            top2 = jnp.where(rowid == 0, p2, jnp.where(rowid == 1, p1, s2[:SUBLANES]))
            s1 = jnp.concatenate([top1, s1[SUBLANES:]], axis=0)
            s2 = jnp.concatenate([top2, s2[SUBLANES:]], axis=0)
            conv = s2 * w0 + s1 * w1 + g * w2 + bc_ref[:, cs]
            y_sc[rows, cs] = (conv * jax.nn.sigmoid(conv) * val[rows]).astype(BF16)
            last_ref[sg, :, cs] = g[seg - SUBLANES:]
        if tiles_per_batch > 1:
            carry_sc[:, cs] = gate[tm - SUBLANES:]
    o_ref[...] = x_ref[...] + _dot(y_sc[...], wd_ref[...])


def _conv_ffn(x2d, g3, w_up, w_conv, b_conv, w_down, state, *, seq, ff_chunk):
    n, d = x2d.shape
    dff = w_down.shape[0]
    tm = min(FFN_TILE, n)
    seg = min(seq, tm)
    tiles_per_batch = seq // seg
    nb = tm // seg
    assert n % tm == 0 and seq % seg == 0 and tm % seg == 0 and dff % ff_chunk == 0 and seg >= SUBLANES
    rows = pl.BlockSpec((tm, d), lambda i: (i, 0))
    full = _resident
    per_batch = lambda r: pl.BlockSpec((nb, r, dff), lambda i: (i // tiles_per_batch, 0, 0))
    return pl.pallas_call(
        functools.partial(_ffn_kernel, seg=seg, tiles_per_batch=tiles_per_batch, ff_chunk=ff_chunk),
        grid=(n // tm,),
        in_specs=[rows, full(g3), full(w_up), full(w_conv), full(b_conv), full(w_down), per_batch(CONV_W - 1)],
        out_specs=[rows, per_batch(SUBLANES)],
        out_shape=[jax.ShapeDtypeStruct((n, d), F32), jax.ShapeDtypeStruct((n // seq, SUBLANES, dff), F32)],
        scratch_shapes=[pltpu.VMEM((SUBLANES, dff), F32), pltpu.VMEM((tm, dff), BF16)],
        compiler_params=_params("arbitrary"), name="conv_ffn",
    )(x2d, g3, w_up, w_conv, b_conv, w_down, state)


def _row(v):
    return v.astype(F32).reshape(1, -1)


def _layer_weights(l, w_in, b_f, g_qa, g_ka, rel_bias, g_qb, g_kb, w_o, g_norm1, g_norm2, g_mem, w_mq, w_mkv,
                   g_mq, g_mk, w_mo, g_norm3, w_up, w_conv, b_conv, w_down):
    n_main = 6 * WIDTH
    tile = lambda g: jnp.tile(g[l].astype(F32), N_HEADS)
    blk = jnp.arange(MXU_COLS) // HEAD_DIM
    return dict(
        w_main=w_in[l][:, :n_main].astype(BF16),
        w_f=jnp.pad(w_in[l][:, n_main:].T, ((0, BF16_ROWS - N_HEADS), (0, 0))).astype(BF16),
        b_f=b_f[l].astype(F32).reshape(N_HEADS, 1),
        gains=jnp.stack([tile(g_qa), tile(g_ka), tile(g_qb), tile(g_kb)]),
        seg=(blk[:, None] == blk[None, :]).astype(BF16),
        rel=rel_bias[l],
        w_o=w_o[l].astype(BF16), g1=_row(g_norm1[l]), g2=_row(g_norm2[l]), g3=_row(g_norm3[l]),
        g_mem=_row(g_mem[l]), w_mq=w_mq[l].astype(BF16), w_mkv=w_mkv[l].astype(BF16),
        g_mq=_row(g_mq[l]), g_mk=_row(g_mk[l]), w_mo=w_mo[l].astype(BF16),
        w_up=w_up[l].astype(BF16), w_conv=w_conv[l].astype(F32), b_conv=_row(b_conv[l]),
        w_down=w_down[l].astype(BF16))


def _post_attention(x2d, oa, ob, mk, mv, state, w, *, seq, ff_chunk):
    n, d = x2d.shape
    x2 = _mid(x2d, oa.reshape(n, WIDTH), ob.reshape(n, WIDTH), w["w_o"], w["g2"], w["w_mq"], w["g_mq"],
              mk, mv, w["w_mo"], seq=seq)
    y, last = _conv_ffn(x2, w["g3"], w["w_up"], w["w_conv"], w["b_conv"], w["w_down"], state,
                        seq=seq, ff_chunk=ff_chunk)
    return y, last[:, SUBLANES - (CONV_W - 1):]


def _prompt_layer(x, mem, w, bias):
    b, t, d = x.shape
    la = min(BAND_PAST, t)
    assert la == ROW_TILE and t % ROW_TILE == 0
    qa, ka, va, qb, kb, vb, ka_tail, va_tail, kb_f, vb_f, logf = _in_proj(
        x.reshape(b * t, d), w["g1"], w["w_main"], w["w_f"], w["b_f"], w["gains"], w["seg"],
        tail_period=t // ROW_TILE, time_minor=True)
    r3 = lambda a: a.reshape(b, t, WIDTH)
    oa = _band_prompt(r3(qa), r3(ka), r3(va), bias)
    logf = logf.reshape(N_HEADS, b, t).transpose(1, 0, 2)
    crow = _cumsum_lanes(logf.reshape(b * N_HEADS, t)).reshape(b, N_HEADS, t)
    logf = logf.transpose(0, 2, 1)
    ob = _fox_prompt(r3(qb), r3(kb), r3(vb), crow)
    mk_f, mv_f, mk, mv = _mem_kv(mem, w["g_mem"], w["w_mkv"], w["g_mk"])
    dff = w["w_down"].shape[0]
    y, conv = _post_attention(x.reshape(b * t, d), oa, ob, mk, mv, jnp.zeros((b, CONV_W - 1, dff), F32), w,
                              seq=t, ff_chunk=dff // 11)
    heads = lambda a, n: a.reshape(b, N_HEADS, HEAD_DIM, n).transpose(0, 3, 1, 2)
    mem_heads = lambda a: (a.reshape(b, mem.shape[1], 2, N_MEM_HEADS, LANES).transpose(0, 1, 3, 2, 4)
                           .reshape(b, mem.shape[1], N_MEM_HEADS, d // N_MEM_HEADS))
    return (y.reshape(b, t, d), heads(ka_tail, la), heads(va_tail, la), heads(kb_f, t), heads(vb_f, t), logf,
            mem_heads(mk_f), mem_heads(mv_f), conv)


def _sample_layer(x, cache_a_k, cache_a_v, cache_b_k, cache_b_v, cache_b_logf, cache_mem_k, cache_mem_v,
                  state_conv, w, bias):
    b, s, d = x.shape
    past = cache_b_k.shape[1]
    assert ROW_TILE % s == 0 and (b * s) % ROW_TILE == 0
    qa, ka, va, qb, kb, vb, ka_f, va_f, kb_f, vb_f, logf = _in_proj(
        x.reshape(b * s, d), w["g1"], w["w_main"], w["w_f"], w["b_f"], w["gains"], w["seg"], tail_period=1,
        time_minor=False)
    r3 = lambda a: a.reshape(b, s, WIDTH)
    flat = lambda a: a.reshape(a.shape[0], a.shape[1], -1)
    time_minor = lambda a: a.transpose(0, 2, 3, 1)
    oa = _band_sample(r3(qa), r3(ka), r3(va), time_minor(cache_a_k), time_minor(cache_a_v), bias)
    logf = logf.reshape(N_HEADS, b, s).transpose(1, 0, 2)
    total = past + s
    all_logf = jnp.concatenate([cache_b_logf.astype(F32).transpose(0, 2, 1), logf], axis=2)
    all_logf = jnp.pad(all_logf, ((0, 0), (0, 0), (0, -total % LANES)))
    logf = logf.transpose(0, 2, 1)
    cum = _cumsum_lanes(all_logf.reshape(b * N_HEADS, -1)).reshape(b, N_HEADS, -1)
    crow_new = cum[:, :, past:total]
    pad_rows = lambda a: jnp.pad(a, ((0, 0), (0, -s % LANES), (0, 0)))
    ob = _fox_sample(r3(qb), pad_rows(r3(kb)), pad_rows(r3(vb)), time_minor(cache_b_k), time_minor(cache_b_v),
                     cum[:, :, :past], jnp.pad(crow_new, ((0, 0), (0, 0), (0, -s % LANES))))
    dff = w["w_down"].shape[0]
    y, conv = _post_attention(x.reshape(b * s, d), oa, ob, _cache_mem_rows(cache_mem_k.astype(F32)),
                              _cache_mem_rows(cache_mem_v.astype(F32)), state_conv.astype(F32), w,
                              seq=s, ff_chunk=dff // 11)
    heads = lambda a: a.reshape(b, s, N_HEADS, HEAD_DIM)
    return y.reshape(b, s, d), heads(ka_f), heads(va_f), heads(kb_f), heads(vb_f), logf, conv


def kernel(x_prompt, x_sample, cache_a_k, cache_a_v, cache_b_k, cache_b_v, cache_b_logf, cache_mem_k, cache_mem_v, state_conv, mem_prompt, w_in, b_f, g_qa, g_ka, rel_bias, g_qb, g_kb, w_o, g_norm1, g_norm2, g_mem, w_mq, w_mkv, g_mq, g_mk, w_mo, g_norm3, w_up, w_conv, b_conv, w_down):
    depth = w_in.shape[0]
    xp, xs = x_prompt, x_sample
    outs_p, outs_s = [], []
    for l in range(depth):
        w = _layer_weights(l, w_in, b_f, g_qa, g_ka, rel_bias, g_qb, g_kb, w_o, g_norm1, g_norm2, g_mem, w_mq,
                           w_mkv, g_mq, g_mk, w_mo, g_norm3, w_up, w_conv, b_conv, w_down)
        bias = _band_bias(w["rel"])
        xp, *rest_p = _prompt_layer(xp, mem_prompt, w, bias)
        xs, *rest_s = _sample_layer(xs, cache_a_k[l], cache_a_v[l], cache_b_k[l], cache_b_v[l], cache_b_logf[l],
                                    cache_mem_k[l], cache_mem_v[l], state_conv[l], w, bias)
        outs_p.append(rest_p)
        outs_s.append(rest_s)
    stack = lambda outs: [jnp.stack(col, axis=0) for col in zip(*outs)]
    return (xp, xs, *stack(outs_p), *stack(outs_s))
```

```python
import functools

import jax
import jax.numpy as jnp
from jax import lax
from jax.experimental import pallas as pl
from jax.experimental.pallas import tpu as pltpu

F32 = jnp.float32
BF16 = jnp.bfloat16
EPS = 1e-6
NEG_INF = float("-inf")

HEAD_DIM = 64
N_HEADS = 8
WIDTH = N_HEADS * HEAD_DIM
CHUNK = 64
N_PREV_CHUNKS = 8
BAND_PAST = N_PREV_CHUNKS * CHUNK
REL_CLIP = 128
N_MEM_HEADS = 4
CONV_W = 3

ROW_TILE = 512
FFN_TILE = 1024
MID_TILE = 1024
BAND_Q = 256
BAND_WIN = BAND_PAST + BAND_Q
FOX_BLOCK = 512
FOX_CACHE_BLOCK = 2048
FOX_BAND = 64
MXU_COLS = 256
LANES = 128
SUBLANES = 8
BF16_ROWS = 16
VMEM_LIMIT = 56 * 1024 * 1024
LOG2E = 1.4426950408889634


def _dot(a, b):
    return jnp.dot(a, b, preferred_element_type=F32)


def _dot_nt(a, b):
    return lax.dot_general(a, b, (((1,), (1,)), ((), ())), preferred_element_type=F32)


def _rms(x, g):
    return x * lax.rsqrt(jnp.mean(x * x, axis=-1, keepdims=True) + EPS) * g


def _split3(x):
    hi = x.astype(BF16)
    r1 = x - hi.astype(F32)
    mid = r1.astype(BF16)
    lo = (r1 - mid.astype(F32)).astype(BF16)
    return hi, mid, lo


def _store_heads(ref, x):
    rows = x.shape[0]
    for h in range(N_HEADS):
        ref[pl.ds(h, rows, stride=N_HEADS), :] = x[:, h * HEAD_DIM:(h + 1) * HEAD_DIM]


def _resident(a):
    return pl.BlockSpec(a.shape, lambda i: (0,) * a.ndim, pipeline_mode=pl.Buffered(1))


def _params(*sem):
    return pltpu.CompilerParams(dimension_semantics=sem, vmem_limit_bytes=VMEM_LIMIT)


def _inproj_kernel(x_ref, g1_ref, w_ref, wf_ref, bf_ref, gains_ref, seg_ref,
                   qa_ref, ka_ref, va_ref, qb_ref, kb_ref, vb_ref,
                   kaf_ref, vaf_ref, kbf_ref, vbf_ref, logf_ref, *, tail_period, time_minor):
    h = _rms(x_ref[...], g1_ref[...]).astype(BF16)
    seg = seg_ref[...]

    def store_f32(ref, x):
        if time_minor:
            ref[...] = x.T
        else:
            _store_heads(ref, x)

    def proj(g):
        return _dot(h, w_ref[:, g * WIDTH:(g + 1) * WIDTH])

    def head_norm(p, row):
        p2 = (p * p).astype(BF16)
        ss = jnp.concatenate([_dot(p2[:, :MXU_COLS], seg), _dot(p2[:, MXU_COLS:], seg)], axis=1)
        return p * lax.rsqrt(ss * (1.0 / HEAD_DIM) + EPS) * gains_ref[row:row + 1, :]

    is_tail = (pl.program_id(0) % tail_period) == tail_period - 1
    scale = HEAD_DIM ** -0.5 * LOG2E

    kb = head_norm(proj(4), 3)
    kb_ref[...] = kb.astype(BF16)
    store_f32(kbf_ref.at[0], kb)
    vb = proj(5)
    vb_ref[...] = vb.astype(BF16)
    store_f32(vbf_ref.at[0], vb)
    z = _dot_nt(wf_ref[...], h)[:N_HEADS] + bf_ref[...]
    logf_ref[...] = jnp.minimum(z, 0.0) - jnp.log1p(jnp.exp(-jnp.abs(z)))
    qb_ref[...] = (head_norm(proj(3), 2) * scale).astype(BF16)
    qa_ref[...] = (head_norm(proj(0), 0) * scale).astype(BF16)
    ka = head_norm(proj(1), 1)
    ka_ref[...] = ka.astype(BF16)
    va = proj(2)
    va_ref[...] = va.astype(BF16)

    @pl.when(is_tail)
    def _():
        store_f32(kaf_ref.at[0], ka)
        store_f32(vaf_ref.at[0], va)


def _in_proj(x2d, g1, w_main, w_f, b_f, gains, seg, *, tail_period, time_minor):
    n, d = x2d.shape
    tm = ROW_TILE
    assert n % tm == 0 and (n // tm) % tail_period == 0
    n_tail = n // tm // tail_period
    row_bf = jax.ShapeDtypeStruct((n, WIDTH), BF16)
    rows = pl.BlockSpec((tm, WIDTH), lambda i: (i, 0))
    if time_minor:
        head_f = jax.ShapeDtypeStruct((n_tail, WIDTH, tm * tail_period), F32)
        tail_f = jax.ShapeDtypeStruct((n_tail, WIDTH, tm), F32)
        head_rows = pl.BlockSpec((1, WIDTH, tm), lambda i: (i // tail_period, 0, i % tail_period))
        tail = pl.BlockSpec((1, WIDTH, tm), lambda i: (i // tail_period, 0, 0))
    else:
        head_f = jax.ShapeDtypeStruct((n // tm, tm * N_HEADS, HEAD_DIM), F32)
        tail_f = jax.ShapeDtypeStruct((n_tail, tm * N_HEADS, HEAD_DIM), F32)
        head_rows = pl.BlockSpec((1, tm * N_HEADS, HEAD_DIM), lambda i: (i, 0, 0))
        tail = pl.BlockSpec((1, tm * N_HEADS, HEAD_DIM), lambda i: (i // tail_period, 0, 0))
    full = _resident
    return pl.pallas_call(
        functools.partial(_inproj_kernel, tail_period=tail_period, time_minor=time_minor),
        grid=(n // tm,),
        in_specs=[pl.BlockSpec((tm, d), lambda i: (i, 0)), full(g1), full(w_main), full(w_f),
                  full(b_f), full(gains), full(seg)],
        out_specs=[rows] * 6 + [tail, tail, head_rows, head_rows, pl.BlockSpec((N_HEADS, tm), lambda i: (0, i))],
        out_shape=[row_bf] * 6 + [tail_f, tail_f, head_f, head_f, jax.ShapeDtypeStruct((N_HEADS, n), F32)],
        compiler_params=_params("arbitrary"),
        name="in_proj",
    )(x2d, g1, w_main, w_f, b_f, gains, seg)


def _store_mem_rows(ref, x, hh):
    n = x.shape[0]
    for lg in range(2):
        ref[pl.ds(lg * N_MEM_HEADS + hh, n, stride=2 * N_MEM_HEADS), :] = x[:, lg * LANES:(lg + 1) * LANES]


def _memkv_kernel(mem_ref, g_ref, w_ref, gk_ref, kf_ref, vf_ref, kb_ref, vb_ref):
    d = mem_ref.shape[-1]
    hd = d // N_MEM_HEADS
    h = _rms(mem_ref[0], g_ref[...]).astype(BF16)
    kv = _dot(h, w_ref[...])
    for hh in range(N_MEM_HEADS):
        cs = slice(hh * hd, (hh + 1) * hd)
        k = _rms(kv[:, cs], gk_ref[...])
        v = kv[:, d + hh * hd:d + (hh + 1) * hd]
        _store_mem_rows(kf_ref.at[0], k, hh)
        _store_mem_rows(vf_ref.at[0], v, hh)
        kb_ref[0, :, cs] = k.astype(BF16)
        vb_ref[0, :, cs] = v.astype(BF16)


def _mem_kv(mem, g_mem, w_mkv, g_mk):
    b, n, d = mem.shape
    assert d == N_MEM_HEADS * 2 * LANES
    blk = pl.BlockSpec((1, n, d), lambda i: (i, 0, 0))
    blk_f = pl.BlockSpec((1, n * 2 * N_MEM_HEADS, LANES), lambda i: (i, 0, 0))
    full = _resident
    f = jax.ShapeDtypeStruct((b, n * 2 * N_MEM_HEADS, LANES), F32)
    h = jax.ShapeDtypeStruct((b, n, d), BF16)
    return pl.pallas_call(
        _memkv_kernel, grid=(b,),
        in_specs=[blk, full(g_mem), full(w_mkv), full(g_mk)],
        out_specs=[blk_f, blk_f, blk, blk], out_shape=[f, f, h, h],
        compiler_params=_params("arbitrary"), name="mem_kv",
    )(mem, g_mem, w_mkv, g_mk)


def _cumsum_kernel(x_ref, o_ref):
    r, t = x_ref.shape
    k = lax.broadcasted_iota(jnp.int32, (LANES, LANES), 0)
    j = lax.broadcasted_iota(jnp.int32, (LANES, LANES), 1)
    tri = jnp.where(k <= j, 1.0, 0.0).astype(BF16)
    carry = jnp.zeros((r, 1), F32)
    for blk in range(t // LANES):
        hi, mid, lo = _split3(x_ref[:, blk * LANES:(blk + 1) * LANES])
        cs = _dot(hi, tri) + _dot(mid, tri) + _dot(lo, tri)
        o_ref[:, blk * LANES:(blk + 1) * LANES] = (cs + carry) * LOG2E
        carry = carry + cs[:, LANES - 1:LANES]


def _cumsum_lanes(x):
    assert x.shape[1] % LANES == 0
    return pl.pallas_call(_cumsum_kernel, out_shape=jax.ShapeDtypeStruct(x.shape, F32), name="cumsum")(x)


def _band_bias_kernel(tbl_ref, o_ref):
    nrel = tbl_ref.shape[1]
    qblk, win = o_ref.shape[1:]
    wrap = qblk + win
    jp = lax.broadcasted_iota(jnp.int32, (nrel, wrap), 1)
    r = lax.broadcasted_iota(jnp.int32, (nrel, wrap), 0)
    d = jnp.where(jp < win, jp, jp - wrap)
    rel = jnp.clip(BAND_PAST - d, -REL_CLIP, REL_CLIP) + REL_CLIP
    onehot = jnp.where(rel == r, 1.0, 0.0).astype(BF16)
    hi, mid, lo = _split3(tbl_ref[...])
    base = (_dot(hi, onehot) + _dot(mid, onehot) + _dot(lo, onehot)) * LOG2E
    qc = lax.broadcasted_iota(jnp.int32, (qblk, win), 0) // CHUNK
    kc = lax.broadcasted_iota(jnp.int32, (qblk, win), 1) // CHUNK
    inband = (kc >= qc) & (kc <= qc + N_PREV_CHUNKS)
    for h in range(N_HEADS):
        row = jnp.broadcast_to(base[h:h + 1, :], (qblk, wrap))
        toeplitz = pltpu.roll(row, 0, 1, stride=1, stride_axis=0)
        o_ref[h] = jnp.where(inband, toeplitz[:, :win], NEG_INF)


def _band_bias(rel_table):
    nrel = 2 * REL_CLIP + 1
    pad = -nrel % LANES
    tbl = jnp.pad(rel_table.astype(F32), ((0, 0), (0, pad)))
    return pl.pallas_call(
        _band_bias_kernel, out_shape=jax.ShapeDtypeStruct((N_HEADS, BAND_Q, BAND_WIN), F32),
        name="band_bias")(tbl)


def _softmax_pv(s, vs, transposed=None):
    transposed = transposed or (False,) * len(vs)
    m = jnp.max(s, axis=-1, keepdims=True)
    e = jnp.exp2(s - m)
    l = jnp.sum(e, axis=-1, keepdims=True)
    eb = e.astype(BF16)
    o, lo = None, 0
    for v, vt in zip(vs, transposed):
        keys = v.shape[1] if vt else v.shape[0]
        part = _dot_nt(eb[:, lo:lo + keys], v) if vt else _dot(eb[:, lo:lo + keys], v)
        o = part if o is None else o + part
        lo += keys
    return o * (1.0 / l)


def _band_prompt_kernel(q_ref, k0_ref, k1_ref, k2_ref, v0_ref, v1_ref, v2_ref, bias_ref, o_ref, s_sc, m_sc, qm_sc):
    i = pl.program_id(1)
    tq = q_ref.shape[1]
    k_refs, v_refs = (k0_ref, k1_ref, k2_ref), (v0_ref, v1_ref, v2_ref)
    bands = [slice(r, r + CHUNK) for r in range(0, tq, CHUNK)]
    n_groups = BAND_WIN // LANES
    per_block = tq // LANES
    group = lambda g: slice(g * LANES, (g + 1) * LANES)
    visible = lambda c: [g for g in range(n_groups) if 2 * g + 1 >= c and 2 * g <= c + N_PREV_CHUNKS]
    low = _low_half(tq)
    for h in range(N_HEADS):
        q_pair = _pair(q_ref, h)
        qm_sc[h] = jnp.where(low if h % 2 == 0 else ~low, q_pair, jnp.zeros_like(q_pair))

    def logits(early):
        for h in range(N_HEADS):
            qm = qm_sc[h]
            parts = [_dot_nt(qm, _pair(k_ref, h)) for k_ref in k_refs]
            for c, rows in enumerate(bands):
                mx = None
                for g in visible(c):
                    p, gp = divmod(g, per_block)
                    sg = parts[p][rows, group(gp)] + bias_ref[h, rows, group(g)]
                    if early and p < 2:
                        sg = jnp.where(i + p >= 2, sg, NEG_INF)
                    s_sc[h, rows, group(g)] = sg
                    mx = sg if mx is None else jnp.maximum(mx, sg)
                m_sc[h, rows] = jnp.broadcast_to(jnp.max(mx, axis=-1, keepdims=True), (CHUNK, LANES))

    values = functools.partial(_band_values, v_refs, o_ref, s_sc, m_sc, bands, visible)

    @pl.when(i >= 2)
    def _():
        logits(False)

    @pl.when(i < 2)
    def _():
        logits(True)

    values()


def _band_values(v_refs, o_ref, s_sc, m_sc, bands, visible):
    tq = o_ref.shape[1]
    n_groups = s_sc.shape[2] // LANES
    group = lambda g: slice(g * LANES, (g + 1) * LANES)
    for h in range(N_HEADS):
        p_bands, invs = [], []
        for c, rows in enumerate(bands):
            m = m_sc[h, rows]
            rs, ps = None, []
            for g in range(n_groups):
                if g in visible(c):
                    e = jnp.exp2(s_sc[h, rows, group(g)] - m)
                    ps.append(e.astype(BF16))
                    rs = e if rs is None else rs + e
                else:
                    ps.append(jnp.zeros((CHUNK, LANES), BF16))
            p_bands.append(jnp.concatenate(ps, axis=1))
            invs.append(1.0 / jnp.sum(rs, axis=-1, keepdims=True))
        p_all = jnp.concatenate(p_bands, axis=0)
        o = None
        for p, v_ref in enumerate(v_refs):
            part = _dot(p_all[:, p * tq:(p + 1) * tq], _pair(v_ref, h))
            o = part if o is None else o + part
        half = slice((h % 2) * HEAD_DIM, (h % 2 + 1) * HEAD_DIM)
        o_ref[0, :, h * HEAD_DIM:(h + 1) * HEAD_DIM] = (o * jnp.concatenate(invs, axis=0))[:, half].astype(BF16)


def _band_prompt(qa, ka, va, bias):
    b, t, _ = qa.shape
    tq = BAND_Q
    assert t % tq == 0
    blk = lambda back: pl.BlockSpec((1, tq, WIDTH), lambda bi, i: (bi, jnp.maximum(i - back, 0), 0))
    return pl.pallas_call(
        _band_prompt_kernel, grid=(b, t // tq),
        in_specs=[blk(0), blk(2), blk(1), blk(0), blk(2), blk(1), blk(0),
                  pl.BlockSpec(bias.shape, lambda bi, i: (0, 0, 0))],
        out_specs=blk(0), out_shape=jax.ShapeDtypeStruct((b, t, WIDTH), BF16),
        scratch_shapes=[pltpu.VMEM((N_HEADS, tq, BAND_WIN), F32), pltpu.VMEM((N_HEADS, tq, LANES), F32),
                        pltpu.VMEM((N_HEADS, tq, LANES), BF16)],
        compiler_params=_params("arbitrary", "arbitrary"), name="band_prompt",
    )(qa, ka, ka, ka, va, va, va, bias)


def _band_sample_kernel(q_ref, kc_ref, vc_ref, kn_ref, vn_ref, bias_ref, o_ref):
    for h in range(N_HEADS):
        hs = slice(h * HEAD_DIM, (h + 1) * HEAD_DIM)
        q = q_ref[0, :, hs]
        s = jnp.concatenate([_dot(q, kc_ref[0, h].astype(BF16)), _dot_nt(q, kn_ref[0, :, hs])], axis=1)
        s = s + bias_ref[h, :CHUNK, :BAND_PAST + CHUNK]
        o = _softmax_pv(s, [vc_ref[0, h].astype(BF16), vn_ref[0, :, hs]], transposed=(True, False))
        o_ref[0, :, hs] = o.astype(BF16)


def _band_sample(qa, ka, va, cache_k, cache_v, bias):
    b, s, _ = qa.shape
    la = cache_k.shape[3]
    assert s == CHUNK and la == BAND_PAST
    new = pl.BlockSpec((1, s, WIDTH), lambda bi: (bi, 0, 0))
    old = pl.BlockSpec((1, N_HEADS, HEAD_DIM, la), lambda bi: (bi, 0, 0, 0))
    return pl.pallas_call(
        _band_sample_kernel, grid=(b,),
        in_specs=[new, old, old, new, new, pl.BlockSpec(bias.shape, lambda bi: (0, 0, 0))],
        out_specs=new, out_shape=jax.ShapeDtypeStruct((b, s, WIDTH), BF16),
        compiler_params=_params("arbitrary"), name="band_sample",
    )(qa, cache_k, cache_v, ka, va, bias)


EXT_PARTS = 3
EXT_LANES = EXT_PARTS * N_HEADS


def _bf16_pieces(x):
    pieces = []
    for _ in range(EXT_PARTS - 1):
        pieces.append(x.astype(BF16).astype(F32))
        x = x - pieces[-1]
    return pieces + [x]


def _key_ext_kernel(c_ref, o_ref, *, transpose):
    c = c_ref[0]
    t = c.shape[1]
    x = jnp.concatenate(_bf16_pieces(-c) + [jnp.ones((EXT_LANES, t), F32), jnp.zeros((LANES - 2 * EXT_LANES, t), F32)],
                        axis=0)
    o_ref[0] = (x.T if transpose else x).astype(BF16)


def _key_ext(c, transpose):
    b, nh, t = c.shape
    assert nh == N_HEADS and t % LANES == 0
    out = (b, t, LANES) if transpose else (b, LANES, t)
    return pl.pallas_call(
        functools.partial(_key_ext_kernel, transpose=transpose), grid=(b,),
        in_specs=[pl.BlockSpec((1, nh, t), lambda i: (i, 0, 0))],
        out_specs=pl.BlockSpec((1,) + out[1:], lambda i: (i, 0, 0)),
        out_shape=jax.ShapeDtypeStruct(out, BF16), compiler_params=_params("arbitrary"), name="key_ext")(c)


def _query_ext(cq, h):
    lane = lax.broadcasted_iota(jnp.int32, cq.shape, 1)
    x = jnp.where((lane < EXT_LANES) & ((lane & (N_HEADS - 1)) == h), 1.0, 0.0)
    for i, piece in enumerate(_bf16_pieces(cq)):
        x = jnp.where(lane == EXT_LANES + i * N_HEADS + h, piece, x)
    return x.astype(BF16)


def _fox_scratch(tq, tk):
    rows = lambda w: pltpu.VMEM((N_HEADS, tq, w), F32)
    return [rows(tk), rows(LANES), rows(LANES), rows(LANES),
            pltpu.VMEM((N_HEADS, tq, 2 * LANES), BF16), pltpu.VMEM((N_HEADS // 2, tq, LANES), F32)]


def _low_half(rows):
    return lax.broadcasted_iota(jnp.int32, (rows, LANES), 1) < HEAD_DIM


def _fox_init(q_ref, cq_ref, sc):
    _, m_sc, _, l_sc, qm_sc, acc_sc = sc
    tq = m_sc.shape[1]
    low = _low_half(tq)
    for h in range(N_HEADS):
        cq = jnp.broadcast_to(cq_ref[0, h:h + 1, :], (LANES, cq_ref.shape[2])).T[:tq]
        q_pair = q_ref[0, :, (h // 2) * LANES:(h // 2 + 1) * LANES]
        q_head = jnp.where(low if h % 2 == 0 else ~low, q_pair, jnp.zeros_like(q_pair))
        qm_sc[h] = jnp.concatenate([q_head, _query_ext(cq, h)], axis=1)
    m_sc[...] = jnp.full(m_sc.shape, NEG_INF, F32)
    l_sc[...] = jnp.zeros(l_sc.shape, F32)
    acc_sc[...] = jnp.zeros(acc_sc.shape, F32)


def _fox_block(qk, pv, tk, causal, sc):
    s_sc, m_sc, a_sc, l_sc, qm_sc, acc_sc = sc
    tq = s_sc.shape[1]
    band = min(tq, FOX_BAND)
    bands = [slice(r, r + band) for r in range(0, tq, band)]
    n_groups = tk // LANES
    group = lambda g: slice(g * LANES, (g + 1) * LANES)

    def visible(rows):
        if not causal:
            return [(g, False) for g in range(n_groups)]
        return [(g, (g + 1) * LANES - 1 > rows.start) for g in range(n_groups) if g * LANES < rows.stop]

    if causal:
        ahead = (lax.broadcasted_iota(jnp.int32, (band, LANES), 0)
                 - lax.broadcasted_iota(jnp.int32, (band, LANES), 1))
    for h in range(N_HEADS):
        s = qk(h, qm_sc[h])
        for rows in bands:
            mx = None
            for g, masked in visible(rows):
                sg = s[rows, group(g)]
                if masked:
                    sg = jnp.where(ahead >= g * LANES - rows.start, sg, NEG_INF)
                s_sc[h, rows, group(g)] = sg
                mx = sg if mx is None else jnp.maximum(mx, sg)
            m_prev = m_sc[h, rows]
            m_new = jnp.maximum(m_prev, jnp.max(mx, axis=-1, keepdims=True))
            a_sc[h, rows] = jnp.exp2(m_prev - m_new)
            m_sc[h, rows] = m_new
    for h in range(N_HEADS):
        p_bands = []
        for rows in bands:
            seen = visible(rows)
            m_new = m_sc[h, rows]
            rs, ps = None, []
            for g, _ in seen:
                p = jnp.exp2(s_sc[h, rows, group(g)] - m_new)
                ps.append(p.astype(BF16))
                rs = p if rs is None else rs + p
            l_sc[h, rows] = a_sc[h, rows] * l_sc[h, rows] + rs
            ps += [jnp.zeros((band, LANES), BF16)] * (n_groups - len(seen))
            p_bands.append(jnp.concatenate(ps, axis=1))
        half = slice((h % 2) * HEAD_DIM, (h % 2 + 1) * HEAD_DIM)
        new = a_sc[h] * acc_sc[h // 2] + pv(h, jnp.concatenate(p_bands, axis=0))
        acc_sc[h // 2, :, half] = new[:, half]


def _fox_finish(o_ref, sc):
    l_sc, acc_sc = sc[3], sc[5]
    low = _low_half(acc_sc.shape[1])
    total = lambda h: jnp.sum(l_sc[h], axis=-1, keepdims=True)
    for g in range(N_HEADS // 2):
        inv = jnp.where(low, 1.0 / total(2 * g), 1.0 / total(2 * g + 1))
        o_ref[0, :, g * LANES:(g + 1) * LANES] = (acc_sc[g] * inv).astype(BF16)


def _head(ref, h):
    return ref[0, :, h * HEAD_DIM:(h + 1) * HEAD_DIM]


def _pair(ref, h):
    return ref[0, :, (h // 2) * LANES:(h // 2 + 1) * LANES]


def _row_major_products(k_ref, kext_ref, v_ref):
    return ((lambda h, qm: _dot_nt(qm, jnp.concatenate([_pair(k_ref, h), kext_ref[0]], axis=1))),
            (lambda h, p: _dot(p, _pair(v_ref, h))))


def _fox_prompt_kernel(qi_ref, kj_ref, q_ref, k_ref, v_ref, kext_ref, cq_ref, o_ref, *sc):
    p = pl.program_id(1)
    qi, kj = qi_ref[p], kj_ref[p]
    tk = k_ref.shape[1]
    qk, pv = _row_major_products(k_ref, kext_ref, v_ref)

    for first in (True, False):
        started = (kj == 0) if first else (kj > 0)

        @pl.when(started & (kj < qi))
        def _():
            if first:
                _fox_init(q_ref, cq_ref, sc)
            _fox_block(qk, pv, tk, False, sc)

        @pl.when(started & (kj == qi))
        def _():
            if first:
                _fox_init(q_ref, cq_ref, sc)
            _fox_block(qk, pv, tk, True, sc)
            _fox_finish(o_ref, sc)


def _fox_prompt(qb, kb, vb, crow):
    b, t, _ = qb.shape
    tq = FOX_BLOCK
    assert t % tq == 0
    pairs = [(i, j) for i in range(t // tq) for j in range(i + 1)]
    qi = jnp.array([p[0] for p in pairs], jnp.int32)
    kj = jnp.array([p[1] for p in pairs], jnp.int32)
    qblk = pl.BlockSpec((1, tq, WIDTH), lambda bi, p, qi, kj: (bi, qi[p], 0))
    kblk = pl.BlockSpec((1, tq, WIDTH), lambda bi, p, qi, kj: (bi, kj[p], 0))
    grid_spec = pltpu.PrefetchScalarGridSpec(
        num_scalar_prefetch=2, grid=(b, len(pairs)),
        in_specs=[qblk, kblk, kblk,
                  pl.BlockSpec((1, tq, LANES), lambda bi, p, qi, kj: (bi, kj[p], 0)),
                  pl.BlockSpec((1, N_HEADS, tq), lambda bi, p, qi, kj: (bi, 0, qi[p]))],
        out_specs=qblk, scratch_shapes=_fox_scratch(tq, tq))
    return pl.pallas_call(
        _fox_prompt_kernel, grid_spec=grid_spec, out_shape=jax.ShapeDtypeStruct((b, t, WIDTH), BF16),
        compiler_params=_params("arbitrary", "arbitrary"), name="fox_prompt",
    )(qi, kj, qb, kb, vb, _key_ext(crow, True), crow)


def _fox_sample_kernel(q_ref, kc_ref, vc_ref, kn_ref, vn_ref, kext_ref, kext_new_ref, cq_ref, o_ref, *sc):
    j = pl.program_id(1)
    last = pl.num_programs(1) - 1

    @pl.when(j == 0)
    def _():
        _fox_init(q_ref, cq_ref, sc)

    def cached_pair(ref, h):
        g = h // 2
        return ref[0, 2 * g:2 * g + 2].astype(BF16).reshape(2 * HEAD_DIM, ref.shape[3])

    _fox_block(lambda h, qm: _dot(qm, jnp.concatenate([cached_pair(kc_ref, h), kext_ref[0]], axis=0)),
               lambda h, p: _dot_nt(p, cached_pair(vc_ref, h)), kc_ref.shape[3], False, sc)

    @pl.when(j == last)
    def _():
        qk, pv = _row_major_products(kn_ref, kext_new_ref, vn_ref)
        _fox_block(qk, pv, kn_ref.shape[1], True, sc)
        _fox_finish(o_ref, sc)


def _fox_sample(qb, kb, vb, cache_k, cache_v, crow, crow_new):
    b, s, _ = qb.shape
    past = cache_k.shape[3]
    tk, tn = min(FOX_CACHE_BLOCK, past), kb.shape[1]
    assert past % tk == 0 and tn % LANES == 0 and tn >= s
    nk = past // tk
    new = lambda r: pl.BlockSpec((1, r, WIDTH), lambda bi, j: (bi, 0, 0))
    old = pl.BlockSpec((1, N_HEADS, HEAD_DIM, tk), lambda bi, j: (bi, 0, 0, j))
    return pl.pallas_call(
        _fox_sample_kernel, grid=(b, nk),
        in_specs=[new(s), old, old, new(tn), new(tn),
                  pl.BlockSpec((1, LANES, tk), lambda bi, j: (bi, 0, j)),
                  pl.BlockSpec((1, tn, LANES), lambda bi, j: (bi, 0, 0)),
                  pl.BlockSpec((1, N_HEADS, tn), lambda bi, j: (bi, 0, 0))],
        out_specs=new(s), out_shape=jax.ShapeDtypeStruct((b, s, WIDTH), BF16),
        scratch_shapes=_fox_scratch(s, tk),
        compiler_params=_params("arbitrary", "arbitrary"), name="fox_sample",
    )(qb, cache_k, cache_v, kb, vb, _key_ext(crow, False), _key_ext(crow_new, True), crow_new)


def _mid_kernel(x_ref, oa_ref, ob_ref, wo_ref, g2_ref, wq_ref, gq_ref, mk_ref, mv_ref, wmo_ref, o_ref,
                om_sc, *, seg):
    tm, d = x_ref.shape
    hd = d // N_MEM_HEADS
    x1 = x_ref[...] + _dot(oa_ref[...], wo_ref[:WIDTH, :]) + _dot(ob_ref[...], wo_ref[WIDTH:, :])
    h2 = _rms(x1, g2_ref[...]).astype(BF16)
    qm = _dot(h2, wq_ref[...])
    scale = hd ** -0.5 * LOG2E
    for hh in range(N_MEM_HEADS):
        cs = slice(hh * hd, (hh + 1) * hd)
        q = (_rms(qm[:, cs], gq_ref[...]) * scale).astype(BF16)
        def mem_head(ref, sg):
            if ref.shape[2] == d:
                return ref[sg, :, cs]
            rows = lambda lg: ref[sg, pl.ds(lg * N_MEM_HEADS + hh, ref.shape[1] // (2 * N_MEM_HEADS),
                                            stride=2 * N_MEM_HEADS), :]
            return jnp.concatenate([rows(0), rows(1)], axis=1).astype(BF16)

        for sg in range(tm // seg):
            rs = slice(sg * seg, (sg + 1) * seg)
            s = _dot_nt(q[rs], mem_head(mk_ref, sg))
            om_sc[rs, cs] = _softmax_pv(s, [mem_head(mv_ref, sg)]).astype(BF16)
    o_ref[...] = x1 + _dot(om_sc[...], wmo_ref[...])


def _cache_mem_rows(x):
    b, n, nh, hd = x.shape
    assert nh == N_MEM_HEADS and hd == 2 * LANES
    return x.reshape(b, n, nh, 2, LANES).transpose(0, 1, 3, 2, 4).reshape(b, n * 2 * nh, LANES)


def _mid(x2d, oa, ob, w_o, g2, w_mq, g_mq, mk, mv, w_mo, *, seq):
    n, d = x2d.shape
    tm = MID_TILE if seq % MID_TILE == 0 else ROW_TILE
    seg = min(seq, tm)
    tiles_per_batch = seq // seg
    nb = tm // seg
    assert n % tm == 0 and seq % seg == 0 and tm % seg == 0
    rows = lambda w: pl.BlockSpec((tm, w), lambda i: (i, 0))
    full = _resident
    mem = pl.BlockSpec((nb,) + mk.shape[1:], lambda i: (i // tiles_per_batch, 0, 0))
    return pl.pallas_call(
        functools.partial(_mid_kernel, seg=seg), grid=(n // tm,),
        in_specs=[rows(d), rows(WIDTH), rows(WIDTH), full(w_o), full(g2), full(w_mq), full(g_mq), mem, mem,
                  full(w_mo)],
        out_specs=rows(d), out_shape=jax.ShapeDtypeStruct((n, d), F32),
        scratch_shapes=[pltpu.VMEM((tm, d), BF16)],
        compiler_params=_params("arbitrary"), name="mid",
    )(x2d, oa, ob, w_o, g2, w_mq, g_mq, mk, mv, w_mo)


def _ffn_kernel(x_ref, g3_ref, wup_ref, wc_ref, bc_ref, wd_ref, st_ref, o_ref, last_ref, carry_sc, y_sc,
                *, seg, tiles_per_batch, ff_chunk):
    tm, d = x_ref.shape
    dff = wd_ref.shape[0]
    i = pl.program_id(0)
    h3 = _rms(x_ref[...], g3_ref[...]).astype(BF16)
    if tiles_per_batch > 1:
        @pl.when(i == 0)
        def _():
            carry_sc[...] = jnp.zeros(carry_sc.shape, F32)
    rowid = lax.broadcasted_iota(jnp.int32, (SUBLANES, 1), 0)
    first_tile = (i % tiles_per_batch) == 0
    for c in range(dff // ff_chunk):
        cs = slice(c * ff_chunk, (c + 1) * ff_chunk)
        gate = _dot(h3, wup_ref[:, cs])
        val = _dot(h3, wup_ref[:, dff + c * ff_chunk:dff + (c + 1) * ff_chunk])
        w0, w1, w2 = wc_ref[0:1, cs], wc_ref[1:2, cs], wc_ref[2:3, cs]
        for sg in range(tm // seg):
            rows = slice(sg * seg, (sg + 1) * seg)
            g = gate[rows]
            if tiles_per_batch == 1:
                prev = st_ref[sg, :, cs]
            else:
                prev = jnp.where(first_tile, st_ref[0, :, cs], carry_sc[SUBLANES - 2:, cs])
            p2, p1 = prev[0:1], prev[1:2]
            s1 = pltpu.roll(g, 1, 0)
            s2 = pltpu.roll(g, 2, 0)
            top1 = jnp.where(rowid == 0, p1, s1[:SUBLANES])
            top2 = jnp.where(rowid == 0, p2, jnp.where(rowid == 1, p1, s2[:SUBLANES]))
            s1 = jnp.concatenate([top1, s1[SUBLANES:]], axis=0)
            s2 = jnp.concatenate([top2, s2[SUBLANES:]], axis=0)
            conv = s2 * w0 + s1 * w1 + g * w2 + bc_ref[:, cs]
            y_sc[rows, cs] = (conv * jax.nn.sigmoid(conv) * val[rows]).astype(BF16)
            last_ref[sg, :, cs] = g[seg - SUBLANES:]
        if tiles_per_batch > 1:
            carry_sc[:, cs] = gate[tm - SUBLANES:]
    o_ref[...] = x_ref[...] + _dot(y_sc[...], wd_ref[...])


def _conv_ffn(x2d, g3, w_up, w_conv, b_conv, w_down, state, *, seq, ff_chunk):
    n, d = x2d.shape
    dff = w_down.shape[0]
    tm = min(FFN_TILE, n)
    seg = min(seq, tm)
    tiles_per_batch = seq // seg
    nb = tm // seg
    assert n % tm == 0 and seq % seg == 0 and tm % seg == 0 and dff % ff_chunk == 0 and seg >= SUBLANES
    rows = pl.BlockSpec((tm, d), lambda i: (i, 0))
    full = _resident
    per_batch = lambda r: pl.BlockSpec((nb, r, dff), lambda i: (i // tiles_per_batch, 0, 0))
    return pl.pallas_call(
        functools.partial(_ffn_kernel, seg=seg, tiles_per_batch=tiles_per_batch, ff_chunk=ff_chunk),
        grid=(n // tm,),
        in_specs=[rows, full(g3), full(w_up), full(w_conv), full(b_conv), full(w_down), per_batch(CONV_W - 1)],
        out_specs=[rows, per_batch(SUBLANES)],
        out_shape=[jax.ShapeDtypeStruct((n, d), F32), jax.ShapeDtypeStruct((n // seq, SUBLANES, dff), F32)],
        scratch_shapes=[pltpu.VMEM((SUBLANES, dff), F32), pltpu.VMEM((tm, dff), BF16)],
        compiler_params=_params("arbitrary"), name="conv_ffn",
    )(x2d, g3, w_up, w_conv, b_conv, w_down, state)


def _row(v):
    return v.astype(F32).reshape(1, -1)


def _layer_weights(l, w_in, b_f, g_qa, g_ka, rel_bias, g_qb, g_kb, w_o, g_norm1, g_norm2, g_mem, w_mq, w_mkv,
                   g_mq, g_mk, w_mo, g_norm3, w_up, w_conv, b_conv, w_down):
    n_main = 6 * WIDTH
    tile = lambda g: jnp.tile(g[l].astype(F32), N_HEADS)
    blk = jnp.arange(MXU_COLS) // HEAD_DIM
    return dict(
        w_main=w_in[l][:, :n_main].astype(BF16),
        w_f=jnp.pad(w_in[l][:, n_main:].T, ((0, BF16_ROWS - N_HEADS), (0, 0))).astype(BF16),
        b_f=b_f[l].astype(F32).reshape(N_HEADS, 1),
        gains=jnp.stack([tile(g_qa), tile(g_ka), tile(g_qb), tile(g_kb)]),
        seg=(blk[:, None] == blk[None, :]).astype(BF16),
        rel=rel_bias[l],
        w_o=w_o[l].astype(BF16), g1=_row(g_norm1[l]), g2=_row(g_norm2[l]), g3=_row(g_norm3[l]),
        g_mem=_row(g_mem[l]), w_mq=w_mq[l].astype(BF16), w_mkv=w_mkv[l].astype(BF16),
        g_mq=_row(g_mq[l]), g_mk=_row(g_mk[l]), w_mo=w_mo[l].astype(BF16),
        w_up=w_up[l].astype(BF16), w_conv=w_conv[l].astype(F32), b_conv=_row(b_conv[l]),
        w_down=w_down[l].astype(BF16))


def _post_attention(x2d, oa, ob, mk, mv, state, w, *, seq, ff_chunk):
    n, d = x2d.shape
    x2 = _mid(x2d, oa.reshape(n, WIDTH), ob.reshape(n, WIDTH), w["w_o"], w["g2"], w["w_mq"], w["g_mq"],
              mk, mv, w["w_mo"], seq=seq)
    y, last = _conv_ffn(x2, w["g3"], w["w_up"], w["w_conv"], w["b_conv"], w["w_down"], state,
                        seq=seq, ff_chunk=ff_chunk)
    return y, last[:, SUBLANES - (CONV_W - 1):]


def _prompt_layer(x, mem, w, bias):
    b, t, d = x.shape
    la = min(BAND_PAST, t)
    assert la == ROW_TILE and t % ROW_TILE == 0
    qa, ka, va, qb, kb, vb, ka_tail, va_tail, kb_f, vb_f, logf = _in_proj(
        x.reshape(b * t, d), w["g1"], w["w_main"], w["w_f"], w["b_f"], w["gains"], w["seg"],
        tail_period=t // ROW_TILE, time_minor=True)
    r3 = lambda a: a.reshape(b, t, WIDTH)
    oa = _band_prompt(r3(qa), r3(ka), r3(va), bias)
    logf = logf.reshape(N_HEADS, b, t).transpose(1, 0, 2)
    crow = _cumsum_lanes(logf.reshape(b * N_HEADS, t)).reshape(b, N_HEADS, t)
    logf = logf.transpose(0, 2, 1)
    ob = _fox_prompt(r3(qb), r3(kb), r3(vb), crow)
    mk_f, mv_f, mk, mv = _mem_kv(mem, w["g_mem"], w["w_mkv"], w["g_mk"])
    dff = w["w_down"].shape[0]
    y, conv = _post_attention(x.reshape(b * t, d), oa, ob, mk, mv, jnp.zeros((b, CONV_W - 1, dff), F32), w,
                              seq=t, ff_chunk=dff // 11)
    heads = lambda a, n: a.reshape(b, N_HEADS, HEAD_DIM, n).transpose(0, 3, 1, 2)
    mem_heads = lambda a: (a.reshape(b, mem.shape[1], 2, N_MEM_HEADS, LANES).transpose(0, 1, 3, 2, 4)
                           .reshape(b, mem.shape[1], N_MEM_HEADS, d // N_MEM_HEADS))
    return (y.reshape(b, t, d), heads(ka_tail, la), heads(va_tail, la), heads(kb_f, t), heads(vb_f, t), logf,
            mem_heads(mk_f), mem_heads(mv_f), conv)


def _sample_layer(x, cache_a_k, cache_a_v, cache_b_k, cache_b_v, cache_b_logf, cache_mem_k, cache_mem_v,
                  state_conv, w, bias):
    b, s, d = x.shape
    past = cache_b_k.shape[1]
    assert ROW_TILE % s == 0 and (b * s) % ROW_TILE == 0
    qa, ka, va, qb, kb, vb, ka_f, va_f, kb_f, vb_f, logf = _in_proj(
        x.reshape(b * s, d), w["g1"], w["w_main"], w["w_f"], w["b_f"], w["gains"], w["seg"], tail_period=1,
        time_minor=False)
    r3 = lambda a: a.reshape(b, s, WIDTH)
    flat = lambda a: a.reshape(a.shape[0], a.shape[1], -1)
    time_minor = lambda a: a.transpose(0, 2, 3, 1)
    oa = _band_sample(r3(qa), r3(ka), r3(va), time_minor(cache_a_k), time_minor(cache_a_v), bias)
    logf = logf.reshape(N_HEADS, b, s).transpose(1, 0, 2)
    total = past + s
    all_logf = jnp.concatenate([cache_b_logf.astype(F32).transpose(0, 2, 1), logf], axis=2)
    all_logf = jnp.pad(all_logf, ((0, 0), (0, 0), (0, -total % LANES)))
    logf = logf.transpose(0, 2, 1)
    cum = _cumsum_lanes(all_logf.reshape(b * N_HEADS, -1)).reshape(b, N_HEADS, -1)
    crow_new = cum[:, :, past:total]
    pad_rows = lambda a: jnp.pad(a, ((0, 0), (0, -s % LANES), (0, 0)))
    ob = _fox_sample(r3(qb), pad_rows(r3(kb)), pad_rows(r3(vb)), time_minor(cache_b_k), time_minor(cache_b_v),
                     cum[:, :, :past], jnp.pad(crow_new, ((0, 0), (0, 0), (0, -s % LANES))))
    dff = w["w_down"].shape[0]
    y, conv = _post_attention(x.reshape(b * s, d), oa, ob, _cache_mem_rows(cache_mem_k.astype(F32)),
                              _cache_mem_rows(cache_mem_v.astype(F32)), state_conv.astype(F32), w,
                              seq=s, ff_chunk=dff // 11)
    heads = lambda a: a.reshape(b, s, N_HEADS, HEAD_DIM)
    return y.reshape(b, s, d), heads(ka_f), heads(va_f), heads(kb_f), heads(vb_f), logf, conv


def kernel(x_prompt, x_sample, cache_a_k, cache_a_v, cache_b_k, cache_b_v, cache_b_logf, cache_mem_k, cache_mem_v, state_conv, mem_prompt, w_in, b_f, g_qa, g_ka, rel_bias, g_qb, g_kb, w_o, g_norm1, g_norm2, g_mem, w_mq, w_mkv, g_mq, g_mk, w_mo, g_norm3, w_up, w_conv, b_conv, w_down):
    depth = w_in.shape[0]
    xp, xs = x_prompt, x_sample
    outs_p, outs_s = [], []
    for l in range(depth):
        w = _layer_weights(l, w_in, b_f, g_qa, g_ka, rel_bias, g_qb, g_kb, w_o, g_norm1, g_norm2, g_mem, w_mq,
                           w_mkv, g_mq, g_mk, w_mo, g_norm3, w_up, w_conv, b_conv, w_down)
        bias = _band_bias(w["rel"])
        xp, *rest_p = _prompt_layer(xp, mem_prompt, w, bias)
        xs, *rest_s = _sample_layer(xs, cache_a_k[l], cache_a_v[l], cache_b_k[l], cache_b_v[l], cache_b_logf[l],
                                    cache_mem_k[l], cache_mem_v[l], state_conv[l], w, bias)
        outs_p.append(rest_p)
        outs_s.append(rest_s)
    stack = lambda outs: [jnp.stack(col, axis=0) for col in zip(*outs)]
    return (xp, xs, *stack(outs_p), *stack(outs_s))
```

```python
import functools

import jax
import jax.numpy as jnp
from jax import lax
from jax.experimental import pallas as pl
from jax.experimental.pallas import tpu as pltpu

F32 = jnp.float32
BF16 = jnp.bfloat16
EPS = 1e-6
NEG_INF = float("-inf")

HEAD_DIM = 64
N_HEADS = 8
WIDTH = N_HEADS * HEAD_DIM
CHUNK = 64
N_PREV_CHUNKS = 8
BAND_PAST = N_PREV_CHUNKS * CHUNK
REL_CLIP = 128
N_MEM_HEADS = 4
CONV_W = 3

ROW_TILE = 512
FFN_TILE = 1024
MID_TILE = 1024
BAND_Q = 256
BAND_WIN = BAND_PAST + BAND_Q
FOX_BLOCK = 512
FOX_CACHE_BLOCK = 2048
FOX_BAND = 64
MXU_COLS = 256
LANES = 128
SUBLANES = 8
BF16_ROWS = 16
VMEM_LIMIT = 56 * 1024 * 1024
LOG2E = 1.4426950408889634


def _dot(a, b):
    return jnp.dot(a, b, preferred_element_type=F32)


def _dot_nt(a, b):
    return lax.dot_general(a, b, (((1,), (1,)), ((), ())), preferred_element_type=F32)


def _rms(x, g):
    return x * lax.rsqrt(jnp.mean(x * x, axis=-1, keepdims=True) + EPS) * g


def _split3(x):
    hi = x.astype(BF16)
    r1 = x - hi.astype(F32)
    mid = r1.astype(BF16)
    lo = (r1 - mid.astype(F32)).astype(BF16)
    return hi, mid, lo


def _store_heads(ref, x):
    rows = x.shape[0]
    for h in range(N_HEADS):
        ref[pl.ds(h, rows, stride=N_HEADS), :] = x[:, h * HEAD_DIM:(h + 1) * HEAD_DIM]


def _resident(a):
    return pl.BlockSpec(a.shape, lambda i: (0,) * a.ndim, pipeline_mode=pl.Buffered(1))


def _params(*sem):
    return pltpu.CompilerParams(dimension_semantics=sem, vmem_limit_bytes=VMEM_LIMIT)


def _inproj_kernel(x_ref, g1_ref, w_ref, wf_ref, bf_ref, gains_ref, seg_ref,
                   qa_ref, ka_ref, va_ref, qb_ref, kb_ref, vb_ref,
                   kaf_ref, vaf_ref, kbf_ref, vbf_ref, logf_ref, *, tail_period, time_minor):
    h = _rms(x_ref[...], g1_ref[...]).astype(BF16)
    seg = seg_ref[...]

    def store_f32(ref, x):
        if time_minor:
            ref[...] = x.T
        else:
            _store_heads(ref, x)

    def proj(g):
        return _dot(h, w_ref[:, g * WIDTH:(g + 1) * WIDTH])

    def head_norm(p, row):
        p2 = (p * p).astype(BF16)
        ss = jnp.concatenate([_dot(p2[:, :MXU_COLS], seg), _dot(p2[:, MXU_COLS:], seg)], axis=1)
        return p * lax.rsqrt(ss * (1.0 / HEAD_DIM) + EPS) * gains_ref[row:row + 1, :]

    is_tail = (pl.program_id(0) % tail_period) == tail_period - 1
    scale = HEAD_DIM ** -0.5 * LOG2E

    kb = head_norm(proj(4), 3)
    kb_ref[...] = kb.astype(BF16)
    store_f32(kbf_ref.at[0], kb)
    vb = proj(5)
    vb_ref[...] = vb.astype(BF16)
    store_f32(vbf_ref.at[0], vb)
    z = _dot_nt(wf_ref[...], h)[:N_HEADS] + bf_ref[...]
    logf_ref[...] = jnp.minimum(z, 0.0) - jnp.log1p(jnp.exp(-jnp.abs(z)))
    qb_ref[...] = (head_norm(proj(3), 2) * scale).astype(BF16)
    qa_ref[...] = (head_norm(proj(0), 0) * scale).astype(BF16)
    ka = head_norm(proj(1), 1)
    ka_ref[...] = ka.astype(BF16)
    va = proj(2)
    va_ref[...] = va.astype(BF16)

    @pl.when(is_tail)
    def _():
        store_f32(kaf_ref.at[0], ka)
        store_f32(vaf_ref.at[0], va)


def _in_proj(x2d, g1, w_main, w_f, b_f, gains, seg, *, tail_period, time_minor):
    n, d = x2d.shape
    tm = ROW_TILE
    assert n % tm == 0 and (n // tm) % tail_period == 0
    n_tail = n // tm // tail_period
    row_bf = jax.ShapeDtypeStruct((n, WIDTH), BF16)
    rows = pl.BlockSpec((tm, WIDTH), lambda i: (i, 0))
    if time_minor:
        head_f = jax.ShapeDtypeStruct((n_tail, WIDTH, tm * tail_period), F32)
        tail_f = jax.ShapeDtypeStruct((n_tail, WIDTH, tm), F32)
        head_rows = pl.BlockSpec((1, WIDTH, tm), lambda i: (i // tail_period, 0, i % tail_period))
        tail = pl.BlockSpec((1, WIDTH, tm), lambda i: (i // tail_period, 0, 0))
    else:
        head_f = jax.ShapeDtypeStruct((n // tm, tm * N_HEADS, HEAD_DIM), F32)
        tail_f = jax.ShapeDtypeStruct((n_tail, tm * N_HEADS, HEAD_DIM), F32)
        head_rows = pl.BlockSpec((1, tm * N_HEADS, HEAD_DIM), lambda i: (i, 0, 0))
        tail = pl.BlockSpec((1, tm * N_HEADS, HEAD_DIM), lambda i: (i // tail_period, 0, 0))
    full = _resident
    return pl.pallas_call(
        functools.partial(_inproj_kernel, tail_period=tail_period, time_minor=time_minor),
        grid=(n // tm,),
        in_specs=[pl.BlockSpec((tm, d), lambda i: (i, 0)), full(g1), full(w_main), full(w_f),
                  full(b_f), full(gains), full(seg)],
        out_specs=[rows] * 6 + [tail, tail, head_rows, head_rows, pl.BlockSpec((N_HEADS, tm), lambda i: (0, i))],
        out_shape=[row_bf] * 6 + [tail_f, tail_f, head_f, head_f, jax.ShapeDtypeStruct((N_HEADS, n), F32)],
        compiler_params=_params("arbitrary"),
        name="in_proj",
    )(x2d, g1, w_main, w_f, b_f, gains, seg)


def _store_mem_rows(ref, x, hh):
    n = x.shape[0]
    for lg in range(2):
        ref[pl.ds(lg * N_MEM_HEADS + hh, n, stride=2 * N_MEM_HEADS), :] = x[:, lg * LANES:(lg + 1) * LANES]


def _memkv_kernel(mem_ref, g_ref, w_ref, gk_ref, kf_ref, vf_ref, kb_ref, vb_ref):
    d = mem_ref.shape[-1]
    hd = d // N_MEM_HEADS
    h = _rms(mem_ref[0], g_ref[...]).astype(BF16)
    kv = _dot(h, w_ref[...])
    for hh in range(N_MEM_HEADS):
        cs = slice(hh * hd, (hh + 1) * hd)
        k = _rms(kv[:, cs], gk_ref[...])
        v = kv[:, d + hh * hd:d + (hh + 1) * hd]
        _store_mem_rows(kf_ref.at[0], k, hh)
        _store_mem_rows(vf_ref.at[0], v, hh)
        kb_ref[0, :, cs] = k.astype(BF16)
        vb_ref[0, :, cs] = v.astype(BF16)


def _mem_kv(mem, g_mem, w_mkv, g_mk):
    b, n, d = mem.shape
    assert d == N_MEM_HEADS * 2 * LANES
    blk = pl.BlockSpec((1, n, d), lambda i: (i, 0, 0))
    blk_f = pl.BlockSpec((1, n * 2 * N_MEM_HEADS, LANES), lambda i: (i, 0, 0))
    full = _resident
    f = jax.ShapeDtypeStruct((b, n * 2 * N_MEM_HEADS, LANES), F32)
    h = jax.ShapeDtypeStruct((b, n, d), BF16)
    return pl.pallas_call(
        _memkv_kernel, grid=(b,),
        in_specs=[blk, full(g_mem), full(w_mkv), full(g_mk)],
        out_specs=[blk_f, blk_f, blk, blk], out_shape=[f, f, h, h],
        compiler_params=_params("arbitrary"), name="mem_kv",
    )(mem, g_mem, w_mkv, g_mk)


def _cumsum_kernel(x_ref, o_ref):
    r, t = x_ref.shape
    k = lax.broadcasted_iota(jnp.int32, (LANES, LANES), 0)
    j = lax.broadcasted_iota(jnp.int32, (LANES, LANES), 1)
    tri = jnp.where(k <= j, 1.0, 0.0).astype(BF16)
    carry = jnp.zeros((r, 1), F32)
    for blk in range(t // LANES):
        hi, mid, lo = _split3(x_ref[:, blk * LANES:(blk + 1) * LANES])
        cs = _dot(hi, tri) + _dot(mid, tri) + _dot(lo, tri)
        o_ref[:, blk * LANES:(blk + 1) * LANES] = (cs + carry) * LOG2E
        carry = carry + cs[:, LANES - 1:LANES]


def _cumsum_lanes(x):
    assert x.shape[1] % LANES == 0
    return pl.pallas_call(_cumsum_kernel, out_shape=jax.ShapeDtypeStruct(x.shape, F32), name="cumsum")(x)


def _band_bias_kernel(tbl_ref, o_ref):
    nrel = tbl_ref.shape[1]
    qblk, win = o_ref.shape[1:]
    wrap = qblk + win
    jp = lax.broadcasted_iota(jnp.int32, (nrel, wrap), 1)
    r = lax.broadcasted_iota(jnp.int32, (nrel, wrap), 0)
    d = jnp.where(jp < win, jp, jp - wrap)
    rel = jnp.clip(BAND_PAST - d, -REL_CLIP, REL_CLIP) + REL_CLIP
    onehot = jnp.where(rel == r, 1.0, 0.0).astype(BF16)
    hi, mid, lo = _split3(tbl_ref[...])
    base = (_dot(hi, onehot) + _dot(mid, onehot) + _dot(lo, onehot)) * LOG2E
    qc = lax.broadcasted_iota(jnp.int32, (qblk, win), 0) // CHUNK
    kc = lax.broadcasted_iota(jnp.int32, (qblk, win), 1) // CHUNK
    inband = (kc >= qc) & (kc <= qc + N_PREV_CHUNKS)
    for h in range(N_HEADS):
        row = jnp.broadcast_to(base[h:h + 1, :], (qblk, wrap))
        toeplitz = pltpu.roll(row, 0, 1, stride=1, stride_axis=0)
        o_ref[h] = jnp.where(inband, toeplitz[:, :win], NEG_INF)


def _band_bias(rel_table):
    nrel = 2 * REL_CLIP + 1
    pad = -nrel % LANES
    tbl = jnp.pad(rel_table.astype(F32), ((0, 0), (0, pad)))
    return pl.pallas_call(
        _band_bias_kernel, out_shape=jax.ShapeDtypeStruct((N_HEADS, BAND_Q, BAND_WIN), F32),
        name="band_bias")(tbl)


def _softmax_pv(s, vs, transposed=None):
    transposed = transposed or (False,) * len(vs)
    m = jnp.max(s, axis=-1, keepdims=True)
    e = jnp.exp2(s - m)
    l = jnp.sum(e, axis=-1, keepdims=True)
    eb = e.astype(BF16)
    o, lo = None, 0
    for v, vt in zip(vs, transposed):
        keys = v.shape[1] if vt else v.shape[0]
        part = _dot_nt(eb[:, lo:lo + keys], v) if vt else _dot(eb[:, lo:lo + keys], v)
        o = part if o is None else o + part
        lo += keys
    return o * (1.0 / l)


def _band_prompt_kernel(q_ref, k0_ref, k1_ref, k2_ref, v0_ref, v1_ref, v2_ref, bias_ref, o_ref, s_sc, m_sc, qm_sc):
    i = pl.program_id(1)
    tq = q_ref.shape[1]
    k_refs, v_refs = (k0_ref, k1_ref, k2_ref), (v0_ref, v1_ref, v2_ref)
    bands = [slice(r, r + CHUNK) for r in range(0, tq, CHUNK)]
    n_groups = BAND_WIN // LANES
    per_block = tq // LANES
    group = lambda g: slice(g * LANES, (g + 1) * LANES)
    visible = lambda c: [g for g in range(n_groups) if 2 * g + 1 >= c and 2 * g <= c + N_PREV_CHUNKS]
    low = _low_half(tq)
    for h in range(N_HEADS):
        q_pair = _pair(q_ref, h)
        qm_sc[h] = jnp.where(low if h % 2 == 0 else ~low, q_pair, jnp.zeros_like(q_pair))

    def logits(early):
        for h in range(N_HEADS):
            qm = qm_sc[h]
            parts = [_dot_nt(qm, _pair(k_ref, h)) for k_ref in k_refs]
            for c, rows in enumerate(bands):
                mx = None
                for g in visible(c):
                    p, gp = divmod(g, per_block)
                    sg = parts[p][rows, group(gp)] + bias_ref[h, rows, group(g)]
                    if early and p < 2:
                        sg = jnp.where(i + p >= 2, sg, NEG_INF)
                    s_sc[h, rows, group(g)] = sg
                    mx = sg if mx is None else jnp.maximum(mx, sg)
                m_sc[h, rows] = jnp.broadcast_to(jnp.max(mx, axis=-1, keepdims=True), (CHUNK, LANES))

    values = functools.partial(_band_values, v_refs, o_ref, s_sc, m_sc, bands, visible)

    @pl.when(i >= 2)
    def _():
        logits(False)

    @pl.when(i < 2)
    def _():
        logits(True)

    values()


def _band_values(v_refs, o_ref, s_sc, m_sc, bands, visible):
    tq = o_ref.shape[1]
    n_groups = s_sc.shape[2] // LANES
    group = lambda g: slice(g * LANES, (g + 1) * LANES)
    for h in range(N_HEADS):
        p_bands, invs = [], []
        for c, rows in enumerate(bands):
            m = m_sc[h, rows]
            rs, ps = None, []
            for g in range(n_groups):
                if g in visible(c):
                    e = jnp.exp2(s_sc[h, rows, group(g)] - m)
                    ps.append(e.astype(BF16))
                    rs = e if rs is None else rs + e
                else:
                    ps.append(jnp.zeros((CHUNK, LANES), BF16))
            p_bands.append(jnp.concatenate(ps, axis=1))
            invs.append(1.0 / jnp.sum(rs, axis=-1, keepdims=True))
        p_all = jnp.concatenate(p_bands, axis=0)
        o = None
        for p, v_ref in enumerate(v_refs):
            part = _dot(p_all[:, p * tq:(p + 1) * tq], _pair(v_ref, h))
            o = part if o is None else o + part
        half = slice((h % 2) * HEAD_DIM, (h % 2 + 1) * HEAD_DIM)
        o_ref[0, :, h * HEAD_DIM:(h + 1) * HEAD_DIM] = (o * jnp.concatenate(invs, axis=0))[:, half].astype(BF16)


def _band_prompt(qa, ka, va, bias):
    b, t, _ = qa.shape
    tq = BAND_Q
    assert t % tq == 0
    blk = lambda back: pl.BlockSpec((1, tq, WIDTH), lambda bi, i: (bi, jnp.maximum(i - back, 0), 0))
    return pl.pallas_call(
        _band_prompt_kernel, grid=(b, t // tq),
        in_specs=[blk(0), blk(2), blk(1), blk(0), blk(2), blk(1), blk(0),
                  pl.BlockSpec(bias.shape, lambda bi, i: (0, 0, 0))],
        out_specs=blk(0), out_shape=jax.ShapeDtypeStruct((b, t, WIDTH), BF16),
        scratch_shapes=[pltpu.VMEM((N_HEADS, tq, BAND_WIN), F32), pltpu.VMEM((N_HEADS, tq, LANES), F32),
                        pltpu.VMEM((N_HEADS, tq, LANES), BF16)],
        compiler_params=_params("arbitrary", "arbitrary"), name="band_prompt",
    )(qa, ka, ka, ka, va, va, va, bias)


def _band_sample_kernel(q_ref, kc_ref, vc_ref, kn_ref, vn_ref, bias_ref, o_ref):
    for h in range(N_HEADS):
        hs = slice(h * HEAD_DIM, (h + 1) * HEAD_DIM)
        q = q_ref[0, :, hs]
        s = jnp.concatenate([_dot(q, kc_ref[0, h].astype(BF16)), _dot_nt(q, kn_ref[0, :, hs])], axis=1)
        s = s + bias_ref[h, :CHUNK, :BAND_PAST + CHUNK]
        o = _softmax_pv(s, [vc_ref[0, h].astype(BF16), vn_ref[0, :, hs]], transposed=(True, False))
        o_ref[0, :, hs] = o.astype(BF16)


def _band_sample(qa, ka, va, cache_k, cache_v, bias):
    b, s, _ = qa.shape
    la = cache_k.shape[3]
    assert s == CHUNK and la == BAND_PAST
    new = pl.BlockSpec((1, s, WIDTH), lambda bi: (bi, 0, 0))
    old = pl.BlockSpec((1, N_HEADS, HEAD_DIM, la), lambda bi: (bi, 0, 0, 0))
    return pl.pallas_call(
        _band_sample_kernel, grid=(b,),
        in_specs=[new, old, old, new, new, pl.BlockSpec(bias.shape, lambda bi: (0, 0, 0))],
        out_specs=new, out_shape=jax.ShapeDtypeStruct((b, s, WIDTH), BF16),
        compiler_params=_params("arbitrary"), name="band_sample",
    )(qa, cache_k, cache_v, ka, va, bias)


def _fox_scratch(tq, tk):
    rows = lambda w: pltpu.VMEM((N_HEADS, tq, w), F32)
    return [rows(tk), rows(LANES), rows(LANES), rows(LANES), rows(LANES),
            pltpu.VMEM((N_HEADS, tq, LANES), BF16), pltpu.VMEM((N_HEADS // 2, tq, LANES), F32)]


def _low_half(rows):
    return lax.broadcasted_iota(jnp.int32, (rows, LANES), 1) < HEAD_DIM


def _fox_init(q_ref, cq_ref, sc):
    _, cq_sc, m_sc, _, l_sc, qm_sc, acc_sc = sc
    tq = cq_sc.shape[1]
    low = _low_half(tq)
    for h in range(N_HEADS):
        cq_sc[h] = jnp.broadcast_to(cq_ref[0, h:h + 1, :], (LANES, cq_ref.shape[2])).T[:tq]
        q_pair = q_ref[0, :, (h // 2) * LANES:(h // 2 + 1) * LANES]
        qm_sc[h] = jnp.where(low if h % 2 == 0 else ~low, q_pair, jnp.zeros_like(q_pair))
    m_sc[...] = jnp.full(m_sc.shape, NEG_INF, F32)
    l_sc[...] = jnp.zeros(l_sc.shape, F32)
    acc_sc[...] = jnp.zeros(acc_sc.shape, F32)


def _fox_block(qk, pv, crow_ref, tk, causal, sc):
    s_sc, cq_sc, m_sc, a_sc, l_sc, qm_sc, acc_sc = sc
    tq = s_sc.shape[1]
    band = min(tq, FOX_BAND)
    bands = [slice(r, r + band) for r in range(0, tq, band)]
    n_groups = tk // LANES
    group = lambda g: slice(g * LANES, (g + 1) * LANES)

    def visible(rows):
        if not causal:
            return [(g, False) for g in range(n_groups)]
        return [(g, (g + 1) * LANES - 1 > rows.start) for g in range(n_groups) if g * LANES < rows.stop]

    if causal:
        ahead = (lax.broadcasted_iota(jnp.int32, (band, LANES), 0)
                 - lax.broadcasted_iota(jnp.int32, (band, LANES), 1))
    for h in range(N_HEADS):
        s = qk(h, qm_sc[h])
        for rows in bands:
            cq = cq_sc[h, rows]
            mx = None
            for g, masked in visible(rows):
                sg = s[rows, group(g)] + (cq - crow_ref[0, h:h + 1, group(g)])
                if masked:
                    sg = jnp.where(ahead >= g * LANES - rows.start, sg, NEG_INF)
                s_sc[h, rows, group(g)] = sg
                mx = sg if mx is None else jnp.maximum(mx, sg)
            m_prev = m_sc[h, rows]
            m_new = jnp.maximum(m_prev, jnp.max(mx, axis=-1, keepdims=True))
            a_sc[h, rows] = jnp.exp2(m_prev - m_new)
            m_sc[h, rows] = m_new
    for h in range(N_HEADS):
        p_bands = []
        for rows in bands:
            seen = visible(rows)
            m_new = m_sc[h, rows]
            rs, ps = None, []
            for g, _ in seen:
                p = jnp.exp2(s_sc[h, rows, group(g)] - m_new)
                ps.append(p.astype(BF16))
                rs = p if rs is None else rs + p
            l_sc[h, rows] = a_sc[h, rows] * l_sc[h, rows] + rs
            ps += [jnp.zeros((band, LANES), BF16)] * (n_groups - len(seen))
            p_bands.append(jnp.concatenate(ps, axis=1))
        half = slice((h % 2) * HEAD_DIM, (h % 2 + 1) * HEAD_DIM)
        new = a_sc[h] * acc_sc[h // 2] + pv(h, jnp.concatenate(p_bands, axis=0))
        acc_sc[h // 2, :, half] = new[:, half]


def _fox_finish(o_ref, sc):
    l_sc, acc_sc = sc[4], sc[6]
    low = _low_half(acc_sc.shape[1])
    total = lambda h: jnp.sum(l_sc[h], axis=-1, keepdims=True)
    for g in range(N_HEADS // 2):
        inv = jnp.where(low, 1.0 / total(2 * g), 1.0 / total(2 * g + 1))
        o_ref[0, :, g * LANES:(g + 1) * LANES] = (acc_sc[g] * inv).astype(BF16)


def _head(ref, h):
    return ref[0, :, h * HEAD_DIM:(h + 1) * HEAD_DIM]


def _pair(ref, h):
    return ref[0, :, (h // 2) * LANES:(h // 2 + 1) * LANES]


def _row_major_products(k_ref, v_ref):
    return (lambda h, qm: _dot_nt(qm, _pair(k_ref, h))), (lambda h, p: _dot(p, _pair(v_ref, h)))


def _fox_prompt_kernel(qi_ref, kj_ref, q_ref, k_ref, v_ref, crow_ref, cq_ref, o_ref, *sc):
    p = pl.program_id(1)
    qi, kj = qi_ref[p], kj_ref[p]
    tk = k_ref.shape[1]
    qk, pv = _row_major_products(k_ref, v_ref)

    for first in (True, False):
        started = (kj == 0) if first else (kj > 0)

        @pl.when(started & (kj < qi))
        def _():
            if first:
                _fox_init(q_ref, cq_ref, sc)
            _fox_block(qk, pv, crow_ref, tk, False, sc)

        @pl.when(started & (kj == qi))
        def _():
            if first:
                _fox_init(q_ref, cq_ref, sc)
            _fox_block(qk, pv, crow_ref, tk, True, sc)
            _fox_finish(o_ref, sc)


def _fox_prompt(qb, kb, vb, crow):
    b, t, _ = qb.shape
    tq = FOX_BLOCK
    assert t % tq == 0
    pairs = [(i, j) for i in range(t // tq) for j in range(i + 1)]
    qi = jnp.array([p[0] for p in pairs], jnp.int32)
    kj = jnp.array([p[1] for p in pairs], jnp.int32)
    qblk = pl.BlockSpec((1, tq, WIDTH), lambda bi, p, qi, kj: (bi, qi[p], 0))
    kblk = pl.BlockSpec((1, tq, WIDTH), lambda bi, p, qi, kj: (bi, kj[p], 0))
    grid_spec = pltpu.PrefetchScalarGridSpec(
        num_scalar_prefetch=2, grid=(b, len(pairs)),
        in_specs=[qblk, kblk, kblk,
                  pl.BlockSpec((1, N_HEADS, tq), lambda bi, p, qi, kj: (bi, 0, kj[p])),
                  pl.BlockSpec((1, N_HEADS, tq), lambda bi, p, qi, kj: (bi, 0, qi[p]))],
        out_specs=qblk, scratch_shapes=_fox_scratch(tq, tq))
    return pl.pallas_call(
        _fox_prompt_kernel, grid_spec=grid_spec, out_shape=jax.ShapeDtypeStruct((b, t, WIDTH), BF16),
        compiler_params=_params("arbitrary", "arbitrary"), name="fox_prompt",
    )(qi, kj, qb, kb, vb, crow, crow)


def _fox_sample_kernel(q_ref, kc_ref, vc_ref, kn_ref, vn_ref, crow_ref, crow_new_ref, o_ref, *sc):
    j = pl.program_id(1)
    last = pl.num_programs(1) - 1

    @pl.when(j == 0)
    def _():
        _fox_init(q_ref, crow_new_ref, sc)

    def cached_pair(ref, h):
        g = h // 2
        return ref[0, 2 * g:2 * g + 2].astype(BF16).reshape(2 * HEAD_DIM, ref.shape[3])

    _fox_block(lambda h, qm: _dot(qm, cached_pair(kc_ref, h)),
               lambda h, p: _dot_nt(p, cached_pair(vc_ref, h)), crow_ref, kc_ref.shape[3], False, sc)

    @pl.when(j == last)
    def _():
        qk, pv = _row_major_products(kn_ref, vn_ref)
        _fox_block(qk, pv, crow_new_ref, kn_ref.shape[1], True, sc)
        _fox_finish(o_ref, sc)


def _fox_sample(qb, kb, vb, cache_k, cache_v, crow, crow_new):
    b, s, _ = qb.shape
    past = cache_k.shape[3]
    tk, tn = min(FOX_CACHE_BLOCK, past), kb.shape[1]
    assert past % tk == 0 and tn % LANES == 0 and tn >= s
    nk = past // tk
    new = lambda r: pl.BlockSpec((1, r, WIDTH), lambda bi, j: (bi, 0, 0))
    old = pl.BlockSpec((1, N_HEADS, HEAD_DIM, tk), lambda bi, j: (bi, 0, 0, j))
    return pl.pallas_call(
        _fox_sample_kernel, grid=(b, nk),
        in_specs=[new(s), old, old, new(tn), new(tn),
                  pl.BlockSpec((1, N_HEADS, tk), lambda bi, j: (bi, 0, j)),
                  pl.BlockSpec((1, N_HEADS, tn), lambda bi, j: (bi, 0, 0))],
        out_specs=new(s), out_shape=jax.ShapeDtypeStruct((b, s, WIDTH), BF16),
        scratch_shapes=_fox_scratch(s, tk),
        compiler_params=_params("arbitrary", "arbitrary"), name="fox_sample",
    )(qb, cache_k, cache_v, kb, vb, crow, crow_new)


def _mid_kernel(x_ref, oa_ref, ob_ref, wo_ref, g2_ref, wq_ref, gq_ref, mk_ref, mv_ref, wmo_ref, o_ref,
                om_sc, *, seg):
    tm, d = x_ref.shape
    hd = d // N_MEM_HEADS
    x1 = x_ref[...] + _dot(oa_ref[...], wo_ref[:WIDTH, :]) + _dot(ob_ref[...], wo_ref[WIDTH:, :])
    h2 = _rms(x1, g2_ref[...]).astype(BF16)
    qm = _dot(h2, wq_ref[...])
    scale = hd ** -0.5 * LOG2E
    for hh in range(N_MEM_HEADS):
        cs = slice(hh * hd, (hh + 1) * hd)
        q = (_rms(qm[:, cs], gq_ref[...]) * scale).astype(BF16)
        def mem_head(ref, sg):
            if ref.shape[2] == d:
                return ref[sg, :, cs]
            rows = lambda lg: ref[sg, pl.ds(lg * N_MEM_HEADS + hh, ref.shape[1] // (2 * N_MEM_HEADS),
                                            stride=2 * N_MEM_HEADS), :]
            return jnp.concatenate([rows(0), rows(1)], axis=1).astype(BF16)

        for sg in range(tm // seg):
            rs = slice(sg * seg, (sg + 1) * seg)
            s = _dot_nt(q[rs], mem_head(mk_ref, sg))
            om_sc[rs, cs] = _softmax_pv(s, [mem_head(mv_ref, sg)]).astype(BF16)
    o_ref[...] = x1 + _dot(om_sc[...], wmo_ref[...])


def _cache_mem_rows(x):
    b, n, nh, hd = x.shape
    assert nh == N_MEM_HEADS and hd == 2 * LANES
    return x.reshape(b, n, nh, 2, LANES).transpose(0, 1, 3, 2, 4).reshape(b, n * 2 * nh, LANES)


def _mid(x2d, oa, ob, w_o, g2, w_mq, g_mq, mk, mv, w_mo, *, seq):
    n, d = x2d.shape
    tm = MID_TILE if seq % MID_TILE == 0 else ROW_TILE
    seg = min(seq, tm)
    tiles_per_batch = seq // seg
    nb = tm // seg
    assert n % tm == 0 and seq % seg == 0 and tm % seg == 0
    rows = lambda w: pl.BlockSpec((tm, w), lambda i: (i, 0))
    full = _resident
    mem = pl.BlockSpec((nb,) + mk.shape[1:], lambda i: (i // tiles_per_batch, 0, 0))
    return pl.pallas_call(
        functools.partial(_mid_kernel, seg=seg), grid=(n // tm,),
        in_specs=[rows(d), rows(WIDTH), rows(WIDTH), full(w_o), full(g2), full(w_mq), full(g_mq), mem, mem,
                  full(w_mo)],
        out_specs=rows(d), out_shape=jax.ShapeDtypeStruct((n, d), F32),
        scratch_shapes=[pltpu.VMEM((tm, d), BF16)],
        compiler_params=_params("arbitrary"), name="mid",
    )(x2d, oa, ob, w_o, g2, w_mq, g_mq, mk, mv, w_mo)


def _ffn_kernel(x_ref, g3_ref, wup_ref, wc_ref, bc_ref, wd_ref, st_ref, o_ref, last_ref, carry_sc, y_sc,
                *, seg, tiles_per_batch, ff_chunk):
    tm, d = x_ref.shape
    dff = wd_ref.shape[0]
    i = pl.program_id(0)
    h3 = _rms(x_ref[...], g3_ref[...]).astype(BF16)
    if tiles_per_batch > 1:
        @pl.when(i == 0)
        def _():
            carry_sc[...] = jnp.zeros(carry_sc.shape, F32)
    rowid = lax.broadcasted_iota(jnp.int32, (SUBLANES, 1), 0)
    first_tile = (i % tiles_per_batch) == 0
    for c in range(dff // ff_chunk):
        cs = slice(c * ff_chunk, (c + 1) * ff_chunk)
        gate = _dot(h3, wup_ref[:, cs])
        val = _dot(h3, wup_ref[:, dff + c * ff_chunk:dff + (c + 1) * ff_chunk])
        w0, w1, w2 = wc_ref[0:1, cs], wc_ref[1:2, cs], wc_ref[2:3, cs]
        for sg in range(tm // seg):
            rows = slice(sg * seg, (sg + 1) * seg)
            g = gate[rows]
            if tiles_per_batch == 1:
                prev = st_ref[sg, :, cs]
            else:
                prev = jnp.where(first_tile, st_ref[0, :, cs], carry_sc[SUBLANES - 2:, cs])
            p2, p1 = prev[0:1], prev[1:2]
            s1 = pltpu.roll(g, 1, 0)
            s2 = pltpu.roll(g, 2, 0)
            top1 = jnp.where(rowid == 0, p1, s1[:SUBLANES])
            top2 = jnp.where(rowid == 0, p2, jnp.where(rowid == 1, p1, s2[:SUBLANES]))
            s1 = jnp.concatenate([top1, s1[SUBLANES:]], axis=0)
            s2 = jnp.concatenate([top2, s2[SUBLANES:]], axis=0)
            conv = s2 * w0 + s1 * w1 + g * w2 + bc_ref[:, cs]
            y_sc[rows, cs] = (conv * jax.nn.sigmoid(conv) * val[rows]).astype(BF16)
            last_ref[sg, :, cs] = g[seg - SUBLANES:]
        if tiles_per_batch > 1:
            carry_sc[:, cs] = gate[tm - SUBLANES:]
    o_ref[...] = x_ref[...] + _dot(y_sc[...], wd_ref[...])


def _conv_ffn(x2d, g3, w_up, w_conv, b_conv, w_down, state, *, seq, ff_chunk):
    n, d = x2d.shape
    dff = w_down.shape[0]
    tm = min(FFN_TILE, n)
    seg = min(seq, tm)
    tiles_per_batch = seq // seg
    nb = tm // seg
    assert n % tm == 0 and seq % seg == 0 and tm % seg == 0 and dff % ff_chunk == 0 and seg >= SUBLANES
    rows = pl.BlockSpec((tm, d), lambda i: (i, 0))
    full = _resident
    per_batch = lambda r: pl.BlockSpec((nb, r, dff), lambda i: (i // tiles_per_batch, 0, 0))
    return pl.pallas_call(
        functools.partial(_ffn_kernel, seg=seg, tiles_per_batch=tiles_per_batch, ff_chunk=ff_chunk),
        grid=(n // tm,),
        in_specs=[rows, full(g3), full(w_up), full(w_conv), full(b_conv), full(w_down), per_batch(CONV_W - 1)],
        out_specs=[rows, per_batch(SUBLANES)],
        out_shape=[jax.ShapeDtypeStruct((n, d), F32), jax.ShapeDtypeStruct((n // seq, SUBLANES, dff), F32)],
        scratch_shapes=[pltpu.VMEM((SUBLANES, dff), F32), pltpu.VMEM((tm, dff), BF16)],
        compiler_params=_params("arbitrary"), name="conv_ffn",
    )(x2d, g3, w_up, w_conv, b_conv, w_down, state)


def _row(v):
    return v.astype(F32).reshape(1, -1)


def _layer_weights(l, w_in, b_f, g_qa, g_ka, rel_bias, g_qb, g_kb, w_o, g_norm1, g_norm2, g_mem, w_mq, w_mkv,
                   g_mq, g_mk, w_mo, g_norm3, w_up, w_conv, b_conv, w_down):
    n_main = 6 * WIDTH
    tile = lambda g: jnp.tile(g[l].astype(F32), N_HEADS)
    blk = jnp.arange(MXU_COLS) // HEAD_DIM
    return dict(
        w_main=w_in[l][:, :n_main].astype(BF16),
        w_f=jnp.pad(w_in[l][:, n_main:].T, ((0, BF16_ROWS - N_HEADS), (0, 0))).astype(BF16),
        b_f=b_f[l].astype(F32).reshape(N_HEADS, 1),
        gains=jnp.stack([tile(g_qa), tile(g_ka), tile(g_qb), tile(g_kb)]),
        seg=(blk[:, None] == blk[None, :]).astype(BF16),
        rel=rel_bias[l],
        w_o=w_o[l].astype(BF16), g1=_row(g_norm1[l]), g2=_row(g_norm2[l]), g3=_row(g_norm3[l]),
        g_mem=_row(g_mem[l]), w_mq=w_mq[l].astype(BF16), w_mkv=w_mkv[l].astype(BF16),
        g_mq=_row(g_mq[l]), g_mk=_row(g_mk[l]), w_mo=w_mo[l].astype(BF16),
        w_up=w_up[l].astype(BF16), w_conv=w_conv[l].astype(F32), b_conv=_row(b_conv[l]),
        w_down=w_down[l].astype(BF16))


def _post_attention(x2d, oa, ob, mk, mv, state, w, *, seq, ff_chunk):
    n, d = x2d.shape
    x2 = _mid(x2d, oa.reshape(n, WIDTH), ob.reshape(n, WIDTH), w["w_o"], w["g2"], w["w_mq"], w["g_mq"],
              mk, mv, w["w_mo"], seq=seq)
    y, last = _conv_ffn(x2, w["g3"], w["w_up"], w["w_conv"], w["b_conv"], w["w_down"], state,
                        seq=seq, ff_chunk=ff_chunk)
    return y, last[:, SUBLANES - (CONV_W - 1):]


def _prompt_layer(x, mem, w, bias):
    b, t, d = x.shape
    la = min(BAND_PAST, t)
    assert la == ROW_TILE and t % ROW_TILE == 0
    qa, ka, va, qb, kb, vb, ka_tail, va_tail, kb_f, vb_f, logf = _in_proj(
        x.reshape(b * t, d), w["g1"], w["w_main"], w["w_f"], w["b_f"], w["gains"], w["seg"],
        tail_period=t // ROW_TILE, time_minor=True)
    r3 = lambda a: a.reshape(b, t, WIDTH)
    oa = _band_prompt(r3(qa), r3(ka), r3(va), bias)
    logf = logf.reshape(N_HEADS, b, t).transpose(1, 0, 2)
    crow = _cumsum_lanes(logf.reshape(b * N_HEADS, t)).reshape(b, N_HEADS, t)
    logf = logf.transpose(0, 2, 1)
    ob = _fox_prompt(r3(qb), r3(kb), r3(vb), crow)
    mk_f, mv_f, mk, mv = _mem_kv(mem, w["g_mem"], w["w_mkv"], w["g_mk"])
    dff = w["w_down"].shape[0]
    y, conv = _post_attention(x.reshape(b * t, d), oa, ob, mk, mv, jnp.zeros((b, CONV_W - 1, dff), F32), w,
                              seq=t, ff_chunk=dff // 11)
    heads = lambda a, n: a.reshape(b, N_HEADS, HEAD_DIM, n).transpose(0, 3, 1, 2)
    mem_heads = lambda a: (a.reshape(b, mem.shape[1], 2, N_MEM_HEADS, LANES).transpose(0, 1, 3, 2, 4)
                           .reshape(b, mem.shape[1], N_MEM_HEADS, d // N_MEM_HEADS))
    return (y.reshape(b, t, d), heads(ka_tail, la), heads(va_tail, la), heads(kb_f, t), heads(vb_f, t), logf,
            mem_heads(mk_f), mem_heads(mv_f), conv)


def _sample_layer(x, cache_a_k, cache_a_v, cache_b_k, cache_b_v, cache_b_logf, cache_mem_k, cache_mem_v,
                  state_conv, w, bias):
    b, s, d = x.shape
    past = cache_b_k.shape[1]
    assert ROW_TILE % s == 0 and (b * s) % ROW_TILE == 0
    qa, ka, va, qb, kb, vb, ka_f, va_f, kb_f, vb_f, logf = _in_proj(
        x.reshape(b * s, d), w["g1"], w["w_main"], w["w_f"], w["b_f"], w["gains"], w["seg"], tail_period=1,
        time_minor=False)
    r3 = lambda a: a.reshape(b, s, WIDTH)
    flat = lambda a: a.reshape(a.shape[0], a.shape[1], -1)
    time_minor = lambda a: a.transpose(0, 2, 3, 1)
    oa = _band_sample(r3(qa), r3(ka), r3(va), time_minor(cache_a_k), time_minor(cache_a_v), bias)
    logf = logf.reshape(N_HEADS, b, s).transpose(1, 0, 2)
    total = past + s
    all_logf = jnp.concatenate([cache_b_logf.astype(F32).transpose(0, 2, 1), logf], axis=2)
    all_logf = jnp.pad(all_logf, ((0, 0), (0, 0), (0, -total % LANES)))
    logf = logf.transpose(0, 2, 1)
    cum = _cumsum_lanes(all_logf.reshape(b * N_HEADS, -1)).reshape(b, N_HEADS, -1)
    crow_new = cum[:, :, past:total]
    pad_rows = lambda a: jnp.pad(a, ((0, 0), (0, -s % LANES), (0, 0)))
    ob = _fox_sample(r3(qb), pad_rows(r3(kb)), pad_rows(r3(vb)), time_minor(cache_b_k), time_minor(cache_b_v),
                     cum[:, :, :past], jnp.pad(crow_new, ((0, 0), (0, 0), (0, -s % LANES))))
    dff = w["w_down"].shape[0]
    y, conv = _post_attention(x.reshape(b * s, d), oa, ob, _cache_mem_rows(cache_mem_k.astype(F32)),
                              _cache_mem_rows(cache_mem_v.astype(F32)), state_conv.astype(F32), w,
                              seq=s, ff_chunk=dff // 11)
    heads = lambda a: a.reshape(b, s, N_HEADS, HEAD_DIM)
    return y.reshape(b, s, d), heads(ka_f), heads(va_f), heads(kb_f), heads(vb_f), logf, conv


def kernel(x_prompt, x_sample, cache_a_k, cache_a_v, cache_b_k, cache_b_v, cache_b_logf, cache_mem_k, cache_mem_v, state_conv, mem_prompt, w_in, b_f, g_qa, g_ka, rel_bias, g_qb, g_kb, w_o, g_norm1, g_norm2, g_mem, w_mq, w_mkv, g_mq, g_mk, w_mo, g_norm3, w_up, w_conv, b_conv, w_down):
    depth = w_in.shape[0]
    xp, xs = x_prompt, x_sample
    outs_p, outs_s = [], []
    for l in range(depth):
        w = _layer_weights(l, w_in, b_f, g_qa, g_ka, rel_bias, g_qb, g_kb, w_o, g_norm1, g_norm2, g_mem, w_mq,
                           w_mkv, g_mq, g_mk, w_mo, g_norm3, w_up, w_conv, b_conv, w_down)
        bias = _band_bias(w["rel"])
        xp, *rest_p = _prompt_layer(xp, mem_prompt, w, bias)
        xs, *rest_s = _sample_layer(xs, cache_a_k[l], cache_a_v[l], cache_b_k[l], cache_b_v[l], cache_b_logf[l],
                                    cache_mem_k[l], cache_mem_v[l], state_conv[l], w, bias)
        outs_p.append(rest_p)
        outs_s.append(rest_s)
    stack = lambda outs: [jnp.stack(col, axis=0) for col in zip(*outs)]
    return (xp, xs, *stack(outs_p), *stack(outs_s))
```

```python
import functools

import jax
import jax.numpy as jnp
from jax import lax
from jax.experimental import pallas as pl
from jax.experimental.pallas import tpu as pltpu

F32 = jnp.float32
BF16 = jnp.bfloat16
EPS = 1e-6
NEG_INF = float("-inf")

HEAD_DIM = 64
N_HEADS = 8
WIDTH = N_HEADS * HEAD_DIM
CHUNK = 64
N_PREV_CHUNKS = 8
BAND_PAST = N_PREV_CHUNKS * CHUNK
REL_CLIP = 128
N_MEM_HEADS = 4
CONV_W = 3

ROW_TILE = 512
FFN_TILE = 1024
MID_TILE = 1024
PROMPT_TILE = 1024
BAND_Q = 256
BAND_WIN = BAND_PAST + BAND_Q
FOX_BLOCK = 512
FOX_CACHE_BLOCK = 2048
FOX_BAND = 64
MXU_COLS = 256
LANES = 128
SUBLANES = 8
BF16_ROWS = 16
VMEM_LIMIT = 56 * 1024 * 1024
LOG2E = 1.4426950408889634


def _dot(a, b):
    return jnp.dot(a, b, preferred_element_type=F32)


def _dot_nt(a, b):
    return lax.dot_general(a, b, (((1,), (1,)), ((), ())), preferred_element_type=F32)


def _rms(x, g):
    return x * lax.rsqrt(jnp.mean(x * x, axis=-1, keepdims=True) + EPS) * g


def _split3(x):
    hi = x.astype(BF16)
    r1 = x - hi.astype(F32)
    mid = r1.astype(BF16)
    lo = (r1 - mid.astype(F32)).astype(BF16)
    return hi, mid, lo


def _store_heads(ref, x):
    rows = x.shape[0]
    for h in range(N_HEADS):
        ref[pl.ds(h, rows, stride=N_HEADS), :] = x[:, h * HEAD_DIM:(h + 1) * HEAD_DIM]


def _resident(a):
    return pl.BlockSpec(a.shape, lambda i: (0,) * a.ndim, pipeline_mode=pl.Buffered(1))


def _params(*sem):
    return pltpu.CompilerParams(dimension_semantics=sem, vmem_limit_bytes=VMEM_LIMIT)


def _inproj_kernel(x_ref, g1_ref, w_ref, wf_ref, bf_ref, gains_ref, seg_ref,
                   qa_ref, ka_ref, va_ref, qb_ref, kb_ref, vb_ref,
                   kaf_ref, vaf_ref, kbf_ref, vbf_ref, logf_ref, *, tail_period, tail_rows, time_minor):
    h = _rms(x_ref[...], g1_ref[...]).astype(BF16)
    seg = seg_ref[...]

    def store_f32(ref, x):
        if time_minor:
            ref[...] = x.T
        else:
            _store_heads(ref, x)

    def proj(g):
        return _dot(h, w_ref[:, g * WIDTH:(g + 1) * WIDTH])

    def head_norm(p, row):
        p2 = (p * p).astype(BF16)
        ss = jnp.concatenate([_dot(p2[:, :MXU_COLS], seg), _dot(p2[:, MXU_COLS:], seg)], axis=1)
        return p * lax.rsqrt(ss * (1.0 / HEAD_DIM) + EPS) * gains_ref[row:row + 1, :]

    is_tail = (pl.program_id(0) % tail_period) == tail_period - 1
    scale = HEAD_DIM ** -0.5 * LOG2E

    kb = head_norm(proj(4), 3)
    kb_ref[...] = kb.astype(BF16)
    store_f32(kbf_ref.at[0], kb)
    vb = proj(5)
    vb_ref[...] = vb.astype(BF16)
    store_f32(vbf_ref.at[0], vb)
    z = _dot_nt(wf_ref[...], h)[:N_HEADS] + bf_ref[...]
    logf_ref[...] = jnp.minimum(z, 0.0) - jnp.log1p(jnp.exp(-jnp.abs(z)))
    qb_ref[...] = (head_norm(proj(3), 2) * scale).astype(BF16)
    qa_ref[...] = (head_norm(proj(0), 0) * scale).astype(BF16)
    ka = head_norm(proj(1), 1)
    ka_ref[...] = ka.astype(BF16)
    va = proj(2)
    va_ref[...] = va.astype(BF16)

    @pl.when(is_tail)
    def _():
        store_f32(kaf_ref.at[0], ka[-tail_rows:])
        store_f32(vaf_ref.at[0], va[-tail_rows:])


def _in_proj(x2d, g1, w_main, w_f, b_f, gains, seg, *, tm, tail_period, tail_rows, time_minor):
    n, d = x2d.shape
    assert n % tm == 0 and (n // tm) % tail_period == 0 and (tail_rows == tm or (time_minor and tail_rows <= tm))
    n_tail = n // tm // tail_period
    row_bf = jax.ShapeDtypeStruct((n, WIDTH), BF16)
    rows = pl.BlockSpec((tm, WIDTH), lambda i: (i, 0))
    if time_minor:
        head_f = jax.ShapeDtypeStruct((n_tail, WIDTH, tm * tail_period), F32)
        tail_f = jax.ShapeDtypeStruct((n_tail, WIDTH, tail_rows), F32)
        head_rows = pl.BlockSpec((1, WIDTH, tm), lambda i: (i // tail_period, 0, i % tail_period))
        tail = pl.BlockSpec((1, WIDTH, tail_rows), lambda i: (i // tail_period, 0, 0))
    else:
        head_f = jax.ShapeDtypeStruct((n // tm, tm * N_HEADS, HEAD_DIM), F32)
        tail_f = jax.ShapeDtypeStruct((n_tail, tm * N_HEADS, HEAD_DIM), F32)
        head_rows = pl.BlockSpec((1, tm * N_HEADS, HEAD_DIM), lambda i: (i, 0, 0))
        tail = pl.BlockSpec((1, tm * N_HEADS, HEAD_DIM), lambda i: (i // tail_period, 0, 0))
    full = _resident
    return pl.pallas_call(
        functools.partial(_inproj_kernel, tail_period=tail_period, tail_rows=tail_rows, time_minor=time_minor),
        grid=(n // tm,),
        in_specs=[pl.BlockSpec((tm, d), lambda i: (i, 0)), full(g1), full(w_main), full(w_f),
                  full(b_f), full(gains), full(seg)],
        out_specs=[rows] * 6 + [tail, tail, head_rows, head_rows, pl.BlockSpec((N_HEADS, tm), lambda i: (0, i))],
        out_shape=[row_bf] * 6 + [tail_f, tail_f, head_f, head_f, jax.ShapeDtypeStruct((N_HEADS, n), F32)],
        compiler_params=_params("arbitrary"),
        name="in_proj",
    )(x2d, g1, w_main, w_f, b_f, gains, seg)


def _store_mem_rows(ref, x, hh):
    n = x.shape[0]
    for lg in range(2):
        ref[pl.ds(lg * N_MEM_HEADS + hh, n, stride=2 * N_MEM_HEADS), :] = x[:, lg * LANES:(lg + 1) * LANES]


def _memkv_kernel(mem_ref, g_ref, w_ref, gk_ref, kf_ref, vf_ref, kb_ref, vb_ref):
    d = mem_ref.shape[-1]
    hd = d // N_MEM_HEADS
    h = _rms(mem_ref[0], g_ref[...]).astype(BF16)
    kv = _dot(h, w_ref[...])
    for hh in range(N_MEM_HEADS):
        cs = slice(hh * hd, (hh + 1) * hd)
        k = _rms(kv[:, cs], gk_ref[...])
        v = kv[:, d + hh * hd:d + (hh + 1) * hd]
        _store_mem_rows(kf_ref.at[0], k, hh)
        _store_mem_rows(vf_ref.at[0], v, hh)
        kb_ref[0, :, cs] = k.astype(BF16)
        vb_ref[0, :, cs] = v.astype(BF16)


def _mem_kv(mem, g_mem, w_mkv, g_mk):
    b, n, d = mem.shape
    assert d == N_MEM_HEADS * 2 * LANES
    blk = pl.BlockSpec((1, n, d), lambda i: (i, 0, 0))
    blk_f = pl.BlockSpec((1, n * 2 * N_MEM_HEADS, LANES), lambda i: (i, 0, 0))
    full = _resident
    f = jax.ShapeDtypeStruct((b, n * 2 * N_MEM_HEADS, LANES), F32)
    h = jax.ShapeDtypeStruct((b, n, d), BF16)
    return pl.pallas_call(
        _memkv_kernel, grid=(b,),
        in_specs=[blk, full(g_mem), full(w_mkv), full(g_mk)],
        out_specs=[blk_f, blk_f, blk, blk], out_shape=[f, f, h, h],
        compiler_params=_params("arbitrary"), name="mem_kv",
    )(mem, g_mem, w_mkv, g_mk)


def _cumsum_kernel(x_ref, o_ref):
    r, t = x_ref.shape
    k = lax.broadcasted_iota(jnp.int32, (LANES, LANES), 0)
    j = lax.broadcasted_iota(jnp.int32, (LANES, LANES), 1)
    tri = jnp.where(k <= j, 1.0, 0.0).astype(BF16)
    carry = jnp.zeros((r, 1), F32)
    for blk in range(t // LANES):
        hi, mid, lo = _split3(x_ref[:, blk * LANES:(blk + 1) * LANES])
        cs = _dot(hi, tri) + _dot(mid, tri) + _dot(lo, tri)
        o_ref[:, blk * LANES:(blk + 1) * LANES] = (cs + carry) * LOG2E
        carry = carry + cs[:, LANES - 1:LANES]


def _cumsum_lanes(x):
    assert x.shape[1] % LANES == 0
    return pl.pallas_call(_cumsum_kernel, out_shape=jax.ShapeDtypeStruct(x.shape, F32), name="cumsum")(x)


def _band_bias_kernel(tbl_ref, o_ref):
    nrel = tbl_ref.shape[1]
    qblk, win = o_ref.shape[1:]
    wrap = qblk + win
    jp = lax.broadcasted_iota(jnp.int32, (nrel, wrap), 1)
    r = lax.broadcasted_iota(jnp.int32, (nrel, wrap), 0)
    d = jnp.where(jp < win, jp, jp - wrap)
    rel = jnp.clip(BAND_PAST - d, -REL_CLIP, REL_CLIP) + REL_CLIP
    onehot = jnp.where(rel == r, 1.0, 0.0).astype(BF16)
    hi, mid, lo = _split3(tbl_ref[...])
    base = (_dot(hi, onehot) + _dot(mid, onehot) + _dot(lo, onehot)) * LOG2E
    qc = lax.broadcasted_iota(jnp.int32, (qblk, win), 0) // CHUNK
    kc = lax.broadcasted_iota(jnp.int32, (qblk, win), 1) // CHUNK
    inband = (kc >= qc) & (kc <= qc + N_PREV_CHUNKS)
    for h in range(N_HEADS):
        row = jnp.broadcast_to(base[h:h + 1, :], (qblk, wrap))
        toeplitz = pltpu.roll(row, 0, 1, stride=1, stride_axis=0)
        o_ref[h] = jnp.where(inband, toeplitz[:, :win], NEG_INF)


def _band_bias(rel_table):
    nrel = 2 * REL_CLIP + 1
    pad = -nrel % LANES
    tbl = jnp.pad(rel_table.astype(F32), ((0, 0), (0, pad)))
    return pl.pallas_call(
        _band_bias_kernel, out_shape=jax.ShapeDtypeStruct((N_HEADS, BAND_Q, BAND_WIN), F32),
        name="band_bias")(tbl)


def _zero_after(x):
    bits = pltpu.bitcast(x[:SUBLANES, :LANES], jnp.uint32)
    return ((bits >> 16) >> 16)[:1].astype(F32)


def _softmax_pv(s, vs, transposed=None):
    transposed = transposed or (False,) * len(vs)
    m = jnp.max(s, axis=-1, keepdims=True)
    e = jnp.exp2(s - m)
    l = jnp.sum(e, axis=-1, keepdims=True)
    eb = e.astype(BF16)
    o, lo = None, 0
    for v, vt in zip(vs, transposed):
        keys = v.shape[1] if vt else v.shape[0]
        part = _dot_nt(eb[:, lo:lo + keys], v) if vt else _dot(eb[:, lo:lo + keys], v)
        o = part if o is None else o + part
        lo += keys
    return o * (1.0 / l)


def _band_prompt_kernel(q_ref, k0_ref, k1_ref, k2_ref, v0_ref, v1_ref, v2_ref, bias_ref, o_ref, s_sc, m_sc, qm_sc):
    i = pl.program_id(1)
    tq = q_ref.shape[1]
    k_refs, v_refs = (k0_ref, k1_ref, k2_ref), (v0_ref, v1_ref, v2_ref)
    bands = [slice(r, r + CHUNK) for r in range(0, tq, CHUNK)]
    n_groups = BAND_WIN // LANES
    per_block = tq // LANES
    group = lambda g: slice(g * LANES, (g + 1) * LANES)
    visible = lambda c: [g for g in range(n_groups) if 2 * g + 1 >= c and 2 * g <= c + N_PREV_CHUNKS]
    low = _low_half(tq)
    for h in range(N_HEADS):
        q_pair = _pair(q_ref, h)
        qm_sc[h] = jnp.where(low if h % 2 == 0 else ~low, q_pair, jnp.zeros_like(q_pair))

    def logits(early):
        for h in range(N_HEADS):
            qm = qm_sc[h]
            parts = [_dot_nt(qm, _pair(k_ref, h)) for k_ref in k_refs]
            for c, rows in enumerate(bands):
                mx = None
                for g in visible(c):
                    p, gp = divmod(g, per_block)
                    sg = parts[p][rows, group(gp)] + bias_ref[h, rows, group(g)]
                    if early and p < 2:
                        sg = jnp.where(i + p >= 2, sg, NEG_INF)
                    s_sc[h, rows, group(g)] = sg
                    mx = sg if mx is None else jnp.maximum(mx, sg)
                m_sc[h, rows] = jnp.broadcast_to(jnp.max(mx, axis=-1, keepdims=True), (CHUNK, LANES))

    values = functools.partial(_band_values, v_refs, o_ref, s_sc, m_sc, bands, visible)

    @pl.when(i >= 2)
    def _():
        logits(False)

    @pl.when(i < 2)
    def _():
        logits(True)

    values()


def _band_values(v_refs, o_ref, s_sc, m_sc, bands, visible):
    tq = o_ref.shape[1]
    n_groups = s_sc.shape[2] // LANES
    group = lambda g: slice(g * LANES, (g + 1) * LANES)
    gate = previous = None
    for h in range(N_HEADS):
        p_bands, invs = [], []
        for c, rows in enumerate(bands):
            m = m_sc[h, rows]
            if gate is not None:
                m = m + gate
            rs, ps = None, []
            for g in range(n_groups):
                if g in visible(c):
                    e = jnp.exp2(s_sc[h, rows, group(g)] - m)
                    ps.append(e.astype(BF16))
                    rs = e if rs is None else rs + e
                else:
                    ps.append(jnp.zeros((CHUNK, LANES), BF16))
            p_bands.append(jnp.concatenate(ps, axis=1))
            invs.append(1.0 / jnp.sum(rs, axis=-1, keepdims=True))
        p_all = jnp.concatenate(p_bands, axis=0)
        o = None
        for p, v_ref in enumerate(v_refs):
            part = _dot(p_all[:, p * tq:(p + 1) * tq], _pair(v_ref, h))
            o = part if o is None else o + part
        gate, previous = previous, _zero_after(o)
        half = slice((h % 2) * HEAD_DIM, (h % 2 + 1) * HEAD_DIM)
        o_ref[0, :, h * HEAD_DIM:(h + 1) * HEAD_DIM] = (o * jnp.concatenate(invs, axis=0))[:, half].astype(BF16)


def _band_prompt(qa, ka, va, bias):
    b, t, _ = qa.shape
    tq = BAND_Q
    assert t % tq == 0
    blk = lambda back: pl.BlockSpec((1, tq, WIDTH), lambda bi, i: (bi, jnp.maximum(i - back, 0), 0))
    return pl.pallas_call(
        _band_prompt_kernel, grid=(b, t // tq),
        in_specs=[blk(0), blk(2), blk(1), blk(0), blk(2), blk(1), blk(0),
                  pl.BlockSpec(bias.shape, lambda bi, i: (0, 0, 0))],
        out_specs=blk(0), out_shape=jax.ShapeDtypeStruct((b, t, WIDTH), BF16),
        scratch_shapes=[pltpu.VMEM((N_HEADS, tq, BAND_WIN), F32), pltpu.VMEM((N_HEADS, tq, LANES), F32),
                        pltpu.VMEM((N_HEADS, tq, LANES), BF16)],
        compiler_params=_params("arbitrary", "arbitrary"), name="band_prompt",
    )(qa, ka, ka, ka, va, va, va, bias)


def _band_sample_kernel(q_ref, kc_ref, vc_ref, kn_ref, vn_ref, bias_ref, o_ref):
    for h in range(N_HEADS):
        hs = slice(h * HEAD_DIM, (h + 1) * HEAD_DIM)
        q = q_ref[0, :, hs]
        s = jnp.concatenate([_dot(q, kc_ref[0, h].astype(BF16)), _dot_nt(q, kn_ref[0, :, hs])], axis=1)
        s = s + bias_ref[h, :CHUNK, :BAND_PAST + CHUNK]
        o = _softmax_pv(s, [vc_ref[0, h].astype(BF16), vn_ref[0, :, hs]], transposed=(True, False))
        o_ref[0, :, hs] = o.astype(BF16)


def _band_sample(qa, ka, va, cache_k, cache_v, bias):
    b, s, _ = qa.shape
    la = cache_k.shape[3]
    assert s == CHUNK and la == BAND_PAST
    new = pl.BlockSpec((1, s, WIDTH), lambda bi: (bi, 0, 0))
    old = pl.BlockSpec((1, N_HEADS, HEAD_DIM, la), lambda bi: (bi, 0, 0, 0))
    return pl.pallas_call(
        _band_sample_kernel, grid=(b,),
        in_specs=[new, old, old, new, new, pl.BlockSpec(bias.shape, lambda bi: (0, 0, 0))],
        out_specs=new, out_shape=jax.ShapeDtypeStruct((b, s, WIDTH), BF16),
        compiler_params=_params("arbitrary"), name="band_sample",
    )(qa, cache_k, cache_v, ka, va, bias)


def _fox_scratch(tq, tk):
    rows = lambda w: pltpu.VMEM((N_HEADS, tq, w), F32)
    return [rows(tk), rows(LANES), rows(LANES), rows(LANES), rows(LANES),
            pltpu.VMEM((N_HEADS, tq, LANES), BF16), pltpu.VMEM((N_HEADS // 2, tq, LANES), F32)]


def _low_half(rows):
    return lax.broadcasted_iota(jnp.int32, (rows, LANES), 1) < HEAD_DIM


def _fox_init(q_ref, cq_ref, sc):
    _, cq_sc, m_sc, _, l_sc, qm_sc, acc_sc = sc
    tq = cq_sc.shape[1]
    low = _low_half(tq)
    for h in range(N_HEADS):
        cq_sc[h] = jnp.broadcast_to(cq_ref[0, h:h + 1, :], (LANES, cq_ref.shape[2])).T[:tq]
        q_pair = q_ref[0, :, (h // 2) * LANES:(h // 2 + 1) * LANES]
        qm_sc[h] = jnp.where(low if h % 2 == 0 else ~low, q_pair, jnp.zeros_like(q_pair))
    m_sc[...] = jnp.full(m_sc.shape, NEG_INF, F32)
    l_sc[...] = jnp.zeros(l_sc.shape, F32)
    acc_sc[...] = jnp.zeros(acc_sc.shape, F32)


def _fox_block(qk, pv, crow_ref, tk, causal, sc):
    s_sc, cq_sc, m_sc, a_sc, l_sc, qm_sc, acc_sc = sc
    tq = s_sc.shape[1]
    band = min(tq, FOX_BAND)
    bands = [slice(r, r + band) for r in range(0, tq, band)]
    n_groups = tk // LANES
    group = lambda g: slice(g * LANES, (g + 1) * LANES)

    def visible(rows):
        if not causal:
            return [(g, False) for g in range(n_groups)]
        return [(g, (g + 1) * LANES - 1 > rows.start) for g in range(n_groups) if g * LANES < rows.stop]

    if causal:
        ahead = (lax.broadcasted_iota(jnp.int32, (band, LANES), 0)
                 - lax.broadcasted_iota(jnp.int32, (band, LANES), 1))
    for h in range(N_HEADS):
        s = qk(h, qm_sc[h])
        for rows in bands:
            cq = cq_sc[h, rows]
            mx = None
            for g, masked in visible(rows):
                sg = s[rows, group(g)] + (cq - crow_ref[0, h:h + 1, group(g)])
                if masked:
                    sg = jnp.where(ahead >= g * LANES - rows.start, sg, NEG_INF)
                s_sc[h, rows, group(g)] = sg
                mx = sg if mx is None else jnp.maximum(mx, sg)
            m_prev = m_sc[h, rows]
            m_new = jnp.maximum(m_prev, jnp.max(mx, axis=-1, keepdims=True))
            a_sc[h, rows] = jnp.exp2(m_prev - m_new)
            m_sc[h, rows] = m_new
    for h in range(N_HEADS):
        p_bands = []
        for rows in bands:
            seen = visible(rows)
            m_new = m_sc[h, rows]
            rs, ps = None, []
            for g, _ in seen:
                p = jnp.exp2(s_sc[h, rows, group(g)] - m_new)
                ps.append(p.astype(BF16))
                rs = p if rs is None else rs + p
            l_sc[h, rows] = a_sc[h, rows] * l_sc[h, rows] + rs
            ps += [jnp.zeros((band, LANES), BF16)] * (n_groups - len(seen))
            p_bands.append(jnp.concatenate(ps, axis=1))
        half = slice((h % 2) * HEAD_DIM, (h % 2 + 1) * HEAD_DIM)
        new = a_sc[h] * acc_sc[h // 2] + pv(h, jnp.concatenate(p_bands, axis=0))
        acc_sc[h // 2, :, half] = new[:, half]


def _fox_finish(o_ref, sc):
    l_sc, acc_sc = sc[4], sc[6]
    low = _low_half(acc_sc.shape[1])
    total = lambda h: jnp.sum(l_sc[h], axis=-1, keepdims=True)
    for g in range(N_HEADS // 2):
        inv = jnp.where(low, 1.0 / total(2 * g), 1.0 / total(2 * g + 1))
        o_ref[0, :, g * LANES:(g + 1) * LANES] = (acc_sc[g] * inv).astype(BF16)


def _head(ref, h):
    return ref[0, :, h * HEAD_DIM:(h + 1) * HEAD_DIM]


def _pair(ref, h):
    return ref[0, :, (h // 2) * LANES:(h // 2 + 1) * LANES]


def _row_major_products(k_ref, v_ref):
    return (lambda h, qm: _dot_nt(qm, _pair(k_ref, h))), (lambda h, p: _dot(p, _pair(v_ref, h)))


def _fox_prompt_kernel(qi_ref, kj_ref, q_ref, k_ref, v_ref, crow_ref, cq_ref, o_ref, *sc):
    p = pl.program_id(1)
    qi, kj = qi_ref[p], kj_ref[p]
    tk = k_ref.shape[1]
    qk, pv = _row_major_products(k_ref, v_ref)

    for first in (True, False):
        started = (kj == 0) if first else (kj > 0)

        @pl.when(started & (kj < qi))
        def _():
            if first:
                _fox_init(q_ref, cq_ref, sc)
            _fox_block(qk, pv, crow_ref, tk, False, sc)

        @pl.when(started & (kj == qi))
        def _():
            if first:
                _fox_init(q_ref, cq_ref, sc)
            _fox_block(qk, pv, crow_ref, tk, True, sc)
            _fox_finish(o_ref, sc)


def _fox_prompt(qb, kb, vb, crow):
    b, t, _ = qb.shape
    tq = FOX_BLOCK
    assert t % tq == 0
    pairs = [(i, j) for i in range(t // tq) for j in range(i + 1)]
    qi = jnp.array([p[0] for p in pairs], jnp.int32)
    kj = jnp.array([p[1] for p in pairs], jnp.int32)
    qblk = pl.BlockSpec((1, tq, WIDTH), lambda bi, p, qi, kj: (bi, qi[p], 0))
    kblk = pl.BlockSpec((1, tq, WIDTH), lambda bi, p, qi, kj: (bi, kj[p], 0))
    grid_spec = pltpu.PrefetchScalarGridSpec(
        num_scalar_prefetch=2, grid=(b, len(pairs)),
        in_specs=[qblk, kblk, kblk,
                  pl.BlockSpec((1, N_HEADS, tq), lambda bi, p, qi, kj: (bi, 0, kj[p])),
                  pl.BlockSpec((1, N_HEADS, tq), lambda bi, p, qi, kj: (bi, 0, qi[p]))],
        out_specs=qblk, scratch_shapes=_fox_scratch(tq, tq))
    return pl.pallas_call(
        _fox_prompt_kernel, grid_spec=grid_spec, out_shape=jax.ShapeDtypeStruct((b, t, WIDTH), BF16),
        compiler_params=_params("arbitrary", "arbitrary"), name="fox_prompt",
    )(qi, kj, qb, kb, vb, crow, crow)


def _fox_sample_kernel(q_ref, kc_ref, vc_ref, kn_ref, vn_ref, crow_ref, crow_new_ref, o_ref, *sc):
    j = pl.program_id(1)
    last = pl.num_programs(1) - 1

    @pl.when(j == 0)
    def _():
        _fox_init(q_ref, crow_new_ref, sc)

    def cached_pair(ref, h):
        g = h // 2
        return ref[0, 2 * g:2 * g + 2].astype(BF16).reshape(2 * HEAD_DIM, ref.shape[3])

    _fox_block(lambda h, qm: _dot(qm, cached_pair(kc_ref, h)),
               lambda h, p: _dot_nt(p, cached_pair(vc_ref, h)), crow_ref, kc_ref.shape[3], False, sc)

    @pl.when(j == last)
    def _():
        qk, pv = _row_major_products(kn_ref, vn_ref)
        _fox_block(qk, pv, crow_new_ref, kn_ref.shape[1], True, sc)
        _fox_finish(o_ref, sc)


def _fox_sample(qb, kb, vb, cache_k, cache_v, crow, crow_new):
    b, s, _ = qb.shape
    past = cache_k.shape[3]
    tk, tn = min(FOX_CACHE_BLOCK, past), kb.shape[1]
    assert past % tk == 0 and tn % LANES == 0 and tn >= s
    nk = past // tk
    new = lambda r: pl.BlockSpec((1, r, WIDTH), lambda bi, j: (bi, 0, 0))
    old = pl.BlockSpec((1, N_HEADS, HEAD_DIM, tk), lambda bi, j: (bi, 0, 0, j))
    return pl.pallas_call(
        _fox_sample_kernel, grid=(b, nk),
        in_specs=[new(s), old, old, new(tn), new(tn),
                  pl.BlockSpec((1, N_HEADS, tk), lambda bi, j: (bi, 0, j)),
                  pl.BlockSpec((1, N_HEADS, tn), lambda bi, j: (bi, 0, 0))],
        out_specs=new(s), out_shape=jax.ShapeDtypeStruct((b, s, WIDTH), BF16),
        scratch_shapes=_fox_scratch(s, tk),
        compiler_params=_params("arbitrary", "arbitrary"), name="fox_sample",
    )(qb, cache_k, cache_v, kb, vb, crow, crow_new)


def _mid_kernel(x_ref, oa_ref, ob_ref, wo_ref, g2_ref, wq_ref, gq_ref, mk_ref, mv_ref, wmo_ref, o_ref,
                om_sc, *, seg):
    tm, d = x_ref.shape
    hd = d // N_MEM_HEADS
    x1 = x_ref[...] + _dot(oa_ref[...], wo_ref[:WIDTH, :]) + _dot(ob_ref[...], wo_ref[WIDTH:, :])
    h2 = _rms(x1, g2_ref[...]).astype(BF16)
    qm = _dot(h2, wq_ref[...])
    scale = hd ** -0.5 * LOG2E
    for hh in range(N_MEM_HEADS):
        cs = slice(hh * hd, (hh + 1) * hd)
        q = (_rms(qm[:, cs], gq_ref[...]) * scale).astype(BF16)
        def mem_head(ref, sg):
            if ref.shape[2] == d:
                return ref[sg, :, cs]
            rows = lambda lg: ref[sg, pl.ds(lg * N_MEM_HEADS + hh, ref.shape[1] // (2 * N_MEM_HEADS),
                                            stride=2 * N_MEM_HEADS), :]
            return jnp.concatenate([rows(0), rows(1)], axis=1).astype(BF16)

        for sg in range(tm // seg):
            rs = slice(sg * seg, (sg + 1) * seg)
            s = _dot_nt(q[rs], mem_head(mk_ref, sg))
            om_sc[rs, cs] = _softmax_pv(s, [mem_head(mv_ref, sg)]).astype(BF16)
    o_ref[...] = x1 + _dot(om_sc[...], wmo_ref[...])


def _cache_mem_rows(x):
    b, n, nh, hd = x.shape
    assert nh == N_MEM_HEADS and hd == 2 * LANES
    return x.reshape(b, n, nh, 2, LANES).transpose(0, 1, 3, 2, 4).reshape(b, n * 2 * nh, LANES)


def _mid(x2d, oa, ob, w_o, g2, w_mq, g_mq, mk, mv, w_mo, *, seq):
    n, d = x2d.shape
    tm = MID_TILE if seq % MID_TILE == 0 else ROW_TILE
    seg = min(seq, tm)
    tiles_per_batch = seq // seg
    nb = tm // seg
    assert n % tm == 0 and seq % seg == 0 and tm % seg == 0
    rows = lambda w: pl.BlockSpec((tm, w), lambda i: (i, 0))
    full = _resident
    mem = pl.BlockSpec((nb,) + mk.shape[1:], lambda i: (i // tiles_per_batch, 0, 0))
    return pl.pallas_call(
        functools.partial(_mid_kernel, seg=seg), grid=(n // tm,),
        in_specs=[rows(d), rows(WIDTH), rows(WIDTH), full(w_o), full(g2), full(w_mq), full(g_mq), mem, mem,
                  full(w_mo)],
        out_specs=rows(d), out_shape=jax.ShapeDtypeStruct((n, d), F32),
        scratch_shapes=[pltpu.VMEM((tm, d), BF16)],
        compiler_params=_params("arbitrary"), name="mid",
    )(x2d, oa, ob, w_o, g2, w_mq, g_mq, mk, mv, w_mo)


def _ffn_kernel(x_ref, g3_ref, wup_ref, wc_ref, bc_ref, wd_ref, st_ref, o_ref, last_ref, carry_sc, y_sc,
                *, seg, tiles_per_batch, ff_chunk):
    tm, d = x_ref.shape
    dff = wd_ref.shape[0]
    i = pl.program_id(0)
    h3 = _rms(x_ref[...], g3_ref[...]).astype(BF16)
    if tiles_per_batch > 1:
        @pl.when(i == 0)
        def _():
            carry_sc[...] = jnp.zeros(carry_sc.shape, F32)
    rowid = lax.broadcasted_iota(jnp.int32, (SUBLANES, 1), 0)
    first_tile = (i % tiles_per_batch) == 0
    for c in range(dff // ff_chunk):
        cs = slice(c * ff_chunk, (c + 1) * ff_chunk)
        gate = _dot(h3, wup_ref[:, cs])
        val = _dot(h3, wup_ref[:, dff + c * ff_chunk:dff + (c + 1) * ff_chunk])
        w0, w1, w2 = wc_ref[0:1, cs], wc_ref[1:2, cs], wc_ref[2:3, cs]
        for sg in range(tm // seg):
            rows = slice(sg * seg, (sg + 1) * seg)
            g = gate[rows]
            if tiles_per_batch == 1:
                prev = st_ref[sg, :, cs]
            else:
                prev = jnp.where(first_tile, st_ref[0, :, cs], carry_sc[SUBLANES - 2:, cs])
            p2, p1 = prev[0:1], prev[1:2]
            s1 = pltpu.roll(g, 1, 0)
            s2 = pltpu.roll(g, 2, 0)
            top1 = jnp.where(rowid == 0, p1, s1[:SUBLANES])
            top2 = jnp.where(rowid == 0, p2, jnp.where(rowid == 1, p1, s2[:SUBLANES]))
            s1 = jnp.concatenate([top1, s1[SUBLANES:]], axis=0)
            s2 = jnp.concatenate([top2, s2[SUBLANES:]], axis=0)
            conv = s2 * w0 + s1 * w1 + g * w2 + bc_ref[:, cs]
            y_sc[rows, cs] = (conv * jax.nn.sigmoid(conv) * val[rows]).astype(BF16)
            last_ref[sg, :, cs] = g[seg - SUBLANES:]
        if tiles_per_batch > 1:
            carry_sc[:, cs] = gate[tm - SUBLANES:]
    o_ref[...] = x_ref[...] + _dot(y_sc[...], wd_ref[...])


def _conv_ffn(x2d, g3, w_up, w_conv, b_conv, w_down, state, *, seq, ff_chunk):
    n, d = x2d.shape
    dff = w_down.shape[0]
    tm = min(FFN_TILE, n)
    seg = min(seq, tm)
    tiles_per_batch = seq // seg
    nb = tm // seg
    assert n % tm == 0 and seq % seg == 0 and tm % seg == 0 and dff % ff_chunk == 0 and seg >= SUBLANES
    rows = pl.BlockSpec((tm, d), lambda i: (i, 0))
    full = _resident
    per_batch = lambda r: pl.BlockSpec((nb, r, dff), lambda i: (i // tiles_per_batch, 0, 0))
    return pl.pallas_call(
        functools.partial(_ffn_kernel, seg=seg, tiles_per_batch=tiles_per_batch, ff_chunk=ff_chunk),
        grid=(n // tm,),
        in_specs=[rows, full(g3), full(w_up), full(w_conv), full(b_conv), full(w_down), per_batch(CONV_W - 1)],
        out_specs=[rows, per_batch(SUBLANES)],
        out_shape=[jax.ShapeDtypeStruct((n, d), F32), jax.ShapeDtypeStruct((n // seq, SUBLANES, dff), F32)],
        scratch_shapes=[pltpu.VMEM((SUBLANES, dff), F32), pltpu.VMEM((tm, dff), BF16)],
        compiler_params=_params("arbitrary"), name="conv_ffn",
    )(x2d, g3, w_up, w_conv, b_conv, w_down, state)


def _row(v):
    return v.astype(F32).reshape(1, -1)


def _layer_weights(l, w_in, b_f, g_qa, g_ka, rel_bias, g_qb, g_kb, w_o, g_norm1, g_norm2, g_mem, w_mq, w_mkv,
                   g_mq, g_mk, w_mo, g_norm3, w_up, w_conv, b_conv, w_down):
    n_main = 6 * WIDTH
    tile = lambda g: jnp.tile(g[l].astype(F32), N_HEADS)
    blk = jnp.arange(MXU_COLS) // HEAD_DIM
    return dict(
        w_main=w_in[l][:, :n_main].astype(BF16),
        w_f=jnp.pad(w_in[l][:, n_main:].T, ((0, BF16_ROWS - N_HEADS), (0, 0))).astype(BF16),
        b_f=b_f[l].astype(F32).reshape(N_HEADS, 1),
        gains=jnp.stack([tile(g_qa), tile(g_ka), tile(g_qb), tile(g_kb)]),
        seg=(blk[:, None] == blk[None, :]).astype(BF16),
        rel=rel_bias[l],
        w_o=w_o[l].astype(BF16), g1=_row(g_norm1[l]), g2=_row(g_norm2[l]), g3=_row(g_norm3[l]),
        g_mem=_row(g_mem[l]), w_mq=w_mq[l].astype(BF16), w_mkv=w_mkv[l].astype(BF16),
        g_mq=_row(g_mq[l]), g_mk=_row(g_mk[l]), w_mo=w_mo[l].astype(BF16),
        w_up=w_up[l].astype(BF16), w_conv=w_conv[l].astype(F32), b_conv=_row(b_conv[l]),
        w_down=w_down[l].astype(BF16))


def _post_attention(x2d, oa, ob, mk, mv, state, w, *, seq, ff_chunk):
    n, d = x2d.shape
    x2 = _mid(x2d, oa.reshape(n, WIDTH), ob.reshape(n, WIDTH), w["w_o"], w["g2"], w["w_mq"], w["g_mq"],
              mk, mv, w["w_mo"], seq=seq)
    y, last = _conv_ffn(x2, w["g3"], w["w_up"], w["w_conv"], w["b_conv"], w["w_down"], state,
                        seq=seq, ff_chunk=ff_chunk)
    return y, last[:, SUBLANES - (CONV_W - 1):]


def _prompt_layer(x, mem, w, bias):
    b, t, d = x.shape
    la = min(BAND_PAST, t)
    tm = PROMPT_TILE if t % PROMPT_TILE == 0 else ROW_TILE
    assert la <= tm and t % tm == 0
    qa, ka, va, qb, kb, vb, ka_tail, va_tail, kb_f, vb_f, logf = _in_proj(
        x.reshape(b * t, d), w["g1"], w["w_main"], w["w_f"], w["b_f"], w["gains"], w["seg"],
        tm=tm, tail_period=t // tm, tail_rows=la, time_minor=True)
    r3 = lambda a: a.reshape(b, t, WIDTH)
    oa = _band_prompt(r3(qa), r3(ka), r3(va), bias)
    logf = logf.reshape(N_HEADS, b, t).transpose(1, 0, 2)
    crow = _cumsum_lanes(logf.reshape(b * N_HEADS, t)).reshape(b, N_HEADS, t)
    logf = logf.transpose(0, 2, 1)
    ob = _fox_prompt(r3(qb), r3(kb), r3(vb), crow)
    mk_f, mv_f, mk, mv = _mem_kv(mem, w["g_mem"], w["w_mkv"], w["g_mk"])
    dff = w["w_down"].shape[0]
    y, conv = _post_attention(x.reshape(b * t, d), oa, ob, mk, mv, jnp.zeros((b, CONV_W - 1, dff), F32), w,
                              seq=t, ff_chunk=dff // 11)
    heads = lambda a, n: a.reshape(b, N_HEADS, HEAD_DIM, n).transpose(0, 3, 1, 2)
    mem_heads = lambda a: (a.reshape(b, mem.shape[1], 2, N_MEM_HEADS, LANES).transpose(0, 1, 3, 2, 4)
                           .reshape(b, mem.shape[1], N_MEM_HEADS, d // N_MEM_HEADS))
    return (y.reshape(b, t, d), heads(ka_tail, la), heads(va_tail, la), heads(kb_f, t), heads(vb_f, t), logf,
            mem_heads(mk_f), mem_heads(mv_f), conv)


def _sample_layer(x, cache_a_k, cache_a_v, cache_b_k, cache_b_v, cache_b_logf, cache_mem_k, cache_mem_v,
                  state_conv, w, bias):
    b, s, d = x.shape
    past = cache_b_k.shape[1]
    assert ROW_TILE % s == 0 and (b * s) % ROW_TILE == 0
    qa, ka, va, qb, kb, vb, ka_f, va_f, kb_f, vb_f, logf = _in_proj(
        x.reshape(b * s, d), w["g1"], w["w_main"], w["w_f"], w["b_f"], w["gains"], w["seg"], tm=ROW_TILE,
        tail_period=1, tail_rows=ROW_TILE, time_minor=False)
    r3 = lambda a: a.reshape(b, s, WIDTH)
    flat = lambda a: a.reshape(a.shape[0], a.shape[1], -1)
    time_minor = lambda a: a.transpose(0, 2, 3, 1)
    oa = _band_sample(r3(qa), r3(ka), r3(va), time_minor(cache_a_k), time_minor(cache_a_v), bias)
    logf = logf.reshape(N_HEADS, b, s).transpose(1, 0, 2)
    total = past + s
    all_logf = jnp.concatenate([cache_b_logf.astype(F32).transpose(0, 2, 1), logf], axis=2)
    all_logf = jnp.pad(all_logf, ((0, 0), (0, 0), (0, -total % LANES)))
    logf = logf.transpose(0, 2, 1)
    cum = _cumsum_lanes(all_logf.reshape(b * N_HEADS, -1)).reshape(b, N_HEADS, -1)
    crow_new = cum[:, :, past:total]
    pad_rows = lambda a: jnp.pad(a, ((0, 0), (0, -s % LANES), (0, 0)))
    ob = _fox_sample(r3(qb), pad_rows(r3(kb)), pad_rows(r3(vb)), time_minor(cache_b_k), time_minor(cache_b_v),
                     cum[:, :, :past], jnp.pad(crow_new, ((0, 0), (0, 0), (0, -s % LANES))))
    dff = w["w_down"].shape[0]
    y, conv = _post_attention(x.reshape(b * s, d), oa, ob, _cache_mem_rows(cache_mem_k.astype(F32)),
                              _cache_mem_rows(cache_mem_v.astype(F32)), state_conv.astype(F32), w,
                              seq=s, ff_chunk=dff // 11)
    heads = lambda a: a.reshape(b, s, N_HEADS, HEAD_DIM)
    return y.reshape(b, s, d), heads(ka_f), heads(va_f), heads(kb_f), heads(vb_f), logf, conv


def kernel(x_prompt, x_sample, cache_a_k, cache_a_v, cache_b_k, cache_b_v, cache_b_logf, cache_mem_k, cache_mem_v, state_conv, mem_prompt, w_in, b_f, g_qa, g_ka, rel_bias, g_qb, g_kb, w_o, g_norm1, g_norm2, g_mem, w_mq, w_mkv, g_mq, g_mk, w_mo, g_norm3, w_up, w_conv, b_conv, w_down):
    depth = w_in.shape[0]
    xp, xs = x_prompt, x_sample
    outs_p, outs_s = [], []
    for l in range(depth):
        w = _layer_weights(l, w_in, b_f, g_qa, g_ka, rel_bias, g_qb, g_kb, w_o, g_norm1, g_norm2, g_mem, w_mq,
                           w_mkv, g_mq, g_mk, w_mo, g_norm3, w_up, w_conv, b_conv, w_down)
        bias = _band_bias(w["rel"])
        xp, *rest_p = _prompt_layer(xp, mem_prompt, w, bias)
        xs, *rest_s = _sample_layer(xs, cache_a_k[l], cache_a_v[l], cache_b_k[l], cache_b_v[l], cache_b_logf[l],
                                    cache_mem_k[l], cache_mem_v[l], state_conv[l], w, bias)
        outs_p.append(rest_p)
        outs_s.append(rest_s)
    stack = lambda outs: [jnp.stack(col, axis=0) for col in zip(*outs)]
    return (xp, xs, *stack(outs_p), *stack(outs_s))
```

```python
import functools

import jax
import jax.numpy as jnp
from jax import lax
from jax.experimental import pallas as pl
from jax.experimental.pallas import tpu as pltpu

F32 = jnp.float32
BF16 = jnp.bfloat16
EPS = 1e-6
NEG_INF = float("-inf")

HEAD_DIM = 64
N_HEADS = 8
WIDTH = N_HEADS * HEAD_DIM
CHUNK = 64
N_PREV_CHUNKS = 8
BAND_PAST = N_PREV_CHUNKS * CHUNK
REL_CLIP = 128
N_MEM_HEADS = 4
CONV_W = 3

ROW_TILE = 512
FFN_TILE = 1024
MID_TILE = 1024
PROMPT_TILE = 1024
BAND_Q = 256
BAND_WIN = BAND_PAST + BAND_Q
FOX_BLOCK = 512
FOX_KEY_BLOCKS = 2
FOX_CACHE_BLOCK = 2048
FOX_BAND = 64
MXU_COLS = 256
LANES = 128
SUBLANES = 8
BF16_ROWS = 16
VMEM_LIMIT = 56 * 1024 * 1024
LOG2E = 1.4426950408889634


def _dot(a, b):
    return jnp.dot(a, b, preferred_element_type=F32)


def _dot_nt(a, b):
    return lax.dot_general(a, b, (((1,), (1,)), ((), ())), preferred_element_type=F32)


def _rms(x, g):
    return x * lax.rsqrt(jnp.mean(x * x, axis=-1, keepdims=True) + EPS) * g


def _split3(x):
    hi = x.astype(BF16)
    r1 = x - hi.astype(F32)
    mid = r1.astype(BF16)
    lo = (r1 - mid.astype(F32)).astype(BF16)
    return hi, mid, lo


def _store_heads(ref, x):
    rows = x.shape[0]
    for h in range(N_HEADS):
        ref[pl.ds(h, rows, stride=N_HEADS), :] = x[:, h * HEAD_DIM:(h + 1) * HEAD_DIM]


def _resident(a):
    return pl.BlockSpec(a.shape, lambda i: (0,) * a.ndim, pipeline_mode=pl.Buffered(1))


def _params(*sem):
    return pltpu.CompilerParams(dimension_semantics=sem, vmem_limit_bytes=VMEM_LIMIT)


def _inproj_kernel(x_ref, g1_ref, w_ref, wf_ref, bf_ref, gains_ref, seg_ref,
                   qa_ref, ka_ref, va_ref, qb_ref, kb_ref, vb_ref,
                   kaf_ref, vaf_ref, kbf_ref, vbf_ref, logf_ref, *, tail_period, tail_rows, time_minor):
    h = _rms(x_ref[...], g1_ref[...]).astype(BF16)
    seg = seg_ref[...]

    def store_f32(ref, x):
        if time_minor:
            ref[...] = x.T
        else:
            _store_heads(ref, x)

    def proj(g):
        return _dot(h, w_ref[:, g * WIDTH:(g + 1) * WIDTH])

    def head_norm(p, row):
        p2 = (p * p).astype(BF16)
        ss = jnp.concatenate([_dot(p2[:, :MXU_COLS], seg), _dot(p2[:, MXU_COLS:], seg)], axis=1)
        return p * lax.rsqrt(ss * (1.0 / HEAD_DIM) + EPS) * gains_ref[row:row + 1, :]

    is_tail = (pl.program_id(0) % tail_period) == tail_period - 1
    scale = HEAD_DIM ** -0.5 * LOG2E

    kb = head_norm(proj(4), 3)
    kb_ref[...] = kb.astype(BF16)
    store_f32(kbf_ref.at[0], kb)
    vb = proj(5)
    vb_ref[...] = vb.astype(BF16)
    store_f32(vbf_ref.at[0], vb)
    z = _dot_nt(wf_ref[...], h)[:N_HEADS] + bf_ref[...]
    logf_ref[...] = jnp.minimum(z, 0.0) - jnp.log1p(jnp.exp(-jnp.abs(z)))
    qb_ref[...] = (head_norm(proj(3), 2) * scale).astype(BF16)
    qa_ref[...] = (head_norm(proj(0), 0) * scale).astype(BF16)
    ka = head_norm(proj(1), 1)
    ka_ref[...] = ka.astype(BF16)
    va = proj(2)
    va_ref[...] = va.astype(BF16)

    @pl.when(is_tail)
    def _():
        store_f32(kaf_ref.at[0], ka[-tail_rows:])
        store_f32(vaf_ref.at[0], va[-tail_rows:])


def _in_proj(x2d, g1, w_main, w_f, b_f, gains, seg, *, tm, tail_period, tail_rows, time_minor):
    n, d = x2d.shape
    assert n % tm == 0 and (n // tm) % tail_period == 0 and (tail_rows == tm or (time_minor and tail_rows <= tm))
    n_tail = n // tm // tail_period
    row_bf = jax.ShapeDtypeStruct((n, WIDTH), BF16)
    rows = pl.BlockSpec((tm, WIDTH), lambda i: (i, 0))
    if time_minor:
        head_f = jax.ShapeDtypeStruct((n_tail, WIDTH, tm * tail_period), F32)
        tail_f = jax.ShapeDtypeStruct((n_tail, WIDTH, tail_rows), F32)
        head_rows = pl.BlockSpec((1, WIDTH, tm), lambda i: (i // tail_period, 0, i % tail_period))
        tail = pl.BlockSpec((1, WIDTH, tail_rows), lambda i: (i // tail_period, 0, 0))
    else:
        head_f = jax.ShapeDtypeStruct((n // tm, tm * N_HEADS, HEAD_DIM), F32)
        tail_f = jax.ShapeDtypeStruct((n_tail, tm * N_HEADS, HEAD_DIM), F32)
        head_rows = pl.BlockSpec((1, tm * N_HEADS, HEAD_DIM), lambda i: (i, 0, 0))
        tail = pl.BlockSpec((1, tm * N_HEADS, HEAD_DIM), lambda i: (i // tail_period, 0, 0))
    full = _resident
    return pl.pallas_call(
        functools.partial(_inproj_kernel, tail_period=tail_period, tail_rows=tail_rows, time_minor=time_minor),
        grid=(n // tm,),
        in_specs=[pl.BlockSpec((tm, d), lambda i: (i, 0)), full(g1), full(w_main), full(w_f),
                  full(b_f), full(gains), full(seg)],
        out_specs=[rows] * 6 + [tail, tail, head_rows, head_rows, pl.BlockSpec((N_HEADS, tm), lambda i: (0, i))],
        out_shape=[row_bf] * 6 + [tail_f, tail_f, head_f, head_f, jax.ShapeDtypeStruct((N_HEADS, n), F32)],
        compiler_params=_params("arbitrary"),
        name="in_proj",
    )(x2d, g1, w_main, w_f, b_f, gains, seg)


def _store_mem_rows(ref, x, hh):
    n = x.shape[0]
    for lg in range(2):
        ref[pl.ds(lg * N_MEM_HEADS + hh, n, stride=2 * N_MEM_HEADS), :] = x[:, lg * LANES:(lg + 1) * LANES]


def _memkv_kernel(mem_ref, g_ref, w_ref, gk_ref, kf_ref, vf_ref, kb_ref, vb_ref):
    d = mem_ref.shape[-1]
    hd = d // N_MEM_HEADS
    h = _rms(mem_ref[0], g_ref[...]).astype(BF16)
    kv = _dot(h, w_ref[...])
    for hh in range(N_MEM_HEADS):
        cs = slice(hh * hd, (hh + 1) * hd)
        k = _rms(kv[:, cs], gk_ref[...])
        v = kv[:, d + hh * hd:d + (hh + 1) * hd]
        _store_mem_rows(kf_ref.at[0], k, hh)
        _store_mem_rows(vf_ref.at[0], v, hh)
        kb_ref[0, :, cs] = k.astype(BF16)
        vb_ref[0, :, cs] = v.astype(BF16)


def _mem_kv(mem, g_mem, w_mkv, g_mk):
    b, n, d = mem.shape
    assert d == N_MEM_HEADS * 2 * LANES
    blk = pl.BlockSpec((1, n, d), lambda i: (i, 0, 0))
    blk_f = pl.BlockSpec((1, n * 2 * N_MEM_HEADS, LANES), lambda i: (i, 0, 0))
    full = _resident
    f = jax.ShapeDtypeStruct((b, n * 2 * N_MEM_HEADS, LANES), F32)
    h = jax.ShapeDtypeStruct((b, n, d), BF16)
    return pl.pallas_call(
        _memkv_kernel, grid=(b,),
        in_specs=[blk, full(g_mem), full(w_mkv), full(g_mk)],
        out_specs=[blk_f, blk_f, blk, blk], out_shape=[f, f, h, h],
        compiler_params=_params("arbitrary"), name="mem_kv",
    )(mem, g_mem, w_mkv, g_mk)


def _cumsum_kernel(x_ref, o_ref):
    r, t = x_ref.shape
    k = lax.broadcasted_iota(jnp.int32, (LANES, LANES), 0)
    j = lax.broadcasted_iota(jnp.int32, (LANES, LANES), 1)
    tri = jnp.where(k <= j, 1.0, 0.0).astype(BF16)
    carry = jnp.zeros((r, 1), F32)
    for blk in range(t // LANES):
        hi, mid, lo = _split3(x_ref[:, blk * LANES:(blk + 1) * LANES])
        cs = _dot(hi, tri) + _dot(mid, tri) + _dot(lo, tri)
        o_ref[:, blk * LANES:(blk + 1) * LANES] = (cs + carry) * LOG2E
        carry = carry + cs[:, LANES - 1:LANES]


def _cumsum_lanes(x):
    assert x.shape[1] % LANES == 0
    return pl.pallas_call(_cumsum_kernel, out_shape=jax.ShapeDtypeStruct(x.shape, F32), name="cumsum")(x)


def _band_bias_kernel(tbl_ref, o_ref):
    nrel = tbl_ref.shape[1]
    qblk, win = o_ref.shape[1:]
    wrap = qblk + win
    jp = lax.broadcasted_iota(jnp.int32, (nrel, wrap), 1)
    r = lax.broadcasted_iota(jnp.int32, (nrel, wrap), 0)
    d = jnp.where(jp < win, jp, jp - wrap)
    rel = jnp.clip(BAND_PAST - d, -REL_CLIP, REL_CLIP) + REL_CLIP
    onehot = jnp.where(rel == r, 1.0, 0.0).astype(BF16)
    hi, mid, lo = _split3(tbl_ref[...])
    base = (_dot(hi, onehot) + _dot(mid, onehot) + _dot(lo, onehot)) * LOG2E
    qc = lax.broadcasted_iota(jnp.int32, (qblk, win), 0) // CHUNK
    kc = lax.broadcasted_iota(jnp.int32, (qblk, win), 1) // CHUNK
    inband = (kc >= qc) & (kc <= qc + N_PREV_CHUNKS)
    for h in range(N_HEADS):
        row = jnp.broadcast_to(base[h:h + 1, :], (qblk, wrap))
        toeplitz = pltpu.roll(row, 0, 1, stride=1, stride_axis=0)
        o_ref[h] = jnp.where(inband, toeplitz[:, :win], NEG_INF)


def _band_bias(rel_table):
    nrel = 2 * REL_CLIP + 1
    pad = -nrel % LANES
    tbl = jnp.pad(rel_table.astype(F32), ((0, 0), (0, pad)))
    return pl.pallas_call(
        _band_bias_kernel, out_shape=jax.ShapeDtypeStruct((N_HEADS, BAND_Q, BAND_WIN), F32),
        name="band_bias")(tbl)


def _zero_after(x):
    bits = pltpu.bitcast(x[:SUBLANES, :LANES], jnp.uint32)
    return ((bits >> 16) >> 16)[:1].astype(F32)


def _softmax_pv(s, vs, transposed=None):
    transposed = transposed or (False,) * len(vs)
    m = jnp.max(s, axis=-1, keepdims=True)
    e = jnp.exp2(s - m)
    l = jnp.sum(e, axis=-1, keepdims=True)
    eb = e.astype(BF16)
    o, lo = None, 0
    for v, vt in zip(vs, transposed):
        keys = v.shape[1] if vt else v.shape[0]
        part = _dot_nt(eb[:, lo:lo + keys], v) if vt else _dot(eb[:, lo:lo + keys], v)
        o = part if o is None else o + part
        lo += keys
    return o * (1.0 / l)


def _band_prompt_kernel(q_ref, k0_ref, k1_ref, k2_ref, v0_ref, v1_ref, v2_ref, bias_ref, o_ref, s_sc, m_sc, qm_sc):
    i = pl.program_id(1)
    tq = q_ref.shape[1]
    k_refs, v_refs = (k0_ref, k1_ref, k2_ref), (v0_ref, v1_ref, v2_ref)
    bands = [slice(r, r + CHUNK) for r in range(0, tq, CHUNK)]
    n_groups = BAND_WIN // LANES
    per_block = tq // LANES
    group = lambda g: slice(g * LANES, (g + 1) * LANES)
    visible = lambda c: [g for g in range(n_groups) if 2 * g + 1 >= c and 2 * g <= c + N_PREV_CHUNKS]
    low = _low_half(tq)
    for h in range(N_HEADS):
        q_pair = _pair(q_ref, h)
        qm_sc[h] = jnp.where(low if h % 2 == 0 else ~low, q_pair, jnp.zeros_like(q_pair))

    def logits(early):
        for h in range(N_HEADS):
            qm = qm_sc[h]
            parts = [_dot_nt(qm, _pair(k_ref, h)) for k_ref in k_refs]
            for c, rows in enumerate(bands):
                mx = None
                for g in visible(c):
                    p, gp = divmod(g, per_block)
                    sg = parts[p][rows, group(gp)] + bias_ref[h, rows, group(g)]
                    if early and p < 2:
                        sg = jnp.where(i + p >= 2, sg, NEG_INF)
                    s_sc[h, rows, group(g)] = sg
                    mx = sg if mx is None else jnp.maximum(mx, sg)
                m_sc[h, rows] = jnp.broadcast_to(jnp.max(mx, axis=-1, keepdims=True), (CHUNK, LANES))

    values = functools.partial(_band_values, v_refs, o_ref, s_sc, m_sc, bands, visible)

    @pl.when(i >= 2)
    def _():
        logits(False)

    @pl.when(i < 2)
    def _():
        logits(True)

    values()


def _band_values(v_refs, o_ref, s_sc, m_sc, bands, visible):
    tq = o_ref.shape[1]
    n_groups = s_sc.shape[2] // LANES
    group = lambda g: slice(g * LANES, (g + 1) * LANES)
    gate = previous = None
    for h in range(N_HEADS):
        p_bands, invs = [], []
        for c, rows in enumerate(bands):
            m = m_sc[h, rows]
            if gate is not None:
                m = m + gate
            rs, ps = None, []
            for g in range(n_groups):
                if g in visible(c):
                    e = jnp.exp2(s_sc[h, rows, group(g)] - m)
                    ps.append(e.astype(BF16))
                    rs = e if rs is None else rs + e
                else:
                    ps.append(jnp.zeros((CHUNK, LANES), BF16))
            p_bands.append(jnp.concatenate(ps, axis=1))
            invs.append(1.0 / jnp.sum(rs, axis=-1, keepdims=True))
        p_all = jnp.concatenate(p_bands, axis=0)
        o = None
        for p, v_ref in enumerate(v_refs):
            part = _dot(p_all[:, p * tq:(p + 1) * tq], _pair(v_ref, h))
            o = part if o is None else o + part
        gate, previous = previous, _zero_after(o)
        half = slice((h % 2) * HEAD_DIM, (h % 2 + 1) * HEAD_DIM)
        o_ref[0, :, h * HEAD_DIM:(h + 1) * HEAD_DIM] = (o * jnp.concatenate(invs, axis=0))[:, half].astype(BF16)


def _band_prompt(qa, ka, va, bias):
    b, t, _ = qa.shape
    tq = BAND_Q
    assert t % tq == 0
    blk = lambda back: pl.BlockSpec((1, tq, WIDTH), lambda bi, i: (bi, jnp.maximum(i - back, 0), 0))
    return pl.pallas_call(
        _band_prompt_kernel, grid=(b, t // tq),
        in_specs=[blk(0), blk(2), blk(1), blk(0), blk(2), blk(1), blk(0),
                  pl.BlockSpec(bias.shape, lambda bi, i: (0, 0, 0))],
        out_specs=blk(0), out_shape=jax.ShapeDtypeStruct((b, t, WIDTH), BF16),
        scratch_shapes=[pltpu.VMEM((N_HEADS, tq, BAND_WIN), F32), pltpu.VMEM((N_HEADS, tq, LANES), F32),
                        pltpu.VMEM((N_HEADS, tq, LANES), BF16)],
        compiler_params=_params("arbitrary", "arbitrary"), name="band_prompt",
    )(qa, ka, ka, ka, va, va, va, bias)


def _band_sample_kernel(q_ref, kc_ref, vc_ref, kn_ref, vn_ref, bias_ref, o_ref):
    for h in range(N_HEADS):
        hs = slice(h * HEAD_DIM, (h + 1) * HEAD_DIM)
        q = q_ref[0, :, hs]
        s = jnp.concatenate([_dot(q, kc_ref[0, h].astype(BF16)), _dot_nt(q, kn_ref[0, :, hs])], axis=1)
        s = s + bias_ref[h, :CHUNK, :BAND_PAST + CHUNK]
        o = _softmax_pv(s, [vc_ref[0, h].astype(BF16), vn_ref[0, :, hs]], transposed=(True, False))
        o_ref[0, :, hs] = o.astype(BF16)


def _band_sample(qa, ka, va, cache_k, cache_v, bias):
    b, s, _ = qa.shape
    la = cache_k.shape[3]
    assert s == CHUNK and la == BAND_PAST
    new = pl.BlockSpec((1, s, WIDTH), lambda bi: (bi, 0, 0))
    old = pl.BlockSpec((1, N_HEADS, HEAD_DIM, la), lambda bi: (bi, 0, 0, 0))
    return pl.pallas_call(
        _band_sample_kernel, grid=(b,),
        in_specs=[new, old, old, new, new, pl.BlockSpec(bias.shape, lambda bi: (0, 0, 0))],
        out_specs=new, out_shape=jax.ShapeDtypeStruct((b, s, WIDTH), BF16),
        compiler_params=_params("arbitrary"), name="band_sample",
    )(qa, cache_k, cache_v, ka, va, bias)


def _fox_scratch(tq, tk):
    rows = lambda w: pltpu.VMEM((N_HEADS, tq, w), F32)
    return [rows(tk), rows(LANES), rows(LANES), rows(LANES), rows(LANES),
            pltpu.VMEM((N_HEADS, tq, LANES), BF16), pltpu.VMEM((N_HEADS // 2, tq, LANES), F32)]


def _low_half(rows):
    return lax.broadcasted_iota(jnp.int32, (rows, LANES), 1) < HEAD_DIM


def _fox_init(q_ref, cq_ref, sc):
    _, cq_sc, m_sc, _, l_sc, qm_sc, acc_sc = sc
    tq = cq_sc.shape[1]
    low = _low_half(tq)
    for h in range(N_HEADS):
        cq_sc[h] = jnp.broadcast_to(cq_ref[0, h:h + 1, :], (LANES, cq_ref.shape[2])).T[:tq]
        q_pair = q_ref[0, :, (h // 2) * LANES:(h // 2 + 1) * LANES]
        qm_sc[h] = jnp.where(low if h % 2 == 0 else ~low, q_pair, jnp.zeros_like(q_pair))
    m_sc[...] = jnp.full(m_sc.shape, NEG_INF, F32)
    l_sc[...] = jnp.zeros(l_sc.shape, F32)
    acc_sc[...] = jnp.zeros(acc_sc.shape, F32)


def _fox_block(qk, pv, crow_ref, tk, causal, sc):
    s_sc, cq_sc, m_sc, a_sc, l_sc, qm_sc, acc_sc = sc
    tq = s_sc.shape[1]
    band = min(tq, FOX_BAND)
    bands = [slice(r, r + band) for r in range(0, tq, band)]
    n_groups = tk // LANES
    group = lambda g: slice(g * LANES, (g + 1) * LANES)

    def visible(rows):
        if not causal:
            return [(g, False) for g in range(n_groups)]
        return [(g, (g + 1) * LANES - 1 > rows.start) for g in range(n_groups) if g * LANES < rows.stop]

    if causal:
        ahead = (lax.broadcasted_iota(jnp.int32, (band, LANES), 0)
                 - lax.broadcasted_iota(jnp.int32, (band, LANES), 1))
    for h in range(N_HEADS):
        s = qk(h, qm_sc[h])
        for rows in bands:
            cq = cq_sc[h, rows]
            mx = None
            for g, masked in visible(rows):
                sg = s[rows, group(g)] + (cq - crow_ref[0, h:h + 1, group(g)])
                if masked:
                    sg = jnp.where(ahead >= g * LANES - rows.start, sg, NEG_INF)
                s_sc[h, rows, group(g)] = sg
                mx = sg if mx is None else jnp.maximum(mx, sg)
            m_prev = m_sc[h, rows]
            m_new = jnp.maximum(m_prev, jnp.max(mx, axis=-1, keepdims=True))
            a_sc[h, rows] = jnp.exp2(m_prev - m_new)
            m_sc[h, rows] = m_new
    for h in range(N_HEADS):
        p_bands = []
        for rows in bands:
            seen = visible(rows)
            m_new = m_sc[h, rows]
            rs, ps = None, []
            for g, _ in seen:
                p = jnp.exp2(s_sc[h, rows, group(g)] - m_new)
                ps.append(p.astype(BF16))
                rs = p if rs is None else rs + p
            l_sc[h, rows] = a_sc[h, rows] * l_sc[h, rows] + rs
            ps += [jnp.zeros((band, LANES), BF16)] * (n_groups - len(seen))
            p_bands.append(jnp.concatenate(ps, axis=1))
        half = slice((h % 2) * HEAD_DIM, (h % 2 + 1) * HEAD_DIM)
        new = a_sc[h] * acc_sc[h // 2] + pv(h, jnp.concatenate(p_bands, axis=0))
        acc_sc[h // 2, :, half] = new[:, half]


def _fox_finish(o_ref, sc):
    l_sc, acc_sc = sc[4], sc[6]
    low = _low_half(acc_sc.shape[1])
    total = lambda h: jnp.sum(l_sc[h], axis=-1, keepdims=True)
    for g in range(N_HEADS // 2):
        inv = jnp.where(low, 1.0 / total(2 * g), 1.0 / total(2 * g + 1))
        o_ref[0, :, g * LANES:(g + 1) * LANES] = (acc_sc[g] * inv).astype(BF16)


def _head(ref, h):
    return ref[0, :, h * HEAD_DIM:(h + 1) * HEAD_DIM]


def _pair(ref, h):
    return ref[0, :, (h // 2) * LANES:(h // 2 + 1) * LANES]


def _row_major_products(k_ref, v_ref):
    return (lambda h, qm: _dot_nt(qm, _pair(k_ref, h))), (lambda h, p: _dot(p, _pair(v_ref, h)))


def _fox_prompt_kernel(qi_ref, kj_ref, q_ref, k_ref, v_ref, crow_ref, cq_ref, o_ref, *sc):
    p = pl.program_id(1)
    qi, kj = qi_ref[p], kj_ref[p]
    tq = q_ref.shape[1]

    for i in range(FOX_KEY_BLOCKS):
        rows = slice(i * tq, (i + 1) * tq)
        qk, pv = _row_major_products(k_ref.at[:, rows, :], v_ref.at[:, rows, :])
        crow = crow_ref.at[:, :, rows]
        blk = FOX_KEY_BLOCKS * kj + i
        for first in ((True, False) if i == 0 else (False,)):
            started = True if i else ((kj == 0) if first else (kj > 0))

            @pl.when(started & (blk < qi))
            def _():
                if first:
                    _fox_init(q_ref, cq_ref, sc)
                _fox_block(qk, pv, crow, tq, False, sc)

            @pl.when(started & (blk == qi))
            def _():
                if first:
                    _fox_init(q_ref, cq_ref, sc)
                _fox_block(qk, pv, crow, tq, True, sc)
                _fox_finish(o_ref, sc)


def _fox_prompt(qb, kb, vb, crow):
    b, t, _ = qb.shape
    tq = FOX_BLOCK
    tk = FOX_KEY_BLOCKS * tq
    assert t % tk == 0
    pairs = [(i, j) for i in range(t // tq) for j in range(i // FOX_KEY_BLOCKS + 1)]
    qi = jnp.array([p[0] for p in pairs], jnp.int32)
    kj = jnp.array([p[1] for p in pairs], jnp.int32)
    qblk = pl.BlockSpec((1, tq, WIDTH), lambda bi, p, qi, kj: (bi, qi[p], 0))
    kblk = pl.BlockSpec((1, tk, WIDTH), lambda bi, p, qi, kj: (bi, kj[p], 0))
    grid_spec = pltpu.PrefetchScalarGridSpec(
        num_scalar_prefetch=2, grid=(b, len(pairs)),
        in_specs=[qblk, kblk, kblk,
                  pl.BlockSpec((1, N_HEADS, tk), lambda bi, p, qi, kj: (bi, 0, kj[p])),
                  pl.BlockSpec((1, N_HEADS, tq), lambda bi, p, qi, kj: (bi, 0, qi[p]))],
        out_specs=qblk, scratch_shapes=_fox_scratch(tq, tq))
    return pl.pallas_call(
        _fox_prompt_kernel, grid_spec=grid_spec, out_shape=jax.ShapeDtypeStruct((b, t, WIDTH), BF16),
        compiler_params=_params("arbitrary", "arbitrary"), name="fox_prompt",
    )(qi, kj, qb, kb, vb, crow, crow)


def _fox_sample_kernel(q_ref, kc_ref, vc_ref, kn_ref, vn_ref, crow_ref, crow_new_ref, o_ref, *sc):
    j = pl.program_id(1)
    last = pl.num_programs(1) - 1

    @pl.when(j == 0)
    def _():
        _fox_init(q_ref, crow_new_ref, sc)

    def cached_pair(ref, h):
        g = h // 2
        return ref[0, 2 * g:2 * g + 2].astype(BF16).reshape(2 * HEAD_DIM, ref.shape[3])

    _fox_block(lambda h, qm: _dot(qm, cached_pair(kc_ref, h)),
               lambda h, p: _dot_nt(p, cached_pair(vc_ref, h)), crow_ref, kc_ref.shape[3], False, sc)

    @pl.when(j == last)
    def _():
        qk, pv = _row_major_products(kn_ref, vn_ref)
        _fox_block(qk, pv, crow_new_ref, kn_ref.shape[1], True, sc)
        _fox_finish(o_ref, sc)


def _fox_sample(qb, kb, vb, cache_k, cache_v, crow, crow_new):
    b, s, _ = qb.shape
    past = cache_k.shape[3]
    tk, tn = min(FOX_CACHE_BLOCK, past), kb.shape[1]
    assert past % tk == 0 and tn % LANES == 0 and tn >= s
    nk = past // tk
    new = lambda r: pl.BlockSpec((1, r, WIDTH), lambda bi, j: (bi, 0, 0))
    old = pl.BlockSpec((1, N_HEADS, HEAD_DIM, tk), lambda bi, j: (bi, 0, 0, j))
    return pl.pallas_call(
        _fox_sample_kernel, grid=(b, nk),
        in_specs=[new(s), old, old, new(tn), new(tn),
                  pl.BlockSpec((1, N_HEADS, tk), lambda bi, j: (bi, 0, j)),
                  pl.BlockSpec((1, N_HEADS, tn), lambda bi, j: (bi, 0, 0))],
        out_specs=new(s), out_shape=jax.ShapeDtypeStruct((b, s, WIDTH), BF16),
        scratch_shapes=_fox_scratch(s, tk),
        compiler_params=_params("arbitrary", "arbitrary"), name="fox_sample",
    )(qb, cache_k, cache_v, kb, vb, crow, crow_new)


def _mid_kernel(x_ref, oa_ref, ob_ref, wo_ref, g2_ref, wq_ref, gq_ref, mk_ref, mv_ref, wmo_ref, o_ref,
                om_sc, *, seg):
    tm, d = x_ref.shape
    hd = d // N_MEM_HEADS
    x1 = x_ref[...] + _dot(oa_ref[...], wo_ref[:WIDTH, :]) + _dot(ob_ref[...], wo_ref[WIDTH:, :])
    h2 = _rms(x1, g2_ref[...]).astype(BF16)
    qm = _dot(h2, wq_ref[...])
    scale = hd ** -0.5 * LOG2E
    for hh in range(N_MEM_HEADS):
        cs = slice(hh * hd, (hh + 1) * hd)
        q = (_rms(qm[:, cs], gq_ref[...]) * scale).astype(BF16)
        def mem_head(ref, sg):
            if ref.shape[2] == d:
                return ref[sg, :, cs]
            rows = lambda lg: ref[sg, pl.ds(lg * N_MEM_HEADS + hh, ref.shape[1] // (2 * N_MEM_HEADS),
                                            stride=2 * N_MEM_HEADS), :]
            return jnp.concatenate([rows(0), rows(1)], axis=1).astype(BF16)

        for sg in range(tm // seg):
            rs = slice(sg * seg, (sg + 1) * seg)
            s = _dot_nt(q[rs], mem_head(mk_ref, sg))
            om_sc[rs, cs] = _softmax_pv(s, [mem_head(mv_ref, sg)]).astype(BF16)
    o_ref[...] = x1 + _dot(om_sc[...], wmo_ref[...])


def _cache_mem_rows(x):
    b, n, nh, hd = x.shape
    assert nh == N_MEM_HEADS and hd == 2 * LANES
    return x.reshape(b, n, nh, 2, LANES).transpose(0, 1, 3, 2, 4).reshape(b, n * 2 * nh, LANES)


def _mid(x2d, oa, ob, w_o, g2, w_mq, g_mq, mk, mv, w_mo, *, seq):
    n, d = x2d.shape
    tm = MID_TILE if seq % MID_TILE == 0 else ROW_TILE
    seg = min(seq, tm)
    tiles_per_batch = seq // seg
    nb = tm // seg
    assert n % tm == 0 and seq % seg == 0 and tm % seg == 0
    rows = lambda w: pl.BlockSpec((tm, w), lambda i: (i, 0))
    full = _resident
    mem = pl.BlockSpec((nb,) + mk.shape[1:], lambda i: (i // tiles_per_batch, 0, 0))
    return pl.pallas_call(
        functools.partial(_mid_kernel, seg=seg), grid=(n // tm,),
        in_specs=[rows(d), rows(WIDTH), rows(WIDTH), full(w_o), full(g2), full(w_mq), full(g_mq), mem, mem,
                  full(w_mo)],
        out_specs=rows(d), out_shape=jax.ShapeDtypeStruct((n, d), F32),
        scratch_shapes=[pltpu.VMEM((tm, d), BF16)],
        compiler_params=_params("arbitrary"), name="mid",
    )(x2d, oa, ob, w_o, g2, w_mq, g_mq, mk, mv, w_mo)


def _ffn_kernel(x_ref, g3_ref, wup_ref, wc_ref, bc_ref, wd_ref, st_ref, o_ref, last_ref, carry_sc, y_sc,
                *, seg, tiles_per_batch, ff_chunk):
    tm, d = x_ref.shape
    dff = wd_ref.shape[0]
    i = pl.program_id(0)
    h3 = _rms(x_ref[...], g3_ref[...]).astype(BF16)
    if tiles_per_batch > 1:
        @pl.when(i == 0)
        def _():
            carry_sc[...] = jnp.zeros(carry_sc.shape, F32)
    rowid = lax.broadcasted_iota(jnp.int32, (SUBLANES, 1), 0)
    first_tile = (i % tiles_per_batch) == 0
    for c in range(dff // ff_chunk):
        cs = slice(c * ff_chunk, (c + 1) * ff_chunk)
        gate = _dot(h3, wup_ref[:, cs])
        val = _dot(h3, wup_ref[:, dff + c * ff_chunk:dff + (c + 1) * ff_chunk])
        w0, w1, w2 = wc_ref[0:1, cs], wc_ref[1:2, cs], wc_ref[2:3, cs]
        for sg in range(tm // seg):
            rows = slice(sg * seg, (sg + 1) * seg)
            g = gate[rows]
            if tiles_per_batch == 1:
                prev = st_ref[sg, :, cs]
            else:
                prev = jnp.where(first_tile, st_ref[0, :, cs], carry_sc[SUBLANES - 2:, cs])
            p2, p1 = prev[0:1], prev[1:2]
            s1 = pltpu.roll(g, 1, 0)
            s2 = pltpu.roll(g, 2, 0)
            top1 = jnp.where(rowid == 0, p1, s1[:SUBLANES])
            top2 = jnp.where(rowid == 0, p2, jnp.where(rowid == 1, p1, s2[:SUBLANES]))
            s1 = jnp.concatenate([top1, s1[SUBLANES:]], axis=0)
            s2 = jnp.concatenate([top2, s2[SUBLANES:]], axis=0)
            conv = s2 * w0 + s1 * w1 + g * w2 + bc_ref[:, cs]
            y_sc[rows, cs] = (conv * jax.nn.sigmoid(conv) * val[rows]).astype(BF16)
            last_ref[sg, :, cs] = g[seg - SUBLANES:]
        if tiles_per_batch > 1:
            carry_sc[:, cs] = gate[tm - SUBLANES:]
    o_ref[...] = x_ref[...] + _dot(y_sc[...], wd_ref[...])


def _conv_ffn(x2d, g3, w_up, w_conv, b_conv, w_down, state, *, seq, ff_chunk):
    n, d = x2d.shape
    dff = w_down.shape[0]
    tm = min(FFN_TILE, n)
    seg = min(seq, tm)
    tiles_per_batch = seq // seg
    nb = tm // seg
    assert n % tm == 0 and seq % seg == 0 and tm % seg == 0 and dff % ff_chunk == 0 and seg >= SUBLANES
    rows = pl.BlockSpec((tm, d), lambda i: (i, 0))
    full = _resident
    per_batch = lambda r: pl.BlockSpec((nb, r, dff), lambda i: (i // tiles_per_batch, 0, 0))
    return pl.pallas_call(
        functools.partial(_ffn_kernel, seg=seg, tiles_per_batch=tiles_per_batch, ff_chunk=ff_chunk),
        grid=(n // tm,),
        in_specs=[rows, full(g3), full(w_up), full(w_conv), full(b_conv), full(w_down), per_batch(CONV_W - 1)],
        out_specs=[rows, per_batch(SUBLANES)],
        out_shape=[jax.ShapeDtypeStruct((n, d), F32), jax.ShapeDtypeStruct((n // seq, SUBLANES, dff), F32)],
        scratch_shapes=[pltpu.VMEM((SUBLANES, dff), F32), pltpu.VMEM((tm, dff), BF16)],
        compiler_params=_params("arbitrary"), name="conv_ffn",
    )(x2d, g3, w_up, w_conv, b_conv, w_down, state)


def _row(v):
    return v.astype(F32).reshape(1, -1)


def _layer_weights(l, w_in, b_f, g_qa, g_ka, rel_bias, g_qb, g_kb, w_o, g_norm1, g_norm2, g_mem, w_mq, w_mkv,
                   g_mq, g_mk, w_mo, g_norm3, w_up, w_conv, b_conv, w_down):
    n_main = 6 * WIDTH
    tile = lambda g: jnp.tile(g[l].astype(F32), N_HEADS)
    blk = jnp.arange(MXU_COLS) // HEAD_DIM
    return dict(
        w_main=w_in[l][:, :n_main].astype(BF16),
        w_f=jnp.pad(w_in[l][:, n_main:].T, ((0, BF16_ROWS - N_HEADS), (0, 0))).astype(BF16),
        b_f=b_f[l].astype(F32).reshape(N_HEADS, 1),
        gains=jnp.stack([tile(g_qa), tile(g_ka), tile(g_qb), tile(g_kb)]),
        seg=(blk[:, None] == blk[None, :]).astype(BF16),
        rel=rel_bias[l],
        w_o=w_o[l].astype(BF16), g1=_row(g_norm1[l]), g2=_row(g_norm2[l]), g3=_row(g_norm3[l]),
        g_mem=_row(g_mem[l]), w_mq=w_mq[l].astype(BF16), w_mkv=w_mkv[l].astype(BF16),
        g_mq=_row(g_mq[l]), g_mk=_row(g_mk[l]), w_mo=w_mo[l].astype(BF16),
        w_up=w_up[l].astype(BF16), w_conv=w_conv[l].astype(F32), b_conv=_row(b_conv[l]),
        w_down=w_down[l].astype(BF16))


def _post_attention(x2d, oa, ob, mk, mv, state, w, *, seq, ff_chunk):
    n, d = x2d.shape
    x2 = _mid(x2d, oa.reshape(n, WIDTH), ob.reshape(n, WIDTH), w["w_o"], w["g2"], w["w_mq"], w["g_mq"],
              mk, mv, w["w_mo"], seq=seq)
    y, last = _conv_ffn(x2, w["g3"], w["w_up"], w["w_conv"], w["b_conv"], w["w_down"], state,
                        seq=seq, ff_chunk=ff_chunk)
    return y, last[:, SUBLANES - (CONV_W - 1):]


def _prompt_layer(x, mem, w, bias):
    b, t, d = x.shape
    la = min(BAND_PAST, t)
    tm = PROMPT_TILE if t % PROMPT_TILE == 0 else ROW_TILE
    assert la <= tm and t % tm == 0
    qa, ka, va, qb, kb, vb, ka_tail, va_tail, kb_f, vb_f, logf = _in_proj(
        x.reshape(b * t, d), w["g1"], w["w_main"], w["w_f"], w["b_f"], w["gains"], w["seg"],
        tm=tm, tail_period=t // tm, tail_rows=la, time_minor=True)
    r3 = lambda a: a.reshape(b, t, WIDTH)
    oa = _band_prompt(r3(qa), r3(ka), r3(va), bias)
    logf = logf.reshape(N_HEADS, b, t).transpose(1, 0, 2)
    crow = _cumsum_lanes(logf.reshape(b * N_HEADS, t)).reshape(b, N_HEADS, t)
    logf = logf.transpose(0, 2, 1)
    ob = _fox_prompt(r3(qb), r3(kb), r3(vb), crow)
    mk_f, mv_f, mk, mv = _mem_kv(mem, w["g_mem"], w["w_mkv"], w["g_mk"])
    dff = w["w_down"].shape[0]
    y, conv = _post_attention(x.reshape(b * t, d), oa, ob, mk, mv, jnp.zeros((b, CONV_W - 1, dff), F32), w,
                              seq=t, ff_chunk=dff // 11)
    heads = lambda a, n: a.reshape(b, N_HEADS, HEAD_DIM, n).transpose(0, 3, 1, 2)
    mem_heads = lambda a: (a.reshape(b, mem.shape[1], 2, N_MEM_HEADS, LANES).transpose(0, 1, 3, 2, 4)
                           .reshape(b, mem.shape[1], N_MEM_HEADS, d // N_MEM_HEADS))
    return (y.reshape(b, t, d), heads(ka_tail, la), heads(va_tail, la), heads(kb_f, t), heads(vb_f, t), logf,
            mem_heads(mk_f), mem_heads(mv_f), conv)


def _sample_layer(x, cache_a_k, cache_a_v, cache_b_k, cache_b_v, cache_b_logf, cache_mem_k, cache_mem_v,
                  state_conv, w, bias):
    b, s, d = x.shape
    past = cache_b_k.shape[1]
    assert ROW_TILE % s == 0 and (b * s) % ROW_TILE == 0
    qa, ka, va, qb, kb, vb, ka_f, va_f, kb_f, vb_f, logf = _in_proj(
        x.reshape(b * s, d), w["g1"], w["w_main"], w["w_f"], w["b_f"], w["gains"], w["seg"], tm=ROW_TILE,
        tail_period=1, tail_rows=ROW_TILE, time_minor=False)
    r3 = lambda a: a.reshape(b, s, WIDTH)
    flat = lambda a: a.reshape(a.shape[0], a.shape[1], -1)
    time_minor = lambda a: a.transpose(0, 2, 3, 1)
    oa = _band_sample(r3(qa), r3(ka), r3(va), time_minor(cache_a_k), time_minor(cache_a_v), bias)
    logf = logf.reshape(N_HEADS, b, s).transpose(1, 0, 2)
    total = past + s
    all_logf = jnp.concatenate([cache_b_logf.astype(F32).transpose(0, 2, 1), logf], axis=2)
    all_logf = jnp.pad(all_logf, ((0, 0), (0, 0), (0, -total % LANES)))
    logf = logf.transpose(0, 2, 1)
    cum = _cumsum_lanes(all_logf.reshape(b * N_HEADS, -1)).reshape(b, N_HEADS, -1)
    crow_new = cum[:, :, past:total]
    pad_rows = lambda a: jnp.pad(a, ((0, 0), (0, -s % LANES), (0, 0)))
    ob = _fox_sample(r3(qb), pad_rows(r3(kb)), pad_rows(r3(vb)), time_minor(cache_b_k), time_minor(cache_b_v),
                     cum[:, :, :past], jnp.pad(crow_new, ((0, 0), (0, 0), (0, -s % LANES))))
    dff = w["w_down"].shape[0]
    y, conv = _post_attention(x.reshape(b * s, d), oa, ob, _cache_mem_rows(cache_mem_k.astype(F32)),
                              _cache_mem_rows(cache_mem_v.astype(F32)), state_conv.astype(F32), w,
                              seq=s, ff_chunk=dff // 11)
    heads = lambda a: a.reshape(b, s, N_HEADS, HEAD_DIM)
    return y.reshape(b, s, d), heads(ka_f), heads(va_f), heads(kb_f), heads(vb_f), logf, conv


def kernel(x_prompt, x_sample, cache_a_k, cache_a_v, cache_b_k, cache_b_v, cache_b_logf, cache_mem_k, cache_mem_v, state_conv, mem_prompt, w_in, b_f, g_qa, g_ka, rel_bias, g_qb, g_kb, w_o, g_norm1, g_norm2, g_mem, w_mq, w_mkv, g_mq, g_mk, w_mo, g_norm3, w_up, w_conv, b_conv, w_down):
    depth = w_in.shape[0]
    xp, xs = x_prompt, x_sample
    outs_p, outs_s = [], []
    for l in range(depth):
        w = _layer_weights(l, w_in, b_f, g_qa, g_ka, rel_bias, g_qb, g_kb, w_o, g_norm1, g_norm2, g_mem, w_mq,
                           w_mkv, g_mq, g_mk, w_mo, g_norm3, w_up, w_conv, b_conv, w_down)
        bias = _band_bias(w["rel"])
        xp, *rest_p = _prompt_layer(xp, mem_prompt, w, bias)
        xs, *rest_s = _sample_layer(xs, cache_a_k[l], cache_a_v[l], cache_b_k[l], cache_b_v[l], cache_b_logf[l],
                                    cache_mem_k[l], cache_mem_v[l], state_conv[l], w, bias)
        outs_p.append(rest_p)
        outs_s.append(rest_s)
    stack = lambda outs: [jnp.stack(col, axis=0) for col in zip(*outs)]
    return (xp, xs, *stack(outs_p), *stack(outs_s))
```

```python
import functools

import jax
import jax.numpy as jnp
from jax import lax
from jax.experimental import pallas as pl
from jax.experimental.pallas import tpu as pltpu

F32 = jnp.float32
BF16 = jnp.bfloat16
EPS = 1e-6
NEG_INF = float("-inf")

HEAD_DIM = 64
N_HEADS = 8
WIDTH = N_HEADS * HEAD_DIM
CHUNK = 64
N_PREV_CHUNKS = 8
BAND_PAST = N_PREV_CHUNKS * CHUNK
REL_CLIP = 128
N_MEM_HEADS = 4
CONV_W = 3

ROW_TILE = 512
FFN_TILE = 1024
MID_TILE = 1024
PROMPT_TILE = 1024
BAND_Q = 256
BAND_WIN = BAND_PAST + BAND_Q
FOX_BLOCK = 512
FOX_KEY_BLOCKS = 2
FOX_CACHE_BLOCK = 2048
FOX_BAND = 64
MXU_COLS = 256
LANES = 128
SUBLANES = 8
BF16_ROWS = 16
VMEM_LIMIT = 56 * 1024 * 1024
LOG2E = 1.4426950408889634


def _dot(a, b):
    return jnp.dot(a, b, preferred_element_type=F32)


def _dot_nt(a, b):
    return lax.dot_general(a, b, (((1,), (1,)), ((), ())), preferred_element_type=F32)


def _rms(x, g):
    return x * lax.rsqrt(jnp.mean(x * x, axis=-1, keepdims=True) + EPS) * g


def _split3(x):
    hi = x.astype(BF16)
    r1 = x - hi.astype(F32)
    mid = r1.astype(BF16)
    lo = (r1 - mid.astype(F32)).astype(BF16)
    return hi, mid, lo


def _store_heads(ref, x):
    rows = x.shape[0]
    for h in range(N_HEADS):
        ref[pl.ds(h, rows, stride=N_HEADS), :] = x[:, h * HEAD_DIM:(h + 1) * HEAD_DIM]


def _resident(a):
    return pl.BlockSpec(a.shape, lambda i: (0,) * a.ndim, pipeline_mode=pl.Buffered(1))


def _params(*sem):
    return pltpu.CompilerParams(dimension_semantics=sem, vmem_limit_bytes=VMEM_LIMIT)


def _inproj_kernel(x_ref, g1_ref, w_ref, wf_ref, bf_ref, gains_ref, seg_ref,
                   qa_ref, ka_ref, va_ref, qb_ref, kb_ref, vb_ref,
                   kaf_ref, vaf_ref, kbf_ref, vbf_ref, logf_ref, *, tail_period, tail_rows, time_minor):
    h = _rms(x_ref[...], g1_ref[...]).astype(BF16)
    seg = seg_ref[...]

    def store_f32(ref, x):
        if time_minor:
            ref[...] = x.T
        else:
            _store_heads(ref, x)

    def proj(g):
        return _dot(h, w_ref[:, g * WIDTH:(g + 1) * WIDTH])

    def head_norm(p, row):
        p2 = (p * p).astype(BF16)
        ss = jnp.concatenate([_dot(p2[:, :MXU_COLS], seg), _dot(p2[:, MXU_COLS:], seg)], axis=1)
        return p * lax.rsqrt(ss * (1.0 / HEAD_DIM) + EPS) * gains_ref[row:row + 1, :]

    is_tail = (pl.program_id(0) % tail_period) == tail_period - 1
    scale = HEAD_DIM ** -0.5 * LOG2E

    kb = head_norm(proj(4), 3)
    kb_ref[...] = kb.astype(BF16)
    store_f32(kbf_ref.at[0], kb)
    vb = proj(5)
    vb_ref[...] = vb.astype(BF16)
    store_f32(vbf_ref.at[0], vb)
    z = _dot_nt(wf_ref[...], h)[:N_HEADS] + bf_ref[...]
    logf_ref[...] = jnp.minimum(z, 0.0) - jnp.log1p(jnp.exp(-jnp.abs(z)))
    qb_ref[...] = (head_norm(proj(3), 2) * scale).astype(BF16)
    qa_ref[...] = (head_norm(proj(0), 0) * scale).astype(BF16)
    ka = head_norm(proj(1), 1)
    ka_ref[...] = ka.astype(BF16)
    va = proj(2)
    va_ref[...] = va.astype(BF16)

    @pl.when(is_tail)
    def _():
        store_f32(kaf_ref.at[0], ka[-tail_rows:])
        store_f32(vaf_ref.at[0], va[-tail_rows:])


def _in_proj(x2d, g1, w_main, w_f, b_f, gains, seg, *, tm, tail_period, tail_rows, time_minor):
    n, d = x2d.shape
    assert n % tm == 0 and (n // tm) % tail_period == 0 and (tail_rows == tm or (time_minor and tail_rows <= tm))
    n_tail = n // tm // tail_period
    row_bf = jax.ShapeDtypeStruct((n, WIDTH), BF16)
    rows = pl.BlockSpec((tm, WIDTH), lambda i: (i, 0))
    if time_minor:
        head_f = jax.ShapeDtypeStruct((n_tail, WIDTH, tm * tail_period), F32)
        tail_f = jax.ShapeDtypeStruct((n_tail, WIDTH, tail_rows), F32)
        head_rows = pl.BlockSpec((1, WIDTH, tm), lambda i: (i // tail_period, 0, i % tail_period))
        tail = pl.BlockSpec((1, WIDTH, tail_rows), lambda i: (i // tail_period, 0, 0))
    else:
        head_f = jax.ShapeDtypeStruct((n // tm, tm * N_HEADS, HEAD_DIM), F32)
        tail_f = jax.ShapeDtypeStruct((n_tail, tm * N_HEADS, HEAD_DIM), F32)
        head_rows = pl.BlockSpec((1, tm * N_HEADS, HEAD_DIM), lambda i: (i, 0, 0))
        tail = pl.BlockSpec((1, tm * N_HEADS, HEAD_DIM), lambda i: (i // tail_period, 0, 0))
    full = _resident
    return pl.pallas_call(
        functools.partial(_inproj_kernel, tail_period=tail_period, tail_rows=tail_rows, time_minor=time_minor),
        grid=(n // tm,),
        in_specs=[pl.BlockSpec((tm, d), lambda i: (i, 0)), full(g1), full(w_main), full(w_f),
                  full(b_f), full(gains), full(seg)],
        out_specs=[rows] * 6 + [tail, tail, head_rows, head_rows, pl.BlockSpec((N_HEADS, tm), lambda i: (0, i))],
        out_shape=[row_bf] * 6 + [tail_f, tail_f, head_f, head_f, jax.ShapeDtypeStruct((N_HEADS, n), F32)],
        compiler_params=_params("arbitrary"),
        name="in_proj",
    )(x2d, g1, w_main, w_f, b_f, gains, seg)


def _store_mem_rows(ref, x, hh):
    n = x.shape[0]
    for lg in range(2):
        ref[pl.ds(lg * N_MEM_HEADS + hh, n, stride=2 * N_MEM_HEADS), :] = x[:, lg * LANES:(lg + 1) * LANES]


def _memkv_kernel(mem_ref, g_ref, w_ref, gk_ref, kf_ref, vf_ref, kb_ref, vb_ref):
    d = mem_ref.shape[-1]
    hd = d // N_MEM_HEADS
    h = _rms(mem_ref[0], g_ref[...]).astype(BF16)
    kv = _dot(h, w_ref[...])
    for hh in range(N_MEM_HEADS):
        cs = slice(hh * hd, (hh + 1) * hd)
        k = _rms(kv[:, cs], gk_ref[...])
        v = kv[:, d + hh * hd:d + (hh + 1) * hd]
        _store_mem_rows(kf_ref.at[0], k, hh)
        _store_mem_rows(vf_ref.at[0], v, hh)
        kb_ref[0, :, cs] = k.astype(BF16)
        vb_ref[0, :, cs] = v.astype(BF16)


def _mem_kv(mem, g_mem, w_mkv, g_mk):
    b, n, d = mem.shape
    assert d == N_MEM_HEADS * 2 * LANES
    blk = pl.BlockSpec((1, n, d), lambda i: (i, 0, 0))
    blk_f = pl.BlockSpec((1, n * 2 * N_MEM_HEADS, LANES), lambda i: (i, 0, 0))
    full = _resident
    f = jax.ShapeDtypeStruct((b, n * 2 * N_MEM_HEADS, LANES), F32)
    h = jax.ShapeDtypeStruct((b, n, d), BF16)
    return pl.pallas_call(
        _memkv_kernel, grid=(b,),
        in_specs=[blk, full(g_mem), full(w_mkv), full(g_mk)],
        out_specs=[blk_f, blk_f, blk, blk], out_shape=[f, f, h, h],
        compiler_params=_params("arbitrary"), name="mem_kv",
    )(mem, g_mem, w_mkv, g_mk)


def _cumsum_kernel(x_ref, o_ref):
    r, t = x_ref.shape
    k = lax.broadcasted_iota(jnp.int32, (LANES, LANES), 0)
    j = lax.broadcasted_iota(jnp.int32, (LANES, LANES), 1)
    tri = jnp.where(k <= j, 1.0, 0.0).astype(BF16)
    carry = jnp.zeros((r, 1), F32)
    for blk in range(t // LANES):
        hi, mid, lo = _split3(x_ref[:, blk * LANES:(blk + 1) * LANES])
        cs = _dot(hi, tri) + _dot(mid, tri) + _dot(lo, tri)
        o_ref[:, blk * LANES:(blk + 1) * LANES] = (cs + carry) * LOG2E
        carry = carry + cs[:, LANES - 1:LANES]


def _cumsum_lanes(x):
    assert x.shape[1] % LANES == 0
    return pl.pallas_call(_cumsum_kernel, out_shape=jax.ShapeDtypeStruct(x.shape, F32), name="cumsum")(x)


def _band_bias_kernel(tbl_ref, o_ref):
    nrel = tbl_ref.shape[1]
    qblk, win = o_ref.shape[1:]
    wrap = qblk + win
    jp = lax.broadcasted_iota(jnp.int32, (nrel, wrap), 1)
    r = lax.broadcasted_iota(jnp.int32, (nrel, wrap), 0)
    d = jnp.where(jp < win, jp, jp - wrap)
    rel = jnp.clip(BAND_PAST - d, -REL_CLIP, REL_CLIP) + REL_CLIP
    onehot = jnp.where(rel == r, 1.0, 0.0).astype(BF16)
    hi, mid, lo = _split3(tbl_ref[...])
    base = (_dot(hi, onehot) + _dot(mid, onehot) + _dot(lo, onehot)) * LOG2E
    qc = lax.broadcasted_iota(jnp.int32, (qblk, win), 0) // CHUNK
    kc = lax.broadcasted_iota(jnp.int32, (qblk, win), 1) // CHUNK
    inband = (kc >= qc) & (kc <= qc + N_PREV_CHUNKS)
    for h in range(N_HEADS):
        row = jnp.broadcast_to(base[h:h + 1, :], (qblk, wrap))
        toeplitz = pltpu.roll(row, 0, 1, stride=1, stride_axis=0)
        o_ref[h] = jnp.where(inband, toeplitz[:, :win], NEG_INF)


def _band_bias(rel_table):
    nrel = 2 * REL_CLIP + 1
    pad = -nrel % LANES
    tbl = jnp.pad(rel_table.astype(F32), ((0, 0), (0, pad)))
    return pl.pallas_call(
        _band_bias_kernel, out_shape=jax.ShapeDtypeStruct((N_HEADS, BAND_Q, BAND_WIN), F32),
        name="band_bias")(tbl)


def _zero_after(x):
    bits = pltpu.bitcast(x[:SUBLANES, :LANES], jnp.uint32)
    return ((bits >> 16) >> 16)[:1].astype(F32)


def _softmax_pv(s, vs, transposed=None):
    transposed = transposed or (False,) * len(vs)
    m = jnp.max(s, axis=-1, keepdims=True)
    e = jnp.exp2(s - m)
    l = jnp.sum(e, axis=-1, keepdims=True)
    eb = e.astype(BF16)
    o, lo = None, 0
    for v, vt in zip(vs, transposed):
        keys = v.shape[1] if vt else v.shape[0]
        part = _dot_nt(eb[:, lo:lo + keys], v) if vt else _dot(eb[:, lo:lo + keys], v)
        o = part if o is None else o + part
        lo += keys
    return o * (1.0 / l)


def _band_prompt_kernel(q_ref, k0_ref, k1_ref, k2_ref, v0_ref, v1_ref, v2_ref, bias_ref, o_ref, s_sc, m_sc, qm_sc):
    i = pl.program_id(1)
    tq = q_ref.shape[1]
    k_refs, v_refs = (k0_ref, k1_ref, k2_ref), (v0_ref, v1_ref, v2_ref)
    bands = [slice(r, r + CHUNK) for r in range(0, tq, CHUNK)]
    n_groups = BAND_WIN // LANES
    per_block = tq // LANES
    group = lambda g: slice(g * LANES, (g + 1) * LANES)
    visible = lambda c: [g for g in range(n_groups) if 2 * g + 1 >= c and 2 * g <= c + N_PREV_CHUNKS]
    low = _low_half(tq)
    for h in range(N_HEADS):
        q_pair = _pair(q_ref, h)
        qm_sc[h] = jnp.where(low if h % 2 == 0 else ~low, q_pair, jnp.zeros_like(q_pair))

    def logits(early):
        for h in range(N_HEADS):
            qm = qm_sc[h]
            parts = [_dot_nt(qm, _pair(k_ref, h)) for k_ref in k_refs]
            for c, rows in enumerate(bands):
                mx = None
                for g in visible(c):
                    p, gp = divmod(g, per_block)
                    sg = parts[p][rows, group(gp)] + bias_ref[h, rows, group(g)]
                    if early and p < 2:
                        sg = jnp.where(i + p >= 2, sg, NEG_INF)
                    s_sc[h, rows, group(g)] = sg
                    mx = sg if mx is None else jnp.maximum(mx, sg)
                m_sc[h, rows] = jnp.broadcast_to(jnp.max(mx, axis=-1, keepdims=True), (CHUNK, LANES))

    values = functools.partial(_band_values, v_refs, o_ref, s_sc, m_sc, bands, visible)

    @pl.when(i >= 2)
    def _():
        logits(False)

    @pl.when(i < 2)
    def _():
        logits(True)

    values()


def _band_values(v_refs, o_ref, s_sc, m_sc, bands, visible):
    tq = o_ref.shape[1]
    n_groups = s_sc.shape[2] // LANES
    group = lambda g: slice(g * LANES, (g + 1) * LANES)
    gate = previous = None
    for h in range(N_HEADS):
        p_bands, invs = [], []
        for c, rows in enumerate(bands):
            m = m_sc[h, rows]
            if gate is not None:
                m = m + gate
            rs, ps = None, []
            for g in range(n_groups):
                if g in visible(c):
                    e = jnp.exp2(s_sc[h, rows, group(g)] - m)
                    ps.append(e.astype(BF16))
                    rs = e if rs is None else rs + e
                else:
                    ps.append(jnp.zeros((CHUNK, LANES), BF16))
            p_bands.append(jnp.concatenate(ps, axis=1))
            invs.append(1.0 / jnp.sum(rs, axis=-1, keepdims=True))
        p_all = jnp.concatenate(p_bands, axis=0)
        o = None
        for p, v_ref in enumerate(v_refs):
            part = _dot(p_all[:, p * tq:(p + 1) * tq], _pair(v_ref, h))
            o = part if o is None else o + part
        gate, previous = previous, _zero_after(o)
        half = slice((h % 2) * HEAD_DIM, (h % 2 + 1) * HEAD_DIM)
        o_ref[0, :, h * HEAD_DIM:(h + 1) * HEAD_DIM] = (o * jnp.concatenate(invs, axis=0))[:, half].astype(BF16)


def _band_prompt(qa, ka, va, bias):
    b, t, _ = qa.shape
    tq = BAND_Q
    assert t % tq == 0
    blk = lambda back: pl.BlockSpec((1, tq, WIDTH), lambda bi, i: (bi, jnp.maximum(i - back, 0), 0))
    return pl.pallas_call(
        _band_prompt_kernel, grid=(b, t // tq),
        in_specs=[blk(0), blk(2), blk(1), blk(0), blk(2), blk(1), blk(0),
                  pl.BlockSpec(bias.shape, lambda bi, i: (0, 0, 0))],
        out_specs=blk(0), out_shape=jax.ShapeDtypeStruct((b, t, WIDTH), BF16),
        scratch_shapes=[pltpu.VMEM((N_HEADS, tq, BAND_WIN), F32), pltpu.VMEM((N_HEADS, tq, LANES), F32),
                        pltpu.VMEM((N_HEADS, tq, LANES), BF16)],
        compiler_params=_params("arbitrary", "arbitrary"), name="band_prompt",
    )(qa, ka, ka, ka, va, va, va, bias)


def _band_sample_kernel(q_ref, kc_ref, vc_ref, kn_ref, vn_ref, bias_ref, o_ref):
    for h in range(N_HEADS):
        hs = slice(h * HEAD_DIM, (h + 1) * HEAD_DIM)
        q = q_ref[0, :, hs]
        s = jnp.concatenate([_dot(q, kc_ref[0, h].astype(BF16)), _dot_nt(q, kn_ref[0, :, hs])], axis=1)
        s = s + bias_ref[h, :CHUNK, :BAND_PAST + CHUNK]
        o = _softmax_pv(s, [vc_ref[0, h].astype(BF16), vn_ref[0, :, hs]], transposed=(True, False))
        o_ref[0, :, hs] = o.astype(BF16)


def _band_sample(qa, ka, va, cache_k, cache_v, bias):
    b, s, _ = qa.shape
    la = cache_k.shape[3]
    assert s == CHUNK and la == BAND_PAST
    new = pl.BlockSpec((1, s, WIDTH), lambda bi: (bi, 0, 0))
    old = pl.BlockSpec((1, N_HEADS, HEAD_DIM, la), lambda bi: (bi, 0, 0, 0))
    return pl.pallas_call(
        _band_sample_kernel, grid=(b,),
        in_specs=[new, old, old, new, new, pl.BlockSpec(bias.shape, lambda bi: (0, 0, 0))],
        out_specs=new, out_shape=jax.ShapeDtypeStruct((b, s, WIDTH), BF16),
        compiler_params=_params("arbitrary"), name="band_sample",
    )(qa, cache_k, cache_v, ka, va, bias)


def _fox_scratch(tq, tk):
    rows = lambda w: pltpu.VMEM((N_HEADS, tq, w), F32)
    return [rows(tk), rows(LANES), rows(LANES), rows(LANES), rows(LANES),
            pltpu.VMEM((N_HEADS, tq, LANES), BF16), pltpu.VMEM((N_HEADS // 2, tq, LANES), F32)]


def _low_half(rows):
    return lax.broadcasted_iota(jnp.int32, (rows, LANES), 1) < HEAD_DIM


def _fox_init(q_ref, cq_ref, sc):
    _, cq_sc, m_sc, _, l_sc, qm_sc, acc_sc = sc
    tq = cq_sc.shape[1]
    low = _low_half(tq)
    for h in range(N_HEADS):
        cq_sc[h] = jnp.broadcast_to(cq_ref[0, h:h + 1, :], (LANES, cq_ref.shape[2])).T[:tq]
        q_pair = q_ref[0, :, (h // 2) * LANES:(h // 2 + 1) * LANES]
        qm_sc[h] = jnp.where(low if h % 2 == 0 else ~low, q_pair, jnp.zeros_like(q_pair))
    m_sc[...] = jnp.full(m_sc.shape, NEG_INF, F32)
    l_sc[...] = jnp.zeros(l_sc.shape, F32)
    acc_sc[...] = jnp.zeros(acc_sc.shape, F32)


def _fox_block(qk, pv, crow_ref, tk, causal, sc):
    s_sc, cq_sc, m_sc, a_sc, l_sc, qm_sc, acc_sc = sc
    tq = s_sc.shape[1]
    band = min(tq, FOX_BAND)
    bands = [slice(r, r + band) for r in range(0, tq, band)]
    n_groups = tk // LANES
    group = lambda g: slice(g * LANES, (g + 1) * LANES)

    def visible(rows):
        if not causal:
            return [(g, False) for g in range(n_groups)]
        return [(g, (g + 1) * LANES - 1 > rows.start) for g in range(n_groups) if g * LANES < rows.stop]

    if causal:
        ahead = (lax.broadcasted_iota(jnp.int32, (band, LANES), 0)
                 - lax.broadcasted_iota(jnp.int32, (band, LANES), 1))
    for h in range(N_HEADS):
        s = qk(h, qm_sc[h])
        for rows in bands:
            cq = cq_sc[h, rows]
            mx = None
            for g, masked in visible(rows):
                sg = s[rows, group(g)] + (cq - crow_ref[0, h:h + 1, group(g)])
                if masked:
                    sg = jnp.where(ahead >= g * LANES - rows.start, sg, NEG_INF)
                s_sc[h, rows, group(g)] = sg
                mx = sg if mx is None else jnp.maximum(mx, sg)
            m_prev = m_sc[h, rows]
            m_new = jnp.maximum(m_prev, jnp.max(mx, axis=-1, keepdims=True))
            a_sc[h, rows] = jnp.exp2(m_prev - m_new)
            m_sc[h, rows] = m_new
    for h in range(N_HEADS):
        p_bands = []
        for rows in bands:
            seen = visible(rows)
            m_new = m_sc[h, rows]
            rs, ps = None, []
            for g, _ in seen:
                p = jnp.exp2(s_sc[h, rows, group(g)] - m_new)
                ps.append(p.astype(BF16))
                rs = p if rs is None else rs + p
            l_sc[h, rows] = a_sc[h, rows] * l_sc[h, rows] + rs
            ps += [jnp.zeros((band, LANES), BF16)] * (n_groups - len(seen))
            p_bands.append(jnp.concatenate(ps, axis=1))
        half = slice((h % 2) * HEAD_DIM, (h % 2 + 1) * HEAD_DIM)
        new = a_sc[h] * acc_sc[h // 2] + pv(h, jnp.concatenate(p_bands, axis=0))
        acc_sc[h // 2, :, half] = new[:, half]


def _fox_finish(o_ref, sc):
    l_sc, acc_sc = sc[4], sc[6]
    low = _low_half(acc_sc.shape[1])
    total = lambda h: jnp.sum(l_sc[h], axis=-1, keepdims=True)
    for g in range(N_HEADS // 2):
        inv = jnp.where(low, 1.0 / total(2 * g), 1.0 / total(2 * g + 1))
        o_ref[0, :, g * LANES:(g + 1) * LANES] = (acc_sc[g] * inv).astype(BF16)


def _pair(ref, h):
    return ref[0, :, (h // 2) * LANES:(h // 2 + 1) * LANES]


def _row_major_products(k_ref, v_ref):
    return (lambda h, qm: _dot_nt(qm, _pair(k_ref, h))), (lambda h, p: _dot(p, _pair(v_ref, h)))


def _fox_prompt_kernel(qi_ref, kj_ref, q_ref, k_ref, v_ref, crow_ref, cq_ref, o_ref, *sc):
    p = pl.program_id(1)
    qi, kj = qi_ref[p], kj_ref[p]
    tq = q_ref.shape[1]

    for i in range(FOX_KEY_BLOCKS):
        rows = slice(i * tq, (i + 1) * tq)
        qk, pv = _row_major_products(k_ref.at[:, rows, :], v_ref.at[:, rows, :])
        crow = crow_ref.at[:, :, rows]
        blk = FOX_KEY_BLOCKS * kj + i
        for first in ((True, False) if i == 0 else (False,)):
            started = True if i else ((kj == 0) if first else (kj > 0))

            @pl.when(started & (blk < qi))
            def _():
                if first:
                    _fox_init(q_ref, cq_ref, sc)
                _fox_block(qk, pv, crow, tq, False, sc)

            @pl.when(started & (blk == qi))
            def _():
                if first:
                    _fox_init(q_ref, cq_ref, sc)
                _fox_block(qk, pv, crow, tq, True, sc)
                _fox_finish(o_ref, sc)


def _fox_prompt(qb, kb, vb, crow):
    b, t, _ = qb.shape
    tq = FOX_BLOCK
    tk = FOX_KEY_BLOCKS * tq
    assert t % tk == 0
    pairs = [(i, j) for i in range(t // tq) for j in range(i // FOX_KEY_BLOCKS + 1)]
    qi = jnp.array([p[0] for p in pairs], jnp.int32)
    kj = jnp.array([p[1] for p in pairs], jnp.int32)
    qblk = pl.BlockSpec((1, tq, WIDTH), lambda bi, p, qi, kj: (bi, qi[p], 0))
    kblk = pl.BlockSpec((1, tk, WIDTH), lambda bi, p, qi, kj: (bi, kj[p], 0))
    grid_spec = pltpu.PrefetchScalarGridSpec(
        num_scalar_prefetch=2, grid=(b, len(pairs)),
        in_specs=[qblk, kblk, kblk,
                  pl.BlockSpec((1, N_HEADS, tk), lambda bi, p, qi, kj: (bi, 0, kj[p])),
                  pl.BlockSpec((1, N_HEADS, tq), lambda bi, p, qi, kj: (bi, 0, qi[p]))],
        out_specs=qblk, scratch_shapes=_fox_scratch(tq, tq))
    return pl.pallas_call(
        _fox_prompt_kernel, grid_spec=grid_spec, out_shape=jax.ShapeDtypeStruct((b, t, WIDTH), BF16),
        compiler_params=_params("arbitrary", "arbitrary"), name="fox_prompt",
    )(qi, kj, qb, kb, vb, crow, crow)


def _fox_sample_kernel(q_ref, kc_ref, vc_ref, kn_ref, vn_ref, crow_ref, crow_new_ref, o_ref, *sc):
    j = pl.program_id(1)
    last = pl.num_programs(1) - 1

    @pl.when(j == 0)
    def _():
        _fox_init(q_ref, crow_new_ref, sc)

    def cached_pair(ref, h):
        g = h // 2
        return ref[0, 2 * g:2 * g + 2].astype(BF16).reshape(2 * HEAD_DIM, ref.shape[3])

    _fox_block(lambda h, qm: _dot(qm, cached_pair(kc_ref, h)),
               lambda h, p: _dot_nt(p, cached_pair(vc_ref, h)), crow_ref, kc_ref.shape[3], False, sc)

    @pl.when(j == last)
    def _():
        qk, pv = _row_major_products(kn_ref, vn_ref)
        _fox_block(qk, pv, crow_new_ref, kn_ref.shape[1], True, sc)
        _fox_finish(o_ref, sc)


def _fox_sample(qb, kb, vb, cache_k, cache_v, crow, crow_new):
    b, s, _ = qb.shape
    past = cache_k.shape[3]
    tk, tn = min(FOX_CACHE_BLOCK, past), kb.shape[1]
    assert past % tk == 0 and tn % LANES == 0 and tn >= s
    nk = past // tk
    new = lambda r: pl.BlockSpec((1, r, WIDTH), lambda bi, j: (bi, 0, 0))
    old = pl.BlockSpec((1, N_HEADS, HEAD_DIM, tk), lambda bi, j: (bi, 0, 0, j))
    return pl.pallas_call(
        _fox_sample_kernel, grid=(b, nk),
        in_specs=[new(s), old, old, new(tn), new(tn),
                  pl.BlockSpec((1, N_HEADS, tk), lambda bi, j: (bi, 0, j)),
                  pl.BlockSpec((1, N_HEADS, tn), lambda bi, j: (bi, 0, 0))],
        out_specs=new(s), out_shape=jax.ShapeDtypeStruct((b, s, WIDTH), BF16),
        scratch_shapes=_fox_scratch(s, tk),
        compiler_params=_params("arbitrary", "arbitrary"), name="fox_sample",
    )(qb, cache_k, cache_v, kb, vb, crow, crow_new)


def _mid_kernel(x_ref, oa_ref, ob_ref, wo_ref, g2_ref, wq_ref, gq_ref, mk_ref, mv_ref, wmo_ref, o_ref,
                om_sc, *, seg):
    tm, d = x_ref.shape
    hd = d // N_MEM_HEADS
    x1 = x_ref[...] + _dot(oa_ref[...], wo_ref[:WIDTH, :]) + _dot(ob_ref[...], wo_ref[WIDTH:, :])
    h2 = _rms(x1, g2_ref[...]).astype(BF16)
    qm = _dot(h2, wq_ref[...])
    scale = hd ** -0.5 * LOG2E
    for hh in range(N_MEM_HEADS):
        cs = slice(hh * hd, (hh + 1) * hd)
        q = (_rms(qm[:, cs], gq_ref[...]) * scale).astype(BF16)
        def mem_head(ref, sg):
            if ref.shape[2] == d:
                return ref[sg, :, cs]
            rows = lambda lg: ref[sg, pl.ds(lg * N_MEM_HEADS + hh, ref.shape[1] // (2 * N_MEM_HEADS),
                                            stride=2 * N_MEM_HEADS), :]
            return jnp.concatenate([rows(0), rows(1)], axis=1).astype(BF16)

        for sg in range(tm // seg):
            rs = slice(sg * seg, (sg + 1) * seg)
            s = _dot_nt(q[rs], mem_head(mk_ref, sg))
            om_sc[rs, cs] = _softmax_pv(s, [mem_head(mv_ref, sg)]).astype(BF16)
    o_ref[...] = x1 + _dot(om_sc[...], wmo_ref[...])


def _cache_mem_rows(x):
    b, n, nh, hd = x.shape
    assert nh == N_MEM_HEADS and hd == 2 * LANES
    return x.reshape(b, n, nh, 2, LANES).transpose(0, 1, 3, 2, 4).reshape(b, n * 2 * nh, LANES)


def _mid(x2d, oa, ob, w_o, g2, w_mq, g_mq, mk, mv, w_mo, *, seq):
    n, d = x2d.shape
    tm = MID_TILE if seq % MID_TILE == 0 else ROW_TILE
    seg = min(seq, tm)
    tiles_per_batch = seq // seg
    nb = tm // seg
    assert n % tm == 0 and seq % seg == 0 and tm % seg == 0
    rows = lambda w: pl.BlockSpec((tm, w), lambda i: (i, 0))
    full = _resident
    mem = pl.BlockSpec((nb,) + mk.shape[1:], lambda i: (i // tiles_per_batch, 0, 0))
    return pl.pallas_call(
        functools.partial(_mid_kernel, seg=seg), grid=(n // tm,),
        in_specs=[rows(d), rows(WIDTH), rows(WIDTH), full(w_o), full(g2), full(w_mq), full(g_mq), mem, mem,
                  full(w_mo)],
        out_specs=rows(d), out_shape=jax.ShapeDtypeStruct((n, d), F32),
        scratch_shapes=[pltpu.VMEM((tm, d), BF16)],
        compiler_params=_params("arbitrary"), name="mid",
    )(x2d, oa, ob, w_o, g2, w_mq, g_mq, mk, mv, w_mo)


def _ffn_kernel(x_ref, g3_ref, wup_ref, wc_ref, bc_ref, wd_ref, st_ref, o_ref, last_ref, carry_sc, y_sc,
                *, seg, tiles_per_batch, ff_chunk):
    tm, d = x_ref.shape
    dff = wd_ref.shape[0]
    i = pl.program_id(0)
    h3 = _rms(x_ref[...], g3_ref[...]).astype(BF16)
    if tiles_per_batch > 1:
        @pl.when(i == 0)
        def _():
            carry_sc[...] = jnp.zeros(carry_sc.shape, F32)
    rowid = lax.broadcasted_iota(jnp.int32, (SUBLANES, 1), 0)
    first_tile = (i % tiles_per_batch) == 0
    for c in range(dff // ff_chunk):
        cs = slice(c * ff_chunk, (c + 1) * ff_chunk)
        gate = _dot(h3, wup_ref[:, cs])
        val = _dot(h3, wup_ref[:, dff + c * ff_chunk:dff + (c + 1) * ff_chunk])
        w0, w1, w2 = wc_ref[0:1, cs], wc_ref[1:2, cs], wc_ref[2:3, cs]
        for sg in range(tm // seg):
            rows = slice(sg * seg, (sg + 1) * seg)
            g = gate[rows]
            if tiles_per_batch == 1:
                prev = st_ref[sg, :, cs]
            else:
                prev = jnp.where(first_tile, st_ref[0, :, cs], carry_sc[SUBLANES - 2:, cs])
            p2, p1 = prev[0:1], prev[1:2]
            s1 = pltpu.roll(g, 1, 0)
            s2 = pltpu.roll(g, 2, 0)
            top1 = jnp.where(rowid == 0, p1, s1[:SUBLANES])
            top2 = jnp.where(rowid == 0, p2, jnp.where(rowid == 1, p1, s2[:SUBLANES]))
            s1 = jnp.concatenate([top1, s1[SUBLANES:]], axis=0)
            s2 = jnp.concatenate([top2, s2[SUBLANES:]], axis=0)
            conv = s2 * w0 + s1 * w1 + g * w2 + bc_ref[:, cs]
            y_sc[rows, cs] = (conv * jax.nn.sigmoid(conv) * val[rows]).astype(BF16)
            last_ref[sg, :, cs] = g[seg - SUBLANES:]
        if tiles_per_batch > 1:
            carry_sc[:, cs] = gate[tm - SUBLANES:]
    o_ref[...] = x_ref[...] + _dot(y_sc[...], wd_ref[...])


def _conv_ffn(x2d, g3, w_up, w_conv, b_conv, w_down, state, *, seq, ff_chunk):
    n, d = x2d.shape
    dff = w_down.shape[0]
    tm = min(FFN_TILE, n)
    seg = min(seq, tm)
    tiles_per_batch = seq // seg
    nb = tm // seg
    assert n % tm == 0 and seq % seg == 0 and tm % seg == 0 and dff % ff_chunk == 0 and seg >= SUBLANES
    rows = pl.BlockSpec((tm, d), lambda i: (i, 0))
    full = _resident
    per_batch = lambda r: pl.BlockSpec((nb, r, dff), lambda i: (i // tiles_per_batch, 0, 0))
    return pl.pallas_call(
        functools.partial(_ffn_kernel, seg=seg, tiles_per_batch=tiles_per_batch, ff_chunk=ff_chunk),
        grid=(n // tm,),
        in_specs=[rows, full(g3), full(w_up), full(w_conv), full(b_conv), full(w_down), per_batch(CONV_W - 1)],
        out_specs=[rows, per_batch(SUBLANES)],
        out_shape=[jax.ShapeDtypeStruct((n, d), F32), jax.ShapeDtypeStruct((n // seq, SUBLANES, dff), F32)],
        scratch_shapes=[pltpu.VMEM((SUBLANES, dff), F32), pltpu.VMEM((tm, dff), BF16)],
        compiler_params=_params("arbitrary"), name="conv_ffn",
    )(x2d, g3, w_up, w_conv, b_conv, w_down, state)


def _row(v):
    return v.astype(F32).reshape(1, -1)


def _layer_weights(l, w_in, b_f, g_qa, g_ka, rel_bias, g_qb, g_kb, w_o, g_norm1, g_norm2, g_mem, w_mq, w_mkv,
                   g_mq, g_mk, w_mo, g_norm3, w_up, w_conv, b_conv, w_down):
    n_main = 6 * WIDTH
    tile = lambda g: jnp.tile(g[l].astype(F32), N_HEADS)
    blk = jnp.arange(MXU_COLS) // HEAD_DIM
    return dict(
        w_main=w_in[l][:, :n_main].astype(BF16),
        w_f=jnp.pad(w_in[l][:, n_main:].T, ((0, BF16_ROWS - N_HEADS), (0, 0))).astype(BF16),
        b_f=b_f[l].astype(F32).reshape(N_HEADS, 1),
        gains=jnp.stack([tile(g_qa), tile(g_ka), tile(g_qb), tile(g_kb)]),
        seg=(blk[:, None] == blk[None, :]).astype(BF16),
        rel=rel_bias[l],
        w_o=w_o[l].astype(BF16), g1=_row(g_norm1[l]), g2=_row(g_norm2[l]), g3=_row(g_norm3[l]),
        g_mem=_row(g_mem[l]), w_mq=w_mq[l].astype(BF16), w_mkv=w_mkv[l].astype(BF16),
        g_mq=_row(g_mq[l]), g_mk=_row(g_mk[l]), w_mo=w_mo[l].astype(BF16),
        w_up=w_up[l].astype(BF16), w_conv=w_conv[l].astype(F32), b_conv=_row(b_conv[l]),
        w_down=w_down[l].astype(BF16))


def _post_attention(x2d, oa, ob, mk, mv, state, w, *, seq, ff_chunk):
    n, d = x2d.shape
    x2 = _mid(x2d, oa.reshape(n, WIDTH), ob.reshape(n, WIDTH), w["w_o"], w["g2"], w["w_mq"], w["g_mq"],
              mk, mv, w["w_mo"], seq=seq)
    y, last = _conv_ffn(x2, w["g3"], w["w_up"], w["w_conv"], w["b_conv"], w["w_down"], state,
                        seq=seq, ff_chunk=ff_chunk)
    return y, last[:, SUBLANES - (CONV_W - 1):]


def _prompt_layer(x, mem, w, bias):
    b, t, d = x.shape
    la = min(BAND_PAST, t)
    tm = PROMPT_TILE if t % PROMPT_TILE == 0 else ROW_TILE
    assert la <= tm and t % tm == 0
    qa, ka, va, qb, kb, vb, ka_tail, va_tail, kb_f, vb_f, logf = _in_proj(
        x.reshape(b * t, d), w["g1"], w["w_main"], w["w_f"], w["b_f"], w["gains"], w["seg"],
        tm=tm, tail_period=t // tm, tail_rows=la, time_minor=True)
    r3 = lambda a: a.reshape(b, t, WIDTH)
    oa = _band_prompt(r3(qa), r3(ka), r3(va), bias)
    logf = logf.reshape(N_HEADS, b, t).transpose(1, 0, 2)
    crow = _cumsum_lanes(logf.reshape(b * N_HEADS, t)).reshape(b, N_HEADS, t)
    logf = logf.transpose(0, 2, 1)
    ob = _fox_prompt(r3(qb), r3(kb), r3(vb), crow)
    mk_f, mv_f, mk, mv = _mem_kv(mem, w["g_mem"], w["w_mkv"], w["g_mk"])
    dff = w["w_down"].shape[0]
    y, conv = _post_attention(x.reshape(b * t, d), oa, ob, mk, mv, jnp.zeros((b, CONV_W - 1, dff), F32), w,
                              seq=t, ff_chunk=dff // 11)
    heads = lambda a, n: a.reshape(b, N_HEADS, HEAD_DIM, n).transpose(0, 3, 1, 2)
    mem_heads = lambda a: (a.reshape(b, mem.shape[1], 2, N_MEM_HEADS, LANES).transpose(0, 1, 3, 2, 4)
                           .reshape(b, mem.shape[1], N_MEM_HEADS, d // N_MEM_HEADS))
    return (y.reshape(b, t, d), heads(ka_tail, la), heads(va_tail, la), heads(kb_f, t), heads(vb_f, t), logf,
            mem_heads(mk_f), mem_heads(mv_f), conv)


def _sample_layer(x, cache_a_k, cache_a_v, cache_b_k, cache_b_v, cache_b_logf, cache_mem_k, cache_mem_v,
                  state_conv, w, bias):
    b, s, d = x.shape
    past = cache_b_k.shape[1]
    assert ROW_TILE % s == 0 and (b * s) % ROW_TILE == 0
    qa, ka, va, qb, kb, vb, ka_f, va_f, kb_f, vb_f, logf = _in_proj(
        x.reshape(b * s, d), w["g1"], w["w_main"], w["w_f"], w["b_f"], w["gains"], w["seg"], tm=ROW_TILE,
        tail_period=1, tail_rows=ROW_TILE, time_minor=False)
    r3 = lambda a: a.reshape(b, s, WIDTH)
    time_minor = lambda a: a.transpose(0, 2, 3, 1)
    oa = _band_sample(r3(qa), r3(ka), r3(va), time_minor(cache_a_k), time_minor(cache_a_v), bias)
    logf = logf.reshape(N_HEADS, b, s).transpose(1, 0, 2)
    total = past + s
    all_logf = jnp.concatenate([cache_b_logf.astype(F32).transpose(0, 2, 1), logf], axis=2)
    all_logf = jnp.pad(all_logf, ((0, 0), (0, 0), (0, -total % LANES)))
    logf = logf.transpose(0, 2, 1)
    cum = _cumsum_lanes(all_logf.reshape(b * N_HEADS, -1)).reshape(b, N_HEADS, -1)
    crow_new = cum[:, :, past:total]
    pad_rows = lambda a: jnp.pad(a, ((0, 0), (0, -s % LANES), (0, 0)))
    ob = _fox_sample(r3(qb), pad_rows(r3(kb)), pad_rows(r3(vb)), time_minor(cache_b_k), time_minor(cache_b_v),
                     cum[:, :, :past], jnp.pad(crow_new, ((0, 0), (0, 0), (0, -s % LANES))))
    dff = w["w_down"].shape[0]
    y, conv = _post_attention(x.reshape(b * s, d), oa, ob, _cache_mem_rows(cache_mem_k.astype(F32)),
                              _cache_mem_rows(cache_mem_v.astype(F32)), state_conv.astype(F32), w,
                              seq=s, ff_chunk=dff // 11)
    heads = lambda a: a.reshape(b, s, N_HEADS, HEAD_DIM)
    return y.reshape(b, s, d), heads(ka_f), heads(va_f), heads(kb_f), heads(vb_f), logf, conv


def kernel(x_prompt, x_sample, cache_a_k, cache_a_v, cache_b_k, cache_b_v, cache_b_logf, cache_mem_k, cache_mem_v, state_conv, mem_prompt, w_in, b_f, g_qa, g_ka, rel_bias, g_qb, g_kb, w_o, g_norm1, g_norm2, g_mem, w_mq, w_mkv, g_mq, g_mk, w_mo, g_norm3, w_up, w_conv, b_conv, w_down):
    depth = w_in.shape[0]
    xp, xs = x_prompt, x_sample
    outs_p, outs_s = [], []
    for l in range(depth):
        w = _layer_weights(l, w_in, b_f, g_qa, g_ka, rel_bias, g_qb, g_kb, w_o, g_norm1, g_norm2, g_mem, w_mq,
                           w_mkv, g_mq, g_mk, w_mo, g_norm3, w_up, w_conv, b_conv, w_down)
        bias = _band_bias(w["rel"])
        xp, *rest_p = _prompt_layer(xp, mem_prompt, w, bias)
        xs, *rest_s = _sample_layer(xs, cache_a_k[l], cache_a_v[l], cache_b_k[l], cache_b_v[l], cache_b_logf[l],
                                    cache_mem_k[l], cache_mem_v[l], state_conv[l], w, bias)
        outs_p.append(rest_p)
        outs_s.append(rest_s)
    stack = lambda outs: [jnp.stack(col, axis=0) for col in zip(*outs)]
    return (xp, xs, *stack(outs_p), *stack(outs_s))
```

```python
import functools

import jax
import jax.numpy as jnp
from jax import lax
from jax.experimental import pallas as pl
from jax.experimental.pallas import tpu as pltpu

F32 = jnp.float32
BF16 = jnp.bfloat16
EPS = 1e-6
NEG_INF = float("-inf")

HEAD_DIM = 64
N_HEADS = 8
WIDTH = N_HEADS * HEAD_DIM
CHUNK = 64
N_PREV_CHUNKS = 8
BAND_PAST = N_PREV_CHUNKS * CHUNK
REL_CLIP = 128
N_MEM_HEADS = 4
CONV_W = 3

ROW_TILE = 512
FFN_TILE = 1024
MID_TILE = 1024
PROMPT_TILE = 1024
BAND_Q = 256
BAND_WIN = BAND_PAST + BAND_Q
FOX_BLOCK = 512
FOX_KEY_BLOCKS = 4
FOX_CACHE_BLOCK = 2048
FOX_BAND = 64
MXU_COLS = 256
LANES = 128
SUBLANES = 8
BF16_ROWS = 16
VMEM_LIMIT = 56 * 1024 * 1024
LOG2E = 1.4426950408889634


def _dot(a, b):
    return jnp.dot(a, b, preferred_element_type=F32)


def _dot_nt(a, b):
    return lax.dot_general(a, b, (((1,), (1,)), ((), ())), preferred_element_type=F32)


def _rms(x, g):
    return x * lax.rsqrt(jnp.mean(x * x, axis=-1, keepdims=True) + EPS) * g


def _split3(x):
    hi = x.astype(BF16)
    r1 = x - hi.astype(F32)
    mid = r1.astype(BF16)
    lo = (r1 - mid.astype(F32)).astype(BF16)
    return hi, mid, lo


def _store_heads(ref, x):
    rows = x.shape[0]
    for h in range(N_HEADS):
        ref[pl.ds(h, rows, stride=N_HEADS), :] = x[:, h * HEAD_DIM:(h + 1) * HEAD_DIM]


def _resident(a):
    return pl.BlockSpec(a.shape, lambda i: (0,) * a.ndim, pipeline_mode=pl.Buffered(1))


def _params(*sem):
    return pltpu.CompilerParams(dimension_semantics=sem, vmem_limit_bytes=VMEM_LIMIT)


def _inproj_kernel(x_ref, g1_ref, w_ref, wf_ref, bf_ref, gains_ref, seg_ref,
                   qa_ref, ka_ref, va_ref, qb_ref, kb_ref, vb_ref,
                   kaf_ref, vaf_ref, kbf_ref, vbf_ref, logf_ref, *, tail_period, tail_rows, time_minor):
    h = _rms(x_ref[...], g1_ref[...]).astype(BF16)
    seg = seg_ref[...]

    def store_f32(ref, x):
        if time_minor:
            ref[...] = x.T
        else:
            _store_heads(ref, x)

    def proj(g):
        return _dot(h, w_ref[:, g * WIDTH:(g + 1) * WIDTH])

    def head_norm(p, row):
        p2 = (p * p).astype(BF16)
        ss = jnp.concatenate([_dot(p2[:, :MXU_COLS], seg), _dot(p2[:, MXU_COLS:], seg)], axis=1)
        return p * lax.rsqrt(ss * (1.0 / HEAD_DIM) + EPS) * gains_ref[row:row + 1, :]

    is_tail = (pl.program_id(0) % tail_period) == tail_period - 1
    scale = HEAD_DIM ** -0.5 * LOG2E

    kb = head_norm(proj(4), 3)
    kb_ref[...] = kb.astype(BF16)
    store_f32(kbf_ref.at[0], kb)
    vb = proj(5)
    vb_ref[...] = vb.astype(BF16)
    store_f32(vbf_ref.at[0], vb)
    z = _dot_nt(wf_ref[...], h)[:N_HEADS] + bf_ref[...]
    logf_ref[...] = jnp.minimum(z, 0.0) - jnp.log1p(jnp.exp(-jnp.abs(z)))
    qb_ref[...] = (head_norm(proj(3), 2) * scale).astype(BF16)
    qa_ref[...] = (head_norm(proj(0), 0) * scale).astype(BF16)
    ka = head_norm(proj(1), 1)
    ka_ref[...] = ka.astype(BF16)
    va = proj(2)
    va_ref[...] = va.astype(BF16)

    @pl.when(is_tail)
    def _():
        store_f32(kaf_ref.at[0], ka[-tail_rows:])
        store_f32(vaf_ref.at[0], va[-tail_rows:])


def _in_proj(x2d, g1, w_main, w_f, b_f, gains, seg, *, tm, tail_period, tail_rows, time_minor):
    n, d = x2d.shape
    assert n % tm == 0 and (n // tm) % tail_period == 0 and (tail_rows == tm or (time_minor and tail_rows <= tm))
    n_tail = n // tm // tail_period
    row_bf = jax.ShapeDtypeStruct((n, WIDTH), BF16)
    rows = pl.BlockSpec((tm, WIDTH), lambda i: (i, 0))
    if time_minor:
        head_f = jax.ShapeDtypeStruct((n_tail, WIDTH, tm * tail_period), F32)
        tail_f = jax.ShapeDtypeStruct((n_tail, WIDTH, tail_rows), F32)
        head_rows = pl.BlockSpec((1, WIDTH, tm), lambda i: (i // tail_period, 0, i % tail_period))
        tail = pl.BlockSpec((1, WIDTH, tail_rows), lambda i: (i // tail_period, 0, 0))
    else:
        head_f = jax.ShapeDtypeStruct((n // tm, tm * N_HEADS, HEAD_DIM), F32)
        tail_f = jax.ShapeDtypeStruct((n_tail, tm * N_HEADS, HEAD_DIM), F32)
        head_rows = pl.BlockSpec((1, tm * N_HEADS, HEAD_DIM), lambda i: (i, 0, 0))
        tail = pl.BlockSpec((1, tm * N_HEADS, HEAD_DIM), lambda i: (i // tail_period, 0, 0))
    full = _resident
    return pl.pallas_call(
        functools.partial(_inproj_kernel, tail_period=tail_period, tail_rows=tail_rows, time_minor=time_minor),
        grid=(n // tm,),
        in_specs=[pl.BlockSpec((tm, d), lambda i: (i, 0)), full(g1), full(w_main), full(w_f),
                  full(b_f), full(gains), full(seg)],
        out_specs=[rows] * 6 + [tail, tail, head_rows, head_rows, pl.BlockSpec((N_HEADS, tm), lambda i: (0, i))],
        out_shape=[row_bf] * 6 + [tail_f, tail_f, head_f, head_f, jax.ShapeDtypeStruct((N_HEADS, n), F32)],
        compiler_params=_params("arbitrary"),
        name="in_proj",
    )(x2d, g1, w_main, w_f, b_f, gains, seg)


def _store_mem_rows(ref, x, hh):
    n = x.shape[0]
    for lg in range(2):
        ref[pl.ds(lg * N_MEM_HEADS + hh, n, stride=2 * N_MEM_HEADS), :] = x[:, lg * LANES:(lg + 1) * LANES]


def _memkv_kernel(mem_ref, g_ref, w_ref, gk_ref, kf_ref, vf_ref, kb_ref, vb_ref):
    d = mem_ref.shape[-1]
    hd = d // N_MEM_HEADS
    h = _rms(mem_ref[0], g_ref[...]).astype(BF16)
    kv = _dot(h, w_ref[...])
    for hh in range(N_MEM_HEADS):
        cs = slice(hh * hd, (hh + 1) * hd)
        k = _rms(kv[:, cs], gk_ref[...])
        v = kv[:, d + hh * hd:d + (hh + 1) * hd]
        _store_mem_rows(kf_ref.at[0], k, hh)
        _store_mem_rows(vf_ref.at[0], v, hh)
        kb_ref[0, :, cs] = k.astype(BF16)
        vb_ref[0, :, cs] = v.astype(BF16)


def _mem_kv(mem, g_mem, w_mkv, g_mk):
    b, n, d = mem.shape
    assert d == N_MEM_HEADS * 2 * LANES
    blk = pl.BlockSpec((1, n, d), lambda i: (i, 0, 0))
    blk_f = pl.BlockSpec((1, n * 2 * N_MEM_HEADS, LANES), lambda i: (i, 0, 0))
    full = _resident
    f = jax.ShapeDtypeStruct((b, n * 2 * N_MEM_HEADS, LANES), F32)
    h = jax.ShapeDtypeStruct((b, n, d), BF16)
    return pl.pallas_call(
        _memkv_kernel, grid=(b,),
        in_specs=[blk, full(g_mem), full(w_mkv), full(g_mk)],
        out_specs=[blk_f, blk_f, blk, blk], out_shape=[f, f, h, h],
        compiler_params=_params("arbitrary"), name="mem_kv",
    )(mem, g_mem, w_mkv, g_mk)


def _cumsum_kernel(x_ref, o_ref):
    r, t = x_ref.shape
    k = lax.broadcasted_iota(jnp.int32, (LANES, LANES), 0)
    j = lax.broadcasted_iota(jnp.int32, (LANES, LANES), 1)
    tri = jnp.where(k <= j, 1.0, 0.0).astype(BF16)
    carry = jnp.zeros((r, 1), F32)
    for blk in range(t // LANES):
        hi, mid, lo = _split3(x_ref[:, blk * LANES:(blk + 1) * LANES])
        cs = _dot(hi, tri) + _dot(mid, tri) + _dot(lo, tri)
        o_ref[:, blk * LANES:(blk + 1) * LANES] = (cs + carry) * LOG2E
        carry = carry + cs[:, LANES - 1:LANES]


def _cumsum_lanes(x):
    assert x.shape[1] % LANES == 0
    return pl.pallas_call(_cumsum_kernel, out_shape=jax.ShapeDtypeStruct(x.shape, F32), name="cumsum")(x)


def _band_bias_kernel(tbl_ref, o_ref):
    nrel = tbl_ref.shape[1]
    qblk, win = o_ref.shape[1:]
    wrap = qblk + win
    jp = lax.broadcasted_iota(jnp.int32, (nrel, wrap), 1)
    r = lax.broadcasted_iota(jnp.int32, (nrel, wrap), 0)
    d = jnp.where(jp < win, jp, jp - wrap)
    rel = jnp.clip(BAND_PAST - d, -REL_CLIP, REL_CLIP) + REL_CLIP
    onehot = jnp.where(rel == r, 1.0, 0.0).astype(BF16)
    hi, mid, lo = _split3(tbl_ref[...])
    base = (_dot(hi, onehot) + _dot(mid, onehot) + _dot(lo, onehot)) * LOG2E
    qc = lax.broadcasted_iota(jnp.int32, (qblk, win), 0) // CHUNK
    kc = lax.broadcasted_iota(jnp.int32, (qblk, win), 1) // CHUNK
    inband = (kc >= qc) & (kc <= qc + N_PREV_CHUNKS)
    for h in range(N_HEADS):
        row = jnp.broadcast_to(base[h:h + 1, :], (qblk, wrap))
        toeplitz = pltpu.roll(row, 0, 1, stride=1, stride_axis=0)
        o_ref[h] = jnp.where(inband, toeplitz[:, :win], NEG_INF)


def _band_bias(rel_table):
    nrel = 2 * REL_CLIP + 1
    pad = -nrel % LANES
    tbl = jnp.pad(rel_table.astype(F32), ((0, 0), (0, pad)))
    return pl.pallas_call(
        _band_bias_kernel, out_shape=jax.ShapeDtypeStruct((N_HEADS, BAND_Q, BAND_WIN), F32),
        name="band_bias")(tbl)


def _zero_after(x):
    bits = pltpu.bitcast(x[:SUBLANES, :LANES], jnp.uint32)
    return ((bits >> 16) >> 16)[:1].astype(F32)


def _softmax_pv(s, vs, transposed=None):
    transposed = transposed or (False,) * len(vs)
    m = jnp.max(s, axis=-1, keepdims=True)
    e = jnp.exp2(s - m)
    l = jnp.sum(e, axis=-1, keepdims=True)
    eb = e.astype(BF16)
    o, lo = None, 0
    for v, vt in zip(vs, transposed):
        keys = v.shape[1] if vt else v.shape[0]
        part = _dot_nt(eb[:, lo:lo + keys], v) if vt else _dot(eb[:, lo:lo + keys], v)
        o = part if o is None else o + part
        lo += keys
    return o * (1.0 / l)


def _band_prompt_kernel(q_ref, k0_ref, k1_ref, k2_ref, v0_ref, v1_ref, v2_ref, bias_ref, o_ref, s_sc, m_sc, qm_sc):
    i = pl.program_id(1)
    tq = q_ref.shape[1]
    k_refs, v_refs = (k0_ref, k1_ref, k2_ref), (v0_ref, v1_ref, v2_ref)
    bands = [slice(r, r + CHUNK) for r in range(0, tq, CHUNK)]
    n_groups = BAND_WIN // LANES
    per_block = tq // LANES
    group = lambda g: slice(g * LANES, (g + 1) * LANES)
    visible = lambda c: [g for g in range(n_groups) if 2 * g + 1 >= c and 2 * g <= c + N_PREV_CHUNKS]
    low = _low_half(tq)
    for h in range(N_HEADS):
        q_pair = _pair(q_ref, h)
        qm_sc[h] = jnp.where(low if h % 2 == 0 else ~low, q_pair, jnp.zeros_like(q_pair))

    def logits(early):
        for h in range(N_HEADS):
            qm = qm_sc[h]
            parts = [_dot_nt(qm, _pair(k_ref, h)) for k_ref in k_refs]
            for c, rows in enumerate(bands):
                mx = None
                for g in visible(c):
                    p, gp = divmod(g, per_block)
                    sg = parts[p][rows, group(gp)] + bias_ref[h, rows, group(g)]
                    if early and p < 2:
                        sg = jnp.where(i + p >= 2, sg, NEG_INF)
                    s_sc[h, rows, group(g)] = sg
                    mx = sg if mx is None else jnp.maximum(mx, sg)
                m_sc[h, rows] = jnp.broadcast_to(jnp.max(mx, axis=-1, keepdims=True), (CHUNK, LANES))

    values = functools.partial(_band_values, v_refs, o_ref, s_sc, m_sc, bands, visible)

    @pl.when(i >= 2)
    def _():
        logits(False)

    @pl.when(i < 2)
    def _():
        logits(True)

    values()


def _band_values(v_refs, o_ref, s_sc, m_sc, bands, visible):
    tq = o_ref.shape[1]
    n_groups = s_sc.shape[2] // LANES
    group = lambda g: slice(g * LANES, (g + 1) * LANES)
    gate = previous = None
    for h in range(N_HEADS):
        p_bands, invs = [], []
        for c, rows in enumerate(bands):
            m = m_sc[h, rows]
            if gate is not None:
                m = m + gate
            rs, ps = None, []
            for g in range(n_groups):
                if g in visible(c):
                    e = jnp.exp2(s_sc[h, rows, group(g)] - m)
                    ps.append(e.astype(BF16))
                    rs = e if rs is None else rs + e
                else:
                    ps.append(jnp.zeros((CHUNK, LANES), BF16))
            p_bands.append(jnp.concatenate(ps, axis=1))
            invs.append(1.0 / jnp.sum(rs, axis=-1, keepdims=True))
        p_all = jnp.concatenate(p_bands, axis=0)
        o = None
        for p, v_ref in enumerate(v_refs):
            part = _dot(p_all[:, p * tq:(p + 1) * tq], _pair(v_ref, h))
            o = part if o is None else o + part
        gate, previous = previous, _zero_after(o)
        half = slice((h % 2) * HEAD_DIM, (h % 2 + 1) * HEAD_DIM)
        o_ref[0, :, h * HEAD_DIM:(h + 1) * HEAD_DIM] = (o * jnp.concatenate(invs, axis=0))[:, half].astype(BF16)


def _band_prompt(qa, ka, va, bias):
    b, t, _ = qa.shape
    tq = BAND_Q
    assert t % tq == 0
    blk = lambda back: pl.BlockSpec((1, tq, WIDTH), lambda bi, i: (bi, jnp.maximum(i - back, 0), 0))
    return pl.pallas_call(
        _band_prompt_kernel, grid=(b, t // tq),
        in_specs=[blk(0), blk(2), blk(1), blk(0), blk(2), blk(1), blk(0),
                  pl.BlockSpec(bias.shape, lambda bi, i: (0, 0, 0))],
        out_specs=blk(0), out_shape=jax.ShapeDtypeStruct((b, t, WIDTH), BF16),
        scratch_shapes=[pltpu.VMEM((N_HEADS, tq, BAND_WIN), F32), pltpu.VMEM((N_HEADS, tq, LANES), F32),
                        pltpu.VMEM((N_HEADS, tq, LANES), BF16)],
        compiler_params=_params("arbitrary", "arbitrary"), name="band_prompt",
    )(qa, ka, ka, ka, va, va, va, bias)


def _band_sample_kernel(q_ref, kc_ref, vc_ref, kn_ref, vn_ref, bias_ref, o_ref):
    for h in range(N_HEADS):
        hs = slice(h * HEAD_DIM, (h + 1) * HEAD_DIM)
        q = q_ref[0, :, hs]
        s = jnp.concatenate([_dot(q, kc_ref[0, h].astype(BF16)), _dot_nt(q, kn_ref[0, :, hs])], axis=1)
        s = s + bias_ref[h, :CHUNK, :BAND_PAST + CHUNK]
        o = _softmax_pv(s, [vc_ref[0, h].astype(BF16), vn_ref[0, :, hs]], transposed=(True, False))
        o_ref[0, :, hs] = o.astype(BF16)


def _band_sample(qa, ka, va, cache_k, cache_v, bias):
    b, s, _ = qa.shape
    la = cache_k.shape[3]
    assert s == CHUNK and la == BAND_PAST
    new = pl.BlockSpec((1, s, WIDTH), lambda bi: (bi, 0, 0))
    old = pl.BlockSpec((1, N_HEADS, HEAD_DIM, la), lambda bi: (bi, 0, 0, 0))
    return pl.pallas_call(
        _band_sample_kernel, grid=(b,),
        in_specs=[new, old, old, new, new, pl.BlockSpec(bias.shape, lambda bi: (0, 0, 0))],
        out_specs=new, out_shape=jax.ShapeDtypeStruct((b, s, WIDTH), BF16),
        compiler_params=_params("arbitrary"), name="band_sample",
    )(qa, cache_k, cache_v, ka, va, bias)


def _fox_scratch(tq, tk):
    rows = lambda w: pltpu.VMEM((N_HEADS, tq, w), F32)
    return [rows(tk), rows(LANES), rows(LANES), rows(LANES), rows(LANES),
            pltpu.VMEM((N_HEADS, tq, LANES), BF16), pltpu.VMEM((N_HEADS // 2, tq, LANES), F32)]


def _low_half(rows):
    return lax.broadcasted_iota(jnp.int32, (rows, LANES), 1) < HEAD_DIM


def _fox_init(q_ref, cq_ref, sc):
    _, cq_sc, m_sc, _, l_sc, qm_sc, acc_sc = sc
    tq = cq_sc.shape[1]
    low = _low_half(tq)
    for h in range(N_HEADS):
        cq_sc[h] = jnp.broadcast_to(cq_ref[0, h:h + 1, :], (LANES, cq_ref.shape[2])).T[:tq]
        q_pair = q_ref[0, :, (h // 2) * LANES:(h // 2 + 1) * LANES]
        qm_sc[h] = jnp.where(low if h % 2 == 0 else ~low, q_pair, jnp.zeros_like(q_pair))
    m_sc[...] = jnp.full(m_sc.shape, NEG_INF, F32)
    l_sc[...] = jnp.zeros(l_sc.shape, F32)
    acc_sc[...] = jnp.zeros(acc_sc.shape, F32)


def _fox_block(qk, pv, crow_ref, tk, causal, sc):
    s_sc, cq_sc, m_sc, a_sc, l_sc, qm_sc, acc_sc = sc
    tq = s_sc.shape[1]
    band = min(tq, FOX_BAND)
    bands = [slice(r, r + band) for r in range(0, tq, band)]
    n_groups = tk // LANES
    group = lambda g: slice(g * LANES, (g + 1) * LANES)

    def visible(rows):
        if not causal:
            return [(g, False) for g in range(n_groups)]
        return [(g, (g + 1) * LANES - 1 > rows.start) for g in range(n_groups) if g * LANES < rows.stop]

    if causal:
        ahead = (lax.broadcasted_iota(jnp.int32, (band, LANES), 0)
                 - lax.broadcasted_iota(jnp.int32, (band, LANES), 1))
    for h in range(N_HEADS):
        s = qk(h, qm_sc[h])
        for rows in bands:
            cq = cq_sc[h, rows]
            mx = None
            for g, masked in visible(rows):
                sg = s[rows, group(g)] + (cq - crow_ref[0, h:h + 1, group(g)])
                if masked:
                    sg = jnp.where(ahead >= g * LANES - rows.start, sg, NEG_INF)
                s_sc[h, rows, group(g)] = sg
                mx = sg if mx is None else jnp.maximum(mx, sg)
            m_prev = m_sc[h, rows]
            m_new = jnp.maximum(m_prev, jnp.max(mx, axis=-1, keepdims=True))
            a_sc[h, rows] = jnp.exp2(m_prev - m_new)
            m_sc[h, rows] = m_new
    for h in range(N_HEADS):
        p_bands = []
        for rows in bands:
            seen = visible(rows)
            m_new = m_sc[h, rows]
            rs, ps = None, []
            for g, _ in seen:
                p = jnp.exp2(s_sc[h, rows, group(g)] - m_new)
                ps.append(p.astype(BF16))
                rs = p if rs is None else rs + p
            l_sc[h, rows] = a_sc[h, rows] * l_sc[h, rows] + rs
            ps += [jnp.zeros((band, LANES), BF16)] * (n_groups - len(seen))
            p_bands.append(jnp.concatenate(ps, axis=1))
        half = slice((h % 2) * HEAD_DIM, (h % 2 + 1) * HEAD_DIM)
        new = a_sc[h] * acc_sc[h // 2] + pv(h, jnp.concatenate(p_bands, axis=0))
        acc_sc[h // 2, :, half] = new[:, half]


def _fox_finish(o_ref, sc):
    l_sc, acc_sc = sc[4], sc[6]
    low = _low_half(acc_sc.shape[1])
    total = lambda h: jnp.sum(l_sc[h], axis=-1, keepdims=True)
    for g in range(N_HEADS // 2):
        inv = jnp.where(low, 1.0 / total(2 * g), 1.0 / total(2 * g + 1))
        o_ref[0, :, g * LANES:(g + 1) * LANES] = (acc_sc[g] * inv).astype(BF16)


def _pair(ref, h):
    return ref[0, :, (h // 2) * LANES:(h // 2 + 1) * LANES]


def _row_major_products(k_ref, v_ref):
    return (lambda h, qm: _dot_nt(qm, _pair(k_ref, h))), (lambda h, p: _dot(p, _pair(v_ref, h)))


def _fox_prompt_kernel(qi_ref, kj_ref, q_ref, k_ref, v_ref, crow_ref, cq_ref, o_ref, *sc):
    p = pl.program_id(1)
    qi, kj = qi_ref[p], kj_ref[p]
    tq = q_ref.shape[1]

    for i in range(FOX_KEY_BLOCKS):
        rows = slice(i * tq, (i + 1) * tq)
        qk, pv = _row_major_products(k_ref.at[:, rows, :], v_ref.at[:, rows, :])
        crow = crow_ref.at[:, :, rows]
        blk = FOX_KEY_BLOCKS * kj + i
        for first in ((True, False) if i == 0 else (False,)):
            started = True if i else ((kj == 0) if first else (kj > 0))

            @pl.when(started & (blk < qi))
            def _():
                if first:
                    _fox_init(q_ref, cq_ref, sc)
                _fox_block(qk, pv, crow, tq, False, sc)

            @pl.when(started & (blk == qi))
            def _():
                if first:
                    _fox_init(q_ref, cq_ref, sc)
                _fox_block(qk, pv, crow, tq, True, sc)
                _fox_finish(o_ref, sc)


def _fox_prompt(qb, kb, vb, crow):
    b, t, _ = qb.shape
    tq = FOX_BLOCK
    tk = FOX_KEY_BLOCKS * tq
    assert t % tk == 0
    pairs = [(i, j) for i in range(t // tq) for j in range(i // FOX_KEY_BLOCKS + 1)]
    qi = jnp.array([p[0] for p in pairs], jnp.int32)
    kj = jnp.array([p[1] for p in pairs], jnp.int32)
    qblk = pl.BlockSpec((1, tq, WIDTH), lambda bi, p, qi, kj: (bi, qi[p], 0))
    kblk = pl.BlockSpec((1, tk, WIDTH), lambda bi, p, qi, kj: (bi, kj[p], 0))
    grid_spec = pltpu.PrefetchScalarGridSpec(
        num_scalar_prefetch=2, grid=(b, len(pairs)),
        in_specs=[qblk, kblk, kblk,
                  pl.BlockSpec((1, N_HEADS, tk), lambda bi, p, qi, kj: (bi, 0, kj[p])),
                  pl.BlockSpec((1, N_HEADS, tq), lambda bi, p, qi, kj: (bi, 0, qi[p]))],
        out_specs=qblk, scratch_shapes=_fox_scratch(tq, tq))
    return pl.pallas_call(
        _fox_prompt_kernel, grid_spec=grid_spec, out_shape=jax.ShapeDtypeStruct((b, t, WIDTH), BF16),
        compiler_params=_params("arbitrary", "arbitrary"), name="fox_prompt",
    )(qi, kj, qb, kb, vb, crow, crow)


def _fox_sample_kernel(q_ref, kc_ref, vc_ref, kn_ref, vn_ref, crow_ref, crow_new_ref, o_ref, *sc):
    j = pl.program_id(1)
    last = pl.num_programs(1) - 1

    @pl.when(j == 0)
    def _():
        _fox_init(q_ref, crow_new_ref, sc)

    def cached_pair(ref, h):
        g = h // 2
        return ref[0, 2 * g:2 * g + 2].astype(BF16).reshape(2 * HEAD_DIM, ref.shape[3])

    _fox_block(lambda h, qm: _dot(qm, cached_pair(kc_ref, h)),
               lambda h, p: _dot_nt(p, cached_pair(vc_ref, h)), crow_ref, kc_ref.shape[3], False, sc)

    @pl.when(j == last)
    def _():
        qk, pv = _row_major_products(kn_ref, vn_ref)
        _fox_block(qk, pv, crow_new_ref, kn_ref.shape[1], True, sc)
        _fox_finish(o_ref, sc)


def _fox_sample(qb, kb, vb, cache_k, cache_v, crow, crow_new):
    b, s, _ = qb.shape
    past = cache_k.shape[3]
    tk, tn = min(FOX_CACHE_BLOCK, past), kb.shape[1]
    assert past % tk == 0 and tn % LANES == 0 and tn >= s
    nk = past // tk
    new = lambda r: pl.BlockSpec((1, r, WIDTH), lambda bi, j: (bi, 0, 0))
    old = pl.BlockSpec((1, N_HEADS, HEAD_DIM, tk), lambda bi, j: (bi, 0, 0, j))
    return pl.pallas_call(
        _fox_sample_kernel, grid=(b, nk),
        in_specs=[new(s), old, old, new(tn), new(tn),
                  pl.BlockSpec((1, N_HEADS, tk), lambda bi, j: (bi, 0, j)),
                  pl.BlockSpec((1, N_HEADS, tn), lambda bi, j: (bi, 0, 0))],
        out_specs=new(s), out_shape=jax.ShapeDtypeStruct((b, s, WIDTH), BF16),
        scratch_shapes=_fox_scratch(s, tk),
        compiler_params=_params("arbitrary", "arbitrary"), name="fox_sample",
    )(qb, cache_k, cache_v, kb, vb, crow, crow_new)


def _mid_kernel(x_ref, oa_ref, ob_ref, wo_ref, g2_ref, wq_ref, gq_ref, mk_ref, mv_ref, wmo_ref, o_ref,
                om_sc, *, seg):
    tm, d = x_ref.shape
    hd = d // N_MEM_HEADS
    x1 = x_ref[...] + _dot(oa_ref[...], wo_ref[:WIDTH, :]) + _dot(ob_ref[...], wo_ref[WIDTH:, :])
    h2 = _rms(x1, g2_ref[...]).astype(BF16)
    qm = _dot(h2, wq_ref[...])
    scale = hd ** -0.5 * LOG2E
    for hh in range(N_MEM_HEADS):
        cs = slice(hh * hd, (hh + 1) * hd)
        q = (_rms(qm[:, cs], gq_ref[...]) * scale).astype(BF16)
        def mem_head(ref, sg):
            if ref.shape[2] == d:
                return ref[sg, :, cs]
            rows = lambda lg: ref[sg, pl.ds(lg * N_MEM_HEADS + hh, ref.shape[1] // (2 * N_MEM_HEADS),
                                            stride=2 * N_MEM_HEADS), :]
            return jnp.concatenate([rows(0), rows(1)], axis=1).astype(BF16)

        for sg in range(tm // seg):
            rs = slice(sg * seg, (sg + 1) * seg)
            s = _dot_nt(q[rs], mem_head(mk_ref, sg))
            om_sc[rs, cs] = _softmax_pv(s, [mem_head(mv_ref, sg)]).astype(BF16)
    o_ref[...] = x1 + _dot(om_sc[...], wmo_ref[...])


def _cache_mem_rows(x):
    b, n, nh, hd = x.shape
    assert nh == N_MEM_HEADS and hd == 2 * LANES
    return x.reshape(b, n, nh, 2, LANES).transpose(0, 1, 3, 2, 4).reshape(b, n * 2 * nh, LANES)


def _mid(x2d, oa, ob, w_o, g2, w_mq, g_mq, mk, mv, w_mo, *, seq):
    n, d = x2d.shape
    tm = MID_TILE if seq % MID_TILE == 0 else ROW_TILE
    seg = min(seq, tm)
    tiles_per_batch = seq // seg
    nb = tm // seg
    assert n % tm == 0 and seq % seg == 0 and tm % seg == 0
    rows = lambda w: pl.BlockSpec((tm, w), lambda i: (i, 0))
    full = _resident
    mem = pl.BlockSpec((nb,) + mk.shape[1:], lambda i: (i // tiles_per_batch, 0, 0))
    return pl.pallas_call(
        functools.partial(_mid_kernel, seg=seg), grid=(n // tm,),
        in_specs=[rows(d), rows(WIDTH), rows(WIDTH), full(w_o), full(g2), full(w_mq), full(g_mq), mem, mem,
                  full(w_mo)],
        out_specs=rows(d), out_shape=jax.ShapeDtypeStruct((n, d), F32),
        scratch_shapes=[pltpu.VMEM((tm, d), BF16)],
        compiler_params=_params("arbitrary"), name="mid",
    )(x2d, oa, ob, w_o, g2, w_mq, g_mq, mk, mv, w_mo)


def _ffn_kernel(x_ref, g3_ref, wup_ref, wc_ref, bc_ref, wd_ref, st_ref, o_ref, last_ref, carry_sc, y_sc,
                *, seg, tiles_per_batch, ff_chunk):
    tm, d = x_ref.shape
    dff = wd_ref.shape[0]
    i = pl.program_id(0)
    h3 = _rms(x_ref[...], g3_ref[...]).astype(BF16)
    if tiles_per_batch > 1:
        @pl.when(i == 0)
        def _():
            carry_sc[...] = jnp.zeros(carry_sc.shape, F32)
    rowid = lax.broadcasted_iota(jnp.int32, (SUBLANES, 1), 0)
    first_tile = (i % tiles_per_batch) == 0
    for c in range(dff // ff_chunk):
        cs = slice(c * ff_chunk, (c + 1) * ff_chunk)
        gate = _dot(h3, wup_ref[:, cs])
        val = _dot(h3, wup_ref[:, dff + c * ff_chunk:dff + (c + 1) * ff_chunk])
        w0, w1, w2 = wc_ref[0:1, cs], wc_ref[1:2, cs], wc_ref[2:3, cs]
        for sg in range(tm // seg):
            rows = slice(sg * seg, (sg + 1) * seg)
            g = gate[rows]
            if tiles_per_batch == 1:
                prev = st_ref[sg, :, cs]
            else:
                prev = jnp.where(first_tile, st_ref[0, :, cs], carry_sc[SUBLANES - 2:, cs])
            p2, p1 = prev[0:1], prev[1:2]
            s1 = pltpu.roll(g, 1, 0)
            s2 = pltpu.roll(g, 2, 0)
            top1 = jnp.where(rowid == 0, p1, s1[:SUBLANES])
            top2 = jnp.where(rowid == 0, p2, jnp.where(rowid == 1, p1, s2[:SUBLANES]))
            s1 = jnp.concatenate([top1, s1[SUBLANES:]], axis=0)
            s2 = jnp.concatenate([top2, s2[SUBLANES:]], axis=0)
            conv = s2 * w0 + s1 * w1 + g * w2 + bc_ref[:, cs]
            y_sc[rows, cs] = (conv * jax.nn.sigmoid(conv) * val[rows]).astype(BF16)
            last_ref[sg, :, cs] = g[seg - SUBLANES:]
        if tiles_per_batch > 1:
            carry_sc[:, cs] = gate[tm - SUBLANES:]
    o_ref[...] = x_ref[...] + _dot(y_sc[...], wd_ref[...])


def _conv_ffn(x2d, g3, w_up, w_conv, b_conv, w_down, state, *, seq, ff_chunk):
    n, d = x2d.shape
    dff = w_down.shape[0]
    tm = min(FFN_TILE, n)
    seg = min(seq, tm)
    tiles_per_batch = seq // seg
    nb = tm // seg
    assert n % tm == 0 and seq % seg == 0 and tm % seg == 0 and dff % ff_chunk == 0 and seg >= SUBLANES
    rows = pl.BlockSpec((tm, d), lambda i: (i, 0))
    full = _resident
    per_batch = lambda r: pl.BlockSpec((nb, r, dff), lambda i: (i // tiles_per_batch, 0, 0))
    return pl.pallas_call(
        functools.partial(_ffn_kernel, seg=seg, tiles_per_batch=tiles_per_batch, ff_chunk=ff_chunk),
        grid=(n // tm,),
        in_specs=[rows, full(g3), full(w_up), full(w_conv), full(b_conv), full(w_down), per_batch(CONV_W - 1)],
        out_specs=[rows, per_batch(SUBLANES)],
        out_shape=[jax.ShapeDtypeStruct((n, d), F32), jax.ShapeDtypeStruct((n // seq, SUBLANES, dff), F32)],
        scratch_shapes=[pltpu.VMEM((SUBLANES, dff), F32), pltpu.VMEM((tm, dff), BF16)],
        compiler_params=_params("arbitrary"), name="conv_ffn",
    )(x2d, g3, w_up, w_conv, b_conv, w_down, state)


def _row(v):
    return v.astype(F32).reshape(1, -1)


def _layer_weights(l, w_in, b_f, g_qa, g_ka, rel_bias, g_qb, g_kb, w_o, g_norm1, g_norm2, g_mem, w_mq, w_mkv,
                   g_mq, g_mk, w_mo, g_norm3, w_up, w_conv, b_conv, w_down):
    n_main = 6 * WIDTH
    tile = lambda g: jnp.tile(g[l].astype(F32), N_HEADS)
    blk = jnp.arange(MXU_COLS) // HEAD_DIM
    return dict(
        w_main=w_in[l][:, :n_main].astype(BF16),
        w_f=jnp.pad(w_in[l][:, n_main:].T, ((0, BF16_ROWS - N_HEADS), (0, 0))).astype(BF16),
        b_f=b_f[l].astype(F32).reshape(N_HEADS, 1),
        gains=jnp.stack([tile(g_qa), tile(g_ka), tile(g_qb), tile(g_kb)]),
        seg=(blk[:, None] == blk[None, :]).astype(BF16),
        rel=rel_bias[l],
        w_o=w_o[l].astype(BF16), g1=_row(g_norm1[l]), g2=_row(g_norm2[l]), g3=_row(g_norm3[l]),
        g_mem=_row(g_mem[l]), w_mq=w_mq[l].astype(BF16), w_mkv=w_mkv[l].astype(BF16),
        g_mq=_row(g_mq[l]), g_mk=_row(g_mk[l]), w_mo=w_mo[l].astype(BF16),
        w_up=w_up[l].astype(BF16), w_conv=w_conv[l].astype(F32), b_conv=_row(b_conv[l]),
        w_down=w_down[l].astype(BF16))


def _post_attention(x2d, oa, ob, mk, mv, state, w, *, seq, ff_chunk):
    n, d = x2d.shape
    x2 = _mid(x2d, oa.reshape(n, WIDTH), ob.reshape(n, WIDTH), w["w_o"], w["g2"], w["w_mq"], w["g_mq"],
              mk, mv, w["w_mo"], seq=seq)
    y, last = _conv_ffn(x2, w["g3"], w["w_up"], w["w_conv"], w["b_conv"], w["w_down"], state,
                        seq=seq, ff_chunk=ff_chunk)
    return y, last[:, SUBLANES - (CONV_W - 1):]


def _prompt_layer(x, mem, w, bias):
    b, t, d = x.shape
    la = min(BAND_PAST, t)
    tm = PROMPT_TILE if t % PROMPT_TILE == 0 else ROW_TILE
    assert la <= tm and t % tm == 0
    qa, ka, va, qb, kb, vb, ka_tail, va_tail, kb_f, vb_f, logf = _in_proj(
        x.reshape(b * t, d), w["g1"], w["w_main"], w["w_f"], w["b_f"], w["gains"], w["seg"],
        tm=tm, tail_period=t // tm, tail_rows=la, time_minor=True)
    r3 = lambda a: a.reshape(b, t, WIDTH)
    oa = _band_prompt(r3(qa), r3(ka), r3(va), bias)
    logf = logf.reshape(N_HEADS, b, t).transpose(1, 0, 2)
    crow = _cumsum_lanes(logf.reshape(b * N_HEADS, t)).reshape(b, N_HEADS, t)
    logf = logf.transpose(0, 2, 1)
    ob = _fox_prompt(r3(qb), r3(kb), r3(vb), crow)
    mk_f, mv_f, mk, mv = _mem_kv(mem, w["g_mem"], w["w_mkv"], w["g_mk"])
    dff = w["w_down"].shape[0]
    y, conv = _post_attention(x.reshape(b * t, d), oa, ob, mk, mv, jnp.zeros((b, CONV_W - 1, dff), F32), w,
                              seq=t, ff_chunk=dff // 11)
    heads = lambda a, n: a.reshape(b, N_HEADS, HEAD_DIM, n).transpose(0, 3, 1, 2)
    mem_heads = lambda a: (a.reshape(b, mem.shape[1], 2, N_MEM_HEADS, LANES).transpose(0, 1, 3, 2, 4)
                           .reshape(b, mem.shape[1], N_MEM_HEADS, d // N_MEM_HEADS))
    return (y.reshape(b, t, d), heads(ka_tail, la), heads(va_tail, la), heads(kb_f, t), heads(vb_f, t), logf,
            mem_heads(mk_f), mem_heads(mv_f), conv)


def _sample_layer(x, cache_a_k, cache_a_v, cache_b_k, cache_b_v, cache_b_logf, cache_mem_k, cache_mem_v,
                  state_conv, w, bias):
    b, s, d = x.shape
    past = cache_b_k.shape[1]
    assert ROW_TILE % s == 0 and (b * s) % ROW_TILE == 0
    qa, ka, va, qb, kb, vb, ka_f, va_f, kb_f, vb_f, logf = _in_proj(
        x.reshape(b * s, d), w["g1"], w["w_main"], w["w_f"], w["b_f"], w["gains"], w["seg"], tm=ROW_TILE,
        tail_period=1, tail_rows=ROW_TILE, time_minor=False)
    r3 = lambda a: a.reshape(b, s, WIDTH)
    time_minor = lambda a: a.transpose(0, 2, 3, 1)
    oa = _band_sample(r3(qa), r3(ka), r3(va), time_minor(cache_a_k), time_minor(cache_a_v), bias)
    logf = logf.reshape(N_HEADS, b, s).transpose(1, 0, 2)
    total = past + s
    all_logf = jnp.concatenate([cache_b_logf.astype(F32).transpose(0, 2, 1), logf], axis=2)
    all_logf = jnp.pad(all_logf, ((0, 0), (0, 0), (0, -total % LANES)))
    logf = logf.transpose(0, 2, 1)
    cum = _cumsum_lanes(all_logf.reshape(b * N_HEADS, -1)).reshape(b, N_HEADS, -1)
    crow_new = cum[:, :, past:total]
    pad_rows = lambda a: jnp.pad(a, ((0, 0), (0, -s % LANES), (0, 0)))
    ob = _fox_sample(r3(qb), pad_rows(r3(kb)), pad_rows(r3(vb)), time_minor(cache_b_k), time_minor(cache_b_v),
                     cum[:, :, :past], jnp.pad(crow_new, ((0, 0), (0, 0), (0, -s % LANES))))
    dff = w["w_down"].shape[0]
    y, conv = _post_attention(x.reshape(b * s, d), oa, ob, _cache_mem_rows(cache_mem_k.astype(F32)),
                              _cache_mem_rows(cache_mem_v.astype(F32)), state_conv.astype(F32), w,
                              seq=s, ff_chunk=dff // 11)
    heads = lambda a: a.reshape(b, s, N_HEADS, HEAD_DIM)
    return y.reshape(b, s, d), heads(ka_f), heads(va_f), heads(kb_f), heads(vb_f), logf, conv


def kernel(x_prompt, x_sample, cache_a_k, cache_a_v, cache_b_k, cache_b_v, cache_b_logf, cache_mem_k, cache_mem_v, state_conv, mem_prompt, w_in, b_f, g_qa, g_ka, rel_bias, g_qb, g_kb, w_o, g_norm1, g_norm2, g_mem, w_mq, w_mkv, g_mq, g_mk, w_mo, g_norm3, w_up, w_conv, b_conv, w_down):
    depth = w_in.shape[0]
    xp, xs = x_prompt, x_sample
    outs_p, outs_s = [], []
    for l in range(depth):
        w = _layer_weights(l, w_in, b_f, g_qa, g_ka, rel_bias, g_qb, g_kb, w_o, g_norm1, g_norm2, g_mem, w_mq,
                           w_mkv, g_mq, g_mk, w_mo, g_norm3, w_up, w_conv, b_conv, w_down)
        bias = _band_bias(w["rel"])
        xp, *rest_p = _prompt_layer(xp, mem_prompt, w, bias)
        xs, *rest_s = _sample_layer(xs, cache_a_k[l], cache_a_v[l], cache_b_k[l], cache_b_v[l], cache_b_logf[l],
                                    cache_mem_k[l], cache_mem_v[l], state_conv[l], w, bias)
        outs_p.append(rest_p)
        outs_s.append(rest_s)
    stack = lambda outs: [jnp.stack(col, axis=0) for col in zip(*outs)]
    return (xp, xs, *stack(outs_p), *stack(outs_s))
```

```python
import functools

import jax
import jax.numpy as jnp
from jax import lax
from jax.experimental import pallas as pl
from jax.experimental.pallas import tpu as pltpu

F32 = jnp.float32
BF16 = jnp.bfloat16
EPS = 1e-6
NEG_INF = float("-inf")

HEAD_DIM = 64
N_HEADS = 8
WIDTH = N_HEADS * HEAD_DIM
CHUNK = 64
N_PREV_CHUNKS = 8
BAND_PAST = N_PREV_CHUNKS * CHUNK
REL_CLIP = 128
N_MEM_HEADS = 4
CONV_W = 3

ROW_TILE = 512
FFN_TILE = 1024
MID_TILE = 1024
PROMPT_TILE = 1024
BAND_Q = 256
BAND_WIN = BAND_PAST + BAND_Q
FOX_BLOCK = 512
FOX_KEY_BLOCKS = 2
FOX_CACHE_BLOCK = 2048
FOX_BAND = 64
MXU_COLS = 256
LANES = 128
SUBLANES = 8
BF16_ROWS = 16
VMEM_LIMIT = 56 * 1024 * 1024
LOG2E = 1.4426950408889634


def _dot(a, b):
    return jnp.dot(a, b, preferred_element_type=F32)


def _dot_nt(a, b):
    return lax.dot_general(a, b, (((1,), (1,)), ((), ())), preferred_element_type=F32)


def _rms(x, g):
    return x * lax.rsqrt(jnp.mean(x * x, axis=-1, keepdims=True) + EPS) * g


def _split3(x):
    hi = x.astype(BF16)
    r1 = x - hi.astype(F32)
    mid = r1.astype(BF16)
    lo = (r1 - mid.astype(F32)).astype(BF16)
    return hi, mid, lo


def _store_heads(ref, x):
    rows = x.shape[0]
    for h in range(N_HEADS):
        ref[pl.ds(h, rows, stride=N_HEADS), :] = x[:, h * HEAD_DIM:(h + 1) * HEAD_DIM]


def _resident(a):
    return pl.BlockSpec(a.shape, lambda i: (0,) * a.ndim, pipeline_mode=pl.Buffered(1))


def _params(*sem):
    return pltpu.CompilerParams(dimension_semantics=sem, vmem_limit_bytes=VMEM_LIMIT)


def _inproj_kernel(x_ref, g1_ref, w_ref, wf_ref, bf_ref, gains_ref, seg_ref,
                   qa_ref, ka_ref, va_ref, qb_ref, kb_ref, vb_ref,
                   kaf_ref, vaf_ref, kbf_ref, vbf_ref, logf_ref, *, tail_period, tail_rows, time_minor):
    h = _rms(x_ref[...], g1_ref[...]).astype(BF16)
    seg = seg_ref[...]

    def store_f32(ref, x):
        if time_minor:
            ref[...] = x.T
        else:
            _store_heads(ref, x)

    def proj(g):
        return _dot_nt(h, w_ref[g * WIDTH:(g + 1) * WIDTH, :])

    def head_norm(p, row):
        p2 = (p * p).astype(BF16)
        ss = jnp.concatenate([_dot(p2[:, :MXU_COLS], seg), _dot(p2[:, MXU_COLS:], seg)], axis=1)
        return p * lax.rsqrt(ss * (1.0 / HEAD_DIM) + EPS) * gains_ref[row:row + 1, :]

    is_tail = (pl.program_id(0) % tail_period) == tail_period - 1
    scale = HEAD_DIM ** -0.5 * LOG2E

    kb = head_norm(proj(4), 3)
    kb_ref[...] = kb.astype(BF16)
    store_f32(kbf_ref.at[0], kb)
    vb = proj(5)
    vb_ref[...] = vb.astype(BF16)
    store_f32(vbf_ref.at[0], vb)
    z = _dot_nt(wf_ref[...], h)[:N_HEADS] + bf_ref[...]
    logf_ref[...] = jnp.minimum(z, 0.0) - jnp.log1p(jnp.exp(-jnp.abs(z)))
    qb_ref[...] = (head_norm(proj(3), 2) * scale).astype(BF16)
    qa_ref[...] = (head_norm(proj(0), 0) * scale).astype(BF16)
    ka = head_norm(proj(1), 1)
    ka_ref[...] = ka.astype(BF16)
    va = proj(2)
    va_ref[...] = va.astype(BF16)

    @pl.when(is_tail)
    def _():
        store_f32(kaf_ref.at[0], ka[-tail_rows:])
        store_f32(vaf_ref.at[0], va[-tail_rows:])


def _in_proj(x2d, g1, w_main, w_f, b_f, gains, seg, *, tm, tail_period, tail_rows, time_minor):
    n, d = x2d.shape
    assert n % tm == 0 and (n // tm) % tail_period == 0 and (tail_rows == tm or (time_minor and tail_rows <= tm))
    n_tail = n // tm // tail_period
    row_bf = jax.ShapeDtypeStruct((n, WIDTH), BF16)
    rows = pl.BlockSpec((tm, WIDTH), lambda i: (i, 0))
    if time_minor:
        head_f = jax.ShapeDtypeStruct((n_tail, WIDTH, tm * tail_period), F32)
        tail_f = jax.ShapeDtypeStruct((n_tail, WIDTH, tail_rows), F32)
        head_rows = pl.BlockSpec((1, WIDTH, tm), lambda i: (i // tail_period, 0, i % tail_period))
        tail = pl.BlockSpec((1, WIDTH, tail_rows), lambda i: (i // tail_period, 0, 0))
    else:
        head_f = jax.ShapeDtypeStruct((n // tm, tm * N_HEADS, HEAD_DIM), F32)
        tail_f = jax.ShapeDtypeStruct((n_tail, tm * N_HEADS, HEAD_DIM), F32)
        head_rows = pl.BlockSpec((1, tm * N_HEADS, HEAD_DIM), lambda i: (i, 0, 0))
        tail = pl.BlockSpec((1, tm * N_HEADS, HEAD_DIM), lambda i: (i // tail_period, 0, 0))
    full = _resident
    return pl.pallas_call(
        functools.partial(_inproj_kernel, tail_period=tail_period, tail_rows=tail_rows, time_minor=time_minor),
        grid=(n // tm,),
        in_specs=[pl.BlockSpec((tm, d), lambda i: (i, 0)), full(g1), full(w_main), full(w_f),
                  full(b_f), full(gains), full(seg)],
        out_specs=[rows] * 6 + [tail, tail, head_rows, head_rows, pl.BlockSpec((N_HEADS, tm), lambda i: (0, i))],
        out_shape=[row_bf] * 6 + [tail_f, tail_f, head_f, head_f, jax.ShapeDtypeStruct((N_HEADS, n), F32)],
        compiler_params=_params("arbitrary"),
        name="in_proj",
    )(x2d, g1, w_main, w_f, b_f, gains, seg)


def _store_mem_rows(ref, x, hh):
    n = x.shape[0]
    for lg in range(2):
        ref[pl.ds(lg * N_MEM_HEADS + hh, n, stride=2 * N_MEM_HEADS), :] = x[:, lg * LANES:(lg + 1) * LANES]


def _memkv_kernel(mem_ref, g_ref, w_ref, gk_ref, kf_ref, vf_ref, kb_ref, vb_ref):
    d = mem_ref.shape[-1]
    hd = d // N_MEM_HEADS
    h = _rms(mem_ref[0], g_ref[...]).astype(BF16)
    kv = _dot(h, w_ref[...])
    for hh in range(N_MEM_HEADS):
        cs = slice(hh * hd, (hh + 1) * hd)
        k = _rms(kv[:, cs], gk_ref[...])
        v = kv[:, d + hh * hd:d + (hh + 1) * hd]
        _store_mem_rows(kf_ref.at[0], k, hh)
        _store_mem_rows(vf_ref.at[0], v, hh)
        kb_ref[0, :, cs] = k.astype(BF16)
        vb_ref[0, :, cs] = v.astype(BF16)


def _mem_kv(mem, g_mem, w_mkv, g_mk):
    b, n, d = mem.shape
    assert d == N_MEM_HEADS * 2 * LANES
    blk = pl.BlockSpec((1, n, d), lambda i: (i, 0, 0))
    blk_f = pl.BlockSpec((1, n * 2 * N_MEM_HEADS, LANES), lambda i: (i, 0, 0))
    full = _resident
    f = jax.ShapeDtypeStruct((b, n * 2 * N_MEM_HEADS, LANES), F32)
    h = jax.ShapeDtypeStruct((b, n, d), BF16)
    return pl.pallas_call(
        _memkv_kernel, grid=(b,),
        in_specs=[blk, full(g_mem), full(w_mkv), full(g_mk)],
        out_specs=[blk_f, blk_f, blk, blk], out_shape=[f, f, h, h],
        compiler_params=_params("arbitrary"), name="mem_kv",
    )(mem, g_mem, w_mkv, g_mk)


def _cumsum_kernel(x_ref, o_ref):
    r, t = x_ref.shape
    k = lax.broadcasted_iota(jnp.int32, (LANES, LANES), 0)
    j = lax.broadcasted_iota(jnp.int32, (LANES, LANES), 1)
    tri = jnp.where(k <= j, 1.0, 0.0).astype(BF16)
    carry = jnp.zeros((r, 1), F32)
    for blk in range(t // LANES):
        hi, mid, lo = _split3(x_ref[:, blk * LANES:(blk + 1) * LANES])
        cs = _dot(hi, tri) + _dot(mid, tri) + _dot(lo, tri)
        o_ref[:, blk * LANES:(blk + 1) * LANES] = (cs + carry) * LOG2E
        carry = carry + cs[:, LANES - 1:LANES]


def _cumsum_lanes(x):
    assert x.shape[1] % LANES == 0
    return pl.pallas_call(_cumsum_kernel, out_shape=jax.ShapeDtypeStruct(x.shape, F32), name="cumsum")(x)


def _band_bias_kernel(tbl_ref, o_ref):
    nrel = tbl_ref.shape[1]
    qblk, win = o_ref.shape[1:]
    wrap = qblk + win
    jp = lax.broadcasted_iota(jnp.int32, (nrel, wrap), 1)
    r = lax.broadcasted_iota(jnp.int32, (nrel, wrap), 0)
    d = jnp.where(jp < win, jp, jp - wrap)
    rel = jnp.clip(BAND_PAST - d, -REL_CLIP, REL_CLIP) + REL_CLIP
    onehot = jnp.where(rel == r, 1.0, 0.0).astype(BF16)
    hi, mid, lo = _split3(tbl_ref[...])
    base = (_dot(hi, onehot) + _dot(mid, onehot) + _dot(lo, onehot)) * LOG2E
    qc = lax.broadcasted_iota(jnp.int32, (qblk, win), 0) // CHUNK
    kc = lax.broadcasted_iota(jnp.int32, (qblk, win), 1) // CHUNK
    inband = (kc >= qc) & (kc <= qc + N_PREV_CHUNKS)
    for h in range(N_HEADS):
        row = jnp.broadcast_to(base[h:h + 1, :], (qblk, wrap))
        toeplitz = pltpu.roll(row, 0, 1, stride=1, stride_axis=0)
        o_ref[h] = jnp.where(inband, toeplitz[:, :win], NEG_INF)


def _band_bias(rel_table):
    nrel = 2 * REL_CLIP + 1
    pad = -nrel % LANES
    tbl = jnp.pad(rel_table.astype(F32), ((0, 0), (0, pad)))
    return pl.pallas_call(
        _band_bias_kernel, out_shape=jax.ShapeDtypeStruct((N_HEADS, BAND_Q, BAND_WIN), F32),
        name="band_bias")(tbl)


def _zero_after(x):
    bits = pltpu.bitcast(x[:SUBLANES, :LANES], jnp.uint32)
    return ((bits >> 16) >> 16)[:1].astype(F32)


def _softmax_pv(s, vs, transposed=None):
    transposed = transposed or (False,) * len(vs)
    m = jnp.max(s, axis=-1, keepdims=True)
    e = jnp.exp2(s - m)
    l = jnp.sum(e, axis=-1, keepdims=True)
    eb = e.astype(BF16)
    o, lo = None, 0
    for v, vt in zip(vs, transposed):
        keys = v.shape[1] if vt else v.shape[0]
        part = _dot_nt(eb[:, lo:lo + keys], v) if vt else _dot(eb[:, lo:lo + keys], v)
        o = part if o is None else o + part
        lo += keys
    return o * (1.0 / l)


def _band_prompt_kernel(q_ref, k0_ref, k1_ref, k2_ref, v0_ref, v1_ref, v2_ref, bias_ref, o_ref, s_sc, m_sc, qm_sc):
    i = pl.program_id(1)
    tq = q_ref.shape[1]
    k_refs, v_refs = (k0_ref, k1_ref, k2_ref), (v0_ref, v1_ref, v2_ref)
    bands = [slice(r, r + CHUNK) for r in range(0, tq, CHUNK)]
    n_groups = BAND_WIN // LANES
    per_block = tq // LANES
    group = lambda g: slice(g * LANES, (g + 1) * LANES)
    visible = lambda c: [g for g in range(n_groups) if 2 * g + 1 >= c and 2 * g <= c + N_PREV_CHUNKS]
    low = _low_half(tq)
    for h in range(N_HEADS):
        q_pair = _pair(q_ref, h)
        qm_sc[h] = jnp.where(low if h % 2 == 0 else ~low, q_pair, jnp.zeros_like(q_pair))

    def logits(early):
        for h in range(N_HEADS):
            qm = qm_sc[h]
            parts = [_dot_nt(qm, _pair(k_ref, h)) for k_ref in k_refs]
            for c, rows in enumerate(bands):
                mx = None
                for g in visible(c):
                    p, gp = divmod(g, per_block)
                    sg = parts[p][rows, group(gp)] + bias_ref[h, rows, group(g)]
                    if early and p < 2:
                        sg = jnp.where(i + p >= 2, sg, NEG_INF)
                    s_sc[h, rows, group(g)] = sg
                    mx = sg if mx is None else jnp.maximum(mx, sg)
                m_sc[h, rows] = jnp.broadcast_to(jnp.max(mx, axis=-1, keepdims=True), (CHUNK, LANES))

    values = functools.partial(_band_values, v_refs, o_ref, s_sc, m_sc, bands, visible)

    @pl.when(i >= 2)
    def _():
        logits(False)

    @pl.when(i < 2)
    def _():
        logits(True)

    values()


def _band_values(v_refs, o_ref, s_sc, m_sc, bands, visible):
    tq = o_ref.shape[1]
    n_groups = s_sc.shape[2] // LANES
    group = lambda g: slice(g * LANES, (g + 1) * LANES)
    gate = previous = None
    for h in range(N_HEADS):
        p_bands, invs = [], []
        for c, rows in enumerate(bands):
            m = m_sc[h, rows]
            if gate is not None:
                m = m + gate
            rs, ps = None, []
            for g in range(n_groups):
                if g in visible(c):
                    e = jnp.exp2(s_sc[h, rows, group(g)] - m)
                    ps.append(e.astype(BF16))
                    rs = e if rs is None else rs + e
                else:
                    ps.append(jnp.zeros((CHUNK, LANES), BF16))
            p_bands.append(jnp.concatenate(ps, axis=1))
            invs.append(1.0 / jnp.sum(rs, axis=-1, keepdims=True))
        p_all = jnp.concatenate(p_bands, axis=0)
        o = None
        for p, v_ref in enumerate(v_refs):
            part = _dot(p_all[:, p * tq:(p + 1) * tq], _pair(v_ref, h))
            o = part if o is None else o + part
        gate, previous = previous, _zero_after(o)
        half = slice((h % 2) * HEAD_DIM, (h % 2 + 1) * HEAD_DIM)
        o_ref[0, :, h * HEAD_DIM:(h + 1) * HEAD_DIM] = (o * jnp.concatenate(invs, axis=0))[:, half].astype(BF16)


def _band_prompt(qa, ka, va, bias):
    b, t, _ = qa.shape
    tq = BAND_Q
    assert t % tq == 0
    blk = lambda back: pl.BlockSpec((1, tq, WIDTH), lambda bi, i: (bi, jnp.maximum(i - back, 0), 0))
    return pl.pallas_call(
        _band_prompt_kernel, grid=(b, t // tq),
        in_specs=[blk(0), blk(2), blk(1), blk(0), blk(2), blk(1), blk(0),
                  pl.BlockSpec(bias.shape, lambda bi, i: (0, 0, 0))],
        out_specs=blk(0), out_shape=jax.ShapeDtypeStruct((b, t, WIDTH), BF16),
        scratch_shapes=[pltpu.VMEM((N_HEADS, tq, BAND_WIN), F32), pltpu.VMEM((N_HEADS, tq, LANES), F32),
                        pltpu.VMEM((N_HEADS, tq, LANES), BF16)],
        compiler_params=_params("arbitrary", "arbitrary"), name="band_prompt",
    )(qa, ka, ka, ka, va, va, va, bias)


def _band_sample_kernel(q_ref, kc_ref, vc_ref, kn_ref, vn_ref, bias_ref, o_ref):
    for h in range(N_HEADS):
        hs = slice(h * HEAD_DIM, (h + 1) * HEAD_DIM)
        q = q_ref[0, :, hs]
        s = jnp.concatenate([_dot(q, kc_ref[0, h].astype(BF16)), _dot_nt(q, kn_ref[0, :, hs])], axis=1)
        s = s + bias_ref[h, :CHUNK, :BAND_PAST + CHUNK]
        o = _softmax_pv(s, [vc_ref[0, h].astype(BF16), vn_ref[0, :, hs]], transposed=(True, False))
        o_ref[0, :, hs] = o.astype(BF16)


def _band_sample(qa, ka, va, cache_k, cache_v, bias):
    b, s, _ = qa.shape
    la = cache_k.shape[3]
    assert s == CHUNK and la == BAND_PAST
    new = pl.BlockSpec((1, s, WIDTH), lambda bi: (bi, 0, 0))
    old = pl.BlockSpec((1, N_HEADS, HEAD_DIM, la), lambda bi: (bi, 0, 0, 0))
    return pl.pallas_call(
        _band_sample_kernel, grid=(b,),
        in_specs=[new, old, old, new, new, pl.BlockSpec(bias.shape, lambda bi: (0, 0, 0))],
        out_specs=new, out_shape=jax.ShapeDtypeStruct((b, s, WIDTH), BF16),
        compiler_params=_params("arbitrary"), name="band_sample",
    )(qa, cache_k, cache_v, ka, va, bias)


def _fox_scratch(tq, tk):
    rows = lambda w: pltpu.VMEM((N_HEADS, tq, w), F32)
    return [rows(tk), rows(LANES), rows(LANES), rows(LANES), rows(LANES),
            pltpu.VMEM((N_HEADS, tq, LANES), BF16), pltpu.VMEM((N_HEADS // 2, tq, LANES), F32)]


def _low_half(rows):
    return lax.broadcasted_iota(jnp.int32, (rows, LANES), 1) < HEAD_DIM


def _fox_init(q_ref, cq_ref, sc):
    _, cq_sc, m_sc, _, l_sc, qm_sc, acc_sc = sc
    tq = cq_sc.shape[1]
    low = _low_half(tq)
    for h in range(N_HEADS):
        cq_sc[h] = jnp.broadcast_to(cq_ref[0, h:h + 1, :], (LANES, cq_ref.shape[2])).T[:tq]
        q_pair = q_ref[0, :, (h // 2) * LANES:(h // 2 + 1) * LANES]
        qm_sc[h] = jnp.where(low if h % 2 == 0 else ~low, q_pair, jnp.zeros_like(q_pair))
    m_sc[...] = jnp.full(m_sc.shape, NEG_INF, F32)
    l_sc[...] = jnp.zeros(l_sc.shape, F32)
    acc_sc[...] = jnp.zeros(acc_sc.shape, F32)


def _fox_block(qk, pv, crow_ref, tk, causal, sc):
    s_sc, cq_sc, m_sc, a_sc, l_sc, qm_sc, acc_sc = sc
    tq = s_sc.shape[1]
    band = min(tq, FOX_BAND)
    bands = [slice(r, r + band) for r in range(0, tq, band)]
    n_groups = tk // LANES
    group = lambda g: slice(g * LANES, (g + 1) * LANES)

    def visible(rows):
        if not causal:
            return [(g, False) for g in range(n_groups)]
        return [(g, (g + 1) * LANES - 1 > rows.start) for g in range(n_groups) if g * LANES < rows.stop]

    if causal:
        ahead = (lax.broadcasted_iota(jnp.int32, (band, LANES), 0)
                 - lax.broadcasted_iota(jnp.int32, (band, LANES), 1))
    for h in range(N_HEADS):
        s = qk(h, qm_sc[h])
        for rows in bands:
            cq = cq_sc[h, rows]
            mx = None
            for g, masked in visible(rows):
                sg = s[rows, group(g)] + (cq - crow_ref[0, h:h + 1, group(g)])
                if masked:
                    sg = jnp.where(ahead >= g * LANES - rows.start, sg, NEG_INF)
                s_sc[h, rows, group(g)] = sg
                mx = sg if mx is None else jnp.maximum(mx, sg)
            m_prev = m_sc[h, rows]
            m_new = jnp.maximum(m_prev, jnp.max(mx, axis=-1, keepdims=True))
            a_sc[h, rows] = jnp.exp2(m_prev - m_new)
            m_sc[h, rows] = m_new
    for h in range(N_HEADS):
        p_bands = []
        for rows in bands:
            seen = visible(rows)
            m_new = m_sc[h, rows]
            rs, ps = None, []
            for g, _ in seen:
                p = jnp.exp2(s_sc[h, rows, group(g)] - m_new)
                ps.append(p.astype(BF16))
                rs = p if rs is None else rs + p
            l_sc[h, rows] = a_sc[h, rows] * l_sc[h, rows] + rs
            ps += [jnp.zeros((band, LANES), BF16)] * (n_groups - len(seen))
            p_bands.append(jnp.concatenate(ps, axis=1))
        half = slice((h % 2) * HEAD_DIM, (h % 2 + 1) * HEAD_DIM)
        new = a_sc[h] * acc_sc[h // 2] + pv(h, jnp.concatenate(p_bands, axis=0))
        acc_sc[h // 2, :, half] = new[:, half]


def _fox_finish(o_ref, sc):
    l_sc, acc_sc = sc[4], sc[6]
    low = _low_half(acc_sc.shape[1])
    total = lambda h: jnp.sum(l_sc[h], axis=-1, keepdims=True)
    for g in range(N_HEADS // 2):
        inv = jnp.where(low, 1.0 / total(2 * g), 1.0 / total(2 * g + 1))
        o_ref[0, :, g * LANES:(g + 1) * LANES] = (acc_sc[g] * inv).astype(BF16)


def _pair(ref, h):
    return ref[0, :, (h // 2) * LANES:(h // 2 + 1) * LANES]


def _row_major_products(k_ref, v_ref):
    return (lambda h, qm: _dot_nt(qm, _pair(k_ref, h))), (lambda h, p: _dot(p, _pair(v_ref, h)))


def _fox_prompt_kernel(qi_ref, kj_ref, q_ref, k_ref, v_ref, crow_ref, cq_ref, o_ref, *sc):
    p = pl.program_id(1)
    qi, kj = qi_ref[p], kj_ref[p]
    tq = q_ref.shape[1]

    for i in range(FOX_KEY_BLOCKS):
        rows = slice(i * tq, (i + 1) * tq)
        qk, pv = _row_major_products(k_ref.at[:, rows, :], v_ref.at[:, rows, :])
        crow = crow_ref.at[:, :, rows]
        blk = FOX_KEY_BLOCKS * kj + i
        for first in ((True, False) if i == 0 else (False,)):
            started = True if i else ((kj == 0) if first else (kj > 0))

            @pl.when(started & (blk < qi))
            def _():
                if first:
                    _fox_init(q_ref, cq_ref, sc)
                _fox_block(qk, pv, crow, tq, False, sc)

            @pl.when(started & (blk == qi))
            def _():
                if first:
                    _fox_init(q_ref, cq_ref, sc)
                _fox_block(qk, pv, crow, tq, True, sc)
                _fox_finish(o_ref, sc)


def _fox_prompt(qb, kb, vb, crow):
    b, t, _ = qb.shape
    tq = FOX_BLOCK
    tk = FOX_KEY_BLOCKS * tq
    assert t % tk == 0
    pairs = [(i, j) for i in range(t // tq) for j in range(i // FOX_KEY_BLOCKS + 1)]
    qi = jnp.array([p[0] for p in pairs], jnp.int32)
    kj = jnp.array([p[1] for p in pairs], jnp.int32)
    qblk = pl.BlockSpec((1, tq, WIDTH), lambda bi, p, qi, kj: (bi, qi[p], 0))
    kblk = pl.BlockSpec((1, tk, WIDTH), lambda bi, p, qi, kj: (bi, kj[p], 0))
    grid_spec = pltpu.PrefetchScalarGridSpec(
        num_scalar_prefetch=2, grid=(b, len(pairs)),
        in_specs=[qblk, kblk, kblk,
                  pl.BlockSpec((1, N_HEADS, tk), lambda bi, p, qi, kj: (bi, 0, kj[p])),
                  pl.BlockSpec((1, N_HEADS, tq), lambda bi, p, qi, kj: (bi, 0, qi[p]))],
        out_specs=qblk, scratch_shapes=_fox_scratch(tq, tq))
    return pl.pallas_call(
        _fox_prompt_kernel, grid_spec=grid_spec, out_shape=jax.ShapeDtypeStruct((b, t, WIDTH), BF16),
        compiler_params=_params("arbitrary", "arbitrary"), name="fox_prompt",
    )(qi, kj, qb, kb, vb, crow, crow)


def _fox_sample_kernel(q_ref, kc_ref, vc_ref, kn_ref, vn_ref, crow_ref, crow_new_ref, o_ref, *sc):
    j = pl.program_id(1)
    last = pl.num_programs(1) - 1

    @pl.when(j == 0)
    def _():
        _fox_init(q_ref, crow_new_ref, sc)

    def cached_pair(ref, h):
        g = h // 2
        return ref[0, 2 * g:2 * g + 2].astype(BF16).reshape(2 * HEAD_DIM, ref.shape[3])

    _fox_block(lambda h, qm: _dot(qm, cached_pair(kc_ref, h)),
               lambda h, p: _dot_nt(p, cached_pair(vc_ref, h)), crow_ref, kc_ref.shape[3], False, sc)

    @pl.when(j == last)
    def _():
        qk, pv = _row_major_products(kn_ref, vn_ref)
        _fox_block(qk, pv, crow_new_ref, kn_ref.shape[1], True, sc)
        _fox_finish(o_ref, sc)


def _fox_sample(qb, kb, vb, cache_k, cache_v, crow, crow_new):
    b, s, _ = qb.shape
    past = cache_k.shape[3]
    tk, tn = min(FOX_CACHE_BLOCK, past), kb.shape[1]
    assert past % tk == 0 and tn % LANES == 0 and tn >= s
    nk = past // tk
    new = lambda r: pl.BlockSpec((1, r, WIDTH), lambda bi, j: (bi, 0, 0))
    old = pl.BlockSpec((1, N_HEADS, HEAD_DIM, tk), lambda bi, j: (bi, 0, 0, j))
    return pl.pallas_call(
        _fox_sample_kernel, grid=(b, nk),
        in_specs=[new(s), old, old, new(tn), new(tn),
                  pl.BlockSpec((1, N_HEADS, tk), lambda bi, j: (bi, 0, j)),
                  pl.BlockSpec((1, N_HEADS, tn), lambda bi, j: (bi, 0, 0))],
        out_specs=new(s), out_shape=jax.ShapeDtypeStruct((b, s, WIDTH), BF16),
        scratch_shapes=_fox_scratch(s, tk),
        compiler_params=_params("arbitrary", "arbitrary"), name="fox_sample",
    )(qb, cache_k, cache_v, kb, vb, crow, crow_new)


def _mid_kernel(x_ref, oa_ref, ob_ref, wo_ref, g2_ref, wq_ref, gq_ref, mk_ref, mv_ref, wmo_ref, o_ref,
                om_sc, *, seg):
    tm, d = x_ref.shape
    hd = d // N_MEM_HEADS
    x1 = x_ref[...] + _dot(oa_ref[...], wo_ref[:WIDTH, :]) + _dot(ob_ref[...], wo_ref[WIDTH:, :])
    h2 = _rms(x1, g2_ref[...]).astype(BF16)
    qm = _dot(h2, wq_ref[...])
    scale = hd ** -0.5 * LOG2E
    for hh in range(N_MEM_HEADS):
        cs = slice(hh * hd, (hh + 1) * hd)
        q = (_rms(qm[:, cs], gq_ref[...]) * scale).astype(BF16)
        def mem_head(ref, sg):
            if ref.shape[2] == d:
                return ref[sg, :, cs]
            rows = lambda lg: ref[sg, pl.ds(lg * N_MEM_HEADS + hh, ref.shape[1] // (2 * N_MEM_HEADS),
                                            stride=2 * N_MEM_HEADS), :]
            return jnp.concatenate([rows(0), rows(1)], axis=1).astype(BF16)

        for sg in range(tm // seg):
            rs = slice(sg * seg, (sg + 1) * seg)
            s = _dot_nt(q[rs], mem_head(mk_ref, sg))
            om_sc[rs, cs] = _softmax_pv(s, [mem_head(mv_ref, sg)]).astype(BF16)
    o_ref[...] = x1 + _dot(om_sc[...], wmo_ref[...])


def _cache_mem_rows(x):
    b, n, nh, hd = x.shape
    assert nh == N_MEM_HEADS and hd == 2 * LANES
    return x.reshape(b, n, nh, 2, LANES).transpose(0, 1, 3, 2, 4).reshape(b, n * 2 * nh, LANES)


def _mid(x2d, oa, ob, w_o, g2, w_mq, g_mq, mk, mv, w_mo, *, seq):
    n, d = x2d.shape
    tm = MID_TILE if seq % MID_TILE == 0 else ROW_TILE
    seg = min(seq, tm)
    tiles_per_batch = seq // seg
    nb = tm // seg
    assert n % tm == 0 and seq % seg == 0 and tm % seg == 0
    rows = lambda w: pl.BlockSpec((tm, w), lambda i: (i, 0))
    full = _resident
    mem = pl.BlockSpec((nb,) + mk.shape[1:], lambda i: (i // tiles_per_batch, 0, 0))
    return pl.pallas_call(
        functools.partial(_mid_kernel, seg=seg), grid=(n // tm,),
        in_specs=[rows(d), rows(WIDTH), rows(WIDTH), full(w_o), full(g2), full(w_mq), full(g_mq), mem, mem,
                  full(w_mo)],
        out_specs=rows(d), out_shape=jax.ShapeDtypeStruct((n, d), F32),
        scratch_shapes=[pltpu.VMEM((tm, d), BF16)],
        compiler_params=_params("arbitrary"), name="mid",
    )(x2d, oa, ob, w_o, g2, w_mq, g_mq, mk, mv, w_mo)


def _ffn_kernel(x_ref, g3_ref, wup_ref, wc_ref, bc_ref, wd_ref, st_ref, o_ref, last_ref, carry_sc, y_sc,
                *, seg, tiles_per_batch, ff_chunk):
    tm, d = x_ref.shape
    dff = wd_ref.shape[0]
    i = pl.program_id(0)
    h3 = _rms(x_ref[...], g3_ref[...]).astype(BF16)
    if tiles_per_batch > 1:
        @pl.when(i == 0)
        def _():
            carry_sc[...] = jnp.zeros(carry_sc.shape, F32)
    rowid = lax.broadcasted_iota(jnp.int32, (SUBLANES, 1), 0)
    first_tile = (i % tiles_per_batch) == 0
    for c in range(dff // ff_chunk):
        cs = slice(c * ff_chunk, (c + 1) * ff_chunk)
        gate = _dot(h3, wup_ref[:, cs])
        val = _dot(h3, wup_ref[:, dff + c * ff_chunk:dff + (c + 1) * ff_chunk])
        w0, w1, w2 = wc_ref[0:1, cs], wc_ref[1:2, cs], wc_ref[2:3, cs]
        for sg in range(tm // seg):
            rows = slice(sg * seg, (sg + 1) * seg)
            g = gate[rows]
            if tiles_per_batch == 1:
                prev = st_ref[sg, :, cs]
            else:
                prev = jnp.where(first_tile, st_ref[0, :, cs], carry_sc[SUBLANES - 2:, cs])
            p2, p1 = prev[0:1], prev[1:2]
            s1 = pltpu.roll(g, 1, 0)
            s2 = pltpu.roll(g, 2, 0)
            top1 = jnp.where(rowid == 0, p1, s1[:SUBLANES])
            top2 = jnp.where(rowid == 0, p2, jnp.where(rowid == 1, p1, s2[:SUBLANES]))
            s1 = jnp.concatenate([top1, s1[SUBLANES:]], axis=0)
            s2 = jnp.concatenate([top2, s2[SUBLANES:]], axis=0)
            conv = s2 * w0 + s1 * w1 + g * w2 + bc_ref[:, cs]
            y_sc[rows, cs] = (conv * jax.nn.sigmoid(conv) * val[rows]).astype(BF16)
            last_ref[sg, :, cs] = g[seg - SUBLANES:]
        if tiles_per_batch > 1:
            carry_sc[:, cs] = gate[tm - SUBLANES:]
    o_ref[...] = x_ref[...] + _dot(y_sc[...], wd_ref[...])


def _conv_ffn(x2d, g3, w_up, w_conv, b_conv, w_down, state, *, seq, ff_chunk):
    n, d = x2d.shape
    dff = w_down.shape[0]
    tm = min(FFN_TILE, n)
    seg = min(seq, tm)
    tiles_per_batch = seq // seg
    nb = tm // seg
    assert n % tm == 0 and seq % seg == 0 and tm % seg == 0 and dff % ff_chunk == 0 and seg >= SUBLANES
    rows = pl.BlockSpec((tm, d), lambda i: (i, 0))
    full = _resident
    per_batch = lambda r: pl.BlockSpec((nb, r, dff), lambda i: (i // tiles_per_batch, 0, 0))
    return pl.pallas_call(
        functools.partial(_ffn_kernel, seg=seg, tiles_per_batch=tiles_per_batch, ff_chunk=ff_chunk),
        grid=(n // tm,),
        in_specs=[rows, full(g3), full(w_up), full(w_conv), full(b_conv), full(w_down), per_batch(CONV_W - 1)],
        out_specs=[rows, per_batch(SUBLANES)],
        out_shape=[jax.ShapeDtypeStruct((n, d), F32), jax.ShapeDtypeStruct((n // seq, SUBLANES, dff), F32)],
        scratch_shapes=[pltpu.VMEM((SUBLANES, dff), F32), pltpu.VMEM((tm, dff), BF16)],
        compiler_params=_params("arbitrary"), name="conv_ffn",
    )(x2d, g3, w_up, w_conv, b_conv, w_down, state)


def _row(v):
    return v.astype(F32).reshape(1, -1)


def _layer_weights(l, w_in, b_f, g_qa, g_ka, rel_bias, g_qb, g_kb, w_o, g_norm1, g_norm2, g_mem, w_mq, w_mkv,
                   g_mq, g_mk, w_mo, g_norm3, w_up, w_conv, b_conv, w_down):
    n_main = 6 * WIDTH
    tile = lambda g: jnp.tile(g[l].astype(F32), N_HEADS)
    blk = jnp.arange(MXU_COLS) // HEAD_DIM
    return dict(
        w_main=w_in[l].T[:n_main].astype(BF16),
        w_f=jnp.pad(w_in[l][:, n_main:].T, ((0, BF16_ROWS - N_HEADS), (0, 0))).astype(BF16),
        b_f=b_f[l].astype(F32).reshape(N_HEADS, 1),
        gains=jnp.stack([tile(g_qa), tile(g_ka), tile(g_qb), tile(g_kb)]),
        seg=(blk[:, None] == blk[None, :]).astype(BF16),
        rel=rel_bias[l],
        w_o=w_o[l].astype(BF16), g1=_row(g_norm1[l]), g2=_row(g_norm2[l]), g3=_row(g_norm3[l]),
        g_mem=_row(g_mem[l]), w_mq=w_mq[l].astype(BF16), w_mkv=w_mkv[l].astype(BF16),
        g_mq=_row(g_mq[l]), g_mk=_row(g_mk[l]), w_mo=w_mo[l].astype(BF16),
        w_up=w_up[l].astype(BF16), w_conv=w_conv[l].astype(F32), b_conv=_row(b_conv[l]),
        w_down=w_down[l].astype(BF16))


def _post_attention(x2d, oa, ob, mk, mv, state, w, *, seq, ff_chunk):
    n, d = x2d.shape
    x2 = _mid(x2d, oa.reshape(n, WIDTH), ob.reshape(n, WIDTH), w["w_o"], w["g2"], w["w_mq"], w["g_mq"],
              mk, mv, w["w_mo"], seq=seq)
    y, last = _conv_ffn(x2, w["g3"], w["w_up"], w["w_conv"], w["b_conv"], w["w_down"], state,
                        seq=seq, ff_chunk=ff_chunk)
    return y, last[:, SUBLANES - (CONV_W - 1):]


def _prompt_layer(x, mem, w, bias):
    b, t, d = x.shape
    la = min(BAND_PAST, t)
    tm = PROMPT_TILE if t % PROMPT_TILE == 0 else ROW_TILE
    assert la <= tm and t % tm == 0
    qa, ka, va, qb, kb, vb, ka_tail, va_tail, kb_f, vb_f, logf = _in_proj(
        x.reshape(b * t, d), w["g1"], w["w_main"], w["w_f"], w["b_f"], w["gains"], w["seg"],
        tm=tm, tail_period=t // tm, tail_rows=la, time_minor=True)
    r3 = lambda a: a.reshape(b, t, WIDTH)
    oa = _band_prompt(r3(qa), r3(ka), r3(va), bias)
    logf = logf.reshape(N_HEADS, b, t).transpose(1, 0, 2)
    crow = _cumsum_lanes(logf.reshape(b * N_HEADS, t)).reshape(b, N_HEADS, t)
    logf = logf.transpose(0, 2, 1)
    ob = _fox_prompt(r3(qb), r3(kb), r3(vb), crow)
    mk_f, mv_f, mk, mv = _mem_kv(mem, w["g_mem"], w["w_mkv"], w["g_mk"])
    dff = w["w_down"].shape[0]
    y, conv = _post_attention(x.reshape(b * t, d), oa, ob, mk, mv, jnp.zeros((b, CONV_W - 1, dff), F32), w,
                              seq=t, ff_chunk=dff // 11)
    heads = lambda a, n: a.reshape(b, N_HEADS, HEAD_DIM, n).transpose(0, 3, 1, 2)
    mem_heads = lambda a: (a.reshape(b, mem.shape[1], 2, N_MEM_HEADS, LANES).transpose(0, 1, 3, 2, 4)
                           .reshape(b, mem.shape[1], N_MEM_HEADS, d // N_MEM_HEADS))
    return (y.reshape(b, t, d), heads(ka_tail, la), heads(va_tail, la), heads(kb_f, t), heads(vb_f, t), logf,
            mem_heads(mk_f), mem_heads(mv_f), conv)


def _sample_layer(x, cache_a_k, cache_a_v, cache_b_k, cache_b_v, cache_b_logf, cache_mem_k, cache_mem_v,
                  state_conv, w, bias):
    b, s, d = x.shape
    past = cache_b_k.shape[1]
    assert ROW_TILE % s == 0 and (b * s) % ROW_TILE == 0
    qa, ka, va, qb, kb, vb, ka_f, va_f, kb_f, vb_f, logf = _in_proj(
        x.reshape(b * s, d), w["g1"], w["w_main"], w["w_f"], w["b_f"], w["gains"], w["seg"], tm=ROW_TILE,
        tail_period=1, tail_rows=ROW_TILE, time_minor=False)
    r3 = lambda a: a.reshape(b, s, WIDTH)
    time_minor = lambda a: a.transpose(0, 2, 3, 1)
    oa = _band_sample(r3(qa), r3(ka), r3(va), time_minor(cache_a_k), time_minor(cache_a_v), bias)
    logf = logf.reshape(N_HEADS, b, s).transpose(1, 0, 2)
    total = past + s
    all_logf = jnp.concatenate([cache_b_logf.astype(F32).transpose(0, 2, 1), logf], axis=2)
    all_logf = jnp.pad(all_logf, ((0, 0), (0, 0), (0, -total % LANES)))
    logf = logf.transpose(0, 2, 1)
    cum = _cumsum_lanes(all_logf.reshape(b * N_HEADS, -1)).reshape(b, N_HEADS, -1)
    crow_new = cum[:, :, past:total]
    pad_rows = lambda a: jnp.pad(a, ((0, 0), (0, -s % LANES), (0, 0)))
    ob = _fox_sample(r3(qb), pad_rows(r3(kb)), pad_rows(r3(vb)), time_minor(cache_b_k), time_minor(cache_b_v),
                     cum[:, :, :past], jnp.pad(crow_new, ((0, 0), (0, 0), (0, -s % LANES))))
    dff = w["w_down"].shape[0]
    y, conv = _post_attention(x.reshape(b * s, d), oa, ob, _cache_mem_rows(cache_mem_k.astype(F32)),
                              _cache_mem_rows(cache_mem_v.astype(F32)), state_conv.astype(F32), w,
                              seq=s, ff_chunk=dff // 11)
    heads = lambda a: a.reshape(b, s, N_HEADS, HEAD_DIM)
    return y.reshape(b, s, d), heads(ka_f), heads(va_f), heads(kb_f), heads(vb_f), logf, conv


def kernel(x_prompt, x_sample, cache_a_k, cache_a_v, cache_b_k, cache_b_v, cache_b_logf, cache_mem_k, cache_mem_v, state_conv, mem_prompt, w_in, b_f, g_qa, g_ka, rel_bias, g_qb, g_kb, w_o, g_norm1, g_norm2, g_mem, w_mq, w_mkv, g_mq, g_mk, w_mo, g_norm3, w_up, w_conv, b_conv, w_down):
    depth = w_in.shape[0]
    xp, xs = x_prompt, x_sample
    outs_p, outs_s = [], []
    for l in range(depth):
        w = _layer_weights(l, w_in, b_f, g_qa, g_ka, rel_bias, g_qb, g_kb, w_o, g_norm1, g_norm2, g_mem, w_mq,
                           w_mkv, g_mq, g_mk, w_mo, g_norm3, w_up, w_conv, b_conv, w_down)
        bias = _band_bias(w["rel"])
        xp, *rest_p = _prompt_layer(xp, mem_prompt, w, bias)
        xs, *rest_s = _sample_layer(xs, cache_a_k[l], cache_a_v[l], cache_b_k[l], cache_b_v[l], cache_b_logf[l],
                                    cache_mem_k[l], cache_mem_v[l], state_conv[l], w, bias)
        outs_p.append(rest_p)
        outs_s.append(rest_s)
    stack = lambda outs: [jnp.stack(col, axis=0) for col in zip(*outs)]
    return (xp, xs, *stack(outs_p), *stack(outs_s))
```

```python
import functools

import jax
import jax.numpy as jnp
from jax import lax
from jax.experimental import pallas as pl
from jax.experimental.pallas import tpu as pltpu

F32 = jnp.float32
BF16 = jnp.bfloat16
EPS = 1e-6
NEG_INF = float("-inf")

HEAD_DIM = 64
N_HEADS = 8
WIDTH = N_HEADS * HEAD_DIM
CHUNK = 64
N_PREV_CHUNKS = 8
BAND_PAST = N_PREV_CHUNKS * CHUNK
REL_CLIP = 128
N_MEM_HEADS = 4
CONV_W = 3

ROW_TILE = 512
FFN_TILE = 1024
MID_TILE = 1024
PROMPT_TILE = 1024
BAND_Q = 256
BAND_WIN = BAND_PAST + BAND_Q
FOX_BLOCK = 512
FOX_KEY_BLOCKS = 2
FOX_CACHE_BLOCK = 2048
FOX_BAND = 64
MXU_COLS = 256
LANES = 128
SUBLANES = 8
BF16_ROWS = 16
VMEM_LIMIT = 56 * 1024 * 1024
LOG2E = 1.4426950408889634


def _dot(a, b):
    return jnp.dot(a, b, preferred_element_type=F32)


def _dot_nt(a, b):
    return lax.dot_general(a, b, (((1,), (1,)), ((), ())), preferred_element_type=F32)


def _rms(x, g):
    return x * lax.rsqrt(jnp.mean(x * x, axis=-1, keepdims=True) + EPS) * g


def _split3(x):
    hi = x.astype(BF16)
    r1 = x - hi.astype(F32)
    mid = r1.astype(BF16)
    lo = (r1 - mid.astype(F32)).astype(BF16)
    return hi, mid, lo


def _store_heads(ref, x):
    rows = x.shape[0]
    for h in range(N_HEADS):
        ref[pl.ds(h, rows, stride=N_HEADS), :] = x[:, h * HEAD_DIM:(h + 1) * HEAD_DIM]


def _resident(a):
    return pl.BlockSpec(a.shape, lambda i: (0,) * a.ndim, pipeline_mode=pl.Buffered(1))


def _params(*sem):
    return pltpu.CompilerParams(dimension_semantics=sem, vmem_limit_bytes=VMEM_LIMIT)


def _inproj_kernel(x_ref, g1_ref, w_ref, wf_ref, bf_ref, gains_ref, seg_ref,
                   qa_ref, ka_ref, va_ref, qb_ref, kb_ref, vb_ref,
                   kaf_ref, vaf_ref, kbf_ref, vbf_ref, logf_ref, *, tail_period, tail_rows, time_minor):
    h = _rms(x_ref[...], g1_ref[...]).astype(BF16)
    seg = seg_ref[...]

    def store_f32(ref, x):
        if time_minor:
            ref[...] = x.T
        else:
            _store_heads(ref, x)

    def proj(g):
        return _dot_nt(h, w_ref[g * WIDTH:(g + 1) * WIDTH, :])

    def head_norm(p, row):
        p2 = (p * p).astype(BF16)
        ss = jnp.concatenate([_dot(p2[:, :MXU_COLS], seg), _dot(p2[:, MXU_COLS:], seg)], axis=1)
        return p * lax.rsqrt(ss * (1.0 / HEAD_DIM) + EPS) * gains_ref[row:row + 1, :]

    is_tail = (pl.program_id(0) % tail_period) == tail_period - 1
    scale = HEAD_DIM ** -0.5 * LOG2E

    kb = head_norm(proj(4), 3)
    kb_ref[...] = kb.astype(BF16)
    store_f32(kbf_ref.at[0], kb)
    vb = proj(5)
    vb_ref[...] = vb.astype(BF16)
    store_f32(vbf_ref.at[0], vb)
    z = _dot_nt(wf_ref[...], h)[:N_HEADS] + bf_ref[...]
    logf_ref[...] = jnp.minimum(z, 0.0) - jnp.log1p(jnp.exp(-jnp.abs(z)))
    qb_ref[...] = (head_norm(proj(3), 2) * scale).astype(BF16)
    qa_ref[...] = (head_norm(proj(0), 0) * scale).astype(BF16)
    ka = head_norm(proj(1), 1)
    ka_ref[...] = ka.astype(BF16)
    va = proj(2)
    va_ref[...] = va.astype(BF16)

    @pl.when(is_tail)
    def _():
        store_f32(kaf_ref.at[0], ka[-tail_rows:])
        store_f32(vaf_ref.at[0], va[-tail_rows:])


def _in_proj(x2d, g1, w_main, w_f, b_f, gains, seg, *, tm, tail_period, tail_rows, time_minor):
    n, d = x2d.shape
    assert n % tm == 0 and (n // tm) % tail_period == 0 and (tail_rows == tm or (time_minor and tail_rows <= tm))
    n_tail = n // tm // tail_period
    row_bf = jax.ShapeDtypeStruct((n, WIDTH), BF16)
    rows = pl.BlockSpec((tm, WIDTH), lambda i: (i, 0))
    if time_minor:
        head_f = jax.ShapeDtypeStruct((n_tail, WIDTH, tm * tail_period), F32)
        tail_f = jax.ShapeDtypeStruct((n_tail, WIDTH, tail_rows), F32)
        head_rows = pl.BlockSpec((1, WIDTH, tm), lambda i: (i // tail_period, 0, i % tail_period))
        tail = pl.BlockSpec((1, WIDTH, tail_rows), lambda i: (i // tail_period, 0, 0))
    else:
        head_f = jax.ShapeDtypeStruct((n // tm, tm * N_HEADS, HEAD_DIM), F32)
        tail_f = jax.ShapeDtypeStruct((n_tail, tm * N_HEADS, HEAD_DIM), F32)
        head_rows = pl.BlockSpec((1, tm * N_HEADS, HEAD_DIM), lambda i: (i, 0, 0))
        tail = pl.BlockSpec((1, tm * N_HEADS, HEAD_DIM), lambda i: (i // tail_period, 0, 0))
    full = _resident
    return pl.pallas_call(
        functools.partial(_inproj_kernel, tail_period=tail_period, tail_rows=tail_rows, time_minor=time_minor),
        grid=(n // tm,),
        in_specs=[pl.BlockSpec((tm, d), lambda i: (i, 0)), full(g1), full(w_main), full(w_f),
                  full(b_f), full(gains), full(seg)],
        out_specs=[rows] * 6 + [tail, tail, head_rows, head_rows, pl.BlockSpec((N_HEADS, tm), lambda i: (0, i))],
        out_shape=[row_bf] * 6 + [tail_f, tail_f, head_f, head_f, jax.ShapeDtypeStruct((N_HEADS, n), F32)],
        compiler_params=_params("arbitrary"),
        name="in_proj",
    )(x2d, g1, w_main, w_f, b_f, gains, seg)


def _store_mem_rows(ref, x, hh):
    n = x.shape[0]
    for lg in range(2):
        ref[pl.ds(lg * N_MEM_HEADS + hh, n, stride=2 * N_MEM_HEADS), :] = x[:, lg * LANES:(lg + 1) * LANES]


def _memkv_kernel(mem_ref, g_ref, w_ref, gk_ref, kf_ref, vf_ref, kb_ref, vb_ref):
    d = mem_ref.shape[-1]
    hd = d // N_MEM_HEADS
    h = _rms(mem_ref[0], g_ref[...]).astype(BF16)
    kv = _dot(h, w_ref[...])
    for hh in range(N_MEM_HEADS):
        cs = slice(hh * hd, (hh + 1) * hd)
        k = _rms(kv[:, cs], gk_ref[...])
        v = kv[:, d + hh * hd:d + (hh + 1) * hd]
        _store_mem_rows(kf_ref.at[0], k, hh)
        _store_mem_rows(vf_ref.at[0], v, hh)
        kb_ref[0, :, cs] = k.astype(BF16)
        vb_ref[0, :, cs] = v.astype(BF16)


def _mem_kv(mem, g_mem, w_mkv, g_mk):
    b, n, d = mem.shape
    assert d == N_MEM_HEADS * 2 * LANES
    blk = pl.BlockSpec((1, n, d), lambda i: (i, 0, 0))
    blk_f = pl.BlockSpec((1, n * 2 * N_MEM_HEADS, LANES), lambda i: (i, 0, 0))
    full = _resident
    f = jax.ShapeDtypeStruct((b, n * 2 * N_MEM_HEADS, LANES), F32)
    h = jax.ShapeDtypeStruct((b, n, d), BF16)
    return pl.pallas_call(
        _memkv_kernel, grid=(b,),
        in_specs=[blk, full(g_mem), full(w_mkv), full(g_mk)],
        out_specs=[blk_f, blk_f, blk, blk], out_shape=[f, f, h, h],
        compiler_params=_params("arbitrary"), name="mem_kv",
    )(mem, g_mem, w_mkv, g_mk)


def _cumsum_kernel(x_ref, o_ref):
    r, t = x_ref.shape
    k = lax.broadcasted_iota(jnp.int32, (LANES, LANES), 0)
    j = lax.broadcasted_iota(jnp.int32, (LANES, LANES), 1)
    tri = jnp.where(k <= j, 1.0, 0.0).astype(BF16)
    carry = jnp.zeros((r, 1), F32)
    for blk in range(t // LANES):
        hi, mid, lo = _split3(x_ref[:, blk * LANES:(blk + 1) * LANES])
        cs = _dot(hi, tri) + _dot(mid, tri) + _dot(lo, tri)
        o_ref[:, blk * LANES:(blk + 1) * LANES] = (cs + carry) * LOG2E
        carry = carry + cs[:, LANES - 1:LANES]


def _cumsum_lanes(x):
    assert x.shape[1] % LANES == 0
    return pl.pallas_call(_cumsum_kernel, out_shape=jax.ShapeDtypeStruct(x.shape, F32), name="cumsum")(x)


def _band_bias_kernel(tbl_ref, o_ref):
    nrel = tbl_ref.shape[1]
    qblk, win = o_ref.shape[1:]
    wrap = qblk + win
    jp = lax.broadcasted_iota(jnp.int32, (nrel, wrap), 1)
    r = lax.broadcasted_iota(jnp.int32, (nrel, wrap), 0)
    d = jnp.where(jp < win, jp, jp - wrap)
    rel = jnp.clip(BAND_PAST - d, -REL_CLIP, REL_CLIP) + REL_CLIP
    onehot = jnp.where(rel == r, 1.0, 0.0).astype(BF16)
    hi, mid, lo = _split3(tbl_ref[...])
    base = (_dot(hi, onehot) + _dot(mid, onehot) + _dot(lo, onehot)) * LOG2E
    qc = lax.broadcasted_iota(jnp.int32, (qblk, win), 0) // CHUNK
    kc = lax.broadcasted_iota(jnp.int32, (qblk, win), 1) // CHUNK
    inband = (kc >= qc) & (kc <= qc + N_PREV_CHUNKS)
    for h in range(N_HEADS):
        row = jnp.broadcast_to(base[h:h + 1, :], (qblk, wrap))
        toeplitz = pltpu.roll(row, 0, 1, stride=1, stride_axis=0)
        o_ref[h] = jnp.where(inband, toeplitz[:, :win], NEG_INF)


def _band_bias(rel_table):
    nrel = 2 * REL_CLIP + 1
    pad = -nrel % LANES
    tbl = jnp.pad(rel_table.astype(F32), ((0, 0), (0, pad)))
    return pl.pallas_call(
        _band_bias_kernel, out_shape=jax.ShapeDtypeStruct((N_HEADS, BAND_Q, BAND_WIN), F32),
        name="band_bias")(tbl)


def _zero_after(x):
    bits = pltpu.bitcast(x[:SUBLANES, :LANES], jnp.uint32)
    return ((bits >> 16) >> 16)[:1].astype(F32)


def _softmax_pv(s, vs, transposed=None):
    transposed = transposed or (False,) * len(vs)
    m = jnp.max(s, axis=-1, keepdims=True)
    e = jnp.exp2(s - m)
    l = jnp.sum(e, axis=-1, keepdims=True)
    eb = e.astype(BF16)
    o, lo = None, 0
    for v, vt in zip(vs, transposed):
        keys = v.shape[1] if vt else v.shape[0]
        part = _dot_nt(eb[:, lo:lo + keys], v) if vt else _dot(eb[:, lo:lo + keys], v)
        o = part if o is None else o + part
        lo += keys
    return o * (1.0 / l)


def _band_prompt_kernel(q_ref, k0_ref, k1_ref, k2_ref, v0_ref, v1_ref, v2_ref, bias_ref, o_ref, s_sc, m_sc, qm_sc):
    i = pl.program_id(1)
    tq = q_ref.shape[1]
    k_refs, v_refs = (k0_ref, k1_ref, k2_ref), (v0_ref, v1_ref, v2_ref)
    bands = [slice(r, r + CHUNK) for r in range(0, tq, CHUNK)]
    n_groups = BAND_WIN // LANES
    per_block = tq // LANES
    group = lambda g: slice(g * LANES, (g + 1) * LANES)
    visible = lambda c: [g for g in range(n_groups) if 2 * g + 1 >= c and 2 * g <= c + N_PREV_CHUNKS]
    low = _low_half(tq)
    for h in range(N_HEADS):
        q_pair = _pair(q_ref, h)
        qm_sc[h] = jnp.where(low if h % 2 == 0 else ~low, q_pair, jnp.zeros_like(q_pair))

    def logits(early):
        for h in range(N_HEADS):
            qm = qm_sc[h]
            parts = [_dot_nt(qm, _pair(k_ref, h)) for k_ref in k_refs]
            for c, rows in enumerate(bands):
                mx = None
                for g in visible(c):
                    p, gp = divmod(g, per_block)
                    sg = parts[p][rows, group(gp)] + bias_ref[h, rows, group(g)]
                    if early and p < 2:
                        sg = jnp.where(i + p >= 2, sg, NEG_INF)
                    s_sc[h, rows, group(g)] = sg
                    mx = sg if mx is None else jnp.maximum(mx, sg)
                m_sc[h, rows] = jnp.broadcast_to(jnp.max(mx, axis=-1, keepdims=True), (CHUNK, LANES))

    values = functools.partial(_band_values, v_refs, o_ref, s_sc, m_sc, bands, visible)

    @pl.when(i >= 2)
    def _():
        logits(False)

    @pl.when(i < 2)
    def _():
        logits(True)

    values()


def _band_values(v_refs, o_ref, s_sc, m_sc, bands, visible):
    tq = o_ref.shape[1]
    n_groups = s_sc.shape[2] // LANES
    group = lambda g: slice(g * LANES, (g + 1) * LANES)
    gate = previous = None
    for h in range(N_HEADS):
        p_bands, invs = [], []
        for c, rows in enumerate(bands):
            m = m_sc[h, rows]
            if gate is not None:
                m = m + gate
            rs, ps = None, []
            for g in range(n_groups):
                if g in visible(c):
                    e = jnp.exp2(s_sc[h, rows, group(g)] - m)
                    ps.append(e.astype(BF16))
                    rs = e if rs is None else rs + e
                else:
                    ps.append(jnp.zeros((CHUNK, LANES), BF16))
            p_bands.append(jnp.concatenate(ps, axis=1))
            invs.append(1.0 / jnp.sum(rs, axis=-1, keepdims=True))
        p_all = jnp.concatenate(p_bands, axis=0)
        o = None
        for p, v_ref in enumerate(v_refs):
            part = _dot(p_all[:, p * tq:(p + 1) * tq], _pair(v_ref, h))
            o = part if o is None else o + part
        gate, previous = previous, _zero_after(o)
        half = slice((h % 2) * HEAD_DIM, (h % 2 + 1) * HEAD_DIM)
        o_ref[0, :, h * HEAD_DIM:(h + 1) * HEAD_DIM] = (o * jnp.concatenate(invs, axis=0))[:, half].astype(BF16)


def _band_prompt(qa, ka, va, bias):
    b, t, _ = qa.shape
    tq = BAND_Q
    assert t % tq == 0
    blk = lambda back: pl.BlockSpec((1, tq, WIDTH), lambda bi, i: (bi, jnp.maximum(i - back, 0), 0))
    return pl.pallas_call(
        _band_prompt_kernel, grid=(b, t // tq),
        in_specs=[blk(0), blk(2), blk(1), blk(0), blk(2), blk(1), blk(0),
                  pl.BlockSpec(bias.shape, lambda bi, i: (0, 0, 0))],
        out_specs=blk(0), out_shape=jax.ShapeDtypeStruct((b, t, WIDTH), BF16),
        scratch_shapes=[pltpu.VMEM((N_HEADS, tq, BAND_WIN), F32), pltpu.VMEM((N_HEADS, tq, LANES), F32),
                        pltpu.VMEM((N_HEADS, tq, LANES), BF16)],
        compiler_params=_params("parallel", "parallel"), name="band_prompt",
    )(qa, ka, ka, ka, va, va, va, bias)


def _band_sample_kernel(q_ref, kc_ref, vc_ref, kn_ref, vn_ref, bias_ref, o_ref):
    for h in range(N_HEADS):
        hs = slice(h * HEAD_DIM, (h + 1) * HEAD_DIM)
        q = q_ref[0, :, hs]
        s = jnp.concatenate([_dot(q, kc_ref[0, h].astype(BF16)), _dot_nt(q, kn_ref[0, :, hs])], axis=1)
        s = s + bias_ref[h, :CHUNK, :BAND_PAST + CHUNK]
        o = _softmax_pv(s, [vc_ref[0, h].astype(BF16), vn_ref[0, :, hs]], transposed=(True, False))
        o_ref[0, :, hs] = o.astype(BF16)


def _band_sample(qa, ka, va, cache_k, cache_v, bias):
    b, s, _ = qa.shape
    la = cache_k.shape[3]
    assert s == CHUNK and la == BAND_PAST
    new = pl.BlockSpec((1, s, WIDTH), lambda bi: (bi, 0, 0))
    old = pl.BlockSpec((1, N_HEADS, HEAD_DIM, la), lambda bi: (bi, 0, 0, 0))
    return pl.pallas_call(
        _band_sample_kernel, grid=(b,),
        in_specs=[new, old, old, new, new, pl.BlockSpec(bias.shape, lambda bi: (0, 0, 0))],
        out_specs=new, out_shape=jax.ShapeDtypeStruct((b, s, WIDTH), BF16),
        compiler_params=_params("arbitrary"), name="band_sample",
    )(qa, cache_k, cache_v, ka, va, bias)


def _fox_scratch(tq, tk):
    rows = lambda w: pltpu.VMEM((N_HEADS, tq, w), F32)
    return [rows(tk), rows(LANES), rows(LANES), rows(LANES), rows(LANES),
            pltpu.VMEM((N_HEADS, tq, LANES), BF16), pltpu.VMEM((N_HEADS // 2, tq, LANES), F32)]


def _low_half(rows):
    return lax.broadcasted_iota(jnp.int32, (rows, LANES), 1) < HEAD_DIM


def _fox_init(q_ref, cq_ref, sc):
    _, cq_sc, m_sc, _, l_sc, qm_sc, acc_sc = sc
    tq = cq_sc.shape[1]
    low = _low_half(tq)
    for h in range(N_HEADS):
        cq_sc[h] = jnp.broadcast_to(cq_ref[0, h:h + 1, :], (LANES, cq_ref.shape[2])).T[:tq]
        q_pair = q_ref[0, :, (h // 2) * LANES:(h // 2 + 1) * LANES]
        qm_sc[h] = jnp.where(low if h % 2 == 0 else ~low, q_pair, jnp.zeros_like(q_pair))
    m_sc[...] = jnp.full(m_sc.shape, NEG_INF, F32)
    l_sc[...] = jnp.zeros(l_sc.shape, F32)
    acc_sc[...] = jnp.zeros(acc_sc.shape, F32)


def _fox_block(qk, pv, crow_ref, tk, causal, sc):
    s_sc, cq_sc, m_sc, a_sc, l_sc, qm_sc, acc_sc = sc
    tq = s_sc.shape[1]
    band = min(tq, FOX_BAND)
    bands = [slice(r, r + band) for r in range(0, tq, band)]
    n_groups = tk // LANES
    group = lambda g: slice(g * LANES, (g + 1) * LANES)

    def visible(rows):
        if not causal:
            return [(g, False) for g in range(n_groups)]
        return [(g, (g + 1) * LANES - 1 > rows.start) for g in range(n_groups) if g * LANES < rows.stop]

    if causal:
        ahead = (lax.broadcasted_iota(jnp.int32, (band, LANES), 0)
                 - lax.broadcasted_iota(jnp.int32, (band, LANES), 1))
    for h in range(N_HEADS):
        s = qk(h, qm_sc[h])
        for rows in bands:
            cq = cq_sc[h, rows]
            mx = None
            for g, masked in visible(rows):
                sg = s[rows, group(g)] + (cq - crow_ref[0, h:h + 1, group(g)])
                if masked:
                    sg = jnp.where(ahead >= g * LANES - rows.start, sg, NEG_INF)
                s_sc[h, rows, group(g)] = sg
                mx = sg if mx is None else jnp.maximum(mx, sg)
            m_prev = m_sc[h, rows]
            m_new = jnp.maximum(m_prev, jnp.max(mx, axis=-1, keepdims=True))
            a_sc[h, rows] = jnp.exp2(m_prev - m_new)
            m_sc[h, rows] = m_new
    for h in range(N_HEADS):
        p_bands = []
        for rows in bands:
            seen = visible(rows)
            m_new = m_sc[h, rows]
            rs, ps = None, []
            for g, _ in seen:
                p = jnp.exp2(s_sc[h, rows, group(g)] - m_new)
                ps.append(p.astype(BF16))
                rs = p if rs is None else rs + p
            l_sc[h, rows] = a_sc[h, rows] * l_sc[h, rows] + rs
            ps += [jnp.zeros((band, LANES), BF16)] * (n_groups - len(seen))
            p_bands.append(jnp.concatenate(ps, axis=1))
        half = slice((h % 2) * HEAD_DIM, (h % 2 + 1) * HEAD_DIM)
        new = a_sc[h] * acc_sc[h // 2] + pv(h, jnp.concatenate(p_bands, axis=0))
        acc_sc[h // 2, :, half] = new[:, half]


def _fox_finish(o_ref, sc):
    l_sc, acc_sc = sc[4], sc[6]
    low = _low_half(acc_sc.shape[1])
    total = lambda h: jnp.sum(l_sc[h], axis=-1, keepdims=True)
    for g in range(N_HEADS // 2):
        inv = jnp.where(low, 1.0 / total(2 * g), 1.0 / total(2 * g + 1))
        o_ref[0, :, g * LANES:(g + 1) * LANES] = (acc_sc[g] * inv).astype(BF16)


def _pair(ref, h):
    return ref[0, :, (h // 2) * LANES:(h // 2 + 1) * LANES]


def _row_major_products(k_ref, v_ref):
    return (lambda h, qm: _dot_nt(qm, _pair(k_ref, h))), (lambda h, p: _dot(p, _pair(v_ref, h)))


def _fox_prompt_kernel(qi_ref, kj_ref, q_ref, k_ref, v_ref, crow_ref, cq_ref, o_ref, *sc):
    p = pl.program_id(1)
    qi, kj = qi_ref[p], kj_ref[p]
    tq = q_ref.shape[1]

    for i in range(FOX_KEY_BLOCKS):
        rows = slice(i * tq, (i + 1) * tq)
        qk, pv = _row_major_products(k_ref.at[:, rows, :], v_ref.at[:, rows, :])
        crow = crow_ref.at[:, :, rows]
        blk = FOX_KEY_BLOCKS * kj + i
        for first in ((True, False) if i == 0 else (False,)):
            started = True if i else ((kj == 0) if first else (kj > 0))

            @pl.when(started & (blk < qi))
            def _():
                if first:
                    _fox_init(q_ref, cq_ref, sc)
                _fox_block(qk, pv, crow, tq, False, sc)

            @pl.when(started & (blk == qi))
            def _():
                if first:
                    _fox_init(q_ref, cq_ref, sc)
                _fox_block(qk, pv, crow, tq, True, sc)
                _fox_finish(o_ref, sc)


def _fox_prompt(qb, kb, vb, crow):
    b, t, _ = qb.shape
    tq = FOX_BLOCK
    tk = FOX_KEY_BLOCKS * tq
    assert t % tk == 0
    pairs = [(i, j) for i in range(t // tq) for j in range(i // FOX_KEY_BLOCKS + 1)]
    qi = jnp.array([p[0] for p in pairs], jnp.int32)
    kj = jnp.array([p[1] for p in pairs], jnp.int32)
    qblk = pl.BlockSpec((1, tq, WIDTH), lambda bi, p, qi, kj: (bi, qi[p], 0))
    kblk = pl.BlockSpec((1, tk, WIDTH), lambda bi, p, qi, kj: (bi, kj[p], 0))
    grid_spec = pltpu.PrefetchScalarGridSpec(
        num_scalar_prefetch=2, grid=(b, len(pairs)),
        in_specs=[qblk, kblk, kblk,
                  pl.BlockSpec((1, N_HEADS, tk), lambda bi, p, qi, kj: (bi, 0, kj[p])),
                  pl.BlockSpec((1, N_HEADS, tq), lambda bi, p, qi, kj: (bi, 0, qi[p]))],
        out_specs=qblk, scratch_shapes=_fox_scratch(tq, tq))
    return pl.pallas_call(
        _fox_prompt_kernel, grid_spec=grid_spec, out_shape=jax.ShapeDtypeStruct((b, t, WIDTH), BF16),
        compiler_params=_params("parallel", "arbitrary"), name="fox_prompt",
    )(qi, kj, qb, kb, vb, crow, crow)


def _fox_sample_kernel(q_ref, kc_ref, vc_ref, kn_ref, vn_ref, crow_ref, crow_new_ref, o_ref, *sc):
    j = pl.program_id(1)
    last = pl.num_programs(1) - 1

    @pl.when(j == 0)
    def _():
        _fox_init(q_ref, crow_new_ref, sc)

    def cached_pair(ref, h):
        g = h // 2
        return ref[0, 2 * g:2 * g + 2].astype(BF16).reshape(2 * HEAD_DIM, ref.shape[3])

    _fox_block(lambda h, qm: _dot(qm, cached_pair(kc_ref, h)),
               lambda h, p: _dot_nt(p, cached_pair(vc_ref, h)), crow_ref, kc_ref.shape[3], False, sc)

    @pl.when(j == last)
    def _():
        qk, pv = _row_major_products(kn_ref, vn_ref)
        _fox_block(qk, pv, crow_new_ref, kn_ref.shape[1], True, sc)
        _fox_finish(o_ref, sc)


def _fox_sample(qb, kb, vb, cache_k, cache_v, crow, crow_new):
    b, s, _ = qb.shape
    past = cache_k.shape[3]
    tk, tn = min(FOX_CACHE_BLOCK, past), kb.shape[1]
    assert past % tk == 0 and tn % LANES == 0 and tn >= s
    nk = past // tk
    new = lambda r: pl.BlockSpec((1, r, WIDTH), lambda bi, j: (bi, 0, 0))
    old = pl.BlockSpec((1, N_HEADS, HEAD_DIM, tk), lambda bi, j: (bi, 0, 0, j))
    return pl.pallas_call(
        _fox_sample_kernel, grid=(b, nk),
        in_specs=[new(s), old, old, new(tn), new(tn),
                  pl.BlockSpec((1, N_HEADS, tk), lambda bi, j: (bi, 0, j)),
                  pl.BlockSpec((1, N_HEADS, tn), lambda bi, j: (bi, 0, 0))],
        out_specs=new(s), out_shape=jax.ShapeDtypeStruct((b, s, WIDTH), BF16),
        scratch_shapes=_fox_scratch(s, tk),
        compiler_params=_params("parallel", "arbitrary"), name="fox_sample",
    )(qb, cache_k, cache_v, kb, vb, crow, crow_new)


def _mid_kernel(x_ref, oa_ref, ob_ref, wo_ref, g2_ref, wq_ref, gq_ref, mk_ref, mv_ref, wmo_ref, o_ref,
                om_sc, *, seg):
    tm, d = x_ref.shape
    hd = d // N_MEM_HEADS
    x1 = x_ref[...] + _dot(oa_ref[...], wo_ref[:WIDTH, :]) + _dot(ob_ref[...], wo_ref[WIDTH:, :])
    h2 = _rms(x1, g2_ref[...]).astype(BF16)
    qm = _dot(h2, wq_ref[...])
    scale = hd ** -0.5 * LOG2E
    for hh in range(N_MEM_HEADS):
        cs = slice(hh * hd, (hh + 1) * hd)
        q = (_rms(qm[:, cs], gq_ref[...]) * scale).astype(BF16)
        def mem_head(ref, sg):
            if ref.shape[2] == d:
                return ref[sg, :, cs]
            rows = lambda lg: ref[sg, pl.ds(lg * N_MEM_HEADS + hh, ref.shape[1] // (2 * N_MEM_HEADS),
                                            stride=2 * N_MEM_HEADS), :]
            return jnp.concatenate([rows(0), rows(1)], axis=1).astype(BF16)

        for sg in range(tm // seg):
            rs = slice(sg * seg, (sg + 1) * seg)
            s = _dot_nt(q[rs], mem_head(mk_ref, sg))
            om_sc[rs, cs] = _softmax_pv(s, [mem_head(mv_ref, sg)]).astype(BF16)
    o_ref[...] = x1 + _dot(om_sc[...], wmo_ref[...])


def _cache_mem_rows(x):
    b, n, nh, hd = x.shape
    assert nh == N_MEM_HEADS and hd == 2 * LANES
    return x.reshape(b, n, nh, 2, LANES).transpose(0, 1, 3, 2, 4).reshape(b, n * 2 * nh, LANES)


def _mid(x2d, oa, ob, w_o, g2, w_mq, g_mq, mk, mv, w_mo, *, seq):
    n, d = x2d.shape
    tm = MID_TILE if seq % MID_TILE == 0 else ROW_TILE
    seg = min(seq, tm)
    tiles_per_batch = seq // seg
    nb = tm // seg
    assert n % tm == 0 and seq % seg == 0 and tm % seg == 0
    rows = lambda w: pl.BlockSpec((tm, w), lambda i: (i, 0))
    full = _resident
    mem = pl.BlockSpec((nb,) + mk.shape[1:], lambda i: (i // tiles_per_batch, 0, 0))
    return pl.pallas_call(
        functools.partial(_mid_kernel, seg=seg), grid=(n // tm,),
        in_specs=[rows(d), rows(WIDTH), rows(WIDTH), full(w_o), full(g2), full(w_mq), full(g_mq), mem, mem,
                  full(w_mo)],
        out_specs=rows(d), out_shape=jax.ShapeDtypeStruct((n, d), F32),
        scratch_shapes=[pltpu.VMEM((tm, d), BF16)],
        compiler_params=_params("parallel"), name="mid",
    )(x2d, oa, ob, w_o, g2, w_mq, g_mq, mk, mv, w_mo)


def _ffn_kernel(x_ref, g3_ref, wup_ref, wc_ref, bc_ref, wd_ref, st_ref, o_ref, last_ref, carry_sc, y_sc,
                *, seg, tiles_per_batch, ff_chunk):
    tm, d = x_ref.shape
    dff = wd_ref.shape[0]
    i = pl.program_id(0)
    h3 = _rms(x_ref[...], g3_ref[...]).astype(BF16)
    if tiles_per_batch > 1:
        @pl.when(i == 0)
        def _():
            carry_sc[...] = jnp.zeros(carry_sc.shape, F32)
    rowid = lax.broadcasted_iota(jnp.int32, (SUBLANES, 1), 0)
    first_tile = (i % tiles_per_batch) == 0
    for c in range(dff // ff_chunk):
        cs = slice(c * ff_chunk, (c + 1) * ff_chunk)
        gate = _dot(h3, wup_ref[:, cs])
        val = _dot(h3, wup_ref[:, dff + c * ff_chunk:dff + (c + 1) * ff_chunk])
        w0, w1, w2 = wc_ref[0:1, cs], wc_ref[1:2, cs], wc_ref[2:3, cs]
        for sg in range(tm // seg):
            rows = slice(sg * seg, (sg + 1) * seg)
            g = gate[rows]
            if tiles_per_batch == 1:
                prev = st_ref[sg, :, cs]
            else:
                prev = jnp.where(first_tile, st_ref[0, :, cs], carry_sc[SUBLANES - 2:, cs])
            p2, p1 = prev[0:1], prev[1:2]
            s1 = pltpu.roll(g, 1, 0)
            s2 = pltpu.roll(g, 2, 0)
            top1 = jnp.where(rowid == 0, p1, s1[:SUBLANES])
            top2 = jnp.where(rowid == 0, p2, jnp.where(rowid == 1, p1, s2[:SUBLANES]))
            s1 = jnp.concatenate([top1, s1[SUBLANES:]], axis=0)
            s2 = jnp.concatenate([top2, s2[SUBLANES:]], axis=0)
            conv = s2 * w0 + s1 * w1 + g * w2 + bc_ref[:, cs]
            y_sc[rows, cs] = (conv * jax.nn.sigmoid(conv) * val[rows]).astype(BF16)
            last_ref[sg, :, cs] = g[seg - SUBLANES:]
        if tiles_per_batch > 1:
            carry_sc[:, cs] = gate[tm - SUBLANES:]
    o_ref[...] = x_ref[...] + _dot(y_sc[...], wd_ref[...])


def _conv_ffn(x2d, g3, w_up, w_conv, b_conv, w_down, state, *, seq, ff_chunk):
    n, d = x2d.shape
    dff = w_down.shape[0]
    tm = min(FFN_TILE, n)
    seg = min(seq, tm)
    tiles_per_batch = seq // seg
    nb = tm // seg
    assert n % tm == 0 and seq % seg == 0 and tm % seg == 0 and dff % ff_chunk == 0 and seg >= SUBLANES
    rows = pl.BlockSpec((tm, d), lambda i: (i, 0))
    full = _resident
    per_batch = lambda r: pl.BlockSpec((nb, r, dff), lambda i: (i // tiles_per_batch, 0, 0))
    return pl.pallas_call(
        functools.partial(_ffn_kernel, seg=seg, tiles_per_batch=tiles_per_batch, ff_chunk=ff_chunk),
        grid=(n // tm,),
        in_specs=[rows, full(g3), full(w_up), full(w_conv), full(b_conv), full(w_down), per_batch(CONV_W - 1)],
        out_specs=[rows, per_batch(SUBLANES)],
        out_shape=[jax.ShapeDtypeStruct((n, d), F32), jax.ShapeDtypeStruct((n // seq, SUBLANES, dff), F32)],
        scratch_shapes=[pltpu.VMEM((SUBLANES, dff), F32), pltpu.VMEM((tm, dff), BF16)],
        compiler_params=_params("arbitrary"), name="conv_ffn",
    )(x2d, g3, w_up, w_conv, b_conv, w_down, state)


def _row(v):
    return v.astype(F32).reshape(1, -1)


def _layer_weights(l, w_in, b_f, g_qa, g_ka, rel_bias, g_qb, g_kb, w_o, g_norm1, g_norm2, g_mem, w_mq, w_mkv,
                   g_mq, g_mk, w_mo, g_norm3, w_up, w_conv, b_conv, w_down):
    n_main = 6 * WIDTH
    tile = lambda g: jnp.tile(g[l].astype(F32), N_HEADS)
    blk = jnp.arange(MXU_COLS) // HEAD_DIM
    return dict(
        w_main=w_in[l].T[:n_main].astype(BF16),
        w_f=jnp.pad(w_in[l][:, n_main:].T, ((0, BF16_ROWS - N_HEADS), (0, 0))).astype(BF16),
        b_f=b_f[l].astype(F32).reshape(N_HEADS, 1),
        gains=jnp.stack([tile(g_qa), tile(g_ka), tile(g_qb), tile(g_kb)]),
        seg=(blk[:, None] == blk[None, :]).astype(BF16),
        rel=rel_bias[l],
        w_o=w_o[l].astype(BF16), g1=_row(g_norm1[l]), g2=_row(g_norm2[l]), g3=_row(g_norm3[l]),
        g_mem=_row(g_mem[l]), w_mq=w_mq[l].astype(BF16), w_mkv=w_mkv[l].astype(BF16),
        g_mq=_row(g_mq[l]), g_mk=_row(g_mk[l]), w_mo=w_mo[l].astype(BF16),
        w_up=w_up[l].astype(BF16), w_conv=w_conv[l].astype(F32), b_conv=_row(b_conv[l]),
        w_down=w_down[l].astype(BF16))


def _post_attention(x2d, oa, ob, mk, mv, state, w, *, seq, ff_chunk):
    n, d = x2d.shape
    x2 = _mid(x2d, oa.reshape(n, WIDTH), ob.reshape(n, WIDTH), w["w_o"], w["g2"], w["w_mq"], w["g_mq"],
              mk, mv, w["w_mo"], seq=seq)
    y, last = _conv_ffn(x2, w["g3"], w["w_up"], w["w_conv"], w["b_conv"], w["w_down"], state,
                        seq=seq, ff_chunk=ff_chunk)
    return y, last[:, SUBLANES - (CONV_W - 1):]


def _prompt_layer(x, mem, w, bias):
    b, t, d = x.shape
    la = min(BAND_PAST, t)
    tm = PROMPT_TILE if t % PROMPT_TILE == 0 else ROW_TILE
    assert la <= tm and t % tm == 0
    qa, ka, va, qb, kb, vb, ka_tail, va_tail, kb_f, vb_f, logf = _in_proj(
        x.reshape(b * t, d), w["g1"], w["w_main"], w["w_f"], w["b_f"], w["gains"], w["seg"],
        tm=tm, tail_period=t // tm, tail_rows=la, time_minor=True)
    r3 = lambda a: a.reshape(b, t, WIDTH)
    oa = _band_prompt(r3(qa), r3(ka), r3(va), bias)
    logf = logf.reshape(N_HEADS, b, t).transpose(1, 0, 2)
    crow = _cumsum_lanes(logf.reshape(b * N_HEADS, t)).reshape(b, N_HEADS, t)
    logf = logf.transpose(0, 2, 1)
    ob = _fox_prompt(r3(qb), r3(kb), r3(vb), crow)
    mk_f, mv_f, mk, mv = _mem_kv(mem, w["g_mem"], w["w_mkv"], w["g_mk"])
    dff = w["w_down"].shape[0]
    y, conv = _post_attention(x.reshape(b * t, d), oa, ob, mk, mv, jnp.zeros((b, CONV_W - 1, dff), F32), w,
                              seq=t, ff_chunk=dff // 11)
    heads = lambda a, n: a.reshape(b, N_HEADS, HEAD_DIM, n).transpose(0, 3, 1, 2)
    mem_heads = lambda a: (a.reshape(b, mem.shape[1], 2, N_MEM_HEADS, LANES).transpose(0, 1, 3, 2, 4)
                           .reshape(b, mem.shape[1], N_MEM_HEADS, d // N_MEM_HEADS))
    return (y.reshape(b, t, d), heads(ka_tail, la), heads(va_tail, la), heads(kb_f, t), heads(vb_f, t), logf,
            mem_heads(mk_f), mem_heads(mv_f), conv)


def _sample_layer(x, cache_a_k, cache_a_v, cache_b_k, cache_b_v, cache_b_logf, cache_mem_k, cache_mem_v,
                  state_conv, w, bias):
    b, s, d = x.shape
    past = cache_b_k.shape[1]
    assert ROW_TILE % s == 0 and (b * s) % ROW_TILE == 0
    qa, ka, va, qb, kb, vb, ka_f, va_f, kb_f, vb_f, logf = _in_proj(
        x.reshape(b * s, d), w["g1"], w["w_main"], w["w_f"], w["b_f"], w["gains"], w["seg"], tm=ROW_TILE,
        tail_period=1, tail_rows=ROW_TILE, time_minor=False)
    r3 = lambda a: a.reshape(b, s, WIDTH)
    time_minor = lambda a: a.transpose(0, 2, 3, 1)
    oa = _band_sample(r3(qa), r3(ka), r3(va), time_minor(cache_a_k), time_minor(cache_a_v), bias)
    logf = logf.reshape(N_HEADS, b, s).transpose(1, 0, 2)
    total = past + s
    all_logf = jnp.concatenate([cache_b_logf.astype(F32).transpose(0, 2, 1), logf], axis=2)
    all_logf = jnp.pad(all_logf, ((0, 0), (0, 0), (0, -total % LANES)))
    logf = logf.transpose(0, 2, 1)
    cum = _cumsum_lanes(all_logf.reshape(b * N_HEADS, -1)).reshape(b, N_HEADS, -1)
    crow_new = cum[:, :, past:total]
    pad_rows = lambda a: jnp.pad(a, ((0, 0), (0, -s % LANES), (0, 0)))
    ob = _fox_sample(r3(qb), pad_rows(r3(kb)), pad_rows(r3(vb)), time_minor(cache_b_k), time_minor(cache_b_v),
                     cum[:, :, :past], jnp.pad(crow_new, ((0, 0), (0, 0), (0, -s % LANES))))
    dff = w["w_down"].shape[0]
    y, conv = _post_attention(x.reshape(b * s, d), oa, ob, _cache_mem_rows(cache_mem_k.astype(F32)),
                              _cache_mem_rows(cache_mem_v.astype(F32)), state_conv.astype(F32), w,
                              seq=s, ff_chunk=dff // 11)
    heads = lambda a: a.reshape(b, s, N_HEADS, HEAD_DIM)
    return y.reshape(b, s, d), heads(ka_f), heads(va_f), heads(kb_f), heads(vb_f), logf, conv


def kernel(x_prompt, x_sample, cache_a_k, cache_a_v, cache_b_k, cache_b_v, cache_b_logf, cache_mem_k, cache_mem_v, state_conv, mem_prompt, w_in, b_f, g_qa, g_ka, rel_bias, g_qb, g_kb, w_o, g_norm1, g_norm2, g_mem, w_mq, w_mkv, g_mq, g_mk, w_mo, g_norm3, w_up, w_conv, b_conv, w_down):
    depth = w_in.shape[0]
    xp, xs = x_prompt, x_sample
    outs_p, outs_s = [], []
    for l in range(depth):
        w = _layer_weights(l, w_in, b_f, g_qa, g_ka, rel_bias, g_qb, g_kb, w_o, g_norm1, g_norm2, g_mem, w_mq,
                           w_mkv, g_mq, g_mk, w_mo, g_norm3, w_up, w_conv, b_conv, w_down)
        bias = _band_bias(w["rel"])
        xp, *rest_p = _prompt_layer(xp, mem_prompt, w, bias)
        xs, *rest_s = _sample_layer(xs, cache_a_k[l], cache_a_v[l], cache_b_k[l], cache_b_v[l], cache_b_logf[l],
                                    cache_mem_k[l], cache_mem_v[l], state_conv[l], w, bias)
        outs_p.append(rest_p)
        outs_s.append(rest_s)
    stack = lambda outs: [jnp.stack(col, axis=0) for col in zip(*outs)]
    return (xp, xs, *stack(outs_p), *stack(outs_s))
```
